```python
import math
import jax, jax.numpy as jnp
from jax import lax
import numpy as np

D_MODEL = 1024
BATCH = 4
SEQ = 4096
DEPTH = 2

HEAD_DIM = 64
N_MIXERS = 4
N_HEADS_PER_MIXER = D_MODEL // (N_MIXERS * HEAD_DIM)
GROUP_W = N_HEADS_PER_MIXER * HEAD_DIM
MLA_NOPE_DIM = 64
MLA_ROPE_DIM = 32
MLA_V_DIM = HEAD_DIM
MLA_Q_LORA = D_MODEL // 4
MLA_KV_LORA = D_MODEL // 8
MIX_W = N_MIXERS * GROUP_W
IN_SIZES = (GROUP_W, GROUP_W, GROUP_W, N_HEADS_PER_MIXER,
            GROUP_W, GROUP_W, GROUP_W,
            GROUP_W, GROUP_W, GROUP_W,
            MLA_Q_LORA, MLA_KV_LORA, MLA_ROPE_DIM,
            MIX_W)
IN_W = sum(IN_SIZES)
Q_BLOCK = 128
MOBA_BLOCK = 256
MOBA_TOPK = 3
MOBA_Q_CHUNK = 64
DIL_PATTERNS = ((128, 1), (512, 4), (2048, 16))
DIL_BLOCK = 128
N_BUCKETS = 32
MAX_DISTANCE = 2048
N_BIAS_HEADS = 2 * N_HEADS_PER_MIXER
PLE_DIM = 256
ROPE_THETA = 10000.0
EPS = 1e-6
NEG = -1e30

kernel_name = "hymba_fox_moba_dilated_mla_block"


def rms_norm(x, g):
    xf = x.astype(jnp.float32)
    y = xf * lax.rsqrt(jnp.mean(xf * xf, axis=-1, keepdims=True) + EPS)
    return y * g.astype(jnp.float32)


def to_heads(t, n):
    b, s, _ = t.shape
    return t.reshape(b, s, n, -1).transpose(0, 2, 1, 3)


def from_heads(o):
    b, h, s, d = o.shape
    return o.transpose(0, 2, 1, 3).reshape(b, s, h * d)


def t5_bucket(dist):
    dist = jnp.maximum(dist, 0)
    max_exact = N_BUCKETS // 2
    d_f = jnp.maximum(dist, 1).astype(jnp.float32)
    large = max_exact + (jnp.log(d_f / max_exact) / math.log(MAX_DISTANCE / max_exact)
                         * (N_BUCKETS - max_exact)).astype(jnp.int32)
    large = jnp.minimum(large, N_BUCKETS - 1)
    return jnp.where(dist < max_exact, dist, large)


def rotary(x, pos):
    half = MLA_ROPE_DIM // 2
    inv = 1.0 / (ROPE_THETA ** (jnp.arange(half, dtype=jnp.float32) * 2.0 / MLA_ROPE_DIM))
    ang = pos.astype(jnp.float32)[:, None] * inv[None, :]
    cos, sin = jnp.cos(ang), jnp.sin(ang)
    x1, x2 = x[..., :half], x[..., half:]
    return jnp.concatenate([x1 * cos - x2 * sin, x1 * sin + x2 * cos], axis=-1)


def causal_block_attention(q, k, v, log_f_cum=None):
    b, h, s, _ = q.shape
    key_pos = jnp.arange(s)

    def one_block(i):
        start = i * Q_BLOCK
        qb = lax.dynamic_slice_in_dim(q, start, Q_BLOCK, axis=2)
        sc = jnp.einsum('bhqd,bhkd->bhqk', qb, k)
        if log_f_cum is not None:
            cq = lax.dynamic_slice_in_dim(log_f_cum, start, Q_BLOCK, axis=2)
            sc = sc + cq[..., :, None] - log_f_cum[..., None, :]
        q_pos = start + jnp.arange(Q_BLOCK)
        sc = jnp.where(key_pos[None, :] <= q_pos[:, None], sc, NEG)
        pr = jax.nn.softmax(sc, axis=-1)
        return jnp.einsum('bhqk,bhkd->bhqd', pr, v)

    out = lax.map(one_block, jnp.arange(s // Q_BLOCK))
    return out.transpose(1, 2, 0, 3, 4).reshape(b, h, s, -1)


def moba_attention(q, k, v, bias_table):
    b, h, s, hd = q.shape
    n_kb = -(-s // MOBA_BLOCK)
    pad = n_kb * MOBA_BLOCK - s
    kp = jnp.pad(k, ((0, 0), (0, 0), (0, pad), (0, 0))).reshape(b, h, n_kb, MOBA_BLOCK, hd)
    vp = jnp.pad(v, ((0, 0), (0, 0), (0, pad), (0, 0))).reshape(b, h, n_kb, MOBA_BLOCK, hd)
    k_mean = kp.mean(axis=3)
    n_sel = min(MOBA_TOPK, n_kb)
    bi = jnp.arange(b)[:, None, None, None]
    hi = jnp.arange(h)[None, :, None, None]
    blk_ids = jnp.arange(n_kb)
    in_blk = jnp.arange(MOBA_BLOCK)

    def one_chunk(i):
        start = i * MOBA_Q_CHUNK
        qc = lax.dynamic_slice_in_dim(q, start, MOBA_Q_CHUNK, axis=2)
        q_pos = start + jnp.arange(MOBA_Q_CHUNK)
        own = start // MOBA_BLOCK
        gate = jnp.einsum('bhqd,bhnd->bhqn', qc, k_mean)
        gate = jnp.where(blk_ids < own, gate, NEG)
        _, sel = lax.top_k(gate, n_sel)
        sel_ok = sel < own
        k_sel = kp[bi, hi, sel]
        v_sel = vp[bi, hi, sel]
        s_sel = jnp.einsum('bhqd,bhqnkd->bhqnk', qc, k_sel)
        pos_sel = sel[..., None] * MOBA_BLOCK + in_blk
        b_sel = bias_table[hi[..., None], t5_bucket(q_pos[:, None, None] - pos_sel)]
        s_sel = jnp.where(sel_ok[..., None], s_sel + b_sel, NEG)
        s_sel = s_sel.reshape(b, h, MOBA_Q_CHUNK, n_sel * MOBA_BLOCK)
        k_own = lax.dynamic_index_in_dim(kp, own, axis=2, keepdims=False)
        v_own = lax.dynamic_index_in_dim(vp, own, axis=2, keepdims=False)
        rel_own = q_pos[:, None] - (own * MOBA_BLOCK + in_blk)[None, :]
        s_own = jnp.einsum('bhqd,bhkd->bhqk', qc, k_own) + bias_table[:, t5_bucket(rel_own)]
        s_own = jnp.where(rel_own >= 0, s_own, NEG)
        pr = jax.nn.softmax(jnp.concatenate([s_sel, s_own], axis=-1), axis=-1)
        p_sel = pr[..., :n_sel * MOBA_BLOCK].reshape(b, h, MOBA_Q_CHUNK, n_sel, MOBA_BLOCK)
        p_own = pr[..., n_sel * MOBA_BLOCK:]
        return (jnp.einsum('bhqnk,bhqnkd->bhqd', p_sel, v_sel)
                + jnp.einsum('bhqk,bhkd->bhqd', p_own, v_own))

    out = lax.map(one_chunk, jnp.arange(s // MOBA_Q_CHUNK))
    return out.transpose(1, 2, 0, 3, 4).reshape(b, h, s, hd)


def dilated_branch(q, k, v, bias_table, window, dil):
    b, h, s, hd = q.shape
    L = s // dil
    n_back = window // dil
    nb = -(-L // DIL_BLOCK)
    Lp = nb * DIL_BLOCK

    def sub(t):
        return t.reshape(b, h, L, dil, hd).transpose(0, 1, 3, 2, 4)

    qs = jnp.pad(sub(q), ((0, 0), (0, 0), (0, 0), (0, Lp - L), (0, 0))).reshape(b, h, dil, nb, DIL_BLOCK, hd)

    def band(t):
        tp = jnp.pad(sub(t), ((0, 0), (0, 0), (0, 0), (DIL_BLOCK, Lp - L), (0, 0)))
        tp = tp.reshape(b, h, dil, nb + 1, DIL_BLOCK, hd)
        return jnp.concatenate([tp[:, :, :, :-1], tp[:, :, :, 1:]], axis=4)

    kb, vb = band(k), band(v)
    qi = jnp.arange(DIL_BLOCK)[:, None]
    kj = jnp.arange(2 * DIL_BLOCK)[None, :]
    rel = qi + DIL_BLOCK - kj
    key_idx = jnp.arange(nb)[:, None, None] * DIL_BLOCK - DIL_BLOCK + kj
    valid = (rel >= 0) & (rel <= n_back) & (key_idx >= 0)
    bias = bias_table[:, t5_bucket(rel * dil)]
    sc = jnp.einsum('bhrnqd,bhrnkd->bhrnqk', qs, kb) + bias[None, :, None, None]
    sc = jnp.where(valid, sc, NEG)
    lse = jax.nn.logsumexp(sc, axis=-1)
    pr = jnp.exp(sc - lse[..., None])
    o = jnp.einsum('bhrnqk,bhrnkd->bhrnqd', pr, vb)
    o = o.reshape(b, h, dil, Lp, hd)[:, :, :, :L].transpose(0, 1, 3, 2, 4).reshape(b, h, s, hd)
    lse = lse.reshape(b, h, dil, Lp)[..., :L].transpose(0, 1, 3, 2).reshape(b, h, s)
    return o, lse


def dilated_attention(q, k, v, bias_table):
    outs, lses = [], []
    for window, dil in DIL_PATTERNS:
        o, l = dilated_branch(q, k, v, bias_table, window, dil)
        outs.append(o)
        lses.append(l)
    w = jax.nn.softmax(jnp.stack(lses, axis=0), axis=0)
    return jnp.sum(w[..., None] * jnp.stack(outs, axis=0), axis=0)


def split_points():
    pts, acc = [], 0
    for sz in IN_SIZES[:-1]:
        acc += sz
        pts.append(acc)
    return pts


def hybrid_layer(x, p_i, ln_g, w_in, b_forget, qk_gain, mla_q_norm, mla_kv_norm,
                 mla_nope_gain, mla_rope_gain, w_uq, w_ukv, w_out, rel_bias,
                 ple_norm_g, w_ple_gate, w_ple_proj):
    b, s, _ = x.shape
    nh = N_HEADS_PER_MIXER
    scale = HEAD_DIM ** -0.5
    h = rms_norm(x, ln_g)
    proj = h @ w_in
    (fq, fk, fv, ff, mq, mk, mv, dq, dk, dv, cq, ckv, kr, gate) = jnp.split(proj, split_points(), axis=-1)

    q = rms_norm(to_heads(fq, nh), qk_gain[0]) * scale
    k = rms_norm(to_heads(fk, nh), qk_gain[1])
    log_f = jax.nn.log_sigmoid((ff + b_forget).astype(jnp.float32)).transpose(0, 2, 1)
    o_fox = causal_block_attention(q, k, to_heads(fv, nh), jnp.cumsum(log_f, axis=-1))

    q = rms_norm(to_heads(mq, nh), qk_gain[2]) * scale
    k = rms_norm(to_heads(mk, nh), qk_gain[3])
    o_moba = moba_attention(q, k, to_heads(mv, nh), rel_bias[:, :nh].T)

    q = rms_norm(to_heads(dq, nh), qk_gain[4]) * scale
    k = rms_norm(to_heads(dk, nh), qk_gain[5])
    o_dil = dilated_attention(q, k, to_heads(dv, nh), rel_bias[:, nh:].T)

    pos = jnp.arange(s)
    qf = to_heads(rms_norm(cq, mla_q_norm) @ w_uq, nh)
    kvf = to_heads(rms_norm(ckv, mla_kv_norm) @ w_ukv, nh)
    q_nope = rms_norm(qf[..., :MLA_NOPE_DIM], mla_nope_gain[0])
    q_rope = rotary(rms_norm(qf[..., MLA_NOPE_DIM:], mla_rope_gain[0]), pos)
    k_nope = rms_norm(kvf[..., :MLA_NOPE_DIM], mla_nope_gain[1])
    v_mla = kvf[..., MLA_NOPE_DIM:]
    k_rope = rotary(rms_norm(kr, mla_rope_gain[1]), pos)[:, None]
    q_m = jnp.concatenate([q_nope, q_rope], axis=-1) * (MLA_NOPE_DIM + MLA_ROPE_DIM) ** -0.5
    k_m = jnp.concatenate([k_nope, jnp.broadcast_to(k_rope, (b, nh, s, MLA_ROPE_DIM))], axis=-1)
    o_mla = causal_block_attention(q_m, k_m, v_mla)

    mix = jnp.concatenate([from_heads(o_fox), from_heads(o_moba), from_heads(o_dil), from_heads(o_mla)], axis=-1)
    x = x + (mix * jax.nn.silu(gate)) @ w_out

    ple_gate = jax.nn.sigmoid(rms_norm(x, ple_norm_g) @ w_ple_gate)
    return x + ple_gate * (p_i @ w_ple_proj)


def setup_inputs(seed: int = 0) -> dict:
    key = jax.random.key(seed)
    ks = jax.random.split(key, 18)

    def nrm(k, shape, sc):
        return sc * jax.random.normal(k, shape, jnp.float32)

    nh = N_HEADS_PER_MIXER
    return {
        "x": nrm(ks[0], (BATCH, SEQ, D_MODEL), 1.0),
        "p": nrm(ks[1], (DEPTH, BATCH, SEQ, PLE_DIM), 1.0),
        "ln_g": 1.0 + nrm(ks[2], (DEPTH, D_MODEL), 0.05),
        "w_in": nrm(ks[3], (DEPTH, D_MODEL, IN_W), D_MODEL ** -0.5),
        "b_forget": 2.0 + nrm(ks[4], (DEPTH, nh), 0.5),
        "qk_gain": 1.0 + nrm(ks[5], (DEPTH, 6, HEAD_DIM), 0.05),
        "mla_q_norm": 1.0 + nrm(ks[6], (DEPTH, MLA_Q_LORA), 0.05),
        "mla_kv_norm": 1.0 + nrm(ks[7], (DEPTH, MLA_KV_LORA), 0.05),
        "mla_nope_gain": 1.0 + nrm(ks[8], (DEPTH, 2, MLA_NOPE_DIM), 0.05),
        "mla_rope_gain": 1.0 + nrm(ks[9], (DEPTH, 2, MLA_ROPE_DIM), 0.05),
        "w_uq": nrm(ks[10], (DEPTH, MLA_Q_LORA, nh * (MLA_NOPE_DIM + MLA_ROPE_DIM)), MLA_Q_LORA ** -0.5),
        "w_ukv": nrm(ks[11], (DEPTH, MLA_KV_LORA, nh * (MLA_NOPE_DIM + MLA_V_DIM)), MLA_KV_LORA ** -0.5),
        "w_out": nrm(ks[12], (DEPTH, MIX_W, D_MODEL), MIX_W ** -0.5),
        "rel_bias": nrm(ks[13], (N_BUCKETS, N_BIAS_HEADS), 0.2),
        "ple_norm_g": 1.0 + nrm(ks[14], (DEPTH, D_MODEL), 0.05),
        "w_ple_gate": nrm(ks[15], (DEPTH, D_MODEL, D_MODEL), D_MODEL ** -0.5),
        "w_ple_proj": nrm(ks[16], (DEPTH, PLE_DIM, D_MODEL), PLE_DIM ** -0.5),
    }


def reference(x, p, ln_g, w_in, b_forget, qk_gain, mla_q_norm, mla_kv_norm,
              mla_nope_gain, mla_rope_gain, w_uq, w_ukv, w_out, rel_bias,
              ple_norm_g, w_ple_gate, w_ple_proj):
    for i in range(DEPTH):
        x = hybrid_layer(x, p[i], ln_g[i], w_in[i], b_forget[i], qk_gain[i],
                         mla_q_norm[i], mla_kv_norm[i], mla_nope_gain[i], mla_rope_gain[i],
                         w_uq[i], w_ukv[i], w_out[i], rel_bias,
                         ple_norm_g[i], w_ple_gate[i], w_ple_proj[i])
    return x
```

```python
import functools
import math

import numpy as np
import jax
import jax.numpy as jnp
from jax import lax
from jax.experimental import pallas as pl
from jax.experimental.pallas import tpu as pltpu

F32 = jnp.float32
BF16 = jnp.bfloat16

D_MODEL = 1024
N_HEADS = 4
HEAD_DIM = 64
GROUP_W = N_HEADS * HEAD_DIM
MLA_NOPE = 64
MLA_ROPE = 32
MLA_HALF = MLA_ROPE // 2
MLA_QK_W = N_HEADS * (MLA_NOPE + MLA_ROPE)
MLA_Q_LORA = 256
MLA_KV_LORA = 128
PLE_DIM = 256
MOBA_BLOCK = 256
MOBA_TOPK = 3
DIL_PATTERNS = ((128, 1), (512, 4), (2048, 16))
DIL_BLOCK = 128
N_BUCKETS = 32
MAX_DISTANCE = 2048
ROPE_THETA = 10000.0
EPS = 1e-6
NEG = -1e30

LANES = 128
VMEM_LIMIT_BYTES = 56 * 1024 * 1024

COL_FOX, COL_MOBA, COL_DIL = 0, 768, 1536
COL_CQ, COL_CKV, COL_KR, COL_FF, COL_GATE = 2304, 2560, 2688, 2816, 3072
PROJ_W = 4096
PROJ_TN = 512

TM_PROJ = 256
T_PREP = 256
TQ = 256
TK = 256
MOBA_TB_ENTRIES = 8
TM_MERGE = 256


def _bucket_np(d):
    d = np.maximum(np.asarray(d, np.int64), 0)
    max_exact = N_BUCKETS // 2
    d_f = np.maximum(d, 1).astype(np.float64)
    val = np.log(d_f / max_exact) / math.log(MAX_DISTANCE / max_exact) * (N_BUCKETS - max_exact)
    frac = np.abs(val - np.round(val))
    on_edge = (frac < 1e-5) & (d > max_exact) & (val < N_BUCKETS - max_exact - 0.5)
    assert not on_edge.any(), "distance too close to a bucket edge for a static table"
    large = np.minimum(max_exact + np.floor(val + 1e-9).astype(np.int64), N_BUCKETS - 1)
    return np.where(d < max_exact, d, large)


def _bucket_steps(lo, hi):
    ds = np.arange(lo, hi + 1)
    bs = _bucket_np(ds)
    steps = [(int(ds[i]), int(bs[i])) for i in range(1, len(ds)) if bs[i] != bs[i - 1]]
    return int(bs[0]), steps


def _compiler_params(sem):
    return pltpu.CompilerParams(dimension_semantics=sem, vmem_limit_bytes=VMEM_LIMIT_BYTES)


def _bias_from_steps(d, rb_ref, col, lo, hi):
    b0, steps = _bucket_steps(lo, hi)
    val = jnp.full(d.shape, rb_ref[b0, col], F32)
    for t, b in steps:
        val = jnp.where(d >= t, rb_ref[b, col], val)
    return val


def _bias_tables_kernel(rb_ref, moba_ref, dil_ref):
    kl = lax.broadcasted_iota(jnp.int32, (MOBA_BLOCK, MOBA_BLOCK), 0)
    ql = lax.broadcasted_iota(jnp.int32, (MOBA_BLOCK, MOBA_BLOCK), 1)
    for e in range(MOBA_TB_ENTRIES - 1):
        d = jnp.maximum(e * MOBA_BLOCK + ql - kl, 0)
        lo = max(e * MOBA_BLOCK - (MOBA_BLOCK - 1), 0)
        hi = e * MOBA_BLOCK + (MOBA_BLOCK - 1)
        for h in range(N_HEADS):
            moba_ref[e, h] = _bias_from_steps(d, rb_ref, h, lo, hi)
    for h in range(N_HEADS):
        moba_ref[MOBA_TB_ENTRIES - 1, h] = jnp.full((MOBA_BLOCK, MOBA_BLOCK), rb_ref[N_BUCKETS - 1, h], F32)
    qi = lax.broadcasted_iota(jnp.int32, (DIL_BLOCK, 2 * DIL_BLOCK), 0)
    kj = lax.broadcasted_iota(jnp.int32, (DIL_BLOCK, 2 * DIL_BLOCK), 1)
    for p, (_, dil) in enumerate(DIL_PATTERNS):
        d = jnp.maximum((qi + DIL_BLOCK - kj) * dil, 0)
        for h in range(N_HEADS):
            dil_ref[p, h] = _bias_from_steps(d, rb_ref, N_HEADS + h, 0, (2 * DIL_BLOCK - 1) * dil)


def _bias_tables(rel_bias):
    assert _bucket_np((MOBA_TB_ENTRIES - 1) * MOBA_BLOCK - (MOBA_BLOCK - 1)) == N_BUCKETS - 1
    return pl.pallas_call(
        _bias_tables_kernel,
        out_shape=(jax.ShapeDtypeStruct((MOBA_TB_ENTRIES, N_HEADS, MOBA_BLOCK, MOBA_BLOCK), F32),
                   jax.ShapeDtypeStruct((len(DIL_PATTERNS), N_HEADS, DIL_BLOCK, 2 * DIL_BLOCK), F32)),
        in_specs=[pl.BlockSpec(memory_space=pltpu.SMEM)],
        out_specs=(pl.BlockSpec(memory_space=pltpu.VMEM), pl.BlockSpec(memory_space=pltpu.VMEM)),
        compiler_params=pltpu.CompilerParams(vmem_limit_bytes=VMEM_LIMIT_BYTES),
        name="bias_tables",
    )(rel_bias)


def _proj_kernel(x_ref, g_ref, w_ref, o_ref):
    x = x_ref[...]
    ms = jnp.mean(x * x, axis=-1, keepdims=True)
    h = (x * lax.rsqrt(ms + EPS) * g_ref[...]).astype(BF16)
    for c in range(PROJ_W // PROJ_TN):
        sl = slice(c * PROJ_TN, (c + 1) * PROJ_TN)
        o_ref[:, sl] = jnp.dot(h, w_ref[:, sl], preferred_element_type=F32)


def _in_projection(x2, ln_g, w_all):
    rows = x2.shape[0]
    return pl.pallas_call(
        _proj_kernel,
        out_shape=jax.ShapeDtypeStruct((rows, PROJ_W), F32),
        grid=(rows // TM_PROJ,),
        in_specs=[pl.BlockSpec((TM_PROJ, D_MODEL), lambda i: (i, 0)),
                  pl.BlockSpec((1, D_MODEL), lambda i: (0, 0)),
                  pl.BlockSpec((D_MODEL, PROJ_W), lambda i: (0, 0))],
        out_specs=pl.BlockSpec((TM_PROJ, PROJ_W), lambda i: (i, 0)),
        compiler_params=_compiler_params(("arbitrary",)),
        name="in_projection",
    )(x2, ln_g, w_all)


def _group_mean_sq(x, g_mat):
    xx = x * x
    hi = xx.astype(BF16)
    lo = (xx - hi.astype(F32)).astype(BF16)
    return (jnp.dot(hi, g_mat, preferred_element_type=F32)
            + jnp.dot(lo, g_mat, preferred_element_type=F32))


def _group_norm(x, gain, g_mat):
    return x * lax.rsqrt(_group_mean_sq(x, g_mat) + EPS) * gain


def _row_norm(x, gain):
    return x * lax.rsqrt(jnp.mean(x * x, axis=-1, keepdims=True) + EPS) * gain


def _log_sigmoid(x):
    return -(jnp.maximum(-x, 0.0) + jnp.log1p(jnp.exp(-jnp.abs(x))))


def _prep_kernel(pf_ref, pm_ref, pd_ref, pcq_ref, pckv_ref, pkr_ref, pff_ref,
                 gains_ref, bfor_ref, qng_ref, kvng_ref, nopeg_ref, ropeg_ref,
                 wuq_ref, wukv_ref, cos_ref, sin_ref, g64_ref, g32_ref, tri_ref,
                 fqT_ref, fk_ref, fvT_ref, fck_ref, fcrow_ref,
                 mqT_ref, mk_ref, mvT_ref, mkmean_ref,
                 lqT_ref, lk_ref, lvT_ref,
                 dq1_ref, dk1_ref, dv1_ref, dq4_ref, dk4_ref, dv4_ref, dq16_ref, dk16_ref, dv16_ref,
                 carry_ref, sq_ref, sk_ref, sv_ref):
    t = pl.program_id(1)
    g64 = g64_ref[...]
    scale = HEAD_DIM ** -0.5
    W = GROUP_W

    pf = pf_ref[0]
    fq = _group_norm(pf[:, :W], gains_ref[0:1, :], g64) * scale
    fk = _group_norm(pf[:, W:2 * W], gains_ref[1:2, :], g64)
    fqT_ref[0] = fq.T.astype(BF16)
    fk_ref[0] = fk.astype(BF16)
    fvT_ref[0, 0] = pf[:, 2 * W:].T.astype(BF16)

    @pl.when(t == 0)
    def _():
        carry_ref[...] = jnp.zeros_like(carry_ref)

    log_f = _log_sigmoid(pff_ref[0] + bfor_ref[...])
    c = jnp.dot(tri_ref[...], log_f, preferred_element_type=F32,
                precision=lax.Precision.HIGHEST) + carry_ref[...]
    carry_ref[...] = c[T_PREP - 1:T_PREP, :]
    for h in range(N_HEADS):
        fck_ref[0, h] = jnp.broadcast_to(c[:, h:h + 1], (T_PREP, LANES))
    fcrow_ref[0] = c.T[0:8, :]

    pm = pm_ref[0]
    mq = _group_norm(pm[:, :W], gains_ref[2:3, :], g64) * scale
    mk = _group_norm(pm[:, W:2 * W], gains_ref[3:4, :], g64)
    mqT_ref[0] = mq.T.astype(BF16)
    mk_ref[0] = mk.astype(BF16)
    mvT_ref[0, 0] = pm[:, 2 * W:].T.astype(BF16)
    mkmean_ref[0, pl.ds(t, 1), :] = jnp.mean(mk, axis=0, keepdims=True)

    pd = pd_ref[0]
    dq = _group_norm(pd[:, :W], gains_ref[4:5, :], g64) * scale
    dk = _group_norm(pd[:, W:2 * W], gains_ref[5:6, :], g64)
    dv = pd[:, 2 * W:]
    dq1_ref[0] = dq.astype(BF16)
    dk1_ref[0] = dk.astype(BF16)
    dv1_ref[0] = dv.astype(BF16)
    for val, dst in ((dq, sq_ref), (dk, sk_ref), (dv, sv_ref)):
        for half in range(GROUP_W // LANES):
            dst[half] = val[:, half * LANES:(half + 1) * LANES]
    for dil, outs in ((4, (dq4_ref, dk4_ref, dv4_ref)), (16, (dq16_ref, dk16_ref, dv16_ref))):
        n = T_PREP // dil
        for r in range(dil):
            for src, dst in zip((sq_ref, sk_ref, sv_ref), outs):
                dst[0, r] = jnp.concatenate(
                    [src[half, pl.ds(r, n, stride=dil), :] for half in range(GROUP_W // LANES)],
                    axis=1).astype(BF16)

    cos = cos_ref[...]
    sin = sin_ref[...]
    qscale = (MLA_NOPE + MLA_ROPE) ** -0.5
    cqn = _row_norm(pcq_ref[0], qng_ref[...]).astype(BF16)
    qf = jnp.dot(cqn, wuq_ref[...], preferred_element_type=F32)
    q_nope = _group_norm(qf[:, :W], nopeg_ref[0:1, :], g64)
    qr = _group_norm(qf[:, W:], ropeg_ref[0:1, :], g32_ref[...])
    qr = qr * cos + pltpu.roll(qr, LANES // 2, 1) * sin
    lq = jnp.concatenate([q_nope, qr], axis=1) * qscale
    lqT_ref[0] = lq.T.astype(BF16)
    ckvn = _row_norm(pckv_ref[0], kvng_ref[...]).astype(BF16)
    kvf = jnp.dot(ckvn, wukv_ref[...], preferred_element_type=F32)
    k_nope = _group_norm(kvf[:, :W], nopeg_ref[1:2, :], g64)
    kr = _row_norm(pkr_ref[0], ropeg_ref[1:2, :])
    kr = kr * cos + pltpu.roll(kr, LANES // 2, 1) * sin
    lk_ref[0] = jnp.concatenate([k_nope, kr], axis=1).astype(BF16)
    lvT_ref[0, 0] = kvf[:, W:].T.astype(BF16)


def _prep(proj3, consts, lp):
    B, S, _ = proj3.shape
    T = T_PREP
    NT = S // T
    assert S % T == 0 and S // MOBA_BLOCK == NT

    def pspec(width, col):
        return pl.BlockSpec((1, T, width), lambda b, t, c=col // width: (b, t, c))

    def cspec(shape):
        return pl.BlockSpec(shape, lambda b, t: (0,) * len(shape))

    in_specs = [pspec(768, COL_FOX), pspec(768, COL_MOBA), pspec(768, COL_DIL),
                pspec(256, COL_CQ), pspec(128, COL_CKV), pspec(128, COL_KR), pspec(128, COL_FF),
                cspec((6, GROUP_W)), cspec((1, LANES)), cspec((1, MLA_Q_LORA)), cspec((1, MLA_KV_LORA)),
                cspec((2, GROUP_W)), cspec((2, LANES)),
                cspec((MLA_Q_LORA, MLA_QK_W)), cspec((MLA_KV_LORA, 2 * GROUP_W)),
                pl.BlockSpec((T, LANES), lambda b, t: (t, 0)), pl.BlockSpec((T, LANES), lambda b, t: (t, 0)),
                cspec((GROUP_W, GROUP_W)), cspec((LANES, LANES)), cspec((T, T))]

    def qT(w):
        return jax.ShapeDtypeStruct((B, w, S), BF16), pl.BlockSpec((1, w, T), lambda b, t: (b, 0, t))

    def rows(w):
        return jax.ShapeDtypeStruct((B, S, w), BF16), pl.BlockSpec((1, T, w), lambda b, t: (b, t, 0))

    def vT():
        return (jax.ShapeDtypeStruct((B, NT, GROUP_W, T), BF16),
                pl.BlockSpec((1, 1, GROUP_W, T), lambda b, t: (b, t, 0, 0)))

    def resid(dil):
        return (jax.ShapeDtypeStruct((B, dil, S // dil, GROUP_W), BF16),
                pl.BlockSpec((1, dil, T // dil, GROUP_W), lambda b, t: (b, 0, t, 0)))

    outs = [qT(GROUP_W), rows(GROUP_W), vT(),
            (jax.ShapeDtypeStruct((B, N_HEADS, S, LANES), F32),
             pl.BlockSpec((1, N_HEADS, T, LANES), lambda b, t: (b, 0, t, 0))),
            (jax.ShapeDtypeStruct((B, 8, S), F32), pl.BlockSpec((1, 8, T), lambda b, t: (b, 0, t))),
            qT(GROUP_W), rows(GROUP_W), vT(),
            (jax.ShapeDtypeStruct((B, NT, GROUP_W), F32), pl.BlockSpec((1, NT, GROUP_W), lambda b, t: (b, 0, 0))),
            qT(MLA_QK_W), rows(MLA_QK_W), vT(),
            rows(GROUP_W), rows(GROUP_W), rows(GROUP_W),
            resid(4), resid(4), resid(4), resid(16), resid(16), resid(16)]
    return pl.pallas_call(
        _prep_kernel,
        out_shape=tuple(o[0] for o in outs),
        grid=(B, NT),
        in_specs=in_specs,
        out_specs=tuple(o[1] for o in outs),
        scratch_shapes=[pltpu.VMEM((1, LANES), F32)] + [pltpu.VMEM((GROUP_W // LANES, T, LANES), F32)] * 3,
        compiler_params=_compiler_params(("arbitrary", "arbitrary")),
        name="prep",
    )(proj3, proj3, proj3, proj3, proj3, proj3, proj3,
      lp["gains"], lp["b_forget"], lp["q_norm"], lp["kv_norm"], lp["nope_gain"], lp["rope_gain"],
      lp["w_uq"], lp["w_ukv"], consts["cos"], consts["sin"], consts["g64"], consts["g32"], consts["tri"])


def _head_row_mask(dk, h):
    r = lax.broadcasted_iota(jnp.int32, (dk, TQ), 0)
    m = (r >= HEAD_DIM * h) & (r < HEAD_DIM * (h + 1))
    if dk == MLA_QK_W:
        for base in (GROUP_W, GROUP_W + N_HEADS * MLA_HALF):
            lo = base + MLA_HALF * h
            m = m | ((r >= lo) & (r < lo + MLA_HALF))
    return m


def _flash_head(i, h, qm, k_ref, vT_ref, bias_fn):
    kpos = lax.broadcasted_iota(jnp.int32, (TK, TQ), 0)
    qpos = lax.broadcasted_iota(jnp.int32, (TK, TQ), 1)
    causal = kpos <= qpos

    def step(j, carry, diagonal):
        m, l, acc = carry
        kt = k_ref[0, pl.ds(pl.multiple_of(j * TK, TK), TK), :]
        s = jnp.dot(kt, qm, preferred_element_type=F32)
        s = bias_fn(s, j, diagonal)
        if diagonal:
            s = jnp.where(causal, s, NEG)
        m_new = jnp.maximum(m, jnp.max(s, axis=0, keepdims=True))
        p = jnp.exp(s - m_new)
        alpha = jnp.exp(m - m_new)
        l = alpha * l + jnp.sum(p, axis=0, keepdims=True)
        vt = vT_ref[0, j, HEAD_DIM * h:HEAD_DIM * (h + 1), :]
        acc = alpha * acc + jnp.dot(vt, p.astype(BF16), preferred_element_type=F32)
        return m_new, l, acc

    init = (jnp.full((1, TQ), NEG, F32), jnp.zeros((1, TQ), F32), jnp.zeros((HEAD_DIM, TQ), F32))
    carry = step(i, init, True)
    _, l, acc = lax.fori_loop(0, i, lambda j, c: step(j, c, False), carry)
    return acc / l


def _fox_kernel(qT_ref, k_ref, vT_ref, ck_ref, crow_ref, o_ref):
    i = pl.program_id(1)
    q = qT_ref[0]
    outs = []
    for h in range(N_HEADS):
        qm = jnp.where(_head_row_mask(GROUP_W, h), q, jnp.zeros_like(q))
        cq = crow_ref[0, h:h + 1, :]

        def bias_fn(s, j, diagonal, h=h, cq=cq):
            ck = ck_ref[0, h, pl.ds(pl.multiple_of(j * TK, TK), TK), :]
            return s + cq - jnp.concatenate([ck] * (TQ // LANES), axis=1)

        outs.append(_flash_head(i, h, qm, k_ref, vT_ref, bias_fn))
    o_ref[0] = jnp.concatenate(outs, axis=0).T


def _mla_kernel(qT_ref, k_ref, vT_ref, o_ref):
    i = pl.program_id(1)
    q = qT_ref[0]
    outs = []
    for h in range(N_HEADS):
        qm = jnp.where(_head_row_mask(MLA_QK_W, h), q, jnp.zeros_like(q))
        outs.append(_flash_head(i, h, qm, k_ref, vT_ref, lambda s, j, diagonal: s))
    o_ref[0] = jnp.concatenate(outs, axis=0).T


def _moba_kernel(qT_ref, k_ref, vT_ref, kmean_ref, tb_ref, o_ref, sel_ref):
    i = pl.program_id(1)
    q = qT_ref[0]
    kmean = kmean_ref[0].astype(BF16)
    nb = kmean.shape[0]
    blk = lax.broadcasted_iota(jnp.int32, (nb, TQ), 0)
    outs = []
    for h in range(N_HEADS):
        qm = jnp.where(_head_row_mask(GROUP_W, h), q, jnp.zeros_like(q))
        gate = jnp.dot(kmean, qm, preferred_element_type=F32)
        gate = jnp.where(blk < i, gate, NEG)
        rank = jnp.zeros((nb, TQ), jnp.int32)
        for jp in range(nb):
            row = gate[jp:jp + 1, :]
            ahead = (row > gate) | ((row == gate) & (jp < blk))
            rank = rank + ahead.astype(jnp.int32)
        sel_ref[h] = ((rank < MOBA_TOPK) & (blk < i)).astype(F32)

        def bias_fn(s, j, diagonal, h=h):
            s = s + tb_ref[jnp.minimum(i - j, MOBA_TB_ENTRIES - 1), h]
            if diagonal:
                return s
            return jnp.where(sel_ref[h, pl.ds(j, 1), :] > 0.5, s, NEG)

        outs.append(_flash_head(i, h, qm, k_ref, vT_ref, bias_fn))
    o_ref[0] = jnp.concatenate(outs, axis=0).T


def _dense_mixer(kernel_fn, name, qT, k, vT, extra_args=(), extra_specs=(), scratch=()):
    B, dk, S = qT.shape
    NT = S // TK
    assert TQ == TK and S % TQ == 0
    in_specs = [pl.BlockSpec((1, dk, TQ), lambda b, i: (b, 0, i)),
                pl.BlockSpec((1, S, dk), lambda b, i: (b, 0, 0)),
                pl.BlockSpec((1, NT, GROUP_W, TK), lambda b, i: (b, 0, 0, 0))] + list(extra_specs)
    return pl.pallas_call(
        kernel_fn,
        out_shape=jax.ShapeDtypeStruct((B, S, GROUP_W), F32),
        grid=(B, S // TQ),
        in_specs=in_specs,
        out_specs=pl.BlockSpec((1, TQ, GROUP_W), lambda b, i: (b, i, 0)),
        scratch_shapes=list(scratch),
        compiler_params=_compiler_params(("arbitrary", "arbitrary")),
        name=name,
    )(qT, k, vT, *extra_args)


def _dil_kernel(q_ref, kp_ref, kc_ref, vp_ref, vc_ref, tb_ref, o_ref, lse_ref):
    n = pl.program_id(1)
    q = q_ref[0]
    kb = jnp.concatenate([kp_ref[0], kc_ref[0]], axis=0)
    vb = jnp.concatenate([vp_ref[0], vc_ref[0]], axis=0)
    qi = lax.broadcasted_iota(jnp.int32, (DIL_BLOCK, 2 * DIL_BLOCK), 0)
    kj = lax.broadcasted_iota(jnp.int32, (DIL_BLOCK, 2 * DIL_BLOCK), 1)
    rel = qi + DIL_BLOCK - kj
    valid = (rel >= 0) & (rel <= DIL_BLOCK) & ((kj >= DIL_BLOCK) | (n > 0))
    qlane = lax.broadcasted_iota(jnp.int32, (DIL_BLOCK, GROUP_W), 1)
    vlane = lax.broadcasted_iota(jnp.int32, (2 * DIL_BLOCK, GROUP_W), 1)
    o_acc = jnp.zeros((DIL_BLOCK, GROUP_W), F32)
    lse_map = jnp.zeros((DIL_BLOCK, GROUP_W), F32)
    for h in range(N_HEADS):
        in_head = (qlane >= HEAD_DIM * h) & (qlane < HEAD_DIM * (h + 1))
        qm = jnp.where(in_head, q, jnp.zeros_like(q))
        s = lax.dot_general(qm, kb, (((1,), (1,)), ((), ())), preferred_element_type=F32)
        s = jnp.where(valid, s + tb_ref[0, h], NEG)
        m = jnp.max(s, axis=1, keepdims=True)
        lse = m + jnp.log(jnp.sum(jnp.exp(s - m), axis=1, keepdims=True))
        pr = jnp.exp(s - lse).astype(BF16)
        v_head = (vlane >= HEAD_DIM * h) & (vlane < HEAD_DIM * (h + 1))
        o_acc = o_acc + jnp.dot(pr, jnp.where(v_head, vb, jnp.zeros_like(vb)), preferred_element_type=F32)
        lse_map = jnp.where(in_head, lse, lse_map)
    o_ref[0] = o_acc
    lse_ref[0] = lse_map


def _dilated_pattern(p, q, k, v, dil_tb):
    G, L, _ = q.shape
    assert L % DIL_BLOCK == 0
    nb = L // DIL_BLOCK
    cur = pl.BlockSpec((1, DIL_BLOCK, GROUP_W), lambda g, n: (g, n, 0))
    prev = pl.BlockSpec((1, DIL_BLOCK, GROUP_W), lambda g, n: (g, jnp.maximum(n - 1, 0), 0))
    out = jax.ShapeDtypeStruct((G, L, GROUP_W), F32)
    return pl.pallas_call(
        _dil_kernel,
        out_shape=(out, out),
        grid=(G, nb),
        in_specs=[cur, prev, cur, prev, cur,
                  pl.BlockSpec((1, N_HEADS, DIL_BLOCK, 2 * DIL_BLOCK), lambda g, n, p=p: (p, 0, 0, 0))],
        out_specs=(cur, cur),
        compiler_params=_compiler_params(("arbitrary", "arbitrary")),
        name=f"dilated_{DIL_PATTERNS[p][1]}",
    )(q, k, k, v, v, dil_tb)


def _dil_combine_kernel(o1_ref, l1_ref, o4_ref, l4_ref, o16_ref, l16_ref, out_ref, so4, sl4, so16, sl16):
    T = out_ref.shape[1]
    halves = GROUP_W // LANES
    for dil, srcs in ((4, ((o4_ref, so4), (l4_ref, sl4))), (16, ((o16_ref, so16), (l16_ref, sl16)))):
        for src, dst in srcs:
            for r in range(dil):
                for half in range(halves):
                    dst[half, pl.ds(r, T // dil, stride=dil), :] = src[0, r, :, half * LANES:(half + 1) * LANES]
    nat = lambda ref: jnp.concatenate([ref[half] for half in range(halves)], axis=1)
    l1, l4, l16 = l1_ref[0], nat(sl4), nat(sl16)
    m = jnp.maximum(jnp.maximum(l1, l4), l16)
    w1, w4, w16 = jnp.exp(l1 - m), jnp.exp(l4 - m), jnp.exp(l16 - m)
    tot = w1 + w4 + w16
    out_ref[0] = (w1 / tot) * o1_ref[0] + (w4 / tot) * nat(so4) + (w16 / tot) * nat(so16)


def _dil_combine(o1, l1, o4, l4, o16, l16):
    B, S, _ = o1.shape
    T = 256
    nat = pl.BlockSpec((1, T, GROUP_W), lambda b, t: (b, t, 0))
    r4 = pl.BlockSpec((1, 4, T // 4, GROUP_W), lambda b, t: (b, 0, t, 0))
    r16 = pl.BlockSpec((1, 16, T // 16, GROUP_W), lambda b, t: (b, 0, t, 0))
    return pl.pallas_call(
        _dil_combine_kernel,
        out_shape=jax.ShapeDtypeStruct((B, S, GROUP_W), F32),
        grid=(B, S // T),
        in_specs=[nat, nat, r4, r4, r16, r16],
        out_specs=nat,
        scratch_shapes=[pltpu.VMEM((GROUP_W // LANES, T, LANES), F32)] * 4,
        compiler_params=_compiler_params(("arbitrary", "arbitrary")),
        name="dil_combine",
    )(o1, l1, o4, l4, o16, l16)


def _sigmoid(x):
    return 1.0 / (1.0 + jnp.exp(-x))


def _merge_kernel(x_ref, fo_ref, mo_ref, do_ref, lo_ref, gate_ref, p_ref,
                  wout_ref, png_ref, wpg_ref, wpp_ref, o_ref):
    mix = jnp.concatenate([fo_ref[...], mo_ref[...], do_ref[...], lo_ref[...]], axis=1)
    g = gate_ref[...]
    y = jnp.dot((mix * (g * _sigmoid(g))).astype(BF16), wout_ref[...], preferred_element_type=F32)
    x1 = x_ref[...] + y
    hn = _row_norm(x1, png_ref[...]).astype(BF16)
    pg = _sigmoid(jnp.dot(hn, wpg_ref[...], preferred_element_type=F32))
    pp = jnp.dot(p_ref[...].astype(BF16), wpp_ref[...], preferred_element_type=F32)
    o_ref[...] = x1 + pg * pp


def _merge(x2, fo, mo, do, lo, proj, p2, lp):
    rows = x2.shape[0]
    TM = TM_MERGE

    def rspec(w, c=0):
        return pl.BlockSpec((TM, w), lambda i, c=c: (i, c))

    def cspec(shape):
        return pl.BlockSpec(shape, lambda i: (0, 0))

    return pl.pallas_call(
        _merge_kernel,
        out_shape=jax.ShapeDtypeStruct((rows, D_MODEL), F32),
        grid=(rows // TM,),
        in_specs=[rspec(D_MODEL), rspec(GROUP_W), rspec(GROUP_W), rspec(GROUP_W), rspec(GROUP_W),
                  rspec(D_MODEL, COL_GATE // D_MODEL), rspec(PLE_DIM),
                  cspec((D_MODEL, D_MODEL)), cspec((1, D_MODEL)), cspec((D_MODEL, D_MODEL)),
                  cspec((PLE_DIM, D_MODEL))],
        out_specs=rspec(D_MODEL),
        compiler_params=_compiler_params(("arbitrary",)),
        name="merge",
    )(x2, fo, mo, do, lo, proj, p2, lp["w_out"], lp["ple_norm_g"], lp["w_ple_gate"], lp["w_ple_proj"])


def _layer_params(i, ln_g, w_in, b_forget, qk_gain, mla_q_norm, mla_kv_norm, mla_nope_gain,
                  mla_rope_gain, w_uq, w_ukv, w_out, ple_norm_g, w_ple_gate, w_ple_proj):
    W = GROUP_W
    w = w_in[i]
    off = {}
    acc = 0
    for name, sz in (("fq", W), ("fk", W), ("fv", W), ("ff", N_HEADS), ("mq", W), ("mk", W), ("mv", W),
                     ("dq", W), ("dk", W), ("dv", W), ("cq", MLA_Q_LORA), ("ckv", MLA_KV_LORA),
                     ("kr", MLA_ROPE), ("gate", D_MODEL)):
        off[name] = acc
        acc += sz
    kr_idx = np.concatenate([np.tile(np.arange(MLA_HALF), N_HEADS),
                             np.tile(np.arange(MLA_HALF, MLA_ROPE), N_HEADS)]) + off["kr"]
    zeros = lambda n: jnp.zeros((D_MODEL, n), w.dtype)
    w_all = jnp.concatenate([
        w[:, off["fq"]:off["fq"] + 3 * W], w[:, off["mq"]:off["mq"] + 3 * W], w[:, off["dq"]:off["dq"] + 3 * W],
        w[:, off["cq"]:off["cq"] + MLA_Q_LORA], w[:, off["ckv"]:off["ckv"] + MLA_KV_LORA], w[:, kr_idx],
        w[:, off["ff"]:off["ff"] + N_HEADS], zeros(LANES - N_HEADS), zeros(COL_GATE - COL_FF - LANES),
        w[:, off["gate"]:off["gate"] + D_MODEL]], axis=1).astype(BF16)
    assert w_all.shape == (D_MODEL, PROJ_W)

    per_head = MLA_NOPE + MLA_ROPE
    heads = np.arange(N_HEADS)[:, None]
    uq_idx = np.concatenate([(heads * per_head + np.arange(MLA_NOPE)).ravel(),
                             (heads * per_head + MLA_NOPE + np.arange(MLA_HALF)).ravel(),
                             (heads * per_head + MLA_NOPE + MLA_HALF + np.arange(MLA_HALF)).ravel()])
    ukv_idx = np.concatenate([(heads * 2 * HEAD_DIM + np.arange(HEAD_DIM)).ravel(),
                              (heads * 2 * HEAD_DIM + HEAD_DIM + np.arange(HEAD_DIM)).ravel()])
    rope_lane = np.concatenate([np.tile(np.arange(MLA_HALF), N_HEADS),
                                np.tile(np.arange(MLA_HALF, MLA_ROPE), N_HEADS)])
    return {
        "ln_g": ln_g[i][None, :],
        "w_all": w_all,
        "gains": jnp.tile(qk_gain[i], (1, N_HEADS)),
        "b_forget": jnp.pad(b_forget[i], (0, LANES - N_HEADS))[None, :],
        "q_norm": mla_q_norm[i][None, :],
        "kv_norm": mla_kv_norm[i][None, :],
        "nope_gain": jnp.tile(mla_nope_gain[i], (1, N_HEADS)),
        "rope_gain": mla_rope_gain[i][:, rope_lane],
        "w_uq": w_uq[i][:, uq_idx].astype(BF16),
        "w_ukv": w_ukv[i][:, ukv_idx].astype(BF16),
        "w_out": w_out[i].astype(BF16),
        "ple_norm_g": ple_norm_g[i][None, :],
        "w_ple_gate": w_ple_gate[i].astype(BF16),
        "w_ple_proj": w_ple_proj[i].astype(BF16),
    }


def _constants(S):
    inv = 1.0 / (ROPE_THETA ** (jnp.arange(MLA_HALF, dtype=F32) * 2.0 / MLA_ROPE))
    ang = jnp.arange(S).astype(F32)[:, None] * inv[None, :]
    cos = jnp.tile(jnp.cos(ang), (1, 2 * N_HEADS))
    sin = jnp.tile(jnp.sin(ang), (1, 2 * N_HEADS))
    sign = np.concatenate([-np.ones(LANES // 2, np.float32), np.ones(LANES // 2, np.float32)])
    lane = np.arange(GROUP_W)
    g64 = (lane[:, None] // HEAD_DIM == lane[None, :] // HEAD_DIM).astype(np.float32) / HEAD_DIM
    rl = np.arange(LANES)
    rope_head = (rl % (LANES // 2)) // MLA_HALF
    g32 = (rope_head[:, None] == rope_head[None, :]).astype(np.float32) / MLA_ROPE
    tri = np.tril(np.ones((T_PREP, T_PREP), np.float32))
    return {"cos": cos, "sin": sin * sign[None, :], "g64": jnp.asarray(g64, BF16),
            "g32": jnp.asarray(g32, BF16), "tri": jnp.asarray(tri)}


def kernel(x, p, ln_g, w_in, b_forget, qk_gain, mla_q_norm, mla_kv_norm, mla_nope_gain, mla_rope_gain,
           w_uq, w_ukv, w_out, rel_bias, ple_norm_g, w_ple_gate, w_ple_proj):
    B, S, _ = x.shape
    depth = p.shape[0]
    consts = _constants(S)
    moba_tb, dil_tb = _bias_tables(rel_bias)
    x2 = x.reshape(B * S, D_MODEL)
    for i in range(depth):
        lp = _layer_params(i, ln_g, w_in, b_forget, qk_gain, mla_q_norm, mla_kv_norm, mla_nope_gain,
                           mla_rope_gain, w_uq, w_ukv, w_out, ple_norm_g, w_ple_gate, w_ple_proj)
        proj = _in_projection(x2, lp["ln_g"], lp["w_all"])
        (fqT, fk, fvT, fck, fcrow, mqT, mk, mvT, mkmean, lqT, lk, lvT,
         dq1, dk1, dv1, dq4, dk4, dv4, dq16, dk16, dv16) = _prep(proj.reshape(B, S, PROJ_W), consts, lp)
        S_ = S
        fo = _dense_mixer(
            _fox_kernel, "fox", fqT, fk, fvT, (fck, fcrow),
            (pl.BlockSpec((1, N_HEADS, S_, LANES), lambda b, i: (b, 0, 0, 0)),
             pl.BlockSpec((1, 8, TQ), lambda b, i: (b, 0, i))))
        mo = _dense_mixer(
            _moba_kernel, "moba", mqT, mk, mvT, (mkmean, moba_tb),
            (pl.BlockSpec((1, S_ // MOBA_BLOCK, GROUP_W), lambda b, i: (b, 0, 0)),
             pl.BlockSpec((MOBA_TB_ENTRIES, N_HEADS, MOBA_BLOCK, MOBA_BLOCK), lambda b, i: (0, 0, 0, 0))),
            (pltpu.VMEM((N_HEADS, S_ // MOBA_BLOCK, TQ), F32),))
        lo = _dense_mixer(_mla_kernel, "mla", lqT, lk, lvT)
        flat = lambda a: a.reshape(-1, a.shape[-2], GROUP_W)
        o1, l1 = _dilated_pattern(0, dq1, dk1, dv1, dil_tb)
        o4, l4 = _dilated_pattern(1, flat(dq4), flat(dk4), flat(dv4), dil_tb)
        o16, l16 = _dilated_pattern(2, flat(dq16), flat(dk16), flat(dv16), dil_tb)
        do = _dil_combine(o1, l1, o4.reshape(B, 4, S // 4, GROUP_W), l4.reshape(B, 4, S // 4, GROUP_W),
                          o16.reshape(B, 16, S // 16, GROUP_W), l16.reshape(B, 16, S // 16, GROUP_W))
        rs = lambda a: a.reshape(B * S, GROUP_W)
        x2 = _merge(x2, rs(fo), rs(mo), rs(do), rs(lo), proj, p[i].reshape(B * S, PLE_DIM), lp)
    return x2.reshape(B, S, D_MODEL)
```

```python
import functools
import math

import numpy as np
import jax
import jax.numpy as jnp
from jax import lax
from jax.experimental import pallas as pl
from jax.experimental.pallas import tpu as pltpu

F32 = jnp.float32
BF16 = jnp.bfloat16

D_MODEL = 1024
N_HEADS = 4
HEAD_DIM = 64
GROUP_W = N_HEADS * HEAD_DIM
MLA_NOPE = 64
MLA_ROPE = 32
MLA_HALF = MLA_ROPE // 2
MLA_QK_W = N_HEADS * (MLA_NOPE + MLA_ROPE)
MLA_Q_LORA = 256
MLA_KV_LORA = 128
PLE_DIM = 256
MOBA_BLOCK = 256
MOBA_TOPK = 3
DIL_PATTERNS = ((128, 1), (512, 4), (2048, 16))
DIL_BLOCK = 128
N_BUCKETS = 32
MAX_DISTANCE = 2048
ROPE_THETA = 10000.0
EPS = 1e-6
NEG = -1e30
LOG2E = math.log2(math.e)

LANES = 128
VMEM_LIMIT_BYTES = 56 * 1024 * 1024

COL_FOX, COL_MOBA, COL_DIL = 0, 768, 1536
COL_CQ, COL_CKV, COL_KR, COL_FF, COL_GATE = 2304, 2560, 2688, 2816, 3072
PROJ_W = 4096
PROJ_TN = 512

TM_PROJ = 256
T_PREP = 256
TQ = 256
TK = 256
MOBA_TB_ENTRIES = 8
TM_MERGE = 256
V_ROWS = HEAD_DIM + 16
MLA_GROUPS = 2


def _bucket_np(d):
    d = np.maximum(np.asarray(d, np.int64), 0)
    max_exact = N_BUCKETS // 2
    d_f = np.maximum(d, 1).astype(np.float64)
    val = np.log(d_f / max_exact) / math.log(MAX_DISTANCE / max_exact) * (N_BUCKETS - max_exact)
    frac = np.abs(val - np.round(val))
    on_edge = (frac < 1e-5) & (d > max_exact) & (val < N_BUCKETS - max_exact - 0.5)
    assert not on_edge.any(), "distance too close to a bucket edge for a static table"
    large = np.minimum(max_exact + np.floor(val + 1e-9).astype(np.int64), N_BUCKETS - 1)
    return np.where(d < max_exact, d, large)


def _bucket_steps(lo, hi):
    ds = np.arange(lo, hi + 1)
    bs = _bucket_np(ds)
    steps = [(int(ds[i]), int(bs[i])) for i in range(1, len(ds)) if bs[i] != bs[i - 1]]
    return int(bs[0]), steps


def _compiler_params(sem):
    return pltpu.CompilerParams(dimension_semantics=sem, vmem_limit_bytes=VMEM_LIMIT_BYTES)


def _bias_from_steps(d, rb_ref, col, lo, hi):
    b0, steps = _bucket_steps(lo, hi)
    val = jnp.full(d.shape, rb_ref[b0, col], F32)
    for t, b in steps:
        val = jnp.where(d >= t, rb_ref[b, col], val)
    return val


def _bias_tables_kernel(rb_ref, moba_ref, dil_ref):
    kl = lax.broadcasted_iota(jnp.int32, (MOBA_BLOCK, MOBA_BLOCK), 0)
    ql = lax.broadcasted_iota(jnp.int32, (MOBA_BLOCK, MOBA_BLOCK), 1)
    for e in range(MOBA_TB_ENTRIES - 1):
        d = jnp.maximum(e * MOBA_BLOCK + ql - kl, 0)
        lo = max(e * MOBA_BLOCK - (MOBA_BLOCK - 1), 0)
        hi = e * MOBA_BLOCK + (MOBA_BLOCK - 1)
        for h in range(N_HEADS):
            moba_ref[e, h] = _bias_from_steps(d, rb_ref, h, lo, hi) * LOG2E
    for h in range(N_HEADS):
        moba_ref[MOBA_TB_ENTRIES - 1, h] = jnp.full((MOBA_BLOCK, MOBA_BLOCK), rb_ref[N_BUCKETS - 1, h], F32) * LOG2E
    qi = lax.broadcasted_iota(jnp.int32, (DIL_BLOCK, 2 * DIL_BLOCK), 0)
    kj = lax.broadcasted_iota(jnp.int32, (DIL_BLOCK, 2 * DIL_BLOCK), 1)
    for p, (_, dil) in enumerate(DIL_PATTERNS):
        d = jnp.maximum((qi + DIL_BLOCK - kj) * dil, 0)
        for h in range(N_HEADS):
            dil_ref[p, h] = _bias_from_steps(d, rb_ref, N_HEADS + h, 0, (2 * DIL_BLOCK - 1) * dil)


def _bias_tables(rel_bias):
    assert _bucket_np((MOBA_TB_ENTRIES - 1) * MOBA_BLOCK - (MOBA_BLOCK - 1)) == N_BUCKETS - 1
    return pl.pallas_call(
        _bias_tables_kernel,
        out_shape=(jax.ShapeDtypeStruct((MOBA_TB_ENTRIES, N_HEADS, MOBA_BLOCK, MOBA_BLOCK), F32),
                   jax.ShapeDtypeStruct((len(DIL_PATTERNS), N_HEADS, DIL_BLOCK, 2 * DIL_BLOCK), F32)),
        in_specs=[pl.BlockSpec(memory_space=pltpu.SMEM)],
        out_specs=(pl.BlockSpec(memory_space=pltpu.VMEM), pl.BlockSpec(memory_space=pltpu.VMEM)),
        compiler_params=pltpu.CompilerParams(vmem_limit_bytes=VMEM_LIMIT_BYTES),
        name="bias_tables",
    )(rel_bias)


def _proj_kernel(x_ref, g_ref, w_ref, o_ref):
    x = x_ref[...]
    ms = jnp.mean(x * x, axis=-1, keepdims=True)
    h = (x * lax.rsqrt(ms + EPS) * g_ref[...]).astype(BF16)
    for c in range(PROJ_W // PROJ_TN):
        sl = slice(c * PROJ_TN, (c + 1) * PROJ_TN)
        o_ref[:, sl] = jnp.dot(h, w_ref[:, sl], preferred_element_type=F32)


def _in_projection(x2, ln_g, w_all):
    rows = x2.shape[0]
    return pl.pallas_call(
        _proj_kernel,
        out_shape=jax.ShapeDtypeStruct((rows, PROJ_W), F32),
        grid=(rows // TM_PROJ,),
        in_specs=[pl.BlockSpec((TM_PROJ, D_MODEL), lambda i: (i, 0)),
                  pl.BlockSpec((1, D_MODEL), lambda i: (0, 0)),
                  pl.BlockSpec((D_MODEL, PROJ_W), lambda i: (0, 0))],
        out_specs=pl.BlockSpec((TM_PROJ, PROJ_W), lambda i: (i, 0)),
        compiler_params=_compiler_params(("arbitrary",)),
        name="in_projection",
    )(x2, ln_g, w_all)


def _group_mean_sq(x, g_mat):
    xx = x * x
    hi = xx.astype(BF16)
    lo = (xx - hi.astype(F32)).astype(BF16)
    return (jnp.dot(hi, g_mat, preferred_element_type=F32)
            + jnp.dot(lo, g_mat, preferred_element_type=F32))


def _group_norm(x, gain, g_mat):
    return x * lax.rsqrt(_group_mean_sq(x, g_mat) + EPS) * gain


def _row_norm(x, gain):
    return x * lax.rsqrt(jnp.mean(x * x, axis=-1, keepdims=True) + EPS) * gain


def _log_sigmoid(x):
    return -(jnp.maximum(-x, 0.0) + jnp.log1p(jnp.exp(-jnp.abs(x))))


def _prep_kernel(pf_ref, pm_ref, pd_ref, pcq_ref, pckv_ref, pkr_ref, pff_ref,
                 gains_ref, bfor_ref, qng_ref, kvng_ref, nopeg_ref, ropeg_ref,
                 wuq_ref, wukv_ref, cos_ref, sin_ref, g64_ref, g32_ref, tri_ref,
                 fqT_ref, fk_ref, fvT_ref, fck_ref, fcrow_ref,
                 mqT_ref, mk_ref, mvT_ref, mkmean_ref,
                 lqT_ref, lk_ref, lvT_ref,
                 dq1_ref, dk1_ref, dv1_ref, dq4_ref, dk4_ref, dv4_ref, dq16_ref, dk16_ref, dv16_ref,
                 carry_ref, sq_ref, sk_ref, sv_ref):
    t = pl.program_id(1)
    g64 = g64_ref[...]
    scale = HEAD_DIM ** -0.5
    W = GROUP_W
    ones_rows = jnp.ones((V_ROWS - HEAD_DIM, T_PREP), BF16)

    def store_vT(dst, v):
        vT = v.T.astype(BF16)
        for h in range(N_HEADS):
            dst[0, 0, h] = jnp.concatenate([vT[HEAD_DIM * h:HEAD_DIM * (h + 1)], ones_rows], axis=0)

    pf = pf_ref[0]
    fq = _group_norm(pf[:, :W], gains_ref[0:1, :], g64) * (scale * LOG2E)
    fk = _group_norm(pf[:, W:2 * W], gains_ref[1:2, :], g64)
    fqT_ref[0, 0] = fq.T.astype(BF16)
    fk_ref[0, 0] = fk.astype(BF16)
    store_vT(fvT_ref, pf[:, 2 * W:])

    @pl.when(t == 0)
    def _():
        carry_ref[...] = jnp.zeros_like(carry_ref)

    log_f = _log_sigmoid(pff_ref[0] + bfor_ref[...])
    c = jnp.dot(tri_ref[...], log_f, preferred_element_type=F32,
                precision=lax.Precision.HIGHEST) + carry_ref[...]
    carry_ref[...] = c[T_PREP - 1:T_PREP, :]
    c2 = c * LOG2E
    for h in range(N_HEADS):
        fck_ref[0, h] = jnp.broadcast_to(c2[:, h:h + 1], (T_PREP, LANES))
    fcrow_ref[0] = c2.T[0:8, :]

    pm = pm_ref[0]
    mq = _group_norm(pm[:, :W], gains_ref[2:3, :], g64) * (scale * LOG2E)
    mk = _group_norm(pm[:, W:2 * W], gains_ref[3:4, :], g64)
    mqT_ref[0, 0] = mq.T.astype(BF16)
    mk_ref[0, 0] = mk.astype(BF16)
    store_vT(mvT_ref, pm[:, 2 * W:])
    mkmean_ref[0, pl.ds(t, 1), :] = jnp.mean(mk, axis=0, keepdims=True)

    pd = pd_ref[0]
    dq = _group_norm(pd[:, :W], gains_ref[4:5, :], g64) * scale
    dk = _group_norm(pd[:, W:2 * W], gains_ref[5:6, :], g64)
    dv = pd[:, 2 * W:]
    dq1_ref[0] = dq.astype(BF16)
    dk1_ref[0] = dk.astype(BF16)
    dv1_ref[0] = dv.astype(BF16)
    for val, dst in ((dq, sq_ref), (dk, sk_ref), (dv, sv_ref)):
        for half in range(GROUP_W // LANES):
            dst[half] = val[:, half * LANES:(half + 1) * LANES]
    for dil, outs in ((4, (dq4_ref, dk4_ref, dv4_ref)), (16, (dq16_ref, dk16_ref, dv16_ref))):
        n = T_PREP // dil
        for r in range(dil):
            for src, dst in zip((sq_ref, sk_ref, sv_ref), outs):
                dst[0, r] = jnp.concatenate(
                    [src[half, pl.ds(r, n, stride=dil), :] for half in range(GROUP_W // LANES)],
                    axis=1).astype(BF16)

    cos = cos_ref[...]
    sin = sin_ref[...]
    qscale = (MLA_NOPE + MLA_ROPE) ** -0.5
    cqn = _row_norm(pcq_ref[0], qng_ref[...]).astype(BF16)
    rotate = lambda x: x * cos + pltpu.roll(x, LANES // 2, 1) * sin
    qf = jnp.dot(cqn, wuq_ref[...], preferred_element_type=F32)
    q_nope = _group_norm(qf[:, :W], nopeg_ref[0:1, :], g64) * (qscale * LOG2E)
    ckvn = _row_norm(pckv_ref[0], kvng_ref[...]).astype(BF16)
    kvf = jnp.dot(ckvn, wukv_ref[...], preferred_element_type=F32)
    k_nope = _group_norm(kvf[:, :W], nopeg_ref[1:2, :], g64)
    krs = pkr_ref[0]
    kr_ms = jnp.sum(krs * krs, axis=-1, keepdims=True) * (1.0 / (2 * MLA_ROPE))
    kr = rotate(krs * lax.rsqrt(kr_ms + EPS) * ropeg_ref[1:2, :])
    for g in range(MLA_GROUPS):
        slab = qf[:, W + g * LANES:W + (g + 1) * LANES]
        qr = rotate(_group_norm(slab, ropeg_ref[0:1, :], g32_ref[...])) * (qscale * LOG2E)
        lq = jnp.concatenate([q_nope[:, g * LANES:(g + 1) * LANES], qr], axis=1)
        lqT_ref[0, g] = lq.T.astype(BF16)
        lk_ref[0, g] = jnp.concatenate([k_nope[:, g * LANES:(g + 1) * LANES], kr], axis=1).astype(BF16)
    store_vT(lvT_ref, kvf[:, W:])


def _prep(proj3, consts, lp):
    B, S, _ = proj3.shape
    T = T_PREP
    NT = S // T
    assert S % T == 0 and S // MOBA_BLOCK == NT

    def pspec(width, col):
        return pl.BlockSpec((1, T, width), lambda b, t, c=col // width: (b, t, c))

    def cspec(shape):
        return pl.BlockSpec(shape, lambda b, t: (0,) * len(shape))

    in_specs = [pspec(768, COL_FOX), pspec(768, COL_MOBA), pspec(768, COL_DIL),
                pspec(256, COL_CQ), pspec(128, COL_CKV), pspec(128, COL_KR), pspec(128, COL_FF),
                cspec((6, GROUP_W)), cspec((1, LANES)), cspec((1, MLA_Q_LORA)), cspec((1, MLA_KV_LORA)),
                cspec((2, GROUP_W)), cspec((2, LANES)),
                cspec((MLA_Q_LORA, MLA_GROUPS * GROUP_W)), cspec((MLA_KV_LORA, 2 * GROUP_W)),
                pl.BlockSpec((T, LANES), lambda b, t: (t, 0)), pl.BlockSpec((T, LANES), lambda b, t: (t, 0)),
                cspec((GROUP_W, GROUP_W)), cspec((LANES, LANES)), cspec((T, T))]

    def qT(groups):
        return (jax.ShapeDtypeStruct((B, groups, GROUP_W, S), BF16),
                pl.BlockSpec((1, groups, GROUP_W, T), lambda b, t: (b, 0, 0, t)))

    def keys(groups):
        return (jax.ShapeDtypeStruct((B, groups, S, GROUP_W), BF16),
                pl.BlockSpec((1, groups, T, GROUP_W), lambda b, t: (b, 0, t, 0)))

    def rows(w):
        return jax.ShapeDtypeStruct((B, S, w), BF16), pl.BlockSpec((1, T, w), lambda b, t: (b, t, 0))

    def vT():
        return (jax.ShapeDtypeStruct((B, NT, N_HEADS, V_ROWS, T), BF16),
                pl.BlockSpec((1, 1, N_HEADS, V_ROWS, T), lambda b, t: (b, t, 0, 0, 0)))

    def resid(dil):
        return (jax.ShapeDtypeStruct((B, dil, S // dil, GROUP_W), BF16),
                pl.BlockSpec((1, dil, T // dil, GROUP_W), lambda b, t: (b, 0, t, 0)))

    outs = [qT(1), keys(1), vT(),
            (jax.ShapeDtypeStruct((B, N_HEADS, S, LANES), F32),
             pl.BlockSpec((1, N_HEADS, T, LANES), lambda b, t: (b, 0, t, 0))),
            (jax.ShapeDtypeStruct((B, 8, S), F32), pl.BlockSpec((1, 8, T), lambda b, t: (b, 0, t))),
            qT(1), keys(1), vT(),
            (jax.ShapeDtypeStruct((B, NT, GROUP_W), F32), pl.BlockSpec((1, NT, GROUP_W), lambda b, t: (b, 0, 0))),
            qT(MLA_GROUPS), keys(MLA_GROUPS), vT(),
            rows(GROUP_W), rows(GROUP_W), rows(GROUP_W),
            resid(4), resid(4), resid(4), resid(16), resid(16), resid(16)]
    return pl.pallas_call(
        _prep_kernel,
        out_shape=tuple(o[0] for o in outs),
        grid=(B, NT),
        in_specs=in_specs,
        out_specs=tuple(o[1] for o in outs),
        scratch_shapes=[pltpu.VMEM((1, LANES), F32)] + [pltpu.VMEM((GROUP_W // LANES, T, LANES), F32)] * 3,
        compiler_params=_compiler_params(("arbitrary", "arbitrary")),
        name="prep",
    )(proj3, proj3, proj3, proj3, proj3, proj3, proj3,
      lp["gains"], lp["b_forget"], lp["q_norm"], lp["kv_norm"], lp["nope_gain"], lp["rope_gain"],
      lp["w_uq"], lp["w_ukv"], consts["cos"], consts["sin"], consts["g64"], consts["g32"], consts["tri"])


def _stage_head_queries(qT_ref, qm_ref, groups):
    r = lax.broadcasted_iota(jnp.int32, (GROUP_W, TQ), 0)
    for h in range(N_HEADS):
        if groups == 1:
            g, mask = 0, (r >= HEAD_DIM * h) & (r < HEAD_DIM * (h + 1))
        else:
            g, hl = divmod(h, N_HEADS // groups)
            mask = (r >= HEAD_DIM * hl) & (r < HEAD_DIM * (hl + 1))
            for base in (2 * HEAD_DIM, 2 * HEAD_DIM + LANES // 2):
                lo = base + MLA_HALF * hl
                mask = mask | ((r >= lo) & (r < lo + MLA_HALF))
        q = qT_ref[0, g]
        qm_ref[h] = jnp.where(mask, q, jnp.zeros_like(q))


def _flash_sweep(i, groups, qm_ref, k_ref, vT_ref, m_ref, acc_ref, u_ref, o_ref, score_fn):
    kpos = lax.broadcasted_iota(jnp.int32, (TK, TQ), 0)
    qpos = lax.broadcasted_iota(jnp.int32, (TK, TQ), 1)
    causal = kpos <= qpos
    m_ref[...] = jnp.full(m_ref.shape, NEG, F32)
    acc_ref[...] = jnp.zeros(acc_ref.shape, F32)

    def tile(j, diagonal):
        row0 = pl.multiple_of(j * TK, TK)
        shifts, alphas = [], []
        for h in range(N_HEADS):
            kt = k_ref[0, h // (N_HEADS // groups), pl.ds(row0, TK), :]
            s = jnp.dot(kt, qm_ref[h], preferred_element_type=F32)
            u, row = score_fn(h, s, j, diagonal)
            if diagonal:
                u = jnp.where(causal, u, NEG)
            u_ref[h] = u
            m_old = m_ref[h]
            top = jnp.max(u, axis=0, keepdims=True)
            if row is None:
                m_new = jnp.maximum(m_old, top)
                shifts.append(m_new)
            else:
                m_new = jnp.maximum(m_old, top + row)
                shifts.append(m_new - row)
            alphas.append(jnp.exp2(m_old - m_new))
            m_ref[h] = m_new
        for h in range(N_HEADS):
            p = jnp.exp2(u_ref[h] - shifts[h]).astype(BF16)
            acc_ref[h] = alphas[h] * acc_ref[h] + jnp.dot(vT_ref[0, j, h], p, preferred_element_type=F32)

    tile(i, True)

    def body(j, carry):
        tile(j, False)
        return carry

    lax.fori_loop(0, i, body, 0)
    outs = []
    for h in range(N_HEADS):
        acc = acc_ref[h]
        outs.append(acc[:HEAD_DIM] / acc[HEAD_DIM:HEAD_DIM + 1])
    o_ref[0] = jnp.concatenate(outs, axis=0).T


def _fox_kernel(qT_ref, k_ref, vT_ref, ck_ref, crow_ref, o_ref, qm_ref, m_ref, acc_ref, u_ref):
    i = pl.program_id(1)
    _stage_head_queries(qT_ref, qm_ref, 1)

    def score_fn(h, s, j, diagonal):
        ck = ck_ref[0, h, pl.ds(pl.multiple_of(j * TK, TK), TK), :]
        return s - jnp.concatenate([ck] * (TQ // LANES), axis=1), crow_ref[0, h:h + 1, :]

    _flash_sweep(i, 1, qm_ref, k_ref, vT_ref, m_ref, acc_ref, u_ref, o_ref, score_fn)


def _mla_kernel(qT_ref, k_ref, vT_ref, o_ref, qm_ref, m_ref, acc_ref, u_ref):
    i = pl.program_id(1)
    _stage_head_queries(qT_ref, qm_ref, MLA_GROUPS)
    _flash_sweep(i, MLA_GROUPS, qm_ref, k_ref, vT_ref, m_ref, acc_ref, u_ref, o_ref,
                 lambda h, s, j, diagonal: (s, None))


def _moba_kernel(qT_ref, k_ref, vT_ref, kmean_ref, tb_ref, o_ref, qm_ref, m_ref, acc_ref, u_ref, sel_ref):
    i = pl.program_id(1)
    _stage_head_queries(qT_ref, qm_ref, 1)
    kmean = kmean_ref[0].astype(BF16)
    nb = kmean.shape[0]
    blk = lax.broadcasted_iota(jnp.int32, (nb, TQ), 0)
    for h in range(N_HEADS):
        gate = jnp.dot(kmean, qm_ref[h], preferred_element_type=F32)
        gate = jnp.where(blk < i, gate, NEG)
        rank = jnp.zeros((nb, TQ), jnp.int32)
        for jp in range(nb):
            row = gate[jp:jp + 1, :]
            ahead = (row > gate) | ((row == gate) & (jp < blk))
            rank = rank + ahead.astype(jnp.int32)
        sel_ref[h] = ((rank < MOBA_TOPK) & (blk < i)).astype(F32)

    def score_fn(h, s, j, diagonal):
        u = s + tb_ref[jnp.minimum(i - j, MOBA_TB_ENTRIES - 1), h]
        if not diagonal:
            u = jnp.where(sel_ref[h, pl.ds(j, 1), :] > 0.5, u, NEG)
        return u, None

    _flash_sweep(i, 1, qm_ref, k_ref, vT_ref, m_ref, acc_ref, u_ref, o_ref, score_fn)


def _dense_mixer(kernel_fn, name, qT, k, vT, extra_args=(), extra_specs=(), extra_scratch=()):
    B, groups, _, S = qT.shape
    NT = S // TK
    assert TQ == TK and S % TQ == 0
    in_specs = [pl.BlockSpec((1, groups, GROUP_W, TQ), lambda b, i: (b, 0, 0, i)),
                pl.BlockSpec((1, groups, S, GROUP_W), lambda b, i: (b, 0, 0, 0)),
                pl.BlockSpec((1, NT, N_HEADS, V_ROWS, TK), lambda b, i: (b, 0, 0, 0, 0))] + list(extra_specs)
    scratch = [pltpu.VMEM((N_HEADS, GROUP_W, TQ), BF16), pltpu.VMEM((N_HEADS, 1, TQ), F32),
               pltpu.VMEM((N_HEADS, V_ROWS, TQ), F32), pltpu.VMEM((N_HEADS, TK, TQ), F32)] + list(extra_scratch)
    return pl.pallas_call(
        kernel_fn,
        out_shape=jax.ShapeDtypeStruct((B, S, GROUP_W), F32),
        grid=(B, S // TQ),
        in_specs=in_specs,
        out_specs=pl.BlockSpec((1, TQ, GROUP_W), lambda b, i: (b, i, 0)),
        scratch_shapes=scratch,
        compiler_params=_compiler_params(("arbitrary", "arbitrary")),
        name=name,
    )(qT, k, vT, *extra_args)


def _dil_kernel(q_ref, kp_ref, kc_ref, vp_ref, vc_ref, tb_ref, o_ref, lse_ref):
    n = pl.program_id(1)
    q = q_ref[0]
    kb = jnp.concatenate([kp_ref[0], kc_ref[0]], axis=0)
    vb = jnp.concatenate([vp_ref[0], vc_ref[0]], axis=0)
    qi = lax.broadcasted_iota(jnp.int32, (DIL_BLOCK, 2 * DIL_BLOCK), 0)
    kj = lax.broadcasted_iota(jnp.int32, (DIL_BLOCK, 2 * DIL_BLOCK), 1)
    rel = qi + DIL_BLOCK - kj
    valid = (rel >= 0) & (rel <= DIL_BLOCK) & ((kj >= DIL_BLOCK) | (n > 0))
    qlane = lax.broadcasted_iota(jnp.int32, (DIL_BLOCK, GROUP_W), 1)
    vlane = lax.broadcasted_iota(jnp.int32, (2 * DIL_BLOCK, GROUP_W), 1)
    o_acc = jnp.zeros((DIL_BLOCK, GROUP_W), F32)
    lse_map = jnp.zeros((DIL_BLOCK, GROUP_W), F32)
    for h in range(N_HEADS):
        in_head = (qlane >= HEAD_DIM * h) & (qlane < HEAD_DIM * (h + 1))
        qm = jnp.where(in_head, q, jnp.zeros_like(q))
        s = lax.dot_general(qm, kb, (((1,), (1,)), ((), ())), preferred_element_type=F32)
        s = jnp.where(valid, s + tb_ref[0, h], NEG)
        m = jnp.max(s, axis=1, keepdims=True)
        lse = m + jnp.log(jnp.sum(jnp.exp(s - m), axis=1, keepdims=True))
        pr = jnp.exp(s - lse).astype(BF16)
        v_head = (vlane >= HEAD_DIM * h) & (vlane < HEAD_DIM * (h + 1))
        o_acc = o_acc + jnp.dot(pr, jnp.where(v_head, vb, jnp.zeros_like(vb)), preferred_element_type=F32)
        lse_map = jnp.where(in_head, lse, lse_map)
    o_ref[0] = o_acc
    lse_ref[0] = lse_map


def _dilated_pattern(p, q, k, v, dil_tb):
    G, L, _ = q.shape
    assert L % DIL_BLOCK == 0
    nb = L // DIL_BLOCK
    cur = pl.BlockSpec((1, DIL_BLOCK, GROUP_W), lambda g, n: (g, n, 0))
    prev = pl.BlockSpec((1, DIL_BLOCK, GROUP_W), lambda g, n: (g, jnp.maximum(n - 1, 0), 0))
    out = jax.ShapeDtypeStruct((G, L, GROUP_W), F32)
    return pl.pallas_call(
        _dil_kernel,
        out_shape=(out, out),
        grid=(G, nb),
        in_specs=[cur, prev, cur, prev, cur,
                  pl.BlockSpec((1, N_HEADS, DIL_BLOCK, 2 * DIL_BLOCK), lambda g, n, p=p: (p, 0, 0, 0))],
        out_specs=(cur, cur),
        compiler_params=_compiler_params(("arbitrary", "arbitrary")),
        name=f"dilated_{DIL_PATTERNS[p][1]}",
    )(q, k, k, v, v, dil_tb)


def _dil_combine_kernel(o1_ref, l1_ref, o4_ref, l4_ref, o16_ref, l16_ref, out_ref, so4, sl4, so16, sl16):
    T = out_ref.shape[1]
    halves = GROUP_W // LANES
    for dil, srcs in ((4, ((o4_ref, so4), (l4_ref, sl4))), (16, ((o16_ref, so16), (l16_ref, sl16)))):
        for src, dst in srcs:
            for r in range(dil):
                for half in range(halves):
                    dst[half, pl.ds(r, T // dil, stride=dil), :] = src[0, r, :, half * LANES:(half + 1) * LANES]
    nat = lambda ref: jnp.concatenate([ref[half] for half in range(halves)], axis=1)
    l1, l4, l16 = l1_ref[0], nat(sl4), nat(sl16)
    m = jnp.maximum(jnp.maximum(l1, l4), l16)
    w1, w4, w16 = jnp.exp(l1 - m), jnp.exp(l4 - m), jnp.exp(l16 - m)
    tot = w1 + w4 + w16
    out_ref[0] = (w1 / tot) * o1_ref[0] + (w4 / tot) * nat(so4) + (w16 / tot) * nat(so16)


def _dil_combine(o1, l1, o4, l4, o16, l16):
    B, S, _ = o1.shape
    T = 256
    nat = pl.BlockSpec((1, T, GROUP_W), lambda b, t: (b, t, 0))
    r4 = pl.BlockSpec((1, 4, T // 4, GROUP_W), lambda b, t: (b, 0, t, 0))
    r16 = pl.BlockSpec((1, 16, T // 16, GROUP_W), lambda b, t: (b, 0, t, 0))
    return pl.pallas_call(
        _dil_combine_kernel,
        out_shape=jax.ShapeDtypeStruct((B, S, GROUP_W), F32),
        grid=(B, S // T),
        in_specs=[nat, nat, r4, r4, r16, r16],
        out_specs=nat,
        scratch_shapes=[pltpu.VMEM((GROUP_W // LANES, T, LANES), F32)] * 4,
        compiler_params=_compiler_params(("arbitrary", "arbitrary")),
        name="dil_combine",
    )(o1, l1, o4, l4, o16, l16)


def _sigmoid(x):
    return 1.0 / (1.0 + jnp.exp(-x))


def _merge_kernel(x_ref, fo_ref, mo_ref, do_ref, lo_ref, gate_ref, p_ref,
                  wout_ref, png_ref, wpg_ref, wpp_ref, o_ref):
    mix = jnp.concatenate([fo_ref[...], mo_ref[...], do_ref[...], lo_ref[...]], axis=1)
    g = gate_ref[...]
    y = jnp.dot((mix * (g * _sigmoid(g))).astype(BF16), wout_ref[...], preferred_element_type=F32)
    x1 = x_ref[...] + y
    hn = _row_norm(x1, png_ref[...]).astype(BF16)
    pg = _sigmoid(jnp.dot(hn, wpg_ref[...], preferred_element_type=F32))
    pp = jnp.dot(p_ref[...].astype(BF16), wpp_ref[...], preferred_element_type=F32)
    o_ref[...] = x1 + pg * pp


def _merge(x2, fo, mo, do, lo, proj, p2, lp):
    rows = x2.shape[0]
    TM = TM_MERGE

    def rspec(w, c=0):
        return pl.BlockSpec((TM, w), lambda i, c=c: (i, c))

    def cspec(shape):
        return pl.BlockSpec(shape, lambda i: (0, 0))

    return pl.pallas_call(
        _merge_kernel,
        out_shape=jax.ShapeDtypeStruct((rows, D_MODEL), F32),
        grid=(rows // TM,),
        in_specs=[rspec(D_MODEL), rspec(GROUP_W), rspec(GROUP_W), rspec(GROUP_W), rspec(GROUP_W),
                  rspec(D_MODEL, COL_GATE // D_MODEL), rspec(PLE_DIM),
                  cspec((D_MODEL, D_MODEL)), cspec((1, D_MODEL)), cspec((D_MODEL, D_MODEL)),
                  cspec((PLE_DIM, D_MODEL))],
        out_specs=rspec(D_MODEL),
        compiler_params=_compiler_params(("arbitrary",)),
        name="merge",
    )(x2, fo, mo, do, lo, proj, p2, lp["w_out"], lp["ple_norm_g"], lp["w_ple_gate"], lp["w_ple_proj"])


def _layer_params(i, ln_g, w_in, b_forget, qk_gain, mla_q_norm, mla_kv_norm, mla_nope_gain,
                  mla_rope_gain, w_uq, w_ukv, w_out, ple_norm_g, w_ple_gate, w_ple_proj):
    W = GROUP_W
    w = w_in[i]
    off = {}
    acc = 0
    for name, sz in (("fq", W), ("fk", W), ("fv", W), ("ff", N_HEADS), ("mq", W), ("mk", W), ("mv", W),
                     ("dq", W), ("dk", W), ("dv", W), ("cq", MLA_Q_LORA), ("ckv", MLA_KV_LORA),
                     ("kr", MLA_ROPE), ("gate", D_MODEL)):
        off[name] = acc
        acc += sz
    def rope_slab(x1_parts, x2_parts, like):
        pad = jnp.zeros(like.shape[:-1] + (LANES // 2 - 2 * MLA_HALF,), like.dtype)
        return jnp.concatenate(list(x1_parts) + [pad] + list(x2_parts) + [pad], axis=-1)

    zeros = lambda n: jnp.zeros((D_MODEL, n), w.dtype)
    kr_x1 = w[:, off["kr"]:off["kr"] + MLA_HALF]
    kr_x2 = w[:, off["kr"] + MLA_HALF:off["kr"] + MLA_ROPE]
    w_all = jnp.concatenate([
        w[:, off["fq"]:off["fq"] + 3 * W], w[:, off["mq"]:off["mq"] + 3 * W], w[:, off["dq"]:off["dq"] + 3 * W],
        w[:, off["cq"]:off["cq"] + MLA_Q_LORA], w[:, off["ckv"]:off["ckv"] + MLA_KV_LORA],
        rope_slab([kr_x1, kr_x1], [kr_x2, kr_x2], w),
        w[:, off["ff"]:off["ff"] + N_HEADS], zeros(LANES - N_HEADS), zeros(COL_GATE - COL_FF - LANES),
        w[:, off["gate"]:off["gate"] + D_MODEL]], axis=1).astype(BF16)
    assert w_all.shape == (D_MODEL, PROJ_W)

    per_head = MLA_NOPE + MLA_ROPE
    uq = w_uq[i]
    uq_nope = [uq[:, h * per_head:h * per_head + MLA_NOPE] for h in range(N_HEADS)]
    uq_x1 = [uq[:, h * per_head + MLA_NOPE:h * per_head + MLA_NOPE + MLA_HALF] for h in range(N_HEADS)]
    uq_x2 = [uq[:, h * per_head + MLA_NOPE + MLA_HALF:(h + 1) * per_head] for h in range(N_HEADS)]
    hpg = N_HEADS // MLA_GROUPS
    uq_cols = uq_nope + [rope_slab(uq_x1[g * hpg:(g + 1) * hpg], uq_x2[g * hpg:(g + 1) * hpg], uq)
                         for g in range(MLA_GROUPS)]
    ukv = w_ukv[i]
    ukv_cols = ([ukv[:, h * 2 * HEAD_DIM:h * 2 * HEAD_DIM + HEAD_DIM] for h in range(N_HEADS)]
                + [ukv[:, h * 2 * HEAD_DIM + HEAD_DIM:(h + 1) * 2 * HEAD_DIM] for h in range(N_HEADS)])
    rg = mla_rope_gain[i]
    rope_gain = rope_slab([rg[:, :MLA_HALF]] * hpg, [rg[:, MLA_HALF:]] * hpg, rg)
    return {
        "ln_g": ln_g[i][None, :],
        "w_all": w_all,
        "gains": jnp.tile(qk_gain[i], (1, N_HEADS)),
        "b_forget": jnp.pad(b_forget[i], (0, LANES - N_HEADS))[None, :],
        "q_norm": mla_q_norm[i][None, :],
        "kv_norm": mla_kv_norm[i][None, :],
        "nope_gain": jnp.tile(mla_nope_gain[i], (1, N_HEADS)),
        "rope_gain": rope_gain,
        "w_uq": jnp.concatenate(uq_cols, axis=1).astype(BF16),
        "w_ukv": jnp.concatenate(ukv_cols, axis=1).astype(BF16),
        "w_out": w_out[i].astype(BF16),
        "ple_norm_g": ple_norm_g[i][None, :],
        "w_ple_gate": w_ple_gate[i].astype(BF16),
        "w_ple_proj": w_ple_proj[i].astype(BF16),
    }


def _constants(S):
    inv = 1.0 / (ROPE_THETA ** (jnp.arange(MLA_HALF, dtype=F32) * 2.0 / MLA_ROPE))
    ang = jnp.arange(S).astype(F32)[:, None] * inv[None, :]
    cos = jnp.tile(jnp.cos(ang), (1, 2 * N_HEADS))
    sin = jnp.tile(jnp.sin(ang), (1, 2 * N_HEADS))
    sign = np.concatenate([-np.ones(LANES // 2, np.float32), np.ones(LANES // 2, np.float32)])
    lane = np.arange(GROUP_W)
    g64 = (lane[:, None] // HEAD_DIM == lane[None, :] // HEAD_DIM).astype(np.float32) / HEAD_DIM
    within = np.arange(LANES) % (LANES // 2)
    rope_head = np.where(within < 2 * MLA_HALF, within // MLA_HALF, -1)
    g32 = ((rope_head[:, None] == rope_head[None, :]) & (rope_head[:, None] >= 0)).astype(np.float32) / MLA_ROPE
    tri = np.tril(np.ones((T_PREP, T_PREP), np.float32))
    return {"cos": cos, "sin": sin * sign[None, :], "g64": jnp.asarray(g64, BF16),
            "g32": jnp.asarray(g32, BF16), "tri": jnp.asarray(tri)}


def kernel(x, p, ln_g, w_in, b_forget, qk_gain, mla_q_norm, mla_kv_norm, mla_nope_gain, mla_rope_gain,
           w_uq, w_ukv, w_out, rel_bias, ple_norm_g, w_ple_gate, w_ple_proj):
    B, S, _ = x.shape
    depth = p.shape[0]
    consts = _constants(S)
    moba_tb, dil_tb = _bias_tables(rel_bias)
    x2 = x.reshape(B * S, D_MODEL)
    for i in range(depth):
        lp = _layer_params(i, ln_g, w_in, b_forget, qk_gain, mla_q_norm, mla_kv_norm, mla_nope_gain,
                           mla_rope_gain, w_uq, w_ukv, w_out, ple_norm_g, w_ple_gate, w_ple_proj)
        proj = _in_projection(x2, lp["ln_g"], lp["w_all"])
        (fqT, fk, fvT, fck, fcrow, mqT, mk, mvT, mkmean, lqT, lk, lvT,
         dq1, dk1, dv1, dq4, dk4, dv4, dq16, dk16, dv16) = _prep(proj.reshape(B, S, PROJ_W), consts, lp)
        S_ = S
        fo = _dense_mixer(
            _fox_kernel, "fox", fqT, fk, fvT, (fck, fcrow),
            (pl.BlockSpec((1, N_HEADS, S_, LANES), lambda b, i: (b, 0, 0, 0)),
             pl.BlockSpec((1, 8, TQ), lambda b, i: (b, 0, i))))
        mo = _dense_mixer(
            _moba_kernel, "moba", mqT, mk, mvT, (mkmean, moba_tb),
            (pl.BlockSpec((1, S_ // MOBA_BLOCK, GROUP_W), lambda b, i: (b, 0, 0)),
             pl.BlockSpec((MOBA_TB_ENTRIES, N_HEADS, MOBA_BLOCK, MOBA_BLOCK), lambda b, i: (0, 0, 0, 0))),
            (pltpu.VMEM((N_HEADS, S_ // MOBA_BLOCK, TQ), F32),))
        lo = _dense_mixer(_mla_kernel, "mla", lqT, lk, lvT)
        flat = lambda a: a.reshape(-1, a.shape[-2], GROUP_W)
        o1, l1 = _dilated_pattern(0, dq1, dk1, dv1, dil_tb)
        o4, l4 = _dilated_pattern(1, flat(dq4), flat(dk4), flat(dv4), dil_tb)
        o16, l16 = _dilated_pattern(2, flat(dq16), flat(dk16), flat(dv16), dil_tb)
        do = _dil_combine(o1, l1, o4.reshape(B, 4, S // 4, GROUP_W), l4.reshape(B, 4, S // 4, GROUP_W),
                          o16.reshape(B, 16, S // 16, GROUP_W), l16.reshape(B, 16, S // 16, GROUP_W))
        rs = lambda a: a.reshape(B * S, GROUP_W)
        x2 = _merge(x2, rs(fo), rs(mo), rs(do), rs(lo), proj, p[i].reshape(B * S, PLE_DIM), lp)
    return x2.reshape(B, S, D_MODEL)
```

```python
import functools
import math

import numpy as np
import jax
import jax.numpy as jnp
from jax import lax
from jax.experimental import pallas as pl
from jax.experimental.pallas import tpu as pltpu

F32 = jnp.float32
BF16 = jnp.bfloat16

D_MODEL = 1024
N_HEADS = 4
HEAD_DIM = 64
GROUP_W = N_HEADS * HEAD_DIM
MLA_NOPE = 64
MLA_ROPE = 32
MLA_HALF = MLA_ROPE // 2
MLA_QK_W = N_HEADS * (MLA_NOPE + MLA_ROPE)
MLA_Q_LORA = 256
MLA_KV_LORA = 128
PLE_DIM = 256
MOBA_BLOCK = 256
MOBA_TOPK = 3
DIL_PATTERNS = ((128, 1), (512, 4), (2048, 16))
DIL_BLOCK = 128
N_BUCKETS = 32
MAX_DISTANCE = 2048
ROPE_THETA = 10000.0
EPS = 1e-6
NEG = -1e30
LOG2E = math.log2(math.e)

LANES = 128
VMEM_LIMIT_BYTES = 56 * 1024 * 1024

COL_FOX, COL_MOBA, COL_DIL = 0, 768, 1536
COL_CQ, COL_CKV, COL_KR, COL_FF, COL_GATE = 2304, 2560, 2688, 2816, 3072
PROJ_W = 4096
PROJ_TN = 512
STAGE_ROWS = 128

IN_OFF = {}
IN_W = 0
for _name, _size in (("fq", GROUP_W), ("fk", GROUP_W), ("fv", GROUP_W), ("ff", N_HEADS),
                     ("mq", GROUP_W), ("mk", GROUP_W), ("mv", GROUP_W),
                     ("dq", GROUP_W), ("dk", GROUP_W), ("dv", GROUP_W),
                     ("cq", MLA_Q_LORA), ("ckv", MLA_KV_LORA), ("kr", MLA_ROPE), ("gate", D_MODEL)):
    IN_OFF[_name] = IN_W
    IN_W += _size

TM_PROJ = 256
T_PREP = 256
TQ = 256
TK = 256
MOBA_TB_ENTRIES = 8
TM_MERGE = 256
V_ROWS = HEAD_DIM + 16
MLA_GROUPS = 2


def _bucket_np(d):
    d = np.maximum(np.asarray(d, np.int64), 0)
    max_exact = N_BUCKETS // 2
    d_f = np.maximum(d, 1).astype(np.float64)
    val = np.log(d_f / max_exact) / math.log(MAX_DISTANCE / max_exact) * (N_BUCKETS - max_exact)
    frac = np.abs(val - np.round(val))
    on_edge = (frac < 1e-5) & (d > max_exact) & (val < N_BUCKETS - max_exact - 0.5)
    assert not on_edge.any(), "distance too close to a bucket edge for a static table"
    large = np.minimum(max_exact + np.floor(val + 1e-9).astype(np.int64), N_BUCKETS - 1)
    return np.where(d < max_exact, d, large)


def _bucket_steps(lo, hi):
    ds = np.arange(lo, hi + 1)
    bs = _bucket_np(ds)
    steps = [(int(ds[i]), int(bs[i])) for i in range(1, len(ds)) if bs[i] != bs[i - 1]]
    return int(bs[0]), steps


def _compiler_params(sem):
    return pltpu.CompilerParams(dimension_semantics=sem, vmem_limit_bytes=VMEM_LIMIT_BYTES)


def _bias_from_steps(d, rb_ref, col, lo, hi):
    b0, steps = _bucket_steps(lo, hi)
    val = jnp.full(d.shape, rb_ref[b0, col], F32)
    for t, b in steps:
        val = jnp.where(d >= t, rb_ref[b, col], val)
    return val


def _bias_tables_kernel(rb_ref, moba_ref, dil_ref):
    kl = lax.broadcasted_iota(jnp.int32, (MOBA_BLOCK, MOBA_BLOCK), 0)
    ql = lax.broadcasted_iota(jnp.int32, (MOBA_BLOCK, MOBA_BLOCK), 1)
    for e in range(MOBA_TB_ENTRIES - 1):
        d = jnp.maximum(e * MOBA_BLOCK + ql - kl, 0)
        lo = max(e * MOBA_BLOCK - (MOBA_BLOCK - 1), 0)
        hi = e * MOBA_BLOCK + (MOBA_BLOCK - 1)
        for h in range(N_HEADS):
            moba_ref[e, h] = _bias_from_steps(d, rb_ref, h, lo, hi) * LOG2E
    for h in range(N_HEADS):
        moba_ref[MOBA_TB_ENTRIES - 1, h] = jnp.full((MOBA_BLOCK, MOBA_BLOCK), rb_ref[N_BUCKETS - 1, h], F32) * LOG2E
    qi = lax.broadcasted_iota(jnp.int32, (DIL_BLOCK, 2 * DIL_BLOCK), 0)
    kj = lax.broadcasted_iota(jnp.int32, (DIL_BLOCK, 2 * DIL_BLOCK), 1)
    for p, (_, dil) in enumerate(DIL_PATTERNS):
        d = jnp.maximum((qi + DIL_BLOCK - kj) * dil, 0)
        for h in range(N_HEADS):
            dil_ref[p, h] = _bias_from_steps(d, rb_ref, N_HEADS + h, 0, (2 * DIL_BLOCK - 1) * dil)


def _bias_tables(rel_bias):
    assert _bucket_np((MOBA_TB_ENTRIES - 1) * MOBA_BLOCK - (MOBA_BLOCK - 1)) == N_BUCKETS - 1
    return pl.pallas_call(
        _bias_tables_kernel,
        out_shape=(jax.ShapeDtypeStruct((MOBA_TB_ENTRIES, N_HEADS, MOBA_BLOCK, MOBA_BLOCK), F32),
                   jax.ShapeDtypeStruct((len(DIL_PATTERNS), N_HEADS, DIL_BLOCK, 2 * DIL_BLOCK), F32)),
        in_specs=[pl.BlockSpec(memory_space=pltpu.SMEM)],
        out_specs=(pl.BlockSpec(memory_space=pltpu.VMEM), pl.BlockSpec(memory_space=pltpu.VMEM)),
        compiler_params=pltpu.CompilerParams(vmem_limit_bytes=VMEM_LIMIT_BYTES),
        name="bias_tables",
    )(rel_bias)


def _stage_in_weights(w_ref, ekr_ref, ws_ref):
    W = GROUP_W
    moves = ((COL_FOX, IN_OFF["fq"], 3 * W), (COL_MOBA, IN_OFF["mq"], 3 * W), (COL_DIL, IN_OFF["dq"], 3 * W),
             (COL_CQ, IN_OFF["cq"], MLA_Q_LORA), (COL_CKV, IN_OFF["ckv"], MLA_KV_LORA),
             (COL_GATE, IN_OFF["gate"], D_MODEL))
    kr_base = IN_OFF["kr"] // LANES * LANES
    assert IN_OFF["ff"] % LANES == 0 and IN_OFF["kr"] + MLA_ROPE <= kr_base + LANES
    lane = lax.broadcasted_iota(jnp.int32, (STAGE_ROWS, LANES), 1)
    for r0 in range(0, D_MODEL, STAGE_ROWS):
        rows = slice(r0, r0 + STAGE_ROWS)
        for dst, src, width in moves:
            ws_ref[rows, dst:dst + width] = w_ref[0, rows, src:src + width].astype(BF16)
        kr_win = w_ref[0, rows, kr_base:kr_base + LANES].astype(BF16)
        ws_ref[rows, COL_KR:COL_KR + LANES] = jnp.dot(kr_win, ekr_ref[...], preferred_element_type=F32).astype(BF16)
        ff_win = w_ref[0, rows, IN_OFF["ff"]:IN_OFF["ff"] + LANES]
        ws_ref[rows, COL_FF:COL_FF + LANES] = jnp.where(lane < N_HEADS, ff_win, 0.0).astype(BF16)
        ws_ref[rows, COL_FF + LANES:COL_GATE] = jnp.zeros((STAGE_ROWS, COL_GATE - COL_FF - LANES), BF16)


def _proj_kernel(x_ref, g_ref, w_ref, ekr_ref, o_ref, ws_ref):
    @pl.when(pl.program_id(0) == 0)
    def _():
        _stage_in_weights(w_ref, ekr_ref, ws_ref)

    x = x_ref[...]
    ms = jnp.mean(x * x, axis=-1, keepdims=True)
    h = (x * lax.rsqrt(ms + EPS) * g_ref[...]).astype(BF16)
    for c in range(PROJ_W // PROJ_TN):
        sl = slice(c * PROJ_TN, (c + 1) * PROJ_TN)
        o_ref[:, sl] = jnp.dot(h, ws_ref[:, sl], preferred_element_type=F32)


def _in_projection(layer, x2, ln_g, w_in, ekr):
    rows = x2.shape[0]
    return pl.pallas_call(
        _proj_kernel,
        out_shape=jax.ShapeDtypeStruct((rows, PROJ_W), F32),
        grid=(rows // TM_PROJ,),
        in_specs=[pl.BlockSpec((TM_PROJ, D_MODEL), lambda i: (i, 0)),
                  pl.BlockSpec((1, D_MODEL), lambda i: (0, 0)),
                  pl.BlockSpec((1, D_MODEL, IN_W), lambda i: (layer, 0, 0), pipeline_mode=pl.Buffered(1)),
                  pl.BlockSpec((LANES, LANES), lambda i: (0, 0))],
        out_specs=pl.BlockSpec((TM_PROJ, PROJ_W), lambda i: (i, 0)),
        scratch_shapes=[pltpu.VMEM((D_MODEL, PROJ_W), BF16)],
        compiler_params=_compiler_params(("arbitrary",)),
        name="in_projection",
    )(x2, ln_g, w_in, ekr)


def _group_mean_sq(x, g_mat):
    xx = x * x
    hi = xx.astype(BF16)
    lo = (xx - hi.astype(F32)).astype(BF16)
    return (jnp.dot(hi, g_mat, preferred_element_type=F32)
            + jnp.dot(lo, g_mat, preferred_element_type=F32))


def _group_norm(x, gain, g_mat):
    return x * lax.rsqrt(_group_mean_sq(x, g_mat) + EPS) * gain


def _row_norm(x, gain):
    return x * lax.rsqrt(jnp.mean(x * x, axis=-1, keepdims=True) + EPS) * gain


def _log_sigmoid(x):
    return -(jnp.maximum(-x, 0.0) + jnp.log1p(jnp.exp(-jnp.abs(x))))


def _prep_kernel(pf_ref, pm_ref, pd_ref, pcq_ref, pckv_ref, pkr_ref, pff_ref,
                 gains_ref, bfor_ref, qng_ref, kvng_ref, nopeg_ref, ropeg_ref,
                 wuq_ref, wukv_ref, cos_ref, sin_ref, g64_ref, g32_ref, tri_ref,
                 fqT_ref, fk_ref, fvT_ref, fck_ref, fcrow_ref,
                 mqT_ref, mk_ref, mvT_ref, mkmean_ref,
                 lqT_ref, lk_ref, lvT_ref,
                 dq1_ref, dk1_ref, dv1_ref, dq4_ref, dk4_ref, dv4_ref, dq16_ref, dk16_ref, dv16_ref,
                 carry_ref, sq_ref, sk_ref, sv_ref):
    t = pl.program_id(1)
    g64 = g64_ref[...]
    scale = HEAD_DIM ** -0.5
    W = GROUP_W
    ones_rows = jnp.ones((V_ROWS - HEAD_DIM, T_PREP), BF16)

    def store_vT(dst, v):
        vT = v.T.astype(BF16)
        for h in range(N_HEADS):
            dst[0, 0, h] = jnp.concatenate([vT[HEAD_DIM * h:HEAD_DIM * (h + 1)], ones_rows], axis=0)

    pf = pf_ref[0]
    fq = _group_norm(pf[:, :W], gains_ref[0:1, :], g64) * (scale * LOG2E)
    fk = _group_norm(pf[:, W:2 * W], gains_ref[1:2, :], g64)
    fqT_ref[0, 0] = fq.T.astype(BF16)
    fk_ref[0, 0] = fk.astype(BF16)
    store_vT(fvT_ref, pf[:, 2 * W:])

    @pl.when(t == 0)
    def _():
        carry_ref[...] = jnp.zeros_like(carry_ref)

    log_f = _log_sigmoid(pff_ref[0] + bfor_ref[...])
    c = jnp.dot(tri_ref[...], log_f, preferred_element_type=F32,
                precision=lax.Precision.HIGHEST) + carry_ref[...]
    carry_ref[...] = c[T_PREP - 1:T_PREP, :]
    c2 = c * LOG2E
    for h in range(N_HEADS):
        fck_ref[0, h] = jnp.broadcast_to(c2[:, h:h + 1], (T_PREP, LANES))
    fcrow_ref[0] = c2.T[0:8, :]

    pm = pm_ref[0]
    mq = _group_norm(pm[:, :W], gains_ref[2:3, :], g64) * (scale * LOG2E)
    mk = _group_norm(pm[:, W:2 * W], gains_ref[3:4, :], g64)
    mqT_ref[0, 0] = mq.T.astype(BF16)
    mk_ref[0, 0] = mk.astype(BF16)
    store_vT(mvT_ref, pm[:, 2 * W:])
    mkmean_ref[0, pl.ds(t, 1), :] = jnp.mean(mk, axis=0, keepdims=True)

    pd = pd_ref[0]
    dq = _group_norm(pd[:, :W], gains_ref[4:5, :], g64) * scale
    dk = _group_norm(pd[:, W:2 * W], gains_ref[5:6, :], g64)
    dv = pd[:, 2 * W:]
    dq1_ref[0] = dq.astype(BF16)
    dk1_ref[0] = dk.astype(BF16)
    dv1_ref[0] = dv.astype(BF16)
    for val, dst in ((dq, sq_ref), (dk, sk_ref), (dv, sv_ref)):
        for half in range(GROUP_W // LANES):
            dst[half] = val[:, half * LANES:(half + 1) * LANES]
    for dil, outs in ((4, (dq4_ref, dk4_ref, dv4_ref)), (16, (dq16_ref, dk16_ref, dv16_ref))):
        n = T_PREP // dil
        for r in range(dil):
            for src, dst in zip((sq_ref, sk_ref, sv_ref), outs):
                dst[0, r] = jnp.concatenate(
                    [src[half, pl.ds(r, n, stride=dil), :] for half in range(GROUP_W // LANES)],
                    axis=1).astype(BF16)

    cos = cos_ref[...]
    sin = sin_ref[...]
    qscale = (MLA_NOPE + MLA_ROPE) ** -0.5
    cqn = _row_norm(pcq_ref[0], qng_ref[...]).astype(BF16)
    rotate = lambda x: x * cos + pltpu.roll(x, LANES // 2, 1) * sin
    qf = jnp.dot(cqn, wuq_ref[...], preferred_element_type=F32)
    q_nope = _group_norm(qf[:, :W], nopeg_ref[0:1, :], g64) * (qscale * LOG2E)
    ckvn = _row_norm(pckv_ref[0], kvng_ref[...]).astype(BF16)
    kvf = jnp.dot(ckvn, wukv_ref[...], preferred_element_type=F32)
    k_nope = _group_norm(kvf[:, :W], nopeg_ref[1:2, :], g64)
    krs = pkr_ref[0]
    kr_ms = jnp.sum(krs * krs, axis=-1, keepdims=True) * (1.0 / (2 * MLA_ROPE))
    kr = rotate(krs * lax.rsqrt(kr_ms + EPS) * ropeg_ref[1:2, :])
    for g in range(MLA_GROUPS):
        slab = qf[:, W + g * LANES:W + (g + 1) * LANES]
        qr = rotate(_group_norm(slab, ropeg_ref[0:1, :], g32_ref[...])) * (qscale * LOG2E)
        lq = jnp.concatenate([q_nope[:, g * LANES:(g + 1) * LANES], qr], axis=1)
        lqT_ref[0, g] = lq.T.astype(BF16)
        lk_ref[0, g] = jnp.concatenate([k_nope[:, g * LANES:(g + 1) * LANES], kr], axis=1).astype(BF16)
    store_vT(lvT_ref, kvf[:, W:])


def _prep(proj3, consts, lp):
    B, S, _ = proj3.shape
    T = T_PREP
    NT = S // T
    assert S % T == 0 and S // MOBA_BLOCK == NT

    def pspec(width, col):
        return pl.BlockSpec((1, T, width), lambda b, t, c=col // width: (b, t, c))

    def cspec(shape):
        return pl.BlockSpec(shape, lambda b, t: (0,) * len(shape))

    in_specs = [pspec(768, COL_FOX), pspec(768, COL_MOBA), pspec(768, COL_DIL),
                pspec(256, COL_CQ), pspec(128, COL_CKV), pspec(128, COL_KR), pspec(128, COL_FF),
                cspec((6, GROUP_W)), cspec((1, LANES)), cspec((1, MLA_Q_LORA)), cspec((1, MLA_KV_LORA)),
                cspec((2, GROUP_W)), cspec((2, LANES)),
                cspec((MLA_Q_LORA, MLA_GROUPS * GROUP_W)), cspec((MLA_KV_LORA, 2 * GROUP_W)),
                pl.BlockSpec((T, LANES), lambda b, t: (t, 0)), pl.BlockSpec((T, LANES), lambda b, t: (t, 0)),
                cspec((GROUP_W, GROUP_W)), cspec((LANES, LANES)), cspec((T, T))]

    def qT(groups):
        return (jax.ShapeDtypeStruct((B, groups, GROUP_W, S), BF16),
                pl.BlockSpec((1, groups, GROUP_W, T), lambda b, t: (b, 0, 0, t)))

    def keys(groups):
        return (jax.ShapeDtypeStruct((B, groups, S, GROUP_W), BF16),
                pl.BlockSpec((1, groups, T, GROUP_W), lambda b, t: (b, 0, t, 0)))

    def rows(w):
        return jax.ShapeDtypeStruct((B, S, w), BF16), pl.BlockSpec((1, T, w), lambda b, t: (b, t, 0))

    def vT():
        return (jax.ShapeDtypeStruct((B, NT, N_HEADS, V_ROWS, T), BF16),
                pl.BlockSpec((1, 1, N_HEADS, V_ROWS, T), lambda b, t: (b, t, 0, 0, 0)))

    def resid(dil):
        return (jax.ShapeDtypeStruct((B, dil, S // dil, GROUP_W), BF16),
                pl.BlockSpec((1, dil, T // dil, GROUP_W), lambda b, t: (b, 0, t, 0)))

    outs = [qT(1), keys(1), vT(),
            (jax.ShapeDtypeStruct((B, N_HEADS, S, LANES), F32),
             pl.BlockSpec((1, N_HEADS, T, LANES), lambda b, t: (b, 0, t, 0))),
            (jax.ShapeDtypeStruct((B, 8, S), F32), pl.BlockSpec((1, 8, T), lambda b, t: (b, 0, t))),
            qT(1), keys(1), vT(),
            (jax.ShapeDtypeStruct((B, NT, GROUP_W), F32), pl.BlockSpec((1, NT, GROUP_W), lambda b, t: (b, 0, 0))),
            qT(MLA_GROUPS), keys(MLA_GROUPS), vT(),
            rows(GROUP_W), rows(GROUP_W), rows(GROUP_W),
            resid(4), resid(4), resid(4), resid(16), resid(16), resid(16)]
    return pl.pallas_call(
        _prep_kernel,
        out_shape=tuple(o[0] for o in outs),
        grid=(B, NT),
        in_specs=in_specs,
        out_specs=tuple(o[1] for o in outs),
        scratch_shapes=[pltpu.VMEM((1, LANES), F32)] + [pltpu.VMEM((GROUP_W // LANES, T, LANES), F32)] * 3,
        compiler_params=_compiler_params(("arbitrary", "arbitrary")),
        name="prep",
    )(proj3, proj3, proj3, proj3, proj3, proj3, proj3,
      lp["gains"], lp["b_forget"], lp["q_norm"], lp["kv_norm"], lp["nope_gain"], lp["rope_gain"],
      lp["w_uq"], lp["w_ukv"], consts["cos"], consts["sin"], consts["g64"], consts["g32"], consts["tri"])


def _stage_head_queries(qT_ref, qm_ref, groups):
    r = lax.broadcasted_iota(jnp.int32, (GROUP_W, TQ), 0)
    for h in range(N_HEADS):
        if groups == 1:
            g, mask = 0, (r >= HEAD_DIM * h) & (r < HEAD_DIM * (h + 1))
        else:
            g, hl = divmod(h, N_HEADS // groups)
            mask = (r >= HEAD_DIM * hl) & (r < HEAD_DIM * (hl + 1))
            for base in (2 * HEAD_DIM, 2 * HEAD_DIM + LANES // 2):
                lo = base + MLA_HALF * hl
                mask = mask | ((r >= lo) & (r < lo + MLA_HALF))
        q = qT_ref[0, g]
        qm_ref[h] = jnp.where(mask, q, jnp.zeros_like(q))


def _flash_sweep(i, groups, qm_ref, k_ref, vT_ref, m_ref, acc_ref, u_ref, st_ref, o_ref, score_fn):
    kpos = lax.broadcasted_iota(jnp.int32, (TK, TQ), 0)
    qpos = lax.broadcasted_iota(jnp.int32, (TK, TQ), 1)
    causal = kpos <= qpos
    m_ref[...] = jnp.full(m_ref.shape, NEG, F32)
    acc_ref[...] = jnp.zeros(acc_ref.shape, F32)

    def scores(j, slot, diagonal):
        row0 = pl.multiple_of(j * TK, TK)
        for h in range(N_HEADS):
            kt = k_ref[0, h // (N_HEADS // groups), pl.ds(row0, TK), :]
            s = jnp.dot(kt, qm_ref[h], preferred_element_type=F32)
            u, row = score_fn(h, s, j, diagonal)
            if diagonal:
                u = jnp.where(causal, u, NEG)
            u_ref[slot, h] = u
            m_old = m_ref[h]
            top = jnp.max(u, axis=0, keepdims=True)
            if row is None:
                m_new = jnp.maximum(m_old, top)
                st_ref[slot, h] = m_new
            else:
                m_new = jnp.maximum(m_old, top + row)
                st_ref[slot, h] = m_new - row
            st_ref[slot, N_HEADS + h] = jnp.exp2(m_old - m_new)
            m_ref[h] = m_new

    def values(j, slot):
        for h in range(N_HEADS):
            p = jnp.exp2(u_ref[slot, h] - st_ref[slot, h]).astype(BF16)
            acc_ref[h] = (st_ref[slot, N_HEADS + h] * acc_ref[h]
                          + jnp.dot(vT_ref[0, j, h], p, preferred_element_type=F32))

    scores(i, 0, True)

    def step(n, slot):
        values(jnp.where(n == 1, i, n - 2), 1 - slot)
        scores(n - 1, slot, False)

    def pair(t, carry):
        n = 1 + 2 * t
        step(n, 1)
        step(n + 1, 0)
        return carry

    def last_odd(_, carry):
        step(i, 1)
        return carry

    odd = lax.rem(i, 2)
    lax.fori_loop(0, i // 2, pair, 0)
    lax.fori_loop(0, odd, last_odd, 0)
    values(jnp.maximum(i - 1, 0), odd)
    outs = []
    for h in range(N_HEADS):
        acc = acc_ref[h]
        outs.append(acc[:HEAD_DIM] / acc[HEAD_DIM:HEAD_DIM + 1])
    o_ref[0] = jnp.concatenate(outs, axis=0).T


def _fox_kernel(qT_ref, k_ref, vT_ref, ck_ref, crow_ref, o_ref, qm_ref, m_ref, acc_ref, u_ref, st_ref):
    i = pl.program_id(1)
    _stage_head_queries(qT_ref, qm_ref, 1)

    def score_fn(h, s, j, diagonal):
        ck = ck_ref[0, h, pl.ds(pl.multiple_of(j * TK, TK), TK), :]
        return s - jnp.concatenate([ck] * (TQ // LANES), axis=1), crow_ref[0, h:h + 1, :]

    _flash_sweep(i, 1, qm_ref, k_ref, vT_ref, m_ref, acc_ref, u_ref, st_ref, o_ref, score_fn)


def _mla_kernel(qT_ref, k_ref, vT_ref, o_ref, qm_ref, m_ref, acc_ref, u_ref, st_ref):
    i = pl.program_id(1)
    _stage_head_queries(qT_ref, qm_ref, MLA_GROUPS)
    _flash_sweep(i, MLA_GROUPS, qm_ref, k_ref, vT_ref, m_ref, acc_ref, u_ref, st_ref, o_ref,
                 lambda h, s, j, diagonal: (s, None))


def _moba_kernel(qT_ref, k_ref, vT_ref, kmean_ref, tb_ref, o_ref, qm_ref, m_ref, acc_ref, u_ref, st_ref, sel_ref):
    i = pl.program_id(1)
    _stage_head_queries(qT_ref, qm_ref, 1)
    kmean = kmean_ref[0].astype(BF16)
    nb = kmean.shape[0]
    blk = lax.broadcasted_iota(jnp.int32, (nb, TQ), 0)
    for h in range(N_HEADS):
        gate = jnp.dot(kmean, qm_ref[h], preferred_element_type=F32)
        gate = jnp.where(blk < i, gate, NEG)
        rank = jnp.zeros((nb, TQ), jnp.int32)
        for jp in range(nb):
            row = gate[jp:jp + 1, :]
            ahead = (row > gate) | ((row == gate) & (jp < blk))
            rank = rank + ahead.astype(jnp.int32)
        sel_ref[h] = ((rank < MOBA_TOPK) & (blk < i)).astype(F32)

    def score_fn(h, s, j, diagonal):
        u = s + tb_ref[jnp.minimum(i - j, MOBA_TB_ENTRIES - 1), h]
        if not diagonal:
            u = jnp.where(sel_ref[h, pl.ds(j, 1), :] > 0.5, u, NEG)
        return u, None

    _flash_sweep(i, 1, qm_ref, k_ref, vT_ref, m_ref, acc_ref, u_ref, st_ref, o_ref, score_fn)


def _dense_mixer(kernel_fn, name, qT, k, vT, extra_args=(), extra_specs=(), extra_scratch=()):
    B, groups, _, S = qT.shape
    NT = S // TK
    assert TQ == TK and S % TQ == 0
    in_specs = [pl.BlockSpec((1, groups, GROUP_W, TQ), lambda b, i: (b, 0, 0, i)),
                pl.BlockSpec((1, groups, S, GROUP_W), lambda b, i: (b, 0, 0, 0)),
                pl.BlockSpec((1, NT, N_HEADS, V_ROWS, TK), lambda b, i: (b, 0, 0, 0, 0))] + list(extra_specs)
    scratch = [pltpu.VMEM((N_HEADS, GROUP_W, TQ), BF16), pltpu.VMEM((N_HEADS, 1, TQ), F32),
               pltpu.VMEM((N_HEADS, V_ROWS, TQ), F32), pltpu.VMEM((2, N_HEADS, TK, TQ), F32), pltpu.VMEM((2, 2 * N_HEADS, 1, TQ), F32)] + list(extra_scratch)
    return pl.pallas_call(
        kernel_fn,
        out_shape=jax.ShapeDtypeStruct((B, S, GROUP_W), F32),
        grid=(B, S // TQ),
        in_specs=in_specs,
        out_specs=pl.BlockSpec((1, TQ, GROUP_W), lambda b, i: (b, i, 0)),
        scratch_shapes=scratch,
        compiler_params=_compiler_params(("arbitrary", "arbitrary")),
        name=name,
    )(qT, k, vT, *extra_args)


def _dil_kernel(q_ref, kp_ref, kc_ref, vp_ref, vc_ref, tb_ref, o_ref, lse_ref):
    n = pl.program_id(1)
    q = q_ref[0]
    kb = jnp.concatenate([kp_ref[0], kc_ref[0]], axis=0)
    vb = jnp.concatenate([vp_ref[0], vc_ref[0]], axis=0)
    qi = lax.broadcasted_iota(jnp.int32, (DIL_BLOCK, 2 * DIL_BLOCK), 0)
    kj = lax.broadcasted_iota(jnp.int32, (DIL_BLOCK, 2 * DIL_BLOCK), 1)
    rel = qi + DIL_BLOCK - kj
    valid = (rel >= 0) & (rel <= DIL_BLOCK) & ((kj >= DIL_BLOCK) | (n > 0))
    qlane = lax.broadcasted_iota(jnp.int32, (DIL_BLOCK, GROUP_W), 1)
    vlane = lax.broadcasted_iota(jnp.int32, (2 * DIL_BLOCK, GROUP_W), 1)
    o_acc = jnp.zeros((DIL_BLOCK, GROUP_W), F32)
    lse_map = jnp.zeros((DIL_BLOCK, GROUP_W), F32)
    for h in range(N_HEADS):
        in_head = (qlane >= HEAD_DIM * h) & (qlane < HEAD_DIM * (h + 1))
        qm = jnp.where(in_head, q, jnp.zeros_like(q))
        s = lax.dot_general(qm, kb, (((1,), (1,)), ((), ())), preferred_element_type=F32)
        s = jnp.where(valid, s + tb_ref[0, h], NEG)
        m = jnp.max(s, axis=1, keepdims=True)
        lse = m + jnp.log(jnp.sum(jnp.exp(s - m), axis=1, keepdims=True))
        pr = jnp.exp(s - lse).astype(BF16)
        v_head = (vlane >= HEAD_DIM * h) & (vlane < HEAD_DIM * (h + 1))
        o_acc = o_acc + jnp.dot(pr, jnp.where(v_head, vb, jnp.zeros_like(vb)), preferred_element_type=F32)
        lse_map = jnp.where(in_head, lse, lse_map)
    o_ref[0] = o_acc
    lse_ref[0] = lse_map


def _dilated_pattern(p, q, k, v, dil_tb):
    G, L, _ = q.shape
    assert L % DIL_BLOCK == 0
    nb = L // DIL_BLOCK
    cur = pl.BlockSpec((1, DIL_BLOCK, GROUP_W), lambda g, n: (g, n, 0))
    prev = pl.BlockSpec((1, DIL_BLOCK, GROUP_W), lambda g, n: (g, jnp.maximum(n - 1, 0), 0))
    out = jax.ShapeDtypeStruct((G, L, GROUP_W), F32)
    return pl.pallas_call(
        _dil_kernel,
        out_shape=(out, out),
        grid=(G, nb),
        in_specs=[cur, prev, cur, prev, cur,
                  pl.BlockSpec((1, N_HEADS, DIL_BLOCK, 2 * DIL_BLOCK), lambda g, n, p=p: (p, 0, 0, 0))],
        out_specs=(cur, cur),
        compiler_params=_compiler_params(("arbitrary", "arbitrary")),
        name=f"dilated_{DIL_PATTERNS[p][1]}",
    )(q, k, k, v, v, dil_tb)


def _dil_combine_kernel(o1_ref, l1_ref, o4_ref, l4_ref, o16_ref, l16_ref, out_ref, so4, sl4, so16, sl16):
    T = out_ref.shape[1]
    halves = GROUP_W // LANES
    for dil, srcs in ((4, ((o4_ref, so4), (l4_ref, sl4))), (16, ((o16_ref, so16), (l16_ref, sl16)))):
        for src, dst in srcs:
            for r in range(dil):
                for half in range(halves):
                    dst[half, pl.ds(r, T // dil, stride=dil), :] = src[0, r, :, half * LANES:(half + 1) * LANES]
    nat = lambda ref: jnp.concatenate([ref[half] for half in range(halves)], axis=1)
    l1, l4, l16 = l1_ref[0], nat(sl4), nat(sl16)
    m = jnp.maximum(jnp.maximum(l1, l4), l16)
    w1, w4, w16 = jnp.exp(l1 - m), jnp.exp(l4 - m), jnp.exp(l16 - m)
    tot = w1 + w4 + w16
    out_ref[0] = (w1 / tot) * o1_ref[0] + (w4 / tot) * nat(so4) + (w16 / tot) * nat(so16)


def _dil_combine(o1, l1, o4, l4, o16, l16):
    B, S, _ = o1.shape
    T = 256
    nat = pl.BlockSpec((1, T, GROUP_W), lambda b, t: (b, t, 0))
    r4 = pl.BlockSpec((1, 4, T // 4, GROUP_W), lambda b, t: (b, 0, t, 0))
    r16 = pl.BlockSpec((1, 16, T // 16, GROUP_W), lambda b, t: (b, 0, t, 0))
    return pl.pallas_call(
        _dil_combine_kernel,
        out_shape=jax.ShapeDtypeStruct((B, S, GROUP_W), F32),
        grid=(B, S // T),
        in_specs=[nat, nat, r4, r4, r16, r16],
        out_specs=nat,
        scratch_shapes=[pltpu.VMEM((GROUP_W // LANES, T, LANES), F32)] * 4,
        compiler_params=_compiler_params(("arbitrary", "arbitrary")),
        name="dil_combine",
    )(o1, l1, o4, l4, o16, l16)


def _sigmoid(x):
    return 1.0 / (1.0 + jnp.exp(-x))


def _merge_kernel(x_ref, fo_ref, mo_ref, do_ref, lo_ref, gate_ref, p_ref,
                  wout_ref, png_ref, wpg_ref, wpp_ref, o_ref):
    mix = jnp.concatenate([fo_ref[...], mo_ref[...], do_ref[...], lo_ref[...]], axis=1)
    g = gate_ref[...]
    y = jnp.dot((mix * (g * _sigmoid(g))).astype(BF16), wout_ref[...], preferred_element_type=F32)
    x1 = x_ref[...] + y
    hn = _row_norm(x1, png_ref[...]).astype(BF16)
    pg = _sigmoid(jnp.dot(hn, wpg_ref[...], preferred_element_type=F32))
    pp = jnp.dot(p_ref[0].astype(BF16), wpp_ref[...], preferred_element_type=F32)
    o_ref[...] = x1 + pg * pp


def _merge(layer, x2, fo, mo, do, lo, proj, p3, lp):
    rows = x2.shape[0]
    TM = TM_MERGE

    def rspec(w, c=0):
        return pl.BlockSpec((TM, w), lambda i, c=c: (i, c))

    def cspec(shape):
        return pl.BlockSpec(shape, lambda i: (0, 0))

    return pl.pallas_call(
        _merge_kernel,
        out_shape=jax.ShapeDtypeStruct((rows, D_MODEL), F32),
        grid=(rows // TM,),
        in_specs=[rspec(D_MODEL), rspec(GROUP_W), rspec(GROUP_W), rspec(GROUP_W), rspec(GROUP_W),
                  rspec(D_MODEL, COL_GATE // D_MODEL),
                  pl.BlockSpec((1, TM, PLE_DIM), lambda i: (layer, i, 0)),
                  cspec((D_MODEL, D_MODEL)), cspec((1, D_MODEL)), cspec((D_MODEL, D_MODEL)),
                  cspec((PLE_DIM, D_MODEL))],
        out_specs=rspec(D_MODEL),
        compiler_params=_compiler_params(("arbitrary",)),
        name="merge",
    )(x2, fo, mo, do, lo, proj, p3, lp["w_out"], lp["ple_norm_g"], lp["w_ple_gate"], lp["w_ple_proj"])


def _layer_params(i, ln_g, b_forget, qk_gain, mla_q_norm, mla_kv_norm, mla_nope_gain,
                  mla_rope_gain, w_uq, w_ukv, w_out, ple_norm_g, w_ple_gate, w_ple_proj):
    def rope_slab(x1_parts, x2_parts, like):
        pad = jnp.zeros(like.shape[:-1] + (LANES // 2 - 2 * MLA_HALF,), like.dtype)
        return jnp.concatenate(list(x1_parts) + [pad] + list(x2_parts) + [pad], axis=-1)

    per_head = MLA_NOPE + MLA_ROPE
    uq = w_uq[i]
    uq_nope = [uq[:, h * per_head:h * per_head + MLA_NOPE] for h in range(N_HEADS)]
    uq_x1 = [uq[:, h * per_head + MLA_NOPE:h * per_head + MLA_NOPE + MLA_HALF] for h in range(N_HEADS)]
    uq_x2 = [uq[:, h * per_head + MLA_NOPE + MLA_HALF:(h + 1) * per_head] for h in range(N_HEADS)]
    hpg = N_HEADS // MLA_GROUPS
    uq_cols = uq_nope + [rope_slab(uq_x1[g * hpg:(g + 1) * hpg], uq_x2[g * hpg:(g + 1) * hpg], uq)
                         for g in range(MLA_GROUPS)]
    ukv = w_ukv[i]
    ukv_cols = ([ukv[:, h * 2 * HEAD_DIM:h * 2 * HEAD_DIM + HEAD_DIM] for h in range(N_HEADS)]
                + [ukv[:, h * 2 * HEAD_DIM + HEAD_DIM:(h + 1) * 2 * HEAD_DIM] for h in range(N_HEADS)])
    rg = mla_rope_gain[i]
    rope_gain = rope_slab([rg[:, :MLA_HALF]] * hpg, [rg[:, MLA_HALF:]] * hpg, rg)
    return {
        "ln_g": ln_g[i][None, :],
        "gains": jnp.tile(qk_gain[i], (1, N_HEADS)),
        "b_forget": jnp.pad(b_forget[i], (0, LANES - N_HEADS))[None, :],
        "q_norm": mla_q_norm[i][None, :],
        "kv_norm": mla_kv_norm[i][None, :],
        "nope_gain": jnp.tile(mla_nope_gain[i], (1, N_HEADS)),
        "rope_gain": rope_gain,
        "w_uq": jnp.concatenate(uq_cols, axis=1).astype(BF16),
        "w_ukv": jnp.concatenate(ukv_cols, axis=1).astype(BF16),
        "w_out": w_out[i].astype(BF16),
        "ple_norm_g": ple_norm_g[i][None, :],
        "w_ple_gate": w_ple_gate[i].astype(BF16),
        "w_ple_proj": w_ple_proj[i].astype(BF16),
    }


def _constants(S):
    inv = 1.0 / (ROPE_THETA ** (jnp.arange(MLA_HALF, dtype=F32) * 2.0 / MLA_ROPE))
    ang = jnp.arange(S).astype(F32)[:, None] * inv[None, :]
    cos = jnp.tile(jnp.cos(ang), (1, 2 * N_HEADS))
    sin = jnp.tile(jnp.sin(ang), (1, 2 * N_HEADS))
    sign = np.concatenate([-np.ones(LANES // 2, np.float32), np.ones(LANES // 2, np.float32)])
    lane = np.arange(GROUP_W)
    g64 = (lane[:, None] // HEAD_DIM == lane[None, :] // HEAD_DIM).astype(np.float32) / HEAD_DIM
    within = np.arange(LANES) % (LANES // 2)
    rope_head = np.where(within < 2 * MLA_HALF, within // MLA_HALF, -1)
    g32 = ((rope_head[:, None] == rope_head[None, :]) & (rope_head[:, None] >= 0)).astype(np.float32) / MLA_ROPE
    tri = np.tril(np.ones((T_PREP, T_PREP), np.float32))
    ekr = np.zeros((LANES, LANES), np.float32)
    kr_lane0 = IN_OFF["kr"] % LANES
    for c in range(MLA_ROPE):
        half, idx = divmod(c, MLA_HALF)
        for copy in range(N_HEADS // MLA_GROUPS):
            ekr[kr_lane0 + c, half * (LANES // 2) + copy * MLA_HALF + idx] = 1.0
    return {"cos": cos, "sin": sin * sign[None, :], "g64": jnp.asarray(g64, BF16),
            "g32": jnp.asarray(g32, BF16), "tri": jnp.asarray(tri), "ekr": jnp.asarray(ekr, BF16)}


def kernel(x, p, ln_g, w_in, b_forget, qk_gain, mla_q_norm, mla_kv_norm, mla_nope_gain, mla_rope_gain,
           w_uq, w_ukv, w_out, rel_bias, ple_norm_g, w_ple_gate, w_ple_proj):
    B, S, _ = x.shape
    depth = p.shape[0]
    consts = _constants(S)
    moba_tb, dil_tb = _bias_tables(rel_bias)
    x2 = x.reshape(B * S, D_MODEL)
    for i in range(depth):
        lp = _layer_params(i, ln_g, b_forget, qk_gain, mla_q_norm, mla_kv_norm, mla_nope_gain,
                           mla_rope_gain, w_uq, w_ukv, w_out, ple_norm_g, w_ple_gate, w_ple_proj)
        proj = _in_projection(i, x2, lp["ln_g"], w_in, consts["ekr"])
        (fqT, fk, fvT, fck, fcrow, mqT, mk, mvT, mkmean, lqT, lk, lvT,
         dq1, dk1, dv1, dq4, dk4, dv4, dq16, dk16, dv16) = _prep(proj.reshape(B, S, PROJ_W), consts, lp)
        S_ = S
        fo = _dense_mixer(
            _fox_kernel, "fox", fqT, fk, fvT, (fck, fcrow),
            (pl.BlockSpec((1, N_HEADS, S_, LANES), lambda b, i: (b, 0, 0, 0)),
             pl.BlockSpec((1, 8, TQ), lambda b, i: (b, 0, i))))
        mo = _dense_mixer(
            _moba_kernel, "moba", mqT, mk, mvT, (mkmean, moba_tb),
            (pl.BlockSpec((1, S_ // MOBA_BLOCK, GROUP_W), lambda b, i: (b, 0, 0)),
             pl.BlockSpec((MOBA_TB_ENTRIES, N_HEADS, MOBA_BLOCK, MOBA_BLOCK), lambda b, i: (0, 0, 0, 0))),
            (pltpu.VMEM((N_HEADS, S_ // MOBA_BLOCK, TQ), F32),))
        lo = _dense_mixer(_mla_kernel, "mla", lqT, lk, lvT)
        flat = lambda a: a.reshape(-1, a.shape[-2], GROUP_W)
        o1, l1 = _dilated_pattern(0, dq1, dk1, dv1, dil_tb)
        o4, l4 = _dilated_pattern(1, flat(dq4), flat(dk4), flat(dv4), dil_tb)
        o16, l16 = _dilated_pattern(2, flat(dq16), flat(dk16), flat(dv16), dil_tb)
        do = _dil_combine(o1, l1, o4.reshape(B, 4, S // 4, GROUP_W), l4.reshape(B, 4, S // 4, GROUP_W),
                          o16.reshape(B, 16, S // 16, GROUP_W), l16.reshape(B, 16, S // 16, GROUP_W))
        rs = lambda a: a.reshape(B * S, GROUP_W)
        x2 = _merge(i, x2, rs(fo), rs(mo), rs(do), rs(lo), proj, p.reshape(depth, B * S, PLE_DIM), lp)
    return x2.reshape(B, S, D_MODEL)
```

```python
import functools
import math

import numpy as np
import jax
import jax.numpy as jnp
from jax import lax
from jax.experimental import pallas as pl
from jax.experimental.pallas import tpu as pltpu

F32 = jnp.float32
BF16 = jnp.bfloat16

D_MODEL = 1024
N_HEADS = 4
HEAD_DIM = 64
GROUP_W = N_HEADS * HEAD_DIM
MLA_NOPE = 64
MLA_ROPE = 32
MLA_HALF = MLA_ROPE // 2
MLA_QK_W = N_HEADS * (MLA_NOPE + MLA_ROPE)
MLA_Q_LORA = 256
MLA_KV_LORA = 128
PLE_DIM = 256
MOBA_BLOCK = 256
MOBA_TOPK = 3
DIL_PATTERNS = ((128, 1), (512, 4), (2048, 16))
DIL_BLOCK = 128
DIL_BLOCKS_PER_STEP = 4
N_BUCKETS = 32
MAX_DISTANCE = 2048
ROPE_THETA = 10000.0
EPS = 1e-6
NEG = -1e30
LOG2E = math.log2(math.e)

LANES = 128
VMEM_LIMIT_BYTES = 56 * 1024 * 1024

COL_FOX, COL_MOBA, COL_DIL = 0, 768, 1536
COL_CQ, COL_CKV, COL_KR, COL_FF, COL_GATE = 2304, 2560, 2688, 2816, 3072
PROJ_W = 4096
PROJ_TN = 512
STAGE_ROWS = 128

IN_OFF = {}
IN_W = 0
for _name, _size in (("fq", GROUP_W), ("fk", GROUP_W), ("fv", GROUP_W), ("ff", N_HEADS),
                     ("mq", GROUP_W), ("mk", GROUP_W), ("mv", GROUP_W),
                     ("dq", GROUP_W), ("dk", GROUP_W), ("dv", GROUP_W),
                     ("cq", MLA_Q_LORA), ("ckv", MLA_KV_LORA), ("kr", MLA_ROPE), ("gate", D_MODEL)):
    IN_OFF[_name] = IN_W
    IN_W += _size

TM_PROJ = 256
T_PREP = 256
TQ = 256
TK = 256
MOBA_TB_ENTRIES = 8
TM_MERGE = 256
V_ROWS = HEAD_DIM + 16
MLA_GROUPS = 2


def _bucket_np(d):
    d = np.maximum(np.asarray(d, np.int64), 0)
    max_exact = N_BUCKETS // 2
    d_f = np.maximum(d, 1).astype(np.float64)
    val = np.log(d_f / max_exact) / math.log(MAX_DISTANCE / max_exact) * (N_BUCKETS - max_exact)
    frac = np.abs(val - np.round(val))
    on_edge = (frac < 1e-5) & (d > max_exact) & (val < N_BUCKETS - max_exact - 0.5)
    assert not on_edge.any(), "distance too close to a bucket edge for a static table"
    large = np.minimum(max_exact + np.floor(val + 1e-9).astype(np.int64), N_BUCKETS - 1)
    return np.where(d < max_exact, d, large)


def _bucket_steps(lo, hi):
    ds = np.arange(lo, hi + 1)
    bs = _bucket_np(ds)
    steps = [(int(ds[i]), int(bs[i])) for i in range(1, len(ds)) if bs[i] != bs[i - 1]]
    return int(bs[0]), steps


def _compiler_params(sem):
    return pltpu.CompilerParams(dimension_semantics=sem, vmem_limit_bytes=VMEM_LIMIT_BYTES)


def _bias_from_steps(d, rb_ref, col, lo, hi):
    b0, steps = _bucket_steps(lo, hi)
    val = jnp.full(d.shape, rb_ref[b0, col], F32)
    for t, b in steps:
        val = jnp.where(d >= t, rb_ref[b, col], val)
    return val


def _bias_tables_kernel(rb_ref, moba_ref, dil_ref):
    kl = lax.broadcasted_iota(jnp.int32, (MOBA_BLOCK, MOBA_BLOCK), 0)
    ql = lax.broadcasted_iota(jnp.int32, (MOBA_BLOCK, MOBA_BLOCK), 1)
    for e in range(MOBA_TB_ENTRIES - 1):
        d = jnp.maximum(e * MOBA_BLOCK + ql - kl, 0)
        lo = max(e * MOBA_BLOCK - (MOBA_BLOCK - 1), 0)
        hi = e * MOBA_BLOCK + (MOBA_BLOCK - 1)
        for h in range(N_HEADS):
            moba_ref[e, h] = _bias_from_steps(d, rb_ref, h, lo, hi) * LOG2E
    for h in range(N_HEADS):
        moba_ref[MOBA_TB_ENTRIES - 1, h] = jnp.full((MOBA_BLOCK, MOBA_BLOCK), rb_ref[N_BUCKETS - 1, h], F32) * LOG2E
    qi = lax.broadcasted_iota(jnp.int32, (DIL_BLOCK, 2 * DIL_BLOCK), 0)
    kj = lax.broadcasted_iota(jnp.int32, (DIL_BLOCK, 2 * DIL_BLOCK), 1)
    for p, (_, dil) in enumerate(DIL_PATTERNS):
        d = jnp.maximum((qi + DIL_BLOCK - kj) * dil, 0)
        for h in range(N_HEADS):
            dil_ref[p, h] = _bias_from_steps(d, rb_ref, N_HEADS + h, 0, (2 * DIL_BLOCK - 1) * dil)


def _bias_tables(rel_bias):
    assert _bucket_np((MOBA_TB_ENTRIES - 1) * MOBA_BLOCK - (MOBA_BLOCK - 1)) == N_BUCKETS - 1
    return pl.pallas_call(
        _bias_tables_kernel,
        out_shape=(jax.ShapeDtypeStruct((MOBA_TB_ENTRIES, N_HEADS, MOBA_BLOCK, MOBA_BLOCK), F32),
                   jax.ShapeDtypeStruct((len(DIL_PATTERNS), N_HEADS, DIL_BLOCK, 2 * DIL_BLOCK), F32)),
        in_specs=[pl.BlockSpec(memory_space=pltpu.SMEM)],
        out_specs=(pl.BlockSpec(memory_space=pltpu.VMEM), pl.BlockSpec(memory_space=pltpu.VMEM)),
        compiler_params=pltpu.CompilerParams(vmem_limit_bytes=VMEM_LIMIT_BYTES),
        name="bias_tables",
    )(rel_bias)


def _stage_in_weights(w_ref, ekr_ref, ws_ref):
    W = GROUP_W
    moves = ((COL_FOX, IN_OFF["fq"], 3 * W), (COL_MOBA, IN_OFF["mq"], 3 * W), (COL_DIL, IN_OFF["dq"], 3 * W),
             (COL_CQ, IN_OFF["cq"], MLA_Q_LORA), (COL_CKV, IN_OFF["ckv"], MLA_KV_LORA),
             (COL_GATE, IN_OFF["gate"], D_MODEL))
    kr_base = IN_OFF["kr"] // LANES * LANES
    assert IN_OFF["ff"] % LANES == 0 and IN_OFF["kr"] + MLA_ROPE <= kr_base + LANES
    lane = lax.broadcasted_iota(jnp.int32, (STAGE_ROWS, LANES), 1)
    for r0 in range(0, D_MODEL, STAGE_ROWS):
        rows = slice(r0, r0 + STAGE_ROWS)
        for dst, src, width in moves:
            ws_ref[rows, dst:dst + width] = w_ref[0, rows, src:src + width].astype(BF16)
        kr_win = w_ref[0, rows, kr_base:kr_base + LANES].astype(BF16)
        ws_ref[rows, COL_KR:COL_KR + LANES] = jnp.dot(kr_win, ekr_ref[...], preferred_element_type=F32).astype(BF16)
        ff_win = w_ref[0, rows, IN_OFF["ff"]:IN_OFF["ff"] + LANES]
        ws_ref[rows, COL_FF:COL_FF + LANES] = jnp.where(lane < N_HEADS, ff_win, 0.0).astype(BF16)
        ws_ref[rows, COL_FF + LANES:COL_GATE] = jnp.zeros((STAGE_ROWS, COL_GATE - COL_FF - LANES), BF16)


def _proj_kernel(x_ref, g_ref, w_ref, ekr_ref, o_ref, ws_ref):
    @pl.when(pl.program_id(0) == 0)
    def _():
        _stage_in_weights(w_ref, ekr_ref, ws_ref)

    x = x_ref[...]
    ms = jnp.mean(x * x, axis=-1, keepdims=True)
    h = (x * lax.rsqrt(ms + EPS) * g_ref[...]).astype(BF16)
    for c in range(PROJ_W // PROJ_TN):
        sl = slice(c * PROJ_TN, (c + 1) * PROJ_TN)
        o_ref[:, sl] = jnp.dot(h, ws_ref[:, sl], preferred_element_type=F32)


def _in_projection(layer, x2, ln_g, w_in, ekr):
    rows = x2.shape[0]
    return pl.pallas_call(
        _proj_kernel,
        out_shape=jax.ShapeDtypeStruct((rows, PROJ_W), F32),
        grid=(rows // TM_PROJ,),
        in_specs=[pl.BlockSpec((TM_PROJ, D_MODEL), lambda i: (i, 0)),
                  pl.BlockSpec((1, D_MODEL), lambda i: (0, 0)),
                  pl.BlockSpec((1, D_MODEL, IN_W), lambda i: (layer, 0, 0), pipeline_mode=pl.Buffered(1)),
                  pl.BlockSpec((LANES, LANES), lambda i: (0, 0))],
        out_specs=pl.BlockSpec((TM_PROJ, PROJ_W), lambda i: (i, 0)),
        scratch_shapes=[pltpu.VMEM((D_MODEL, PROJ_W), BF16)],
        compiler_params=_compiler_params(("arbitrary",)),
        name="in_projection",
    )(x2, ln_g, w_in, ekr)


def _group_mean_sq(x, g_mat):
    xx = x * x
    hi = xx.astype(BF16)
    lo = (xx - hi.astype(F32)).astype(BF16)
    return (jnp.dot(hi, g_mat, preferred_element_type=F32)
            + jnp.dot(lo, g_mat, preferred_element_type=F32))


def _group_norm(x, gain, g_mat):
    return x * lax.rsqrt(_group_mean_sq(x, g_mat) + EPS) * gain


def _row_norm(x, gain):
    return x * lax.rsqrt(jnp.mean(x * x, axis=-1, keepdims=True) + EPS) * gain


def _log_sigmoid(x):
    return -(jnp.maximum(-x, 0.0) + jnp.log1p(jnp.exp(-jnp.abs(x))))


def _prep_kernel(pf_ref, pm_ref, pd_ref, pcq_ref, pckv_ref, pkr_ref, pff_ref,
                 gains_ref, bfor_ref, qng_ref, kvng_ref, nopeg_ref, ropeg_ref,
                 wuq_ref, wukv_ref, cos_ref, sin_ref, g64_ref, g32_ref, tri_ref,
                 fqT_ref, fk_ref, fvT_ref, fck_ref, fcrow_ref,
                 mqT_ref, mk_ref, mvT_ref, mkmean_ref,
                 lqT_ref, lk_ref, lvT_ref,
                 dq1_ref, dk1_ref, dv1_ref, dq4_ref, dk4_ref, dv4_ref, dq16_ref, dk16_ref, dv16_ref,
                 carry_ref, sq_ref, sk_ref, sv_ref):
    t = pl.program_id(1)
    g64 = g64_ref[...]
    scale = HEAD_DIM ** -0.5
    W = GROUP_W
    ones_rows = jnp.ones((V_ROWS - HEAD_DIM, T_PREP), BF16)

    def store_vT(dst, v):
        vT = v.T.astype(BF16)
        for h in range(N_HEADS):
            dst[0, 0, h] = jnp.concatenate([vT[HEAD_DIM * h:HEAD_DIM * (h + 1)], ones_rows], axis=0)

    pf = pf_ref[0]
    fq = _group_norm(pf[:, :W], gains_ref[0:1, :], g64) * (scale * LOG2E)
    fk = _group_norm(pf[:, W:2 * W], gains_ref[1:2, :], g64)
    fqT_ref[0, 0] = fq.T.astype(BF16)
    fk_ref[0, 0] = fk.astype(BF16)
    store_vT(fvT_ref, pf[:, 2 * W:])

    @pl.when(t == 0)
    def _():
        carry_ref[...] = jnp.zeros_like(carry_ref)

    log_f = _log_sigmoid(pff_ref[0] + bfor_ref[...])
    c = jnp.dot(tri_ref[...], log_f, preferred_element_type=F32,
                precision=lax.Precision.HIGHEST) + carry_ref[...]
    carry_ref[...] = c[T_PREP - 1:T_PREP, :]
    c2 = c * LOG2E
    for h in range(N_HEADS):
        fck_ref[0, h] = jnp.broadcast_to(c2[:, h:h + 1], (T_PREP, LANES))
    fcrow_ref[0] = c2.T[0:8, :]

    pm = pm_ref[0]
    mq = _group_norm(pm[:, :W], gains_ref[2:3, :], g64) * (scale * LOG2E)
    mk = _group_norm(pm[:, W:2 * W], gains_ref[3:4, :], g64)
    mqT_ref[0, 0] = mq.T.astype(BF16)
    mk_ref[0, 0] = mk.astype(BF16)
    store_vT(mvT_ref, pm[:, 2 * W:])
    mkmean_ref[0, pl.ds(t, 1), :] = jnp.mean(mk, axis=0, keepdims=True)

    pd = pd_ref[0]
    dq = _group_norm(pd[:, :W], gains_ref[4:5, :], g64) * scale
    dk = _group_norm(pd[:, W:2 * W], gains_ref[5:6, :], g64)
    dv = pd[:, 2 * W:]
    dq1_ref[0] = dq.astype(BF16)
    dk1_ref[0] = dk.astype(BF16)
    dv1_ref[0] = dv.astype(BF16)
    for val, dst in ((dq, sq_ref), (dk, sk_ref), (dv, sv_ref)):
        for half in range(GROUP_W // LANES):
            dst[half] = val[:, half * LANES:(half + 1) * LANES]
    for dil, outs in ((4, (dq4_ref, dk4_ref, dv4_ref)), (16, (dq16_ref, dk16_ref, dv16_ref))):
        n = T_PREP // dil
        for r in range(dil):
            for src, dst in zip((sq_ref, sk_ref, sv_ref), outs):
                dst[0, r] = jnp.concatenate(
                    [src[half, pl.ds(r, n, stride=dil), :] for half in range(GROUP_W // LANES)],
                    axis=1).astype(BF16)

    cos = cos_ref[...]
    sin = sin_ref[...]
    qscale = (MLA_NOPE + MLA_ROPE) ** -0.5
    cqn = _row_norm(pcq_ref[0], qng_ref[...]).astype(BF16)
    rotate = lambda x: x * cos + pltpu.roll(x, LANES // 2, 1) * sin
    qf = jnp.dot(cqn, wuq_ref[...], preferred_element_type=F32)
    q_nope = _group_norm(qf[:, :W], nopeg_ref[0:1, :], g64) * (qscale * LOG2E)
    ckvn = _row_norm(pckv_ref[0], kvng_ref[...]).astype(BF16)
    kvf = jnp.dot(ckvn, wukv_ref[...], preferred_element_type=F32)
    k_nope = _group_norm(kvf[:, :W], nopeg_ref[1:2, :], g64)
    krs = pkr_ref[0]
    kr_ms = jnp.sum(krs * krs, axis=-1, keepdims=True) * (1.0 / (2 * MLA_ROPE))
    kr = rotate(krs * lax.rsqrt(kr_ms + EPS) * ropeg_ref[1:2, :])
    for g in range(MLA_GROUPS):
        slab = qf[:, W + g * LANES:W + (g + 1) * LANES]
        qr = rotate(_group_norm(slab, ropeg_ref[0:1, :], g32_ref[...])) * (qscale * LOG2E)
        lq = jnp.concatenate([q_nope[:, g * LANES:(g + 1) * LANES], qr], axis=1)
        lqT_ref[0, g] = lq.T.astype(BF16)
        lk_ref[0, g] = jnp.concatenate([k_nope[:, g * LANES:(g + 1) * LANES], kr], axis=1).astype(BF16)
    store_vT(lvT_ref, kvf[:, W:])


def _prep(proj3, consts, lp):
    B, S, _ = proj3.shape
    T = T_PREP
    NT = S // T
    assert S % T == 0 and S // MOBA_BLOCK == NT

    def pspec(width, col):
        return pl.BlockSpec((1, T, width), lambda b, t, c=col // width: (b, t, c))

    def cspec(shape):
        return pl.BlockSpec(shape, lambda b, t: (0,) * len(shape))

    in_specs = [pspec(768, COL_FOX), pspec(768, COL_MOBA), pspec(768, COL_DIL),
                pspec(256, COL_CQ), pspec(128, COL_CKV), pspec(128, COL_KR), pspec(128, COL_FF),
                cspec((6, GROUP_W)), cspec((1, LANES)), cspec((1, MLA_Q_LORA)), cspec((1, MLA_KV_LORA)),
                cspec((2, GROUP_W)), cspec((2, LANES)),
                cspec((MLA_Q_LORA, MLA_GROUPS * GROUP_W)), cspec((MLA_KV_LORA, 2 * GROUP_W)),
                pl.BlockSpec((T, LANES), lambda b, t: (t, 0)), pl.BlockSpec((T, LANES), lambda b, t: (t, 0)),
                cspec((GROUP_W, GROUP_W)), cspec((LANES, LANES)), cspec((T, T))]

    def qT(groups):
        return (jax.ShapeDtypeStruct((B, groups, GROUP_W, S), BF16),
                pl.BlockSpec((1, groups, GROUP_W, T), lambda b, t: (b, 0, 0, t)))

    def keys(groups):
        return (jax.ShapeDtypeStruct((B, groups, S, GROUP_W), BF16),
                pl.BlockSpec((1, groups, T, GROUP_W), lambda b, t: (b, 0, t, 0)))

    def rows(w):
        return jax.ShapeDtypeStruct((B, S, w), BF16), pl.BlockSpec((1, T, w), lambda b, t: (b, t, 0))

    def vT():
        return (jax.ShapeDtypeStruct((B, NT, N_HEADS, V_ROWS, T), BF16),
                pl.BlockSpec((1, 1, N_HEADS, V_ROWS, T), lambda b, t: (b, t, 0, 0, 0)))

    def resid(dil):
        return (jax.ShapeDtypeStruct((B, dil, S // dil, GROUP_W), BF16),
                pl.BlockSpec((1, dil, T // dil, GROUP_W), lambda b, t: (b, 0, t, 0)))

    outs = [qT(1), keys(1), vT(),
            (jax.ShapeDtypeStruct((B, N_HEADS, S, LANES), F32),
             pl.BlockSpec((1, N_HEADS, T, LANES), lambda b, t: (b, 0, t, 0))),
            (jax.ShapeDtypeStruct((B, 8, S), F32), pl.BlockSpec((1, 8, T), lambda b, t: (b, 0, t))),
            qT(1), keys(1), vT(),
            (jax.ShapeDtypeStruct((B, NT, GROUP_W), F32), pl.BlockSpec((1, NT, GROUP_W), lambda b, t: (b, 0, 0))),
            qT(MLA_GROUPS), keys(MLA_GROUPS), vT(),
            rows(GROUP_W), rows(GROUP_W), rows(GROUP_W),
            resid(4), resid(4), resid(4), resid(16), resid(16), resid(16)]
    return pl.pallas_call(
        _prep_kernel,
        out_shape=tuple(o[0] for o in outs),
        grid=(B, NT),
        in_specs=in_specs,
        out_specs=tuple(o[1] for o in outs),
        scratch_shapes=[pltpu.VMEM((1, LANES), F32)] + [pltpu.VMEM((GROUP_W // LANES, T, LANES), F32)] * 3,
        compiler_params=_compiler_params(("arbitrary", "arbitrary")),
        name="prep",
    )(proj3, proj3, proj3, proj3, proj3, proj3, proj3,
      lp["gains"], lp["b_forget"], lp["q_norm"], lp["kv_norm"], lp["nope_gain"], lp["rope_gain"],
      lp["w_uq"], lp["w_ukv"], consts["cos"], consts["sin"], consts["g64"], consts["g32"], consts["tri"])


def _stage_head_queries(qT_ref, qm_ref, groups):
    r = lax.broadcasted_iota(jnp.int32, (GROUP_W, TQ), 0)
    for h in range(N_HEADS):
        if groups == 1:
            g, mask = 0, (r >= HEAD_DIM * h) & (r < HEAD_DIM * (h + 1))
        else:
            g, hl = divmod(h, N_HEADS // groups)
            mask = (r >= HEAD_DIM * hl) & (r < HEAD_DIM * (hl + 1))
            for base in (2 * HEAD_DIM, 2 * HEAD_DIM + LANES // 2):
                lo = base + MLA_HALF * hl
                mask = mask | ((r >= lo) & (r < lo + MLA_HALF))
        q = qT_ref[0, g]
        qm_ref[h] = jnp.where(mask, q, jnp.zeros_like(q))


def _flash_sweep(i, groups, qm_ref, k_ref, vT_ref, m_ref, acc_ref, u_ref, st_ref, o_ref, score_fn):
    kpos = lax.broadcasted_iota(jnp.int32, (TK, TQ), 0)
    qpos = lax.broadcasted_iota(jnp.int32, (TK, TQ), 1)
    causal = kpos <= qpos
    m_ref[...] = jnp.full(m_ref.shape, NEG, F32)
    acc_ref[...] = jnp.zeros(acc_ref.shape, F32)

    def scores(j, slot, diagonal):
        row0 = pl.multiple_of(j * TK, TK)
        for h in range(N_HEADS):
            kt = k_ref[0, h // (N_HEADS // groups), pl.ds(row0, TK), :]
            s = jnp.dot(kt, qm_ref[h], preferred_element_type=F32)
            u, row = score_fn(h, s, j, diagonal)
            if diagonal:
                u = jnp.where(causal, u, NEG)
            u_ref[slot, h] = u
            m_old = m_ref[h]
            top = jnp.max(u, axis=0, keepdims=True)
            if row is None:
                m_new = jnp.maximum(m_old, top)
                st_ref[slot, h] = m_new
            else:
                m_new = jnp.maximum(m_old, top + row)
                st_ref[slot, h] = m_new - row
            st_ref[slot, N_HEADS + h] = jnp.exp2(m_old - m_new)
            m_ref[h] = m_new

    def values(j, slot):
        for h in range(N_HEADS):
            p = jnp.exp2(u_ref[slot, h] - st_ref[slot, h]).astype(BF16)
            acc_ref[h] = (st_ref[slot, N_HEADS + h] * acc_ref[h]
                          + jnp.dot(vT_ref[0, j, h], p, preferred_element_type=F32))

    scores(i, 0, True)

    def step(n, slot):
        scores(n - 1, slot, False)
        values(jnp.where(n == 1, i, n - 2), 1 - slot)

    def pair(t, carry):
        n = 1 + 2 * t
        step(n, 1)
        step(n + 1, 0)
        return carry

    def last_odd(_, carry):
        step(i, 1)
        return carry

    odd = lax.rem(i, 2)
    lax.fori_loop(0, i // 2, pair, 0)
    lax.fori_loop(0, odd, last_odd, 0)
    values(jnp.maximum(i - 1, 0), odd)
    outs = []
    for h in range(N_HEADS):
        acc = acc_ref[h]
        outs.append(acc[:HEAD_DIM] / acc[HEAD_DIM:HEAD_DIM + 1])
    o_ref[0] = jnp.concatenate(outs, axis=0).T


def _fox_kernel(qT_ref, k_ref, vT_ref, ck_ref, crow_ref, o_ref, qm_ref, m_ref, acc_ref, u_ref, st_ref):
    i = pl.program_id(1)
    _stage_head_queries(qT_ref, qm_ref, 1)

    def score_fn(h, s, j, diagonal):
        ck = ck_ref[0, h, pl.ds(pl.multiple_of(j * TK, TK), TK), :]
        return s - jnp.concatenate([ck] * (TQ // LANES), axis=1), crow_ref[0, h:h + 1, :]

    _flash_sweep(i, 1, qm_ref, k_ref, vT_ref, m_ref, acc_ref, u_ref, st_ref, o_ref, score_fn)


def _mla_kernel(qT_ref, k_ref, vT_ref, o_ref, qm_ref, m_ref, acc_ref, u_ref, st_ref):
    i = pl.program_id(1)
    _stage_head_queries(qT_ref, qm_ref, MLA_GROUPS)
    _flash_sweep(i, MLA_GROUPS, qm_ref, k_ref, vT_ref, m_ref, acc_ref, u_ref, st_ref, o_ref,
                 lambda h, s, j, diagonal: (s, None))


def _moba_kernel(qT_ref, k_ref, vT_ref, kmean_ref, tb_ref, o_ref, qm_ref, m_ref, acc_ref, u_ref, st_ref, sel_ref):
    i = pl.program_id(1)
    _stage_head_queries(qT_ref, qm_ref, 1)
    kmean = kmean_ref[0].astype(BF16)
    nb = kmean.shape[0]
    blk = lax.broadcasted_iota(jnp.int32, (nb, TQ), 0)
    for h in range(N_HEADS):
        gate = jnp.dot(kmean, qm_ref[h], preferred_element_type=F32)
        gate = jnp.where(blk < i, gate, NEG)
        rank = jnp.zeros((nb, TQ), jnp.int32)
        for jp in range(nb):
            row = gate[jp:jp + 1, :]
            ahead = (row > gate) | ((row == gate) & (jp < blk))
            rank = rank + ahead.astype(jnp.int32)
        sel_ref[h] = ((rank < MOBA_TOPK) & (blk < i)).astype(F32)

    def score_fn(h, s, j, diagonal):
        u = s + tb_ref[jnp.minimum(i - j, MOBA_TB_ENTRIES - 1), h]
        if not diagonal:
            u = jnp.where(sel_ref[h, pl.ds(j, 1), :] > 0.5, u, NEG)
        return u, None

    _flash_sweep(i, 1, qm_ref, k_ref, vT_ref, m_ref, acc_ref, u_ref, st_ref, o_ref, score_fn)


def _dense_mixer(kernel_fn, name, qT, k, vT, extra_args=(), extra_specs=(), extra_scratch=()):
    B, groups, _, S = qT.shape
    NT = S // TK
    assert TQ == TK and S % TQ == 0
    in_specs = [pl.BlockSpec((1, groups, GROUP_W, TQ), lambda b, i: (b, 0, 0, i)),
                pl.BlockSpec((1, groups, S, GROUP_W), lambda b, i: (b, 0, 0, 0)),
                pl.BlockSpec((1, NT, N_HEADS, V_ROWS, TK), lambda b, i: (b, 0, 0, 0, 0))] + list(extra_specs)
    scratch = [pltpu.VMEM((N_HEADS, GROUP_W, TQ), BF16), pltpu.VMEM((N_HEADS, 1, TQ), F32),
               pltpu.VMEM((N_HEADS, V_ROWS, TQ), F32), pltpu.VMEM((2, N_HEADS, TK, TQ), F32), pltpu.VMEM((2, 2 * N_HEADS, 1, TQ), F32)] + list(extra_scratch)
    return pl.pallas_call(
        kernel_fn,
        out_shape=jax.ShapeDtypeStruct((B, S, GROUP_W), F32),
        grid=(B, S // TQ),
        in_specs=in_specs,
        out_specs=pl.BlockSpec((1, TQ, GROUP_W), lambda b, i: (b, i, 0)),
        scratch_shapes=scratch,
        compiler_params=_compiler_params(("arbitrary", "arbitrary")),
        name=name,
    )(qT, k, vT, *extra_args)


def _dil_kernel(q_ref, kp_ref, kc_ref, vp_ref, vc_ref, tb_ref, o_ref, lse_ref):
    n = pl.program_id(1)
    qb = q_ref.shape[1] // DIL_BLOCK
    k_all = jnp.concatenate([kp_ref[0], kc_ref[0]], axis=0)
    v_all = jnp.concatenate([vp_ref[0], vc_ref[0]], axis=0)
    qi = lax.broadcasted_iota(jnp.int32, (DIL_BLOCK, 2 * DIL_BLOCK), 0)
    kj = lax.broadcasted_iota(jnp.int32, (DIL_BLOCK, 2 * DIL_BLOCK), 1)
    rel = qi + DIL_BLOCK - kj
    in_band = (rel >= 0) & (rel <= DIL_BLOCK)
    qlane = lax.broadcasted_iota(jnp.int32, (DIL_BLOCK, GROUP_W), 1)
    in_head = [(qlane >= HEAD_DIM * h) & (qlane < HEAD_DIM * (h + 1)) for h in range(N_HEADS)]
    for c in range(qb):
        rows = slice(c * DIL_BLOCK, (c + 1) * DIL_BLOCK)
        q = q_ref[0, rows, :]
        kb = k_all[c * DIL_BLOCK:(c + 2) * DIL_BLOCK]
        vb = v_all[c * DIL_BLOCK:(c + 2) * DIL_BLOCK]
        valid = in_band & ((kj >= DIL_BLOCK) | (n > 0)) if c == 0 else in_band
        q_heads = jnp.concatenate([jnp.where(msk, q, jnp.zeros_like(q)) for msk in in_head], axis=0)
        s = lax.dot_general(q_heads, kb, (((1,), (1,)), ((), ())), preferred_element_type=F32)
        s = s.reshape(N_HEADS, DIL_BLOCK, 2 * DIL_BLOCK)
        s = jnp.where(valid[None], s + tb_ref[0], NEG)
        m = jnp.max(s, axis=-1, keepdims=True)
        e = jnp.exp(s - m)
        l = jnp.sum(e, axis=-1, keepdims=True)
        lse = m + jnp.log(l)
        pr = (e * (1.0 / l)).astype(BF16)
        o_all = jnp.dot(pr.reshape(N_HEADS * DIL_BLOCK, 2 * DIL_BLOCK), vb, preferred_element_type=F32)
        o_all = o_all.reshape(N_HEADS, DIL_BLOCK, GROUP_W)
        o_acc = jnp.zeros((DIL_BLOCK, GROUP_W), F32)
        lse_map = jnp.zeros((DIL_BLOCK, GROUP_W), F32)
        for h in range(N_HEADS):
            o_acc = jnp.where(in_head[h], o_all[h], o_acc)
            lse_map = jnp.where(in_head[h], lse[h], lse_map)
        o_ref[0, rows, :] = o_acc
        lse_ref[0, rows, :] = lse_map


def _dilated_pattern(p, q, k, v, dil_tb):
    G, L, _ = q.shape
    assert L % DIL_BLOCK == 0
    nb = L // DIL_BLOCK
    qb = math.gcd(nb, DIL_BLOCKS_PER_STEP)
    cur = pl.BlockSpec((1, qb * DIL_BLOCK, GROUP_W), lambda g, n: (g, n, 0))
    prev = pl.BlockSpec((1, DIL_BLOCK, GROUP_W), lambda g, n: (g, jnp.maximum(n * qb - 1, 0), 0))
    out = jax.ShapeDtypeStruct((G, L, GROUP_W), F32)
    return pl.pallas_call(
        _dil_kernel,
        out_shape=(out, out),
        grid=(G, nb // qb),
        in_specs=[cur, prev, cur, prev, cur,
                  pl.BlockSpec((1, N_HEADS, DIL_BLOCK, 2 * DIL_BLOCK), lambda g, n, p=p: (p, 0, 0, 0))],
        out_specs=(cur, cur),
        compiler_params=_compiler_params(("arbitrary", "arbitrary")),
        name=f"dilated_{DIL_PATTERNS[p][1]}",
    )(q, k, k, v, v, dil_tb)


def _dil_combine_kernel(o1_ref, l1_ref, o4_ref, l4_ref, o16_ref, l16_ref, out_ref, so4, sl4, so16, sl16):
    T = out_ref.shape[1]
    halves = GROUP_W // LANES
    for dil, srcs in ((4, ((o4_ref, so4), (l4_ref, sl4))), (16, ((o16_ref, so16), (l16_ref, sl16)))):
        for src, dst in srcs:
            for r in range(dil):
                for half in range(halves):
                    dst[half, pl.ds(r, T // dil, stride=dil), :] = src[0, r, :, half * LANES:(half + 1) * LANES]
    nat = lambda ref: jnp.concatenate([ref[half] for half in range(halves)], axis=1)
    l1, l4, l16 = l1_ref[0], nat(sl4), nat(sl16)
    m = jnp.maximum(jnp.maximum(l1, l4), l16)
    w1, w4, w16 = jnp.exp(l1 - m), jnp.exp(l4 - m), jnp.exp(l16 - m)
    tot = w1 + w4 + w16
    out_ref[0] = (w1 / tot) * o1_ref[0] + (w4 / tot) * nat(so4) + (w16 / tot) * nat(so16)


def _dil_combine(o1, l1, o4, l4, o16, l16):
    B, S, _ = o1.shape
    T = 256
    nat = pl.BlockSpec((1, T, GROUP_W), lambda b, t: (b, t, 0))
    r4 = pl.BlockSpec((1, 4, T // 4, GROUP_W), lambda b, t: (b, 0, t, 0))
    r16 = pl.BlockSpec((1, 16, T // 16, GROUP_W), lambda b, t: (b, 0, t, 0))
    return pl.pallas_call(
        _dil_combine_kernel,
        out_shape=jax.ShapeDtypeStruct((B, S, GROUP_W), F32),
        grid=(B, S // T),
        in_specs=[nat, nat, r4, r4, r16, r16],
        out_specs=nat,
        scratch_shapes=[pltpu.VMEM((GROUP_W // LANES, T, LANES), F32)] * 4,
        compiler_params=_compiler_params(("arbitrary", "arbitrary")),
        name="dil_combine",
    )(o1, l1, o4, l4, o16, l16)


def _sigmoid(x):
    return 1.0 / (1.0 + jnp.exp(-x))


def _merge_kernel(x_ref, fo_ref, mo_ref, do_ref, lo_ref, gate_ref, p_ref,
                  wout_ref, png_ref, wpg_ref, wpp_ref, o_ref):
    mix = jnp.concatenate([fo_ref[...], mo_ref[...], do_ref[...], lo_ref[...]], axis=1)
    g = gate_ref[...]
    y = jnp.dot((mix * (g * _sigmoid(g))).astype(BF16), wout_ref[...], preferred_element_type=F32)
    x1 = x_ref[...] + y
    hn = _row_norm(x1, png_ref[...]).astype(BF16)
    pg = _sigmoid(jnp.dot(hn, wpg_ref[...], preferred_element_type=F32))
    pp = jnp.dot(p_ref[0].astype(BF16), wpp_ref[...], preferred_element_type=F32)
    o_ref[...] = x1 + pg * pp


def _merge(layer, x2, fo, mo, do, lo, proj, p3, lp):
    rows = x2.shape[0]
    TM = TM_MERGE

    def rspec(w, c=0):
        return pl.BlockSpec((TM, w), lambda i, c=c: (i, c))

    def cspec(shape):
        return pl.BlockSpec(shape, lambda i: (0, 0))

    return pl.pallas_call(
        _merge_kernel,
        out_shape=jax.ShapeDtypeStruct((rows, D_MODEL), F32),
        grid=(rows // TM,),
        in_specs=[rspec(D_MODEL), rspec(GROUP_W), rspec(GROUP_W), rspec(GROUP_W), rspec(GROUP_W),
                  rspec(D_MODEL, COL_GATE // D_MODEL),
                  pl.BlockSpec((1, TM, PLE_DIM), lambda i: (layer, i, 0)),
                  cspec((D_MODEL, D_MODEL)), cspec((1, D_MODEL)), cspec((D_MODEL, D_MODEL)),
                  cspec((PLE_DIM, D_MODEL))],
        out_specs=rspec(D_MODEL),
        compiler_params=_compiler_params(("arbitrary",)),
        name="merge",
    )(x2, fo, mo, do, lo, proj, p3, lp["w_out"], lp["ple_norm_g"], lp["w_ple_gate"], lp["w_ple_proj"])


def _layer_params(i, ln_g, b_forget, qk_gain, mla_q_norm, mla_kv_norm, mla_nope_gain,
                  mla_rope_gain, w_uq, w_ukv, w_out, ple_norm_g, w_ple_gate, w_ple_proj):
    def rope_slab(x1_parts, x2_parts, like):
        pad = jnp.zeros(like.shape[:-1] + (LANES // 2 - 2 * MLA_HALF,), like.dtype)
        return jnp.concatenate(list(x1_parts) + [pad] + list(x2_parts) + [pad], axis=-1)

    per_head = MLA_NOPE + MLA_ROPE
    uq = w_uq[i]
    uq_nope = [uq[:, h * per_head:h * per_head + MLA_NOPE] for h in range(N_HEADS)]
    uq_x1 = [uq[:, h * per_head + MLA_NOPE:h * per_head + MLA_NOPE + MLA_HALF] for h in range(N_HEADS)]
    uq_x2 = [uq[:, h * per_head + MLA_NOPE + MLA_HALF:(h + 1) * per_head] for h in range(N_HEADS)]
    hpg = N_HEADS // MLA_GROUPS
    uq_cols = uq_nope + [rope_slab(uq_x1[g * hpg:(g + 1) * hpg], uq_x2[g * hpg:(g + 1) * hpg], uq)
                         for g in range(MLA_GROUPS)]
    ukv = w_ukv[i]
    ukv_cols = ([ukv[:, h * 2 * HEAD_DIM:h * 2 * HEAD_DIM + HEAD_DIM] for h in range(N_HEADS)]
                + [ukv[:, h * 2 * HEAD_DIM + HEAD_DIM:(h + 1) * 2 * HEAD_DIM] for h in range(N_HEADS)])
    rg = mla_rope_gain[i]
    rope_gain = rope_slab([rg[:, :MLA_HALF]] * hpg, [rg[:, MLA_HALF:]] * hpg, rg)
    return {
        "ln_g": ln_g[i][None, :],
        "gains": jnp.tile(qk_gain[i], (1, N_HEADS)),
        "b_forget": jnp.pad(b_forget[i], (0, LANES - N_HEADS))[None, :],
        "q_norm": mla_q_norm[i][None, :],
        "kv_norm": mla_kv_norm[i][None, :],
        "nope_gain": jnp.tile(mla_nope_gain[i], (1, N_HEADS)),
        "rope_gain": rope_gain,
        "w_uq": jnp.concatenate(uq_cols, axis=1).astype(BF16),
        "w_ukv": jnp.concatenate(ukv_cols, axis=1).astype(BF16),
        "w_out": w_out[i].astype(BF16),
        "ple_norm_g": ple_norm_g[i][None, :],
        "w_ple_gate": w_ple_gate[i].astype(BF16),
        "w_ple_proj": w_ple_proj[i].astype(BF16),
    }


def _constants(S):
    inv = 1.0 / (ROPE_THETA ** (jnp.arange(MLA_HALF, dtype=F32) * 2.0 / MLA_ROPE))
    ang = jnp.arange(S).astype(F32)[:, None] * inv[None, :]
    cos = jnp.tile(jnp.cos(ang), (1, 2 * N_HEADS))
    sin = jnp.tile(jnp.sin(ang), (1, 2 * N_HEADS))
    sign = np.concatenate([-np.ones(LANES // 2, np.float32), np.ones(LANES // 2, np.float32)])
    lane = np.arange(GROUP_W)
    g64 = (lane[:, None] // HEAD_DIM == lane[None, :] // HEAD_DIM).astype(np.float32) / HEAD_DIM
    within = np.arange(LANES) % (LANES // 2)
    rope_head = np.where(within < 2 * MLA_HALF, within // MLA_HALF, -1)
    g32 = ((rope_head[:, None] == rope_head[None, :]) & (rope_head[:, None] >= 0)).astype(np.float32) / MLA_ROPE
    tri = np.tril(np.ones((T_PREP, T_PREP), np.float32))
    ekr = np.zeros((LANES, LANES), np.float32)
    kr_lane0 = IN_OFF["kr"] % LANES
    for c in range(MLA_ROPE):
        half, idx = divmod(c, MLA_HALF)
        for copy in range(N_HEADS // MLA_GROUPS):
            ekr[kr_lane0 + c, half * (LANES // 2) + copy * MLA_HALF + idx] = 1.0
    return {"cos": cos, "sin": sin * sign[None, :], "g64": jnp.asarray(g64, BF16),
            "g32": jnp.asarray(g32, BF16), "tri": jnp.asarray(tri), "ekr": jnp.asarray(ekr, BF16)}


def kernel(x, p, ln_g, w_in, b_forget, qk_gain, mla_q_norm, mla_kv_norm, mla_nope_gain, mla_rope_gain,
           w_uq, w_ukv, w_out, rel_bias, ple_norm_g, w_ple_gate, w_ple_proj):
    B, S, _ = x.shape
    depth = p.shape[0]
    consts = _constants(S)
    moba_tb, dil_tb = _bias_tables(rel_bias)
    x2 = x.reshape(B * S, D_MODEL)
    for i in range(depth):
        lp = _layer_params(i, ln_g, b_forget, qk_gain, mla_q_norm, mla_kv_norm, mla_nope_gain,
                           mla_rope_gain, w_uq, w_ukv, w_out, ple_norm_g, w_ple_gate, w_ple_proj)
        proj = _in_projection(i, x2, lp["ln_g"], w_in, consts["ekr"])
        (fqT, fk, fvT, fck, fcrow, mqT, mk, mvT, mkmean, lqT, lk, lvT,
         dq1, dk1, dv1, dq4, dk4, dv4, dq16, dk16, dv16) = _prep(proj.reshape(B, S, PROJ_W), consts, lp)
        S_ = S
        fo = _dense_mixer(
            _fox_kernel, "fox", fqT, fk, fvT, (fck, fcrow),
            (pl.BlockSpec((1, N_HEADS, S_, LANES), lambda b, i: (b, 0, 0, 0)),
             pl.BlockSpec((1, 8, TQ), lambda b, i: (b, 0, i))))
        mo = _dense_mixer(
            _moba_kernel, "moba", mqT, mk, mvT, (mkmean, moba_tb),
            (pl.BlockSpec((1, S_ // MOBA_BLOCK, GROUP_W), lambda b, i: (b, 0, 0)),
             pl.BlockSpec((MOBA_TB_ENTRIES, N_HEADS, MOBA_BLOCK, MOBA_BLOCK), lambda b, i: (0, 0, 0, 0))),
            (pltpu.VMEM((N_HEADS, S_ // MOBA_BLOCK, TQ), F32),))
        lo = _dense_mixer(_mla_kernel, "mla", lqT, lk, lvT)
        flat = lambda a: a.reshape(-1, a.shape[-2], GROUP_W)
        o1, l1 = _dilated_pattern(0, dq1, dk1, dv1, dil_tb)
        o4, l4 = _dilated_pattern(1, flat(dq4), flat(dk4), flat(dv4), dil_tb)
        o16, l16 = _dilated_pattern(2, flat(dq16), flat(dk16), flat(dv16), dil_tb)
        do = _dil_combine(o1, l1, o4.reshape(B, 4, S // 4, GROUP_W), l4.reshape(B, 4, S // 4, GROUP_W),
                          o16.reshape(B, 16, S // 16, GROUP_W), l16.reshape(B, 16, S // 16, GROUP_W))
        rs = lambda a: a.reshape(B * S, GROUP_W)
        x2 = _merge(i, x2, rs(fo), rs(mo), rs(do), rs(lo), proj, p.reshape(depth, B * S, PLE_DIM), lp)
    return x2.reshape(B, S, D_MODEL)
```

```python
import functools
import math

import numpy as np
import jax
import jax.numpy as jnp
from jax import lax
from jax.experimental import pallas as pl
from jax.experimental.pallas import tpu as pltpu

F32 = jnp.float32
BF16 = jnp.bfloat16

D_MODEL = 1024
N_HEADS = 4
HEAD_DIM = 64
GROUP_W = N_HEADS * HEAD_DIM
MLA_NOPE = 64
MLA_ROPE = 32
MLA_HALF = MLA_ROPE // 2
MLA_QK_W = N_HEADS * (MLA_NOPE + MLA_ROPE)
MLA_Q_LORA = 256
MLA_KV_LORA = 128
PLE_DIM = 256
MOBA_BLOCK = 256
MOBA_TOPK = 3
DIL_PATTERNS = ((128, 1), (512, 4), (2048, 16))
DIL_BLOCK = 128
DIL_BLOCKS_PER_STEP = 4
N_BUCKETS = 32
MAX_DISTANCE = 2048
ROPE_THETA = 10000.0
EPS = 1e-6
NEG = -1e30
LOG2E = math.log2(math.e)

LANES = 128
VMEM_LIMIT_BYTES = 56 * 1024 * 1024

COL_FOX, COL_MOBA, COL_DIL = 0, 768, 1536
COL_CQ, COL_CKV, COL_KR, COL_FF, COL_GATE = 2304, 2560, 2688, 2816, 3072
PROJ_W = 4096
PROJ_TN = 512
STAGE_ROWS = 128

IN_OFF = {}
IN_W = 0
for _name, _size in (("fq", GROUP_W), ("fk", GROUP_W), ("fv", GROUP_W), ("ff", N_HEADS),
                     ("mq", GROUP_W), ("mk", GROUP_W), ("mv", GROUP_W),
                     ("dq", GROUP_W), ("dk", GROUP_W), ("dv", GROUP_W),
                     ("cq", MLA_Q_LORA), ("ckv", MLA_KV_LORA), ("kr", MLA_ROPE), ("gate", D_MODEL)):
    IN_OFF[_name] = IN_W
    IN_W += _size

TM_PROJ = 256
T_PREP = 256
TQ = 256
TK = 256
SWEEP_UNROLL = 4
MOBA_TB_ENTRIES = 8
TM_MERGE = 256
V_ROWS = HEAD_DIM + 16
MLA_GROUPS = 2


def _bucket_np(d):
    d = np.maximum(np.asarray(d, np.int64), 0)
    max_exact = N_BUCKETS // 2
    d_f = np.maximum(d, 1).astype(np.float64)
    val = np.log(d_f / max_exact) / math.log(MAX_DISTANCE / max_exact) * (N_BUCKETS - max_exact)
    frac = np.abs(val - np.round(val))
    on_edge = (frac < 1e-5) & (d > max_exact) & (val < N_BUCKETS - max_exact - 0.5)
    assert not on_edge.any(), "distance too close to a bucket edge for a static table"
    large = np.minimum(max_exact + np.floor(val + 1e-9).astype(np.int64), N_BUCKETS - 1)
    return np.where(d < max_exact, d, large)


def _bucket_steps(lo, hi):
    ds = np.arange(lo, hi + 1)
    bs = _bucket_np(ds)
    steps = [(int(ds[i]), int(bs[i])) for i in range(1, len(ds)) if bs[i] != bs[i - 1]]
    return int(bs[0]), steps


def _compiler_params(sem):
    return pltpu.CompilerParams(dimension_semantics=sem, vmem_limit_bytes=VMEM_LIMIT_BYTES)


def _bias_from_steps(d, rb_ref, col, lo, hi):
    b0, steps = _bucket_steps(lo, hi)
    val = jnp.full(d.shape, rb_ref[b0, col], F32)
    for t, b in steps:
        val = jnp.where(d >= t, rb_ref[b, col], val)
    return val


def _bias_tables_kernel(rb_ref, moba_ref, dil_ref):
    kl = lax.broadcasted_iota(jnp.int32, (MOBA_BLOCK, MOBA_BLOCK), 0)
    ql = lax.broadcasted_iota(jnp.int32, (MOBA_BLOCK, MOBA_BLOCK), 1)
    for e in range(MOBA_TB_ENTRIES - 1):
        d = jnp.maximum(e * MOBA_BLOCK + ql - kl, 0)
        lo = max(e * MOBA_BLOCK - (MOBA_BLOCK - 1), 0)
        hi = e * MOBA_BLOCK + (MOBA_BLOCK - 1)
        for h in range(N_HEADS):
            moba_ref[e, h] = _bias_from_steps(d, rb_ref, h, lo, hi) * LOG2E
    for h in range(N_HEADS):
        moba_ref[MOBA_TB_ENTRIES - 1, h] = jnp.full((MOBA_BLOCK, MOBA_BLOCK), rb_ref[N_BUCKETS - 1, h], F32) * LOG2E
    qi = lax.broadcasted_iota(jnp.int32, (DIL_BLOCK, 2 * DIL_BLOCK), 0)
    kj = lax.broadcasted_iota(jnp.int32, (DIL_BLOCK, 2 * DIL_BLOCK), 1)
    for p, (_, dil) in enumerate(DIL_PATTERNS):
        d = jnp.maximum((qi + DIL_BLOCK - kj) * dil, 0)
        for h in range(N_HEADS):
            dil_ref[p, h] = _bias_from_steps(d, rb_ref, N_HEADS + h, 0, (2 * DIL_BLOCK - 1) * dil)


def _bias_tables(rel_bias):
    assert _bucket_np((MOBA_TB_ENTRIES - 1) * MOBA_BLOCK - (MOBA_BLOCK - 1)) == N_BUCKETS - 1
    return pl.pallas_call(
        _bias_tables_kernel,
        out_shape=(jax.ShapeDtypeStruct((MOBA_TB_ENTRIES, N_HEADS, MOBA_BLOCK, MOBA_BLOCK), F32),
                   jax.ShapeDtypeStruct((len(DIL_PATTERNS), N_HEADS, DIL_BLOCK, 2 * DIL_BLOCK), F32)),
        in_specs=[pl.BlockSpec(memory_space=pltpu.SMEM)],
        out_specs=(pl.BlockSpec(memory_space=pltpu.VMEM), pl.BlockSpec(memory_space=pltpu.VMEM)),
        compiler_params=pltpu.CompilerParams(vmem_limit_bytes=VMEM_LIMIT_BYTES),
        name="bias_tables",
    )(rel_bias)


def _stage_in_weights(w_ref, ekr_ref, ws_ref):
    W = GROUP_W
    moves = ((COL_FOX, IN_OFF["fq"], 3 * W), (COL_MOBA, IN_OFF["mq"], 3 * W), (COL_DIL, IN_OFF["dq"], 3 * W),
             (COL_CQ, IN_OFF["cq"], MLA_Q_LORA), (COL_CKV, IN_OFF["ckv"], MLA_KV_LORA),
             (COL_GATE, IN_OFF["gate"], D_MODEL))
    kr_base = IN_OFF["kr"] // LANES * LANES
    assert IN_OFF["ff"] % LANES == 0 and IN_OFF["kr"] + MLA_ROPE <= kr_base + LANES
    lane = lax.broadcasted_iota(jnp.int32, (STAGE_ROWS, LANES), 1)
    for r0 in range(0, D_MODEL, STAGE_ROWS):
        rows = slice(r0, r0 + STAGE_ROWS)
        for dst, src, width in moves:
            ws_ref[rows, dst:dst + width] = w_ref[0, rows, src:src + width].astype(BF16)
        kr_win = w_ref[0, rows, kr_base:kr_base + LANES].astype(BF16)
        ws_ref[rows, COL_KR:COL_KR + LANES] = jnp.dot(kr_win, ekr_ref[...], preferred_element_type=F32).astype(BF16)
        ff_win = w_ref[0, rows, IN_OFF["ff"]:IN_OFF["ff"] + LANES]
        ws_ref[rows, COL_FF:COL_FF + LANES] = jnp.where(lane < N_HEADS, ff_win, 0.0).astype(BF16)
        ws_ref[rows, COL_FF + LANES:COL_GATE] = jnp.zeros((STAGE_ROWS, COL_GATE - COL_FF - LANES), BF16)


def _proj_kernel(x_ref, g_ref, w_ref, ekr_ref, o_ref, ws_ref):
    @pl.when(pl.program_id(0) == 0)
    def _():
        _stage_in_weights(w_ref, ekr_ref, ws_ref)

    x = x_ref[...]
    ms = jnp.mean(x * x, axis=-1, keepdims=True)
    h = (x * lax.rsqrt(ms + EPS) * g_ref[...]).astype(BF16)
    for c in range(PROJ_W // PROJ_TN):
        sl = slice(c * PROJ_TN, (c + 1) * PROJ_TN)
        o_ref[:, sl] = jnp.dot(h, ws_ref[:, sl], preferred_element_type=F32)


def _in_projection(layer, x2, ln_g, w_in, ekr):
    rows = x2.shape[0]
    return pl.pallas_call(
        _proj_kernel,
        out_shape=jax.ShapeDtypeStruct((rows, PROJ_W), F32),
        grid=(rows // TM_PROJ,),
        in_specs=[pl.BlockSpec((TM_PROJ, D_MODEL), lambda i: (i, 0)),
                  pl.BlockSpec((1, D_MODEL), lambda i: (0, 0)),
                  pl.BlockSpec((1, D_MODEL, IN_W), lambda i: (layer, 0, 0), pipeline_mode=pl.Buffered(1)),
                  pl.BlockSpec((LANES, LANES), lambda i: (0, 0))],
        out_specs=pl.BlockSpec((TM_PROJ, PROJ_W), lambda i: (i, 0)),
        scratch_shapes=[pltpu.VMEM((D_MODEL, PROJ_W), BF16)],
        compiler_params=_compiler_params(("arbitrary",)),
        name="in_projection",
    )(x2, ln_g, w_in, ekr)


def _group_mean_sq(x, g_mat):
    xx = x * x
    hi = xx.astype(BF16)
    lo = (xx - hi.astype(F32)).astype(BF16)
    return (jnp.dot(hi, g_mat, preferred_element_type=F32)
            + jnp.dot(lo, g_mat, preferred_element_type=F32))


def _group_norm(x, gain, g_mat):
    return x * lax.rsqrt(_group_mean_sq(x, g_mat) + EPS) * gain


def _row_norm(x, gain):
    return x * lax.rsqrt(jnp.mean(x * x, axis=-1, keepdims=True) + EPS) * gain


def _log_sigmoid(x):
    return -(jnp.maximum(-x, 0.0) + jnp.log1p(jnp.exp(-jnp.abs(x))))


def _prep_kernel(pf_ref, pm_ref, pd_ref, pcq_ref, pckv_ref, pkr_ref, pff_ref,
                 gains_ref, bfor_ref, qng_ref, kvng_ref, nopeg_ref, ropeg_ref,
                 wuq_ref, wukv_ref, cos_ref, sin_ref, g64_ref, g32_ref, tri_ref,
                 fqT_ref, fk_ref, fvT_ref, fck_ref, fcrow_ref,
                 mqT_ref, mk_ref, mvT_ref, mkmean_ref,
                 lqT_ref, lk_ref, lvT_ref,
                 dq1_ref, dk1_ref, dv1_ref, dq4_ref, dk4_ref, dv4_ref, dq16_ref, dk16_ref, dv16_ref,
                 carry_ref, sq_ref, sk_ref, sv_ref):
    t = pl.program_id(1)
    g64 = g64_ref[...]
    scale = HEAD_DIM ** -0.5
    W = GROUP_W
    ones_rows = jnp.ones((V_ROWS - HEAD_DIM, T_PREP), BF16)

    def store_vT(dst, v):
        vT = v.T.astype(BF16)
        for h in range(N_HEADS):
            dst[0, 0, h] = jnp.concatenate([vT[HEAD_DIM * h:HEAD_DIM * (h + 1)], ones_rows], axis=0)

    pf = pf_ref[0]
    fq = _group_norm(pf[:, :W], gains_ref[0:1, :], g64) * (scale * LOG2E)
    fk = _group_norm(pf[:, W:2 * W], gains_ref[1:2, :], g64)
    fqT_ref[0, 0] = fq.T.astype(BF16)
    fk_ref[0, 0] = fk.astype(BF16)
    store_vT(fvT_ref, pf[:, 2 * W:])

    @pl.when(t == 0)
    def _():
        carry_ref[...] = jnp.zeros_like(carry_ref)

    log_f = _log_sigmoid(pff_ref[0] + bfor_ref[...])
    c = jnp.dot(tri_ref[...], log_f, preferred_element_type=F32,
                precision=lax.Precision.HIGHEST) + carry_ref[...]
    carry_ref[...] = c[T_PREP - 1:T_PREP, :]
    c2 = c * LOG2E
    for h in range(N_HEADS):
        fck_ref[0, h] = jnp.broadcast_to(c2[:, h:h + 1], (T_PREP, LANES))
    fcrow_ref[0] = c2.T[0:8, :]

    pm = pm_ref[0]
    mq = _group_norm(pm[:, :W], gains_ref[2:3, :], g64) * (scale * LOG2E)
    mk = _group_norm(pm[:, W:2 * W], gains_ref[3:4, :], g64)
    mqT_ref[0, 0] = mq.T.astype(BF16)
    mk_ref[0, 0] = mk.astype(BF16)
    store_vT(mvT_ref, pm[:, 2 * W:])
    mkmean_ref[0, pl.ds(t, 1), :] = jnp.mean(mk, axis=0, keepdims=True)

    pd = pd_ref[0]
    dq = _group_norm(pd[:, :W], gains_ref[4:5, :], g64) * scale
    dk = _group_norm(pd[:, W:2 * W], gains_ref[5:6, :], g64)
    dv = pd[:, 2 * W:]
    dq1_ref[0] = dq.astype(BF16)
    dk1_ref[0] = dk.astype(BF16)
    dv1_ref[0] = dv.astype(BF16)
    for val, dst in ((dq, sq_ref), (dk, sk_ref), (dv, sv_ref)):
        for half in range(GROUP_W // LANES):
            dst[half] = val[:, half * LANES:(half + 1) * LANES]
    for dil, outs in ((4, (dq4_ref, dk4_ref, dv4_ref)), (16, (dq16_ref, dk16_ref, dv16_ref))):
        n = T_PREP // dil
        for r in range(dil):
            for src, dst in zip((sq_ref, sk_ref, sv_ref), outs):
                dst[0, r] = jnp.concatenate(
                    [src[half, pl.ds(r, n, stride=dil), :] for half in range(GROUP_W // LANES)],
                    axis=1).astype(BF16)

    cos = cos_ref[...]
    sin = sin_ref[...]
    qscale = (MLA_NOPE + MLA_ROPE) ** -0.5
    cqn = _row_norm(pcq_ref[0], qng_ref[...]).astype(BF16)
    rotate = lambda x: x * cos + pltpu.roll(x, LANES // 2, 1) * sin
    qf = jnp.dot(cqn, wuq_ref[...], preferred_element_type=F32)
    q_nope = _group_norm(qf[:, :W], nopeg_ref[0:1, :], g64) * (qscale * LOG2E)
    ckvn = _row_norm(pckv_ref[0], kvng_ref[...]).astype(BF16)
    kvf = jnp.dot(ckvn, wukv_ref[...], preferred_element_type=F32)
    k_nope = _group_norm(kvf[:, :W], nopeg_ref[1:2, :], g64)
    krs = pkr_ref[0]
    kr_ms = jnp.sum(krs * krs, axis=-1, keepdims=True) * (1.0 / (2 * MLA_ROPE))
    kr = rotate(krs * lax.rsqrt(kr_ms + EPS) * ropeg_ref[1:2, :])
    for g in range(MLA_GROUPS):
        slab = qf[:, W + g * LANES:W + (g + 1) * LANES]
        qr = rotate(_group_norm(slab, ropeg_ref[0:1, :], g32_ref[...])) * (qscale * LOG2E)
        lq = jnp.concatenate([q_nope[:, g * LANES:(g + 1) * LANES], qr], axis=1)
        lqT_ref[0, g] = lq.T.astype(BF16)
        lk_ref[0, g] = jnp.concatenate([k_nope[:, g * LANES:(g + 1) * LANES], kr], axis=1).astype(BF16)
    store_vT(lvT_ref, kvf[:, W:])


def _prep(proj3, consts, lp):
    B, S, _ = proj3.shape
    T = T_PREP
    NT = S // T
    assert S % T == 0 and S // MOBA_BLOCK == NT

    def pspec(width, col):
        return pl.BlockSpec((1, T, width), lambda b, t, c=col // width: (b, t, c))

    def cspec(shape):
        return pl.BlockSpec(shape, lambda b, t: (0,) * len(shape))

    in_specs = [pspec(768, COL_FOX), pspec(768, COL_MOBA), pspec(768, COL_DIL),
                pspec(256, COL_CQ), pspec(128, COL_CKV), pspec(128, COL_KR), pspec(128, COL_FF),
                cspec((6, GROUP_W)), cspec((1, LANES)), cspec((1, MLA_Q_LORA)), cspec((1, MLA_KV_LORA)),
                cspec((2, GROUP_W)), cspec((2, LANES)),
                cspec((MLA_Q_LORA, MLA_GROUPS * GROUP_W)), cspec((MLA_KV_LORA, 2 * GROUP_W)),
                pl.BlockSpec((T, LANES), lambda b, t: (t, 0)), pl.BlockSpec((T, LANES), lambda b, t: (t, 0)),
                cspec((GROUP_W, GROUP_W)), cspec((LANES, LANES)), cspec((T, T))]

    def qT(groups):
        return (jax.ShapeDtypeStruct((B, groups, GROUP_W, S), BF16),
                pl.BlockSpec((1, groups, GROUP_W, T), lambda b, t: (b, 0, 0, t)))

    def keys(groups):
        return (jax.ShapeDtypeStruct((B, groups, S, GROUP_W), BF16),
                pl.BlockSpec((1, groups, T, GROUP_W), lambda b, t: (b, 0, t, 0)))

    def rows(w):
        return jax.ShapeDtypeStruct((B, S, w), BF16), pl.BlockSpec((1, T, w), lambda b, t: (b, t, 0))

    def vT():
        return (jax.ShapeDtypeStruct((B, NT, N_HEADS, V_ROWS, T), BF16),
                pl.BlockSpec((1, 1, N_HEADS, V_ROWS, T), lambda b, t: (b, t, 0, 0, 0)))

    def resid(dil):
        return (jax.ShapeDtypeStruct((B, dil, S // dil, GROUP_W), BF16),
                pl.BlockSpec((1, dil, T // dil, GROUP_W), lambda b, t: (b, 0, t, 0)))

    outs = [qT(1), keys(1), vT(),
            (jax.ShapeDtypeStruct((B, N_HEADS, S, LANES), F32),
             pl.BlockSpec((1, N_HEADS, T, LANES), lambda b, t: (b, 0, t, 0))),
            (jax.ShapeDtypeStruct((B, 8, S), F32), pl.BlockSpec((1, 8, T), lambda b, t: (b, 0, t))),
            qT(1), keys(1), vT(),
            (jax.ShapeDtypeStruct((B, NT, GROUP_W), F32), pl.BlockSpec((1, NT, GROUP_W), lambda b, t: (b, 0, 0))),
            qT(MLA_GROUPS), keys(MLA_GROUPS), vT(),
            rows(GROUP_W), rows(GROUP_W), rows(GROUP_W),
            resid(4), resid(4), resid(4), resid(16), resid(16), resid(16)]
    return pl.pallas_call(
        _prep_kernel,
        out_shape=tuple(o[0] for o in outs),
        grid=(B, NT),
        in_specs=in_specs,
        out_specs=tuple(o[1] for o in outs),
        scratch_shapes=[pltpu.VMEM((1, LANES), F32)] + [pltpu.VMEM((GROUP_W // LANES, T, LANES), F32)] * 3,
        compiler_params=_compiler_params(("arbitrary", "arbitrary")),
        name="prep",
    )(proj3, proj3, proj3, proj3, proj3, proj3, proj3,
      lp["gains"], lp["b_forget"], lp["q_norm"], lp["kv_norm"], lp["nope_gain"], lp["rope_gain"],
      lp["w_uq"], lp["w_ukv"], consts["cos"], consts["sin"], consts["g64"], consts["g32"], consts["tri"])


def _stage_head_queries(qT_ref, qm_ref, groups):
    r = lax.broadcasted_iota(jnp.int32, (GROUP_W, TQ), 0)
    for h in range(N_HEADS):
        if groups == 1:
            g, mask = 0, (r >= HEAD_DIM * h) & (r < HEAD_DIM * (h + 1))
        else:
            g, hl = divmod(h, N_HEADS // groups)
            mask = (r >= HEAD_DIM * hl) & (r < HEAD_DIM * (hl + 1))
            for base in (2 * HEAD_DIM, 2 * HEAD_DIM + LANES // 2):
                lo = base + MLA_HALF * hl
                mask = mask | ((r >= lo) & (r < lo + MLA_HALF))
        q = qT_ref[0, g]
        qm_ref[h] = jnp.where(mask, q, jnp.zeros_like(q))


def _flash_sweep(i, groups, qm_ref, k_ref, vT_ref, m_ref, acc_ref, u_ref, st_ref, o_ref, score_fn):
    kpos = lax.broadcasted_iota(jnp.int32, (TK, TQ), 0)
    qpos = lax.broadcasted_iota(jnp.int32, (TK, TQ), 1)
    causal = kpos <= qpos
    m_ref[...] = jnp.full(m_ref.shape, NEG, F32)
    acc_ref[...] = jnp.zeros(acc_ref.shape, F32)

    def scores(j, slot, diagonal):
        row0 = pl.multiple_of(j * TK, TK)
        for h in range(N_HEADS):
            kt = k_ref[0, h // (N_HEADS // groups), pl.ds(row0, TK), :]
            s = jnp.dot(kt, qm_ref[h], preferred_element_type=F32)
            u, row, live = score_fn(h, s, j, diagonal)
            if diagonal:
                u = jnp.where(causal, u, NEG)
            u_ref[slot, h] = u
            m_old = m_ref[h]
            top = jnp.max(u, axis=0, keepdims=True)
            if row is not None:
                top = top + row
            if live is not None:
                top = jnp.where(live, top, NEG)
            m_new = jnp.maximum(m_old, top)
            shift = m_new if row is None else m_new - row
            if live is not None:
                shift = jnp.where(live, shift, -NEG)
            st_ref[slot, h] = shift
            st_ref[slot, N_HEADS + h] = jnp.exp2(m_old - m_new)
            m_ref[h] = m_new

    def values(j, slot):
        for h in range(N_HEADS):
            p = jnp.exp2((u_ref[slot, h] - st_ref[slot, h]).astype(BF16))
            acc_ref[h] = (st_ref[slot, N_HEADS + h] * acc_ref[h]
                          + jnp.dot(vT_ref[0, j, h], p, preferred_element_type=F32))

    scores(i, 0, True)

    def step(n, slot):
        scores(n - 1, slot, False)
        values(jnp.where(n == 1, i, n - 2), 1 - slot)

    def unrolled(t, carry):
        n = 1 + SWEEP_UNROLL * t
        for k in range(SWEEP_UNROLL):
            step(n + k, (1 + k) % 2)
        return carry

    n_unrolled = i // SWEEP_UNROLL
    lax.fori_loop(0, n_unrolled, unrolled, 0)
    left = i - SWEEP_UNROLL * n_unrolled
    for k in range(SWEEP_UNROLL - 1):
        def leftover(_, carry, k=k):
            step(1 + SWEEP_UNROLL * n_unrolled + k, (1 + k) % 2)
            return carry

        lax.fori_loop(0, (left > k).astype(jnp.int32), leftover, 0)
    values(jnp.maximum(i - 1, 0), lax.rem(i, 2))
    outs = []
    for h in range(N_HEADS):
        acc = acc_ref[h]
        outs.append(acc[:HEAD_DIM] / acc[HEAD_DIM:HEAD_DIM + 1])
    o_ref[0] = jnp.concatenate(outs, axis=0).T


def _fox_kernel(qT_ref, k_ref, vT_ref, ck_ref, crow_ref, o_ref, qm_ref, m_ref, acc_ref, u_ref, st_ref):
    i = pl.program_id(1)
    _stage_head_queries(qT_ref, qm_ref, 1)

    def score_fn(h, s, j, diagonal):
        ck = ck_ref[0, h, pl.ds(pl.multiple_of(j * TK, TK), TK), :]
        return s - jnp.concatenate([ck] * (TQ // LANES), axis=1), crow_ref[0, h:h + 1, :], None

    _flash_sweep(i, 1, qm_ref, k_ref, vT_ref, m_ref, acc_ref, u_ref, st_ref, o_ref, score_fn)


def _mla_kernel(qT_ref, k_ref, vT_ref, o_ref, qm_ref, m_ref, acc_ref, u_ref, st_ref):
    i = pl.program_id(1)
    _stage_head_queries(qT_ref, qm_ref, MLA_GROUPS)
    _flash_sweep(i, MLA_GROUPS, qm_ref, k_ref, vT_ref, m_ref, acc_ref, u_ref, st_ref, o_ref,
                 lambda h, s, j, diagonal: (s, None, None))


def _moba_kernel(qT_ref, k_ref, vT_ref, kmean_ref, tb_ref, o_ref, qm_ref, m_ref, acc_ref, u_ref, st_ref, sel_ref):
    i = pl.program_id(1)
    _stage_head_queries(qT_ref, qm_ref, 1)
    kmean = kmean_ref[0].astype(BF16)
    nb = kmean.shape[0]
    blk = lax.broadcasted_iota(jnp.int32, (nb, TQ), 0)
    for h in range(N_HEADS):
        gate = jnp.dot(kmean, qm_ref[h], preferred_element_type=F32)
        gate = jnp.where(blk < i, gate, NEG)
        rank = jnp.zeros((nb, TQ), jnp.int32)
        for jp in range(nb):
            row = gate[jp:jp + 1, :]
            ahead = (row > gate) | ((row == gate) & (jp < blk))
            rank = rank + ahead.astype(jnp.int32)
        sel_ref[h] = ((rank < MOBA_TOPK) & (blk < i)).astype(F32)

    def score_fn(h, s, j, diagonal):
        u = s + tb_ref[jnp.minimum(i - j, MOBA_TB_ENTRIES - 1), h]
        live = None if diagonal else sel_ref[h, pl.ds(j, 1), :] > 0.5
        return u, None, live

    _flash_sweep(i, 1, qm_ref, k_ref, vT_ref, m_ref, acc_ref, u_ref, st_ref, o_ref, score_fn)


def _dense_mixer(kernel_fn, name, qT, k, vT, extra_args=(), extra_specs=(), extra_scratch=()):
    B, groups, _, S = qT.shape
    NT = S // TK
    assert TQ == TK and S % TQ == 0
    in_specs = [pl.BlockSpec((1, groups, GROUP_W, TQ), lambda b, i: (b, 0, 0, i)),
                pl.BlockSpec((1, groups, S, GROUP_W), lambda b, i: (b, 0, 0, 0)),
                pl.BlockSpec((1, NT, N_HEADS, V_ROWS, TK), lambda b, i: (b, 0, 0, 0, 0))] + list(extra_specs)
    scratch = [pltpu.VMEM((N_HEADS, GROUP_W, TQ), BF16), pltpu.VMEM((N_HEADS, 1, TQ), F32),
               pltpu.VMEM((N_HEADS, V_ROWS, TQ), F32), pltpu.VMEM((2, N_HEADS, TK, TQ), F32), pltpu.VMEM((2, 2 * N_HEADS, 1, TQ), F32)] + list(extra_scratch)
    return pl.pallas_call(
        kernel_fn,
        out_shape=jax.ShapeDtypeStruct((B, S, GROUP_W), F32),
        grid=(B, S // TQ),
        in_specs=in_specs,
        out_specs=pl.BlockSpec((1, TQ, GROUP_W), lambda b, i: (b, i, 0)),
        scratch_shapes=scratch,
        compiler_params=_compiler_params(("arbitrary", "arbitrary")),
        name=name,
    )(qT, k, vT, *extra_args)


def _dil_kernel(q_ref, kp_ref, kc_ref, vp_ref, vc_ref, tb_ref, o_ref, lse_ref):
    n = pl.program_id(1)
    qb = q_ref.shape[1] // DIL_BLOCK
    k_all = jnp.concatenate([kp_ref[0], kc_ref[0]], axis=0)
    v_all = jnp.concatenate([vp_ref[0], vc_ref[0]], axis=0)
    qi = lax.broadcasted_iota(jnp.int32, (DIL_BLOCK, 2 * DIL_BLOCK), 0)
    kj = lax.broadcasted_iota(jnp.int32, (DIL_BLOCK, 2 * DIL_BLOCK), 1)
    rel = qi + DIL_BLOCK - kj
    in_band = (rel >= 0) & (rel <= DIL_BLOCK)
    qlane = lax.broadcasted_iota(jnp.int32, (DIL_BLOCK, GROUP_W), 1)
    in_head = [(qlane >= HEAD_DIM * h) & (qlane < HEAD_DIM * (h + 1)) for h in range(N_HEADS)]
    for c in range(qb):
        rows = slice(c * DIL_BLOCK, (c + 1) * DIL_BLOCK)
        q = q_ref[0, rows, :]
        kb = k_all[c * DIL_BLOCK:(c + 2) * DIL_BLOCK]
        vb = v_all[c * DIL_BLOCK:(c + 2) * DIL_BLOCK]
        valid = in_band & ((kj >= DIL_BLOCK) | (n > 0)) if c == 0 else in_band
        q_heads = jnp.concatenate([jnp.where(msk, q, jnp.zeros_like(q)) for msk in in_head], axis=0)
        s = lax.dot_general(q_heads, kb, (((1,), (1,)), ((), ())), preferred_element_type=F32)
        s = s.reshape(N_HEADS, DIL_BLOCK, 2 * DIL_BLOCK)
        s = jnp.where(valid[None], s + tb_ref[0], NEG)
        m = jnp.max(s, axis=-1, keepdims=True)
        e = jnp.exp(s - m)
        l = jnp.sum(e, axis=-1, keepdims=True)
        lse = m + jnp.log(l)
        pr = (e * (1.0 / l)).astype(BF16)
        o_all = jnp.dot(pr.reshape(N_HEADS * DIL_BLOCK, 2 * DIL_BLOCK), vb, preferred_element_type=F32)
        o_all = o_all.reshape(N_HEADS, DIL_BLOCK, GROUP_W)
        o_acc = jnp.zeros((DIL_BLOCK, GROUP_W), F32)
        lse_map = jnp.zeros((DIL_BLOCK, GROUP_W), F32)
        for h in range(N_HEADS):
            o_acc = jnp.where(in_head[h], o_all[h], o_acc)
            lse_map = jnp.where(in_head[h], lse[h], lse_map)
        o_ref[0, rows, :] = o_acc
        lse_ref[0, rows, :] = lse_map


def _dilated_pattern(p, q, k, v, dil_tb):
    G, L, _ = q.shape
    assert L % DIL_BLOCK == 0
    nb = L // DIL_BLOCK
    qb = math.gcd(nb, DIL_BLOCKS_PER_STEP)
    cur = pl.BlockSpec((1, qb * DIL_BLOCK, GROUP_W), lambda g, n: (g, n, 0))
    prev = pl.BlockSpec((1, DIL_BLOCK, GROUP_W), lambda g, n: (g, jnp.maximum(n * qb - 1, 0), 0))
    out = jax.ShapeDtypeStruct((G, L, GROUP_W), F32)
    return pl.pallas_call(
        _dil_kernel,
        out_shape=(out, out),
        grid=(G, nb // qb),
        in_specs=[cur, prev, cur, prev, cur,
                  pl.BlockSpec((1, N_HEADS, DIL_BLOCK, 2 * DIL_BLOCK), lambda g, n, p=p: (p, 0, 0, 0))],
        out_specs=(cur, cur),
        compiler_params=_compiler_params(("arbitrary", "arbitrary")),
        name=f"dilated_{DIL_PATTERNS[p][1]}",
    )(q, k, k, v, v, dil_tb)


def _dil_combine_kernel(o1_ref, l1_ref, o4_ref, l4_ref, o16_ref, l16_ref, out_ref, so4, sl4, so16, sl16):
    T = out_ref.shape[1]
    halves = GROUP_W // LANES
    for dil, srcs in ((4, ((o4_ref, so4), (l4_ref, sl4))), (16, ((o16_ref, so16), (l16_ref, sl16)))):
        for src, dst in srcs:
            for r in range(dil):
                for half in range(halves):
                    dst[half, pl.ds(r, T // dil, stride=dil), :] = src[0, r, :, half * LANES:(half + 1) * LANES]
    nat = lambda ref: jnp.concatenate([ref[half] for half in range(halves)], axis=1)
    l1, l4, l16 = l1_ref[0], nat(sl4), nat(sl16)
    m = jnp.maximum(jnp.maximum(l1, l4), l16)
    w1, w4, w16 = jnp.exp(l1 - m), jnp.exp(l4 - m), jnp.exp(l16 - m)
    tot = w1 + w4 + w16
    out_ref[0] = (w1 / tot) * o1_ref[0] + (w4 / tot) * nat(so4) + (w16 / tot) * nat(so16)


def _dil_combine(o1, l1, o4, l4, o16, l16):
    B, S, _ = o1.shape
    T = 256
    nat = pl.BlockSpec((1, T, GROUP_W), lambda b, t: (b, t, 0))
    r4 = pl.BlockSpec((1, 4, T // 4, GROUP_W), lambda b, t: (b, 0, t, 0))
    r16 = pl.BlockSpec((1, 16, T // 16, GROUP_W), lambda b, t: (b, 0, t, 0))
    return pl.pallas_call(
        _dil_combine_kernel,
        out_shape=jax.ShapeDtypeStruct((B, S, GROUP_W), F32),
        grid=(B, S // T),
        in_specs=[nat, nat, r4, r4, r16, r16],
        out_specs=nat,
        scratch_shapes=[pltpu.VMEM((GROUP_W // LANES, T, LANES), F32)] * 4,
        compiler_params=_compiler_params(("arbitrary", "arbitrary")),
        name="dil_combine",
    )(o1, l1, o4, l4, o16, l16)


def _sigmoid(x):
    return 1.0 / (1.0 + jnp.exp(-x))


def _merge_kernel(x_ref, fo_ref, mo_ref, do_ref, lo_ref, gate_ref, p_ref,
                  wout_ref, png_ref, wpg_ref, wpp_ref, o_ref):
    mix = jnp.concatenate([fo_ref[...], mo_ref[...], do_ref[...], lo_ref[...]], axis=1)
    g = gate_ref[...]
    y = jnp.dot((mix * (g * _sigmoid(g))).astype(BF16), wout_ref[...], preferred_element_type=F32)
    x1 = x_ref[...] + y
    hn = _row_norm(x1, png_ref[...]).astype(BF16)
    pg = _sigmoid(jnp.dot(hn, wpg_ref[...], preferred_element_type=F32))
    pp = jnp.dot(p_ref[0].astype(BF16), wpp_ref[...], preferred_element_type=F32)
    o_ref[...] = x1 + pg * pp


def _merge(layer, x2, fo, mo, do, lo, proj, p3, lp):
    rows = x2.shape[0]
    TM = TM_MERGE

    def rspec(w, c=0):
        return pl.BlockSpec((TM, w), lambda i, c=c: (i, c))

    def cspec(shape):
        return pl.BlockSpec(shape, lambda i: (0, 0))

    return pl.pallas_call(
        _merge_kernel,
        out_shape=jax.ShapeDtypeStruct((rows, D_MODEL), F32),
        grid=(rows // TM,),
        in_specs=[rspec(D_MODEL), rspec(GROUP_W), rspec(GROUP_W), rspec(GROUP_W), rspec(GROUP_W),
                  rspec(D_MODEL, COL_GATE // D_MODEL),
                  pl.BlockSpec((1, TM, PLE_DIM), lambda i: (layer, i, 0)),
                  cspec((D_MODEL, D_MODEL)), cspec((1, D_MODEL)), cspec((D_MODEL, D_MODEL)),
                  cspec((PLE_DIM, D_MODEL))],
        out_specs=rspec(D_MODEL),
        compiler_params=_compiler_params(("arbitrary",)),
        name="merge",
    )(x2, fo, mo, do, lo, proj, p3, lp["w_out"], lp["ple_norm_g"], lp["w_ple_gate"], lp["w_ple_proj"])


def _layer_params(i, ln_g, b_forget, qk_gain, mla_q_norm, mla_kv_norm, mla_nope_gain,
                  mla_rope_gain, w_uq, w_ukv, w_out, ple_norm_g, w_ple_gate, w_ple_proj):
    def rope_slab(x1_parts, x2_parts, like):
        pad = jnp.zeros(like.shape[:-1] + (LANES // 2 - 2 * MLA_HALF,), like.dtype)
        return jnp.concatenate(list(x1_parts) + [pad] + list(x2_parts) + [pad], axis=-1)

    per_head = MLA_NOPE + MLA_ROPE
    uq = w_uq[i]
    uq_nope = [uq[:, h * per_head:h * per_head + MLA_NOPE] for h in range(N_HEADS)]
    uq_x1 = [uq[:, h * per_head + MLA_NOPE:h * per_head + MLA_NOPE + MLA_HALF] for h in range(N_HEADS)]
    uq_x2 = [uq[:, h * per_head + MLA_NOPE + MLA_HALF:(h + 1) * per_head] for h in range(N_HEADS)]
    hpg = N_HEADS // MLA_GROUPS
    uq_cols = uq_nope + [rope_slab(uq_x1[g * hpg:(g + 1) * hpg], uq_x2[g * hpg:(g + 1) * hpg], uq)
                         for g in range(MLA_GROUPS)]
    ukv = w_ukv[i]
    ukv_cols = ([ukv[:, h * 2 * HEAD_DIM:h * 2 * HEAD_DIM + HEAD_DIM] for h in range(N_HEADS)]
                + [ukv[:, h * 2 * HEAD_DIM + HEAD_DIM:(h + 1) * 2 * HEAD_DIM] for h in range(N_HEADS)])
    rg = mla_rope_gain[i]
    rope_gain = rope_slab([rg[:, :MLA_HALF]] * hpg, [rg[:, MLA_HALF:]] * hpg, rg)
    return {
        "ln_g": ln_g[i][None, :],
        "gains": jnp.tile(qk_gain[i], (1, N_HEADS)),
        "b_forget": jnp.pad(b_forget[i], (0, LANES - N_HEADS))[None, :],
        "q_norm": mla_q_norm[i][None, :],
        "kv_norm": mla_kv_norm[i][None, :],
        "nope_gain": jnp.tile(mla_nope_gain[i], (1, N_HEADS)),
        "rope_gain": rope_gain,
        "w_uq": jnp.concatenate(uq_cols, axis=1).astype(BF16),
        "w_ukv": jnp.concatenate(ukv_cols, axis=1).astype(BF16),
        "w_out": w_out[i].astype(BF16),
        "ple_norm_g": ple_norm_g[i][None, :],
        "w_ple_gate": w_ple_gate[i].astype(BF16),
        "w_ple_proj": w_ple_proj[i].astype(BF16),
    }


def _constants(S):
    inv = 1.0 / (ROPE_THETA ** (jnp.arange(MLA_HALF, dtype=F32) * 2.0 / MLA_ROPE))
    ang = jnp.arange(S).astype(F32)[:, None] * inv[None, :]
    cos = jnp.tile(jnp.cos(ang), (1, 2 * N_HEADS))
    sin = jnp.tile(jnp.sin(ang), (1, 2 * N_HEADS))
    sign = np.concatenate([-np.ones(LANES // 2, np.float32), np.ones(LANES // 2, np.float32)])
    lane = np.arange(GROUP_W)
    g64 = (lane[:, None] // HEAD_DIM == lane[None, :] // HEAD_DIM).astype(np.float32) / HEAD_DIM
    within = np.arange(LANES) % (LANES // 2)
    rope_head = np.where(within < 2 * MLA_HALF, within // MLA_HALF, -1)
    g32 = ((rope_head[:, None] == rope_head[None, :]) & (rope_head[:, None] >= 0)).astype(np.float32) / MLA_ROPE
    tri = np.tril(np.ones((T_PREP, T_PREP), np.float32))
    ekr = np.zeros((LANES, LANES), np.float32)
    kr_lane0 = IN_OFF["kr"] % LANES
    for c in range(MLA_ROPE):
        half, idx = divmod(c, MLA_HALF)
        for copy in range(N_HEADS // MLA_GROUPS):
            ekr[kr_lane0 + c, half * (LANES // 2) + copy * MLA_HALF + idx] = 1.0
    return {"cos": cos, "sin": sin * sign[None, :], "g64": jnp.asarray(g64, BF16),
            "g32": jnp.asarray(g32, BF16), "tri": jnp.asarray(tri), "ekr": jnp.asarray(ekr, BF16)}


def kernel(x, p, ln_g, w_in, b_forget, qk_gain, mla_q_norm, mla_kv_norm, mla_nope_gain, mla_rope_gain,
           w_uq, w_ukv, w_out, rel_bias, ple_norm_g, w_ple_gate, w_ple_proj):
    B, S, _ = x.shape
    depth = p.shape[0]
    consts = _constants(S)
    moba_tb, dil_tb = _bias_tables(rel_bias)
    x2 = x.reshape(B * S, D_MODEL)
    for i in range(depth):
        lp = _layer_params(i, ln_g, b_forget, qk_gain, mla_q_norm, mla_kv_norm, mla_nope_gain,
                           mla_rope_gain, w_uq, w_ukv, w_out, ple_norm_g, w_ple_gate, w_ple_proj)
        proj = _in_projection(i, x2, lp["ln_g"], w_in, consts["ekr"])
        (fqT, fk, fvT, fck, fcrow, mqT, mk, mvT, mkmean, lqT, lk, lvT,
         dq1, dk1, dv1, dq4, dk4, dv4, dq16, dk16, dv16) = _prep(proj.reshape(B, S, PROJ_W), consts, lp)
        S_ = S
        fo = _dense_mixer(
            _fox_kernel, "fox", fqT, fk, fvT, (fck, fcrow),
            (pl.BlockSpec((1, N_HEADS, S_, LANES), lambda b, i: (b, 0, 0, 0)),
             pl.BlockSpec((1, 8, TQ), lambda b, i: (b, 0, i))))
        mo = _dense_mixer(
            _moba_kernel, "moba", mqT, mk, mvT, (mkmean, moba_tb),
            (pl.BlockSpec((1, S_ // MOBA_BLOCK, GROUP_W), lambda b, i: (b, 0, 0)),
             pl.BlockSpec((MOBA_TB_ENTRIES, N_HEADS, MOBA_BLOCK, MOBA_BLOCK), lambda b, i: (0, 0, 0, 0))),
            (pltpu.VMEM((N_HEADS, S_ // MOBA_BLOCK, TQ), F32),))
        lo = _dense_mixer(_mla_kernel, "mla", lqT, lk, lvT)
        flat = lambda a: a.reshape(-1, a.shape[-2], GROUP_W)
        o1, l1 = _dilated_pattern(0, dq1, dk1, dv1, dil_tb)
        o4, l4 = _dilated_pattern(1, flat(dq4), flat(dk4), flat(dv4), dil_tb)
        o16, l16 = _dilated_pattern(2, flat(dq16), flat(dk16), flat(dv16), dil_tb)
        do = _dil_combine(o1, l1, o4.reshape(B, 4, S // 4, GROUP_W), l4.reshape(B, 4, S // 4, GROUP_W),
                          o16.reshape(B, 16, S // 16, GROUP_W), l16.reshape(B, 16, S // 16, GROUP_W))
        rs = lambda a: a.reshape(B * S, GROUP_W)
        x2 = _merge(i, x2, rs(fo), rs(mo), rs(do), rs(lo), proj, p.reshape(depth, B * S, PLE_DIM), lp)
    return x2.reshape(B, S, D_MODEL)
```

```python
import functools
import math

import numpy as np
import jax
import jax.numpy as jnp
from jax import lax
from jax.experimental import pallas as pl
from jax.experimental.pallas import tpu as pltpu

F32 = jnp.float32
BF16 = jnp.bfloat16

D_MODEL = 1024
N_HEADS = 4
HEAD_DIM = 64
GROUP_W = N_HEADS * HEAD_DIM
MLA_NOPE = 64
MLA_ROPE = 32
MLA_HALF = MLA_ROPE // 2
MLA_QK_W = N_HEADS * (MLA_NOPE + MLA_ROPE)
MLA_Q_LORA = 256
MLA_KV_LORA = 128
PLE_DIM = 256
MOBA_BLOCK = 256
MOBA_TOPK = 3
DIL_PATTERNS = ((128, 1), (512, 4), (2048, 16))
DIL_BLOCK = 128
DIL_BLOCKS_PER_STEP = 4
N_BUCKETS = 32
MAX_DISTANCE = 2048
ROPE_THETA = 10000.0
EPS = 1e-6
NEG = -1e30
LOG2E = math.log2(math.e)

LANES = 128
VMEM_LIMIT_BYTES = 56 * 1024 * 1024

COL_FOX, COL_MOBA, COL_DIL = 0, 768, 1536
COL_CQ, COL_CKV, COL_KR, COL_FF, COL_GATE = 2304, 2560, 2688, 2816, 3072
PROJ_W = 4096
PROJ_TN = 512
STAGE_ROWS = 128

IN_OFF = {}
IN_W = 0
for _name, _size in (("fq", GROUP_W), ("fk", GROUP_W), ("fv", GROUP_W), ("ff", N_HEADS),
                     ("mq", GROUP_W), ("mk", GROUP_W), ("mv", GROUP_W),
                     ("dq", GROUP_W), ("dk", GROUP_W), ("dv", GROUP_W),
                     ("cq", MLA_Q_LORA), ("ckv", MLA_KV_LORA), ("kr", MLA_ROPE), ("gate", D_MODEL)):
    IN_OFF[_name] = IN_W
    IN_W += _size

TM_PROJ = 256
T_PREP = 256
TQ = 256
TK = 256
SWEEP_UNROLL = 4
MOBA_TB_ENTRIES = 8
TM_MERGE = 256
V_ROWS = HEAD_DIM + 16
MLA_GROUPS = 2


def _bucket_np(d):
    d = np.maximum(np.asarray(d, np.int64), 0)
    max_exact = N_BUCKETS // 2
    d_f = np.maximum(d, 1).astype(np.float64)
    val = np.log(d_f / max_exact) / math.log(MAX_DISTANCE / max_exact) * (N_BUCKETS - max_exact)
    frac = np.abs(val - np.round(val))
    on_edge = (frac < 1e-5) & (d > max_exact) & (val < N_BUCKETS - max_exact - 0.5)
    assert not on_edge.any(), "distance too close to a bucket edge for a static table"
    large = np.minimum(max_exact + np.floor(val + 1e-9).astype(np.int64), N_BUCKETS - 1)
    return np.where(d < max_exact, d, large)


def _bucket_steps(lo, hi):
    ds = np.arange(lo, hi + 1)
    bs = _bucket_np(ds)
    steps = [(int(ds[i]), int(bs[i])) for i in range(1, len(ds)) if bs[i] != bs[i - 1]]
    return int(bs[0]), steps


def _compiler_params(sem):
    return pltpu.CompilerParams(dimension_semantics=sem, vmem_limit_bytes=VMEM_LIMIT_BYTES)


def _bias_from_steps(d, rb_ref, col, lo, hi):
    b0, steps = _bucket_steps(lo, hi)
    val = jnp.full(d.shape, rb_ref[b0, col], F32)
    for t, b in steps:
        val = jnp.where(d >= t, rb_ref[b, col], val)
    return val


def _bias_tables_kernel(rb_ref, moba_ref, dil_ref):
    kl = lax.broadcasted_iota(jnp.int32, (MOBA_BLOCK, MOBA_BLOCK), 0)
    ql = lax.broadcasted_iota(jnp.int32, (MOBA_BLOCK, MOBA_BLOCK), 1)
    for e in range(MOBA_TB_ENTRIES - 1):
        d = jnp.maximum(e * MOBA_BLOCK + ql - kl, 0)
        lo = max(e * MOBA_BLOCK - (MOBA_BLOCK - 1), 0)
        hi = e * MOBA_BLOCK + (MOBA_BLOCK - 1)
        for h in range(N_HEADS):
            moba_ref[e, h] = _bias_from_steps(d, rb_ref, h, lo, hi) * LOG2E
    for h in range(N_HEADS):
        moba_ref[MOBA_TB_ENTRIES - 1, h] = jnp.full((MOBA_BLOCK, MOBA_BLOCK), rb_ref[N_BUCKETS - 1, h], F32) * LOG2E
    qi = lax.broadcasted_iota(jnp.int32, (DIL_BLOCK, 2 * DIL_BLOCK), 0)
    kj = lax.broadcasted_iota(jnp.int32, (DIL_BLOCK, 2 * DIL_BLOCK), 1)
    for p, (_, dil) in enumerate(DIL_PATTERNS):
        d = jnp.maximum((qi + DIL_BLOCK - kj) * dil, 0)
        for h in range(N_HEADS):
            dil_ref[p, h] = _bias_from_steps(d, rb_ref, N_HEADS + h, 0, (2 * DIL_BLOCK - 1) * dil)


def _bias_tables(rel_bias):
    assert _bucket_np((MOBA_TB_ENTRIES - 1) * MOBA_BLOCK - (MOBA_BLOCK - 1)) == N_BUCKETS - 1
    return pl.pallas_call(
        _bias_tables_kernel,
        out_shape=(jax.ShapeDtypeStruct((MOBA_TB_ENTRIES, N_HEADS, MOBA_BLOCK, MOBA_BLOCK), F32),
                   jax.ShapeDtypeStruct((len(DIL_PATTERNS), N_HEADS, DIL_BLOCK, 2 * DIL_BLOCK), F32)),
        in_specs=[pl.BlockSpec(memory_space=pltpu.SMEM)],
        out_specs=(pl.BlockSpec(memory_space=pltpu.VMEM), pl.BlockSpec(memory_space=pltpu.VMEM)),
        compiler_params=pltpu.CompilerParams(vmem_limit_bytes=VMEM_LIMIT_BYTES),
        name="bias_tables",
    )(rel_bias)


def _stage_in_weights(w_ref, ekr_ref, ws_ref):
    W = GROUP_W
    moves = ((COL_FOX, IN_OFF["fq"], 3 * W), (COL_MOBA, IN_OFF["mq"], 3 * W), (COL_DIL, IN_OFF["dq"], 3 * W),
             (COL_CQ, IN_OFF["cq"], MLA_Q_LORA), (COL_CKV, IN_OFF["ckv"], MLA_KV_LORA),
             (COL_GATE, IN_OFF["gate"], D_MODEL))
    kr_base = IN_OFF["kr"] // LANES * LANES
    assert IN_OFF["ff"] % LANES == 0 and IN_OFF["kr"] + MLA_ROPE <= kr_base + LANES
    lane = lax.broadcasted_iota(jnp.int32, (STAGE_ROWS, LANES), 1)
    for r0 in range(0, D_MODEL, STAGE_ROWS):
        rows = slice(r0, r0 + STAGE_ROWS)
        for dst, src, width in moves:
            ws_ref[rows, dst:dst + width] = w_ref[0, rows, src:src + width].astype(BF16)
        kr_win = w_ref[0, rows, kr_base:kr_base + LANES].astype(BF16)
        ws_ref[rows, COL_KR:COL_KR + LANES] = jnp.dot(kr_win, ekr_ref[...], preferred_element_type=F32).astype(BF16)
        ff_win = w_ref[0, rows, IN_OFF["ff"]:IN_OFF["ff"] + LANES]
        ws_ref[rows, COL_FF:COL_FF + LANES] = jnp.where(lane < N_HEADS, ff_win, 0.0).astype(BF16)
        ws_ref[rows, COL_FF + LANES:COL_GATE] = jnp.zeros((STAGE_ROWS, COL_GATE - COL_FF - LANES), BF16)


def _proj_kernel(x_ref, g_ref, w_ref, ekr_ref, o_ref, ff_ref, ws_ref):
    @pl.when(pl.program_id(0) == 0)
    def _():
        _stage_in_weights(w_ref, ekr_ref, ws_ref)

    x = x_ref[...]
    ms = jnp.mean(x * x, axis=-1, keepdims=True)
    h = (x * lax.rsqrt(ms + EPS) * g_ref[...]).astype(BF16)
    for c in range(PROJ_W // PROJ_TN):
        sl = slice(c * PROJ_TN, (c + 1) * PROJ_TN)
        o_ref[:, sl] = jnp.dot(h, ws_ref[:, sl], preferred_element_type=F32).astype(BF16)
    ff_ref[...] = jnp.dot(h, ws_ref[:, COL_FF:COL_FF + LANES], preferred_element_type=F32)


def _in_projection(layer, x2, ln_g, w_in, ekr):
    rows = x2.shape[0]
    return pl.pallas_call(
        _proj_kernel,
        out_shape=(jax.ShapeDtypeStruct((rows, PROJ_W), BF16), jax.ShapeDtypeStruct((rows, LANES), F32)),
        grid=(rows // TM_PROJ,),
        in_specs=[pl.BlockSpec((TM_PROJ, D_MODEL), lambda i: (i, 0)),
                  pl.BlockSpec((1, D_MODEL), lambda i: (0, 0)),
                  pl.BlockSpec((1, D_MODEL, IN_W), lambda i: (layer, 0, 0), pipeline_mode=pl.Buffered(1)),
                  pl.BlockSpec((LANES, LANES), lambda i: (0, 0))],
        out_specs=(pl.BlockSpec((TM_PROJ, PROJ_W), lambda i: (i, 0)),
                   pl.BlockSpec((TM_PROJ, LANES), lambda i: (i, 0))),
        scratch_shapes=[pltpu.VMEM((D_MODEL, PROJ_W), BF16)],
        compiler_params=_compiler_params(("arbitrary",)),
        name="in_projection",
    )(x2, ln_g, w_in, ekr)


def _group_mean_sq(x, g_mat):
    return jnp.dot((x * x).astype(BF16), g_mat, preferred_element_type=F32)


def _group_norm(x, gain, g_mat):
    return x * lax.rsqrt(_group_mean_sq(x, g_mat) + EPS) * gain


def _row_norm(x, gain):
    return x * lax.rsqrt(jnp.mean(x * x, axis=-1, keepdims=True) + EPS) * gain


def _log_sigmoid(x):
    return -(jnp.maximum(-x, 0.0) + jnp.log1p(jnp.exp(-jnp.abs(x))))


def _prep_kernel(pf_ref, pm_ref, pd_ref, pcq_ref, pckv_ref, pkr_ref, pff_ref,
                 gains_ref, bfor_ref, qng_ref, kvng_ref, nopeg_ref, ropeg_ref,
                 wuq_ref, wukv_ref, cos_ref, sin_ref, g64_ref, g32_ref, tri_ref,
                 fqT_ref, fk_ref, fvT_ref, fck_ref, fcrow_ref,
                 mqT_ref, mk_ref, mvT_ref, mkmean_ref,
                 lqT_ref, lk_ref, lvT_ref,
                 dq_ref, dk_ref, dv_ref, carry_ref):
    t = pl.program_id(1)
    g64 = g64_ref[...]
    scale = HEAD_DIM ** -0.5
    W = GROUP_W
    ones_rows = jnp.ones((V_ROWS - HEAD_DIM, T_PREP), BF16)

    def store_vT(dst, v):
        vT = v.T.astype(BF16)
        for h in range(N_HEADS):
            dst[0, 0, h] = jnp.concatenate([vT[HEAD_DIM * h:HEAD_DIM * (h + 1)], ones_rows], axis=0)

    pf = pf_ref[0].astype(F32)
    fq = _group_norm(pf[:, :W], gains_ref[0:1, :], g64) * (scale * LOG2E)
    fk = _group_norm(pf[:, W:2 * W], gains_ref[1:2, :], g64)
    fqT_ref[0, 0] = fq.T.astype(BF16)
    fk_ref[0, 0] = fk.astype(BF16)
    store_vT(fvT_ref, pf[:, 2 * W:])

    @pl.when(t == 0)
    def _():
        carry_ref[...] = jnp.zeros_like(carry_ref)

    log_f = _log_sigmoid(pff_ref[0] + bfor_ref[...])
    tri = tri_ref[...]
    c = carry_ref[...]
    rest = log_f
    for _ in range(3):
        part = rest.astype(BF16)
        c = c + jnp.dot(tri, part, preferred_element_type=F32)
        rest = rest - part.astype(F32)
    carry_ref[...] = c[T_PREP - 1:T_PREP, :]
    c2 = c * LOG2E
    for h in range(N_HEADS):
        fck_ref[0, h] = jnp.broadcast_to(c2[:, h:h + 1], (T_PREP, LANES))
    fcrow_ref[0] = c2.T[0:8, :]

    pm = pm_ref[0].astype(F32)
    mq = _group_norm(pm[:, :W], gains_ref[2:3, :], g64) * (scale * LOG2E)
    mk = _group_norm(pm[:, W:2 * W], gains_ref[3:4, :], g64)
    mqT_ref[0, 0] = mq.T.astype(BF16)
    mk_ref[0, 0] = mk.astype(BF16)
    store_vT(mvT_ref, pm[:, 2 * W:])
    mkmean_ref[0, pl.ds(t, 1), :] = jnp.mean(mk, axis=0, keepdims=True)

    pd = pd_ref[0].astype(F32)
    dq_ref[0] = (_group_norm(pd[:, :W], gains_ref[4:5, :], g64) * scale).astype(BF16)
    dk_ref[0] = _group_norm(pd[:, W:2 * W], gains_ref[5:6, :], g64).astype(BF16)
    dv_ref[0] = pd_ref[0, :, 2 * W:]

    cos = cos_ref[...]
    sin = sin_ref[...]
    qscale = (MLA_NOPE + MLA_ROPE) ** -0.5
    cqn = _row_norm(pcq_ref[0].astype(F32), qng_ref[...]).astype(BF16)
    rotate = lambda x: x * cos + pltpu.roll(x, LANES // 2, 1) * sin
    qf = jnp.dot(cqn, wuq_ref[...], preferred_element_type=F32)
    q_nope = _group_norm(qf[:, :W], nopeg_ref[0:1, :], g64) * (qscale * LOG2E)
    ckvn = _row_norm(pckv_ref[0].astype(F32), kvng_ref[...]).astype(BF16)
    kvf = jnp.dot(ckvn, wukv_ref[...], preferred_element_type=F32)
    k_nope = _group_norm(kvf[:, :W], nopeg_ref[1:2, :], g64)
    krs = pkr_ref[0].astype(F32)
    kr_ms = jnp.sum(krs * krs, axis=-1, keepdims=True) * (1.0 / (2 * MLA_ROPE))
    kr = rotate(krs * lax.rsqrt(kr_ms + EPS) * ropeg_ref[1:2, :])
    for g in range(MLA_GROUPS):
        slab = qf[:, W + g * LANES:W + (g + 1) * LANES]
        qr = rotate(_group_norm(slab, ropeg_ref[0:1, :], g32_ref[...])) * (qscale * LOG2E)
        lq = jnp.concatenate([q_nope[:, g * LANES:(g + 1) * LANES], qr], axis=1)
        lqT_ref[0, g] = lq.T.astype(BF16)
        lk_ref[0, g] = jnp.concatenate([k_nope[:, g * LANES:(g + 1) * LANES], kr], axis=1).astype(BF16)
    store_vT(lvT_ref, kvf[:, W:])


def _prep(proj3, ff3, consts, lp):
    B, S, _ = proj3.shape
    T = T_PREP
    NT = S // T
    assert S % T == 0 and S // MOBA_BLOCK == NT

    def pspec(width, col):
        return pl.BlockSpec((1, T, width), lambda b, t, c=col // width: (b, t, c))

    def cspec(shape):
        return pl.BlockSpec(shape, lambda b, t: (0,) * len(shape))

    in_specs = [pspec(768, COL_FOX), pspec(768, COL_MOBA), pspec(768, COL_DIL),
                pspec(256, COL_CQ), pspec(128, COL_CKV), pspec(128, COL_KR), pspec(128, 0),
                cspec((6, GROUP_W)), cspec((1, LANES)), cspec((1, MLA_Q_LORA)), cspec((1, MLA_KV_LORA)),
                cspec((2, GROUP_W)), cspec((2, LANES)),
                cspec((MLA_Q_LORA, MLA_GROUPS * GROUP_W)), cspec((MLA_KV_LORA, 2 * GROUP_W)),
                pl.BlockSpec((T, LANES), lambda b, t: (t, 0)), pl.BlockSpec((T, LANES), lambda b, t: (t, 0)),
                cspec((GROUP_W, GROUP_W)), cspec((LANES, LANES)), cspec((T, T))]

    def qT(groups):
        return (jax.ShapeDtypeStruct((B, groups, GROUP_W, S), BF16),
                pl.BlockSpec((1, groups, GROUP_W, T), lambda b, t: (b, 0, 0, t)))

    def keys(groups):
        return (jax.ShapeDtypeStruct((B, groups, S, GROUP_W), BF16),
                pl.BlockSpec((1, groups, T, GROUP_W), lambda b, t: (b, 0, t, 0)))

    def rows(w):
        return jax.ShapeDtypeStruct((B, S, w), BF16), pl.BlockSpec((1, T, w), lambda b, t: (b, t, 0))

    def vT():
        return (jax.ShapeDtypeStruct((B, NT, N_HEADS, V_ROWS, T), BF16),
                pl.BlockSpec((1, 1, N_HEADS, V_ROWS, T), lambda b, t: (b, t, 0, 0, 0)))

    outs = [qT(1), keys(1), vT(),
            (jax.ShapeDtypeStruct((B, N_HEADS, S, LANES), F32),
             pl.BlockSpec((1, N_HEADS, T, LANES), lambda b, t: (b, 0, t, 0))),
            (jax.ShapeDtypeStruct((B, 8, S), F32), pl.BlockSpec((1, 8, T), lambda b, t: (b, 0, t))),
            qT(1), keys(1), vT(),
            (jax.ShapeDtypeStruct((B, NT, GROUP_W), F32), pl.BlockSpec((1, NT, GROUP_W), lambda b, t: (b, 0, 0))),
            qT(MLA_GROUPS), keys(MLA_GROUPS), vT(),
            rows(GROUP_W), rows(GROUP_W), rows(GROUP_W)]
    return pl.pallas_call(
        _prep_kernel,
        out_shape=tuple(o[0] for o in outs),
        grid=(B, NT),
        in_specs=in_specs,
        out_specs=tuple(o[1] for o in outs),
        scratch_shapes=[pltpu.VMEM((1, LANES), F32)],
        compiler_params=_compiler_params(("arbitrary", "arbitrary")),
        name="prep",
    )(proj3, proj3, proj3, proj3, proj3, proj3, ff3,
      lp["gains"], lp["b_forget"], lp["q_norm"], lp["kv_norm"], lp["nope_gain"], lp["rope_gain"],
      lp["w_uq"], lp["w_ukv"], consts["cos"], consts["sin"], consts["g64"], consts["g32"], consts["tri"])


def _stage_head_queries(qT_ref, qm_ref, groups):
    r = lax.broadcasted_iota(jnp.int32, (GROUP_W, TQ), 0)
    for h in range(N_HEADS):
        if groups == 1:
            g, mask = 0, (r >= HEAD_DIM * h) & (r < HEAD_DIM * (h + 1))
        else:
            g, hl = divmod(h, N_HEADS // groups)
            mask = (r >= HEAD_DIM * hl) & (r < HEAD_DIM * (hl + 1))
            for base in (2 * HEAD_DIM, 2 * HEAD_DIM + LANES // 2):
                lo = base + MLA_HALF * hl
                mask = mask | ((r >= lo) & (r < lo + MLA_HALF))
        q = qT_ref[0, g]
        qm_ref[h] = jnp.where(mask, q, jnp.zeros_like(q))


def _flash_sweep(i, groups, qm_ref, k_ref, vT_ref, m_ref, acc_ref, u_ref, st_ref, o_ref, score_fn):
    kpos = lax.broadcasted_iota(jnp.int32, (TK, TQ), 0)
    qpos = lax.broadcasted_iota(jnp.int32, (TK, TQ), 1)
    causal = kpos <= qpos
    m_ref[...] = jnp.full(m_ref.shape, NEG, F32)
    acc_ref[...] = jnp.zeros(acc_ref.shape, F32)

    def scores(j, slot, diagonal):
        row0 = pl.multiple_of(j * TK, TK)
        for h in range(N_HEADS):
            kt = k_ref[0, h // (N_HEADS // groups), pl.ds(row0, TK), :]
            s = jnp.dot(kt, qm_ref[h], preferred_element_type=F32)
            u, row, live = score_fn(h, s, j, diagonal)
            if diagonal:
                u = jnp.where(causal, u, NEG)
            u_ref[slot, h] = u
            m_old = m_ref[h]
            top = jnp.max(u, axis=0, keepdims=True)
            if row is not None:
                top = top + row
            if live is not None:
                top = jnp.where(live, top, NEG)
            m_new = jnp.maximum(m_old, top)
            shift = m_new if row is None else m_new - row
            if live is not None:
                shift = jnp.where(live, shift, -NEG)
            st_ref[slot, h] = shift
            st_ref[slot, N_HEADS + h] = jnp.exp2(m_old - m_new)
            m_ref[h] = m_new

    def values(j, slot):
        for h in range(N_HEADS):
            p = jnp.exp2((u_ref[slot, h] - st_ref[slot, h]).astype(BF16))
            acc_ref[h] = (st_ref[slot, N_HEADS + h] * acc_ref[h]
                          + jnp.dot(vT_ref[0, j, h], p, preferred_element_type=F32))

    scores(i, 0, True)

    def step(n, slot):
        scores(n - 1, slot, False)
        values(jnp.where(n == 1, i, n - 2), 1 - slot)

    def unrolled(t, carry):
        n = 1 + SWEEP_UNROLL * t
        for k in range(SWEEP_UNROLL):
            step(n + k, (1 + k) % 2)
        return carry

    n_unrolled = i // SWEEP_UNROLL
    lax.fori_loop(0, n_unrolled, unrolled, 0)
    left = i - SWEEP_UNROLL * n_unrolled
    for k in range(SWEEP_UNROLL - 1):
        def leftover(_, carry, k=k):
            step(1 + SWEEP_UNROLL * n_unrolled + k, (1 + k) % 2)
            return carry

        lax.fori_loop(0, (left > k).astype(jnp.int32), leftover, 0)
    values(jnp.maximum(i - 1, 0), lax.rem(i, 2))
    outs = []
    for h in range(N_HEADS):
        acc = acc_ref[h]
        outs.append(acc[:HEAD_DIM] / acc[HEAD_DIM:HEAD_DIM + 1])
    o_ref[0] = jnp.concatenate(outs, axis=0).T.astype(BF16)


def _fox_kernel(qT_ref, k_ref, vT_ref, ck_ref, crow_ref, o_ref, qm_ref, m_ref, acc_ref, u_ref, st_ref):
    i = pl.program_id(1)
    _stage_head_queries(qT_ref, qm_ref, 1)

    def score_fn(h, s, j, diagonal):
        ck = ck_ref[0, h, pl.ds(pl.multiple_of(j * TK, TK), TK), :]
        return s - jnp.concatenate([ck] * (TQ // LANES), axis=1), crow_ref[0, h:h + 1, :], None

    _flash_sweep(i, 1, qm_ref, k_ref, vT_ref, m_ref, acc_ref, u_ref, st_ref, o_ref, score_fn)


def _mla_kernel(qT_ref, k_ref, vT_ref, o_ref, qm_ref, m_ref, acc_ref, u_ref, st_ref):
    i = pl.program_id(1)
    _stage_head_queries(qT_ref, qm_ref, MLA_GROUPS)
    _flash_sweep(i, MLA_GROUPS, qm_ref, k_ref, vT_ref, m_ref, acc_ref, u_ref, st_ref, o_ref,
                 lambda h, s, j, diagonal: (s, None, None))


def _moba_kernel(qT_ref, k_ref, vT_ref, kmean_ref, tb_ref, o_ref, qm_ref, m_ref, acc_ref, u_ref, st_ref, sel_ref):
    i = pl.program_id(1)
    _stage_head_queries(qT_ref, qm_ref, 1)
    kmean = kmean_ref[0].astype(BF16)
    nb = kmean.shape[0]
    blk = lax.broadcasted_iota(jnp.int32, (nb, TQ), 0)
    for h in range(N_HEADS):
        gate = jnp.dot(kmean, qm_ref[h], preferred_element_type=F32)
        gate = jnp.where(blk < i, gate, NEG)
        rank = jnp.zeros((nb, TQ), jnp.int32)
        for jp in range(nb):
            row = gate[jp:jp + 1, :]
            ahead = (row > gate) | ((row == gate) & (jp < blk))
            rank = rank + ahead.astype(jnp.int32)
        sel_ref[h] = ((rank < MOBA_TOPK) & (blk < i)).astype(F32)

    def score_fn(h, s, j, diagonal):
        u = s + tb_ref[jnp.minimum(i - j, MOBA_TB_ENTRIES - 1), h]
        live = None if diagonal else sel_ref[h, pl.ds(j, 1), :] > 0.5
        return u, None, live

    _flash_sweep(i, 1, qm_ref, k_ref, vT_ref, m_ref, acc_ref, u_ref, st_ref, o_ref, score_fn)


def _dense_mixer(kernel_fn, name, qT, k, vT, extra_args=(), extra_specs=(), extra_scratch=()):
    B, groups, _, S = qT.shape
    NT = S // TK
    assert TQ == TK and S % TQ == 0
    in_specs = [pl.BlockSpec((1, groups, GROUP_W, TQ), lambda b, i: (b, 0, 0, i)),
                pl.BlockSpec((1, groups, S, GROUP_W), lambda b, i: (b, 0, 0, 0)),
                pl.BlockSpec((1, NT, N_HEADS, V_ROWS, TK), lambda b, i: (b, 0, 0, 0, 0))] + list(extra_specs)
    scratch = [pltpu.VMEM((N_HEADS, GROUP_W, TQ), BF16), pltpu.VMEM((N_HEADS, 1, TQ), F32),
               pltpu.VMEM((N_HEADS, V_ROWS, TQ), F32), pltpu.VMEM((2, N_HEADS, TK, TQ), F32), pltpu.VMEM((2, 2 * N_HEADS, 1, TQ), F32)] + list(extra_scratch)
    return pl.pallas_call(
        kernel_fn,
        out_shape=jax.ShapeDtypeStruct((B, S, GROUP_W), BF16),
        grid=(B, S // TQ),
        in_specs=in_specs,
        out_specs=pl.BlockSpec((1, TQ, GROUP_W), lambda b, i: (b, i, 0)),
        scratch_shapes=scratch,
        compiler_params=_compiler_params(("arbitrary", "arbitrary")),
        name=name,
    )(qT, k, vT, *extra_args)


def _dil_kernel(q_ref, kp_ref, kc_ref, vp_ref, vc_ref, tb_ref, o_ref, lse_ref):
    n = pl.program_id(2)
    qb = q_ref.shape[1] // DIL_BLOCK
    k_all = jnp.concatenate([kp_ref[0], kc_ref[0]], axis=0)
    v_all = jnp.concatenate([vp_ref[0], vc_ref[0]], axis=0)
    qi = lax.broadcasted_iota(jnp.int32, (DIL_BLOCK, 2 * DIL_BLOCK), 0)
    kj = lax.broadcasted_iota(jnp.int32, (DIL_BLOCK, 2 * DIL_BLOCK), 1)
    rel = qi + DIL_BLOCK - kj
    in_band = (rel >= 0) & (rel <= DIL_BLOCK)
    qlane = lax.broadcasted_iota(jnp.int32, (DIL_BLOCK, GROUP_W), 1)
    in_head = [(qlane >= HEAD_DIM * h) & (qlane < HEAD_DIM * (h + 1)) for h in range(N_HEADS)]
    for c in range(qb):
        rows = slice(c * DIL_BLOCK, (c + 1) * DIL_BLOCK)
        q = q_ref[0, rows, :]
        kb = k_all[c * DIL_BLOCK:(c + 2) * DIL_BLOCK]
        vb = v_all[c * DIL_BLOCK:(c + 2) * DIL_BLOCK]
        valid = in_band & ((kj >= DIL_BLOCK) | (n > 0)) if c == 0 else in_band
        q_heads = jnp.concatenate([jnp.where(msk, q, jnp.zeros_like(q)) for msk in in_head], axis=0)
        s = lax.dot_general(q_heads, kb, (((1,), (1,)), ((), ())), preferred_element_type=F32)
        s = s.reshape(N_HEADS, DIL_BLOCK, 2 * DIL_BLOCK)
        s = jnp.where(valid[None], s + tb_ref[0], NEG)
        m = jnp.max(s, axis=-1, keepdims=True)
        e = jnp.exp(s - m)
        l = jnp.sum(e, axis=-1, keepdims=True)
        lse = m + jnp.log(l)
        pr = (e * (1.0 / l)).astype(BF16)
        o_all = jnp.dot(pr.reshape(N_HEADS * DIL_BLOCK, 2 * DIL_BLOCK), vb, preferred_element_type=F32)
        o_all = o_all.reshape(N_HEADS, DIL_BLOCK, GROUP_W)
        o_acc = jnp.zeros((DIL_BLOCK, GROUP_W), F32)
        lse_map = jnp.zeros((DIL_BLOCK, GROUP_W), F32)
        for h in range(N_HEADS):
            o_acc = jnp.where(in_head[h], o_all[h], o_acc)
            lse_map = jnp.where(in_head[h], lse[h], lse_map)
        o_ref[0, rows, :] = o_acc.astype(BF16)
        lse_ref[0, rows, :] = lse_map


def _dilated_pattern(p, q, k, v, dil_tb):
    B, S, _ = q.shape
    dil = DIL_PATTERNS[p][1]
    L = S // dil
    assert S % dil == 0 and L % DIL_BLOCK == 0
    nb = L // DIL_BLOCK
    qb = math.gcd(nb, DIL_BLOCKS_PER_STEP)
    view = lambda a: a.reshape(B, L, dil * GROUP_W)
    cur = pl.BlockSpec((1, qb * DIL_BLOCK, GROUP_W), lambda b, r, n: (b, n, r))
    prev = pl.BlockSpec((1, DIL_BLOCK, GROUP_W), lambda b, r, n: (b, jnp.maximum(n * qb - 1, 0), r))
    o, lse = pl.pallas_call(
        _dil_kernel,
        out_shape=(jax.ShapeDtypeStruct((B, L, dil * GROUP_W), BF16),
                   jax.ShapeDtypeStruct((B, L, dil * GROUP_W), F32)),
        grid=(B, dil, nb // qb),
        in_specs=[cur, prev, cur, prev, cur,
                  pl.BlockSpec((1, N_HEADS, DIL_BLOCK, 2 * DIL_BLOCK), lambda b, r, n, p=p: (p, 0, 0, 0))],
        out_specs=(cur, cur),
        compiler_params=_compiler_params(("arbitrary", "arbitrary", "arbitrary")),
        name=f"dilated_{dil}",
    )(view(q), view(k), view(k), view(v), view(v), dil_tb)
    return o.reshape(B * S, GROUP_W), lse.reshape(B * S, GROUP_W)


def _mix_dilations(o_refs, lse_refs):
    lses = [r[...] for r in lse_refs]
    m = functools.reduce(jnp.maximum, lses)
    ws = [jnp.exp(l - m) for l in lses]
    tot = functools.reduce(jnp.add, ws)
    return functools.reduce(jnp.add, [(w / tot) * o[...].astype(F32) for w, o in zip(ws, o_refs)])


def _sigmoid(x):
    return 1.0 / (1.0 + jnp.exp(-x))


def _merge_kernel(x_ref, fo_ref, mo_ref, lo_ref, o1_ref, l1_ref, o4_ref, l4_ref, o16_ref, l16_ref, gate_ref, p_ref,
                  wout_ref, png_ref, wpg_ref, wpp_ref, o_ref):
    o_dil = _mix_dilations((o1_ref, o4_ref, o16_ref), (l1_ref, l4_ref, l16_ref))
    mix = jnp.concatenate([fo_ref[...].astype(F32), mo_ref[...].astype(F32), o_dil, lo_ref[...].astype(F32)],
                          axis=1)
    g = gate_ref[...].astype(F32)
    y = jnp.dot((mix * (g * _sigmoid(g))).astype(BF16), wout_ref[...], preferred_element_type=F32)
    x1 = x_ref[...] + y
    hn = _row_norm(x1, png_ref[...]).astype(BF16)
    pg = _sigmoid(jnp.dot(hn, wpg_ref[...], preferred_element_type=F32))
    pp = jnp.dot(p_ref[0].astype(BF16), wpp_ref[...], preferred_element_type=F32)
    o_ref[...] = x1 + pg * pp


def _merge(layer, x2, fo, mo, lo, dil_outs, proj, p3, lp):
    rows = x2.shape[0]
    TM = TM_MERGE

    def rspec(w, c=0):
        return pl.BlockSpec((TM, w), lambda i, c=c: (i, c))

    def cspec(shape):
        return pl.BlockSpec(shape, lambda i: (0, 0))

    return pl.pallas_call(
        _merge_kernel,
        out_shape=jax.ShapeDtypeStruct((rows, D_MODEL), F32),
        grid=(rows // TM,),
        in_specs=[rspec(D_MODEL)] + [rspec(GROUP_W)] * (3 + len(dil_outs)) + [
                  rspec(D_MODEL, COL_GATE // D_MODEL),
                  pl.BlockSpec((1, TM, PLE_DIM), lambda i: (layer, i, 0)),
                  cspec((D_MODEL, D_MODEL)), cspec((1, D_MODEL)), cspec((D_MODEL, D_MODEL)),
                  cspec((PLE_DIM, D_MODEL))],
        out_specs=rspec(D_MODEL),
        compiler_params=_compiler_params(("arbitrary",)),
        name="merge",
    )(x2, fo, mo, lo, *dil_outs, proj, p3, lp["w_out"], lp["ple_norm_g"], lp["w_ple_gate"], lp["w_ple_proj"])


def _layer_params(i, ln_g, b_forget, qk_gain, mla_q_norm, mla_kv_norm, mla_nope_gain,
                  mla_rope_gain, w_uq, w_ukv, w_out, ple_norm_g, w_ple_gate, w_ple_proj):
    def rope_slab(x1_parts, x2_parts, like):
        pad = jnp.zeros(like.shape[:-1] + (LANES // 2 - 2 * MLA_HALF,), like.dtype)
        return jnp.concatenate(list(x1_parts) + [pad] + list(x2_parts) + [pad], axis=-1)

    per_head = MLA_NOPE + MLA_ROPE
    uq = w_uq[i]
    uq_nope = [uq[:, h * per_head:h * per_head + MLA_NOPE] for h in range(N_HEADS)]
    uq_x1 = [uq[:, h * per_head + MLA_NOPE:h * per_head + MLA_NOPE + MLA_HALF] for h in range(N_HEADS)]
    uq_x2 = [uq[:, h * per_head + MLA_NOPE + MLA_HALF:(h + 1) * per_head] for h in range(N_HEADS)]
    hpg = N_HEADS // MLA_GROUPS
    uq_cols = uq_nope + [rope_slab(uq_x1[g * hpg:(g + 1) * hpg], uq_x2[g * hpg:(g + 1) * hpg], uq)
                         for g in range(MLA_GROUPS)]
    ukv = w_ukv[i]
    ukv_cols = ([ukv[:, h * 2 * HEAD_DIM:h * 2 * HEAD_DIM + HEAD_DIM] for h in range(N_HEADS)]
                + [ukv[:, h * 2 * HEAD_DIM + HEAD_DIM:(h + 1) * 2 * HEAD_DIM] for h in range(N_HEADS)])
    rg = mla_rope_gain[i]
    rope_gain = rope_slab([rg[:, :MLA_HALF]] * hpg, [rg[:, MLA_HALF:]] * hpg, rg)
    return {
        "ln_g": ln_g[i][None, :],
        "gains": jnp.tile(qk_gain[i], (1, N_HEADS)),
        "b_forget": jnp.pad(b_forget[i], (0, LANES - N_HEADS))[None, :],
        "q_norm": mla_q_norm[i][None, :],
        "kv_norm": mla_kv_norm[i][None, :],
        "nope_gain": jnp.tile(mla_nope_gain[i], (1, N_HEADS)),
        "rope_gain": rope_gain,
        "w_uq": jnp.concatenate(uq_cols, axis=1).astype(BF16),
        "w_ukv": jnp.concatenate(ukv_cols, axis=1).astype(BF16),
        "w_out": w_out[i].astype(BF16),
        "ple_norm_g": ple_norm_g[i][None, :],
        "w_ple_gate": w_ple_gate[i].astype(BF16),
        "w_ple_proj": w_ple_proj[i].astype(BF16),
    }


def _constants(S):
    inv = 1.0 / (ROPE_THETA ** (jnp.arange(MLA_HALF, dtype=F32) * 2.0 / MLA_ROPE))
    ang = jnp.arange(S).astype(F32)[:, None] * inv[None, :]
    cos = jnp.tile(jnp.cos(ang), (1, 2 * N_HEADS))
    sin = jnp.tile(jnp.sin(ang), (1, 2 * N_HEADS))
    sign = np.concatenate([-np.ones(LANES // 2, np.float32), np.ones(LANES // 2, np.float32)])
    lane = np.arange(GROUP_W)
    g64 = (lane[:, None] // HEAD_DIM == lane[None, :] // HEAD_DIM).astype(np.float32) / HEAD_DIM
    within = np.arange(LANES) % (LANES // 2)
    rope_head = np.where(within < 2 * MLA_HALF, within // MLA_HALF, -1)
    g32 = ((rope_head[:, None] == rope_head[None, :]) & (rope_head[:, None] >= 0)).astype(np.float32) / MLA_ROPE
    tri = np.tril(np.ones((T_PREP, T_PREP), np.float32))
    ekr = np.zeros((LANES, LANES), np.float32)
    kr_lane0 = IN_OFF["kr"] % LANES
    for c in range(MLA_ROPE):
        half, idx = divmod(c, MLA_HALF)
        for copy in range(N_HEADS // MLA_GROUPS):
            ekr[kr_lane0 + c, half * (LANES // 2) + copy * MLA_HALF + idx] = 1.0
    return {"cos": cos, "sin": sin * sign[None, :], "g64": jnp.asarray(g64, BF16),
            "g32": jnp.asarray(g32, BF16), "tri": jnp.asarray(tri, BF16), "ekr": jnp.asarray(ekr, BF16)}


def kernel(x, p, ln_g, w_in, b_forget, qk_gain, mla_q_norm, mla_kv_norm, mla_nope_gain, mla_rope_gain,
           w_uq, w_ukv, w_out, rel_bias, ple_norm_g, w_ple_gate, w_ple_proj):
    B, S, _ = x.shape
    depth = p.shape[0]
    consts = _constants(S)
    moba_tb, dil_tb = _bias_tables(rel_bias)
    x2 = x.reshape(B * S, D_MODEL)
    for i in range(depth):
        lp = _layer_params(i, ln_g, b_forget, qk_gain, mla_q_norm, mla_kv_norm, mla_nope_gain,
                           mla_rope_gain, w_uq, w_ukv, w_out, ple_norm_g, w_ple_gate, w_ple_proj)
        proj, ff = _in_projection(i, x2, lp["ln_g"], w_in, consts["ekr"])
        (fqT, fk, fvT, fck, fcrow, mqT, mk, mvT, mkmean, lqT, lk, lvT, dq, dk, dv) = _prep(
            proj.reshape(B, S, PROJ_W), ff.reshape(B, S, LANES), consts, lp)
        S_ = S
        fo = _dense_mixer(
            _fox_kernel, "fox", fqT, fk, fvT, (fck, fcrow),
            (pl.BlockSpec((1, N_HEADS, S_, LANES), lambda b, i: (b, 0, 0, 0)),
             pl.BlockSpec((1, 8, TQ), lambda b, i: (b, 0, i))))
        mo = _dense_mixer(
            _moba_kernel, "moba", mqT, mk, mvT, (mkmean, moba_tb),
            (pl.BlockSpec((1, S_ // MOBA_BLOCK, GROUP_W), lambda b, i: (b, 0, 0)),
             pl.BlockSpec((MOBA_TB_ENTRIES, N_HEADS, MOBA_BLOCK, MOBA_BLOCK), lambda b, i: (0, 0, 0, 0))),
            (pltpu.VMEM((N_HEADS, S_ // MOBA_BLOCK, TQ), F32),))
        lo = _dense_mixer(_mla_kernel, "mla", lqT, lk, lvT)
        dil_outs = [a for pat in range(len(DIL_PATTERNS)) for a in _dilated_pattern(pat, dq, dk, dv, dil_tb)]
        rs = lambda a: a.reshape(B * S, GROUP_W)
        x2 = _merge(i, x2, rs(fo), rs(mo), rs(lo), dil_outs, proj, p.reshape(depth, B * S, PLE_DIM), lp)
    return x2.reshape(B, S, D_MODEL)
```

```python
import functools
import math

import numpy as np
import jax
import jax.numpy as jnp
from jax import lax
from jax.experimental import pallas as pl
from jax.experimental.pallas import tpu as pltpu

F32 = jnp.float32
BF16 = jnp.bfloat16

D_MODEL = 1024
N_HEADS = 4
HEAD_DIM = 64
GROUP_W = N_HEADS * HEAD_DIM
MLA_NOPE = 64
MLA_ROPE = 32
MLA_HALF = MLA_ROPE // 2
MLA_QK_W = N_HEADS * (MLA_NOPE + MLA_ROPE)
MLA_Q_LORA = 256
MLA_KV_LORA = 128
PLE_DIM = 256
MOBA_BLOCK = 256
MOBA_TOPK = 3
DIL_PATTERNS = ((128, 1), (512, 4), (2048, 16))
DIL_BLOCK = 128
DIL_BLOCKS_PER_STEP = 4
N_BUCKETS = 32
MAX_DISTANCE = 2048
ROPE_THETA = 10000.0
EPS = 1e-6
NEG = -1e30
LOG2E = math.log2(math.e)

LANES = 128
VMEM_LIMIT_BYTES = 56 * 1024 * 1024

COL_FOX, COL_MOBA, COL_DIL = 0, 768, 1536
COL_CQ, COL_CKV, COL_KR, COL_FF, COL_GATE = 2304, 2560, 2688, 2816, 3072
PROJ_W = 4096
PROJ_TN = 512
STAGE_ROWS = 128

IN_OFF = {}
IN_W = 0
for _name, _size in (("fq", GROUP_W), ("fk", GROUP_W), ("fv", GROUP_W), ("ff", N_HEADS),
                     ("mq", GROUP_W), ("mk", GROUP_W), ("mv", GROUP_W),
                     ("dq", GROUP_W), ("dk", GROUP_W), ("dv", GROUP_W),
                     ("cq", MLA_Q_LORA), ("ckv", MLA_KV_LORA), ("kr", MLA_ROPE), ("gate", D_MODEL)):
    IN_OFF[_name] = IN_W
    IN_W += _size

TM_PROJ = 256
T_PREP = 256
TQ = 256
TK = 256
SWEEP_UNROLL = 4
MOBA_TB_ENTRIES = 8
TM_MERGE = 256
V_ROWS = HEAD_DIM + 16
MLA_GROUPS = 2


def _bucket_np(d):
    d = np.maximum(np.asarray(d, np.int64), 0)
    max_exact = N_BUCKETS // 2
    d_f = np.maximum(d, 1).astype(np.float64)
    val = np.log(d_f / max_exact) / math.log(MAX_DISTANCE / max_exact) * (N_BUCKETS - max_exact)
    frac = np.abs(val - np.round(val))
    on_edge = (frac < 1e-5) & (d > max_exact) & (val < N_BUCKETS - max_exact - 0.5)
    assert not on_edge.any(), "distance too close to a bucket edge for a static table"
    large = np.minimum(max_exact + np.floor(val + 1e-9).astype(np.int64), N_BUCKETS - 1)
    return np.where(d < max_exact, d, large)


def _bucket_steps(lo, hi):
    ds = np.arange(lo, hi + 1)
    bs = _bucket_np(ds)
    steps = [(int(ds[i]), int(bs[i])) for i in range(1, len(ds)) if bs[i] != bs[i - 1]]
    return int(bs[0]), steps


def _compiler_params(sem):
    return pltpu.CompilerParams(dimension_semantics=sem, vmem_limit_bytes=VMEM_LIMIT_BYTES)


def _bias_from_steps(d, rb_ref, col, lo, hi):
    b0, steps = _bucket_steps(lo, hi)
    val = jnp.full(d.shape, rb_ref[b0, col], F32)
    for t, b in steps:
        val = jnp.where(d >= t, rb_ref[b, col], val)
    return val


def _bias_tables_kernel(rb_ref, moba_ref, dil_ref):
    kl = lax.broadcasted_iota(jnp.int32, (MOBA_BLOCK, MOBA_BLOCK), 0)
    ql = lax.broadcasted_iota(jnp.int32, (MOBA_BLOCK, MOBA_BLOCK), 1)
    for e in range(MOBA_TB_ENTRIES - 1):
        d = jnp.maximum(e * MOBA_BLOCK + ql - kl, 0)
        lo = max(e * MOBA_BLOCK - (MOBA_BLOCK - 1), 0)
        hi = e * MOBA_BLOCK + (MOBA_BLOCK - 1)
        for h in range(N_HEADS):
            moba_ref[e, h] = _bias_from_steps(d, rb_ref, h, lo, hi) * LOG2E
    for h in range(N_HEADS):
        moba_ref[MOBA_TB_ENTRIES - 1, h] = jnp.full((MOBA_BLOCK, MOBA_BLOCK), rb_ref[N_BUCKETS - 1, h], F32) * LOG2E
    qi = lax.broadcasted_iota(jnp.int32, (DIL_BLOCK, 2 * DIL_BLOCK), 0)
    kj = lax.broadcasted_iota(jnp.int32, (DIL_BLOCK, 2 * DIL_BLOCK), 1)
    for p, (_, dil) in enumerate(DIL_PATTERNS):
        d = jnp.maximum((qi + DIL_BLOCK - kj) * dil, 0)
        for h in range(N_HEADS):
            dil_ref[p, h] = _bias_from_steps(d, rb_ref, N_HEADS + h, 0, (2 * DIL_BLOCK - 1) * dil)


def _bias_tables(rel_bias):
    assert _bucket_np((MOBA_TB_ENTRIES - 1) * MOBA_BLOCK - (MOBA_BLOCK - 1)) == N_BUCKETS - 1
    return pl.pallas_call(
        _bias_tables_kernel,
        out_shape=(jax.ShapeDtypeStruct((MOBA_TB_ENTRIES, N_HEADS, MOBA_BLOCK, MOBA_BLOCK), F32),
                   jax.ShapeDtypeStruct((len(DIL_PATTERNS), N_HEADS, DIL_BLOCK, 2 * DIL_BLOCK), F32)),
        in_specs=[pl.BlockSpec(memory_space=pltpu.SMEM)],
        out_specs=(pl.BlockSpec(memory_space=pltpu.VMEM), pl.BlockSpec(memory_space=pltpu.VMEM)),
        compiler_params=pltpu.CompilerParams(vmem_limit_bytes=VMEM_LIMIT_BYTES),
        name="bias_tables",
    )(rel_bias)


def _stage_in_weights(w_ref, ekr_ref, ws_ref):
    W = GROUP_W
    moves = ((COL_FOX, IN_OFF["fq"], 3 * W), (COL_MOBA, IN_OFF["mq"], 3 * W), (COL_DIL, IN_OFF["dq"], 3 * W),
             (COL_CQ, IN_OFF["cq"], MLA_Q_LORA), (COL_CKV, IN_OFF["ckv"], MLA_KV_LORA),
             (COL_GATE, IN_OFF["gate"], D_MODEL))
    kr_base = IN_OFF["kr"] // LANES * LANES
    assert IN_OFF["ff"] % LANES == 0 and IN_OFF["kr"] + MLA_ROPE <= kr_base + LANES
    lane = lax.broadcasted_iota(jnp.int32, (STAGE_ROWS, LANES), 1)
    for r0 in range(0, D_MODEL, STAGE_ROWS):
        rows = slice(r0, r0 + STAGE_ROWS)
        for dst, src, width in moves:
            ws_ref[rows, dst:dst + width] = w_ref[0, rows, src:src + width].astype(BF16)
        kr_win = w_ref[0, rows, kr_base:kr_base + LANES].astype(BF16)
        ws_ref[rows, COL_KR:COL_KR + LANES] = jnp.dot(kr_win, ekr_ref[...], preferred_element_type=F32).astype(BF16)
        ff_win = w_ref[0, rows, IN_OFF["ff"]:IN_OFF["ff"] + LANES]
        ws_ref[rows, COL_FF:COL_FF + LANES] = jnp.where(lane < N_HEADS, ff_win, 0.0).astype(BF16)
        ws_ref[rows, COL_FF + LANES:COL_GATE] = jnp.zeros((STAGE_ROWS, COL_GATE - COL_FF - LANES), BF16)


def _proj_kernel(x_ref, g_ref, w_ref, ekr_ref, o_ref, ff_ref, ws_ref):
    @pl.when(pl.program_id(0) == 0)
    def _():
        _stage_in_weights(w_ref, ekr_ref, ws_ref)

    x = x_ref[...]
    ms = jnp.mean(x * x, axis=-1, keepdims=True)
    h = (x * lax.rsqrt(ms + EPS) * g_ref[...]).astype(BF16)
    for c in range(PROJ_W // PROJ_TN):
        sl = slice(c * PROJ_TN, (c + 1) * PROJ_TN)
        o_ref[:, sl] = jnp.dot(h, ws_ref[:, sl], preferred_element_type=F32).astype(BF16)
    ff_ref[...] = jnp.dot(h, ws_ref[:, COL_FF:COL_FF + LANES], preferred_element_type=F32)


def _in_projection(layer, x2, ln_g, w_in, ekr):
    rows = x2.shape[0]
    return pl.pallas_call(
        _proj_kernel,
        out_shape=(jax.ShapeDtypeStruct((rows, PROJ_W), BF16), jax.ShapeDtypeStruct((rows, LANES), F32)),
        grid=(rows // TM_PROJ,),
        in_specs=[pl.BlockSpec((TM_PROJ, D_MODEL), lambda i: (i, 0)),
                  pl.BlockSpec((1, D_MODEL), lambda i: (0, 0)),
                  pl.BlockSpec((1, D_MODEL, IN_W), lambda i: (layer, 0, 0), pipeline_mode=pl.Buffered(1)),
                  pl.BlockSpec((LANES, LANES), lambda i: (0, 0))],
        out_specs=(pl.BlockSpec((TM_PROJ, PROJ_W), lambda i: (i, 0)),
                   pl.BlockSpec((TM_PROJ, LANES), lambda i: (i, 0))),
        scratch_shapes=[pltpu.VMEM((D_MODEL, PROJ_W), BF16)],
        compiler_params=_compiler_params(("arbitrary",)),
        name="in_projection",
    )(x2, ln_g, w_in, ekr)


def _group_mean_sq(x, g_mat):
    return jnp.dot((x * x).astype(BF16), g_mat, preferred_element_type=F32)


def _group_norm(x, gain, g_mat):
    return x * lax.rsqrt(_group_mean_sq(x, g_mat) + EPS) * gain


def _row_norm(x, gain):
    return x * lax.rsqrt(jnp.mean(x * x, axis=-1, keepdims=True) + EPS) * gain


def _log_sigmoid(x):
    return -(jnp.maximum(-x, 0.0) + jnp.log1p(jnp.exp(-jnp.abs(x))))


def _prep_kernel(pf_ref, pm_ref, pd_ref, pcq_ref, pckv_ref, pkr_ref, pff_ref,
                 gains_ref, bfor_ref, qng_ref, kvng_ref, nopeg_ref, ropeg_ref,
                 wuq_ref, wukv_ref, cos_ref, sin_ref, g64_ref, g32_ref, tri_ref,
                 fqT_ref, fk_ref, fvT_ref, fck_ref, fcrow_ref,
                 mqT_ref, mk_ref, mvT_ref, mkmean_ref,
                 lqT_ref, lk_ref, lvT_ref,
                 dq1_ref, dk1_ref, dv1_ref, dq4_ref, dk4_ref, dv4_ref, dq16_ref, dk16_ref, dv16_ref,
                 carry_ref, sq_ref, sk_ref, sv_ref):
    t = pl.program_id(1)
    g64 = g64_ref[...]
    scale = HEAD_DIM ** -0.5
    W = GROUP_W
    ones_rows = jnp.ones((V_ROWS - HEAD_DIM, T_PREP), BF16)

    def store_vT(dst, v):
        vT = v.T.astype(BF16)
        for h in range(N_HEADS):
            dst[0, 0, h] = jnp.concatenate([vT[HEAD_DIM * h:HEAD_DIM * (h + 1)], ones_rows], axis=0)

    pf = pf_ref[0].astype(F32)
    fq = _group_norm(pf[:, :W], gains_ref[0:1, :], g64) * (scale * LOG2E)
    fk = _group_norm(pf[:, W:2 * W], gains_ref[1:2, :], g64)
    fqT_ref[0, 0] = fq.T.astype(BF16)
    fk_ref[0, 0] = fk.astype(BF16)
    store_vT(fvT_ref, pf[:, 2 * W:])

    @pl.when(t == 0)
    def _():
        carry_ref[...] = jnp.zeros_like(carry_ref)

    log_f = _log_sigmoid(pff_ref[0] + bfor_ref[...])
    tri = tri_ref[...]
    c = carry_ref[...]
    rest = log_f
    for _ in range(3):
        part = rest.astype(BF16)
        c = c + jnp.dot(tri, part, preferred_element_type=F32)
        rest = rest - part.astype(F32)
    carry_ref[...] = c[T_PREP - 1:T_PREP, :]
    c2 = c * LOG2E
    for h in range(N_HEADS):
        fck_ref[0, h] = jnp.broadcast_to(c2[:, h:h + 1], (T_PREP, LANES))
    fcrow_ref[0] = c2.T[0:8, :]

    pm = pm_ref[0].astype(F32)
    mq = _group_norm(pm[:, :W], gains_ref[2:3, :], g64) * (scale * LOG2E)
    mk = _group_norm(pm[:, W:2 * W], gains_ref[3:4, :], g64)
    mqT_ref[0, 0] = mq.T.astype(BF16)
    mk_ref[0, 0] = mk.astype(BF16)
    store_vT(mvT_ref, pm[:, 2 * W:])
    mkmean_ref[0, pl.ds(t, 1), :] = jnp.mean(mk, axis=0, keepdims=True)

    pd = pd_ref[0].astype(F32)
    dq = _group_norm(pd[:, :W], gains_ref[4:5, :], g64) * scale
    dk = _group_norm(pd[:, W:2 * W], gains_ref[5:6, :], g64)
    dv = pd[:, 2 * W:]
    dq1_ref[0] = dq.astype(BF16)
    dk1_ref[0] = dk.astype(BF16)
    dv1_ref[0] = pd_ref[0, :, 2 * W:]
    for val, dst in ((dq, sq_ref), (dk, sk_ref), (dv, sv_ref)):
        for half in range(GROUP_W // LANES):
            dst[half] = val[:, half * LANES:(half + 1) * LANES]
    for dil, outs in ((4, (dq4_ref, dk4_ref, dv4_ref)), (16, (dq16_ref, dk16_ref, dv16_ref))):
        n = T_PREP // dil
        for r in range(dil):
            for src, dst in zip((sq_ref, sk_ref, sv_ref), outs):
                dst[0, r] = jnp.concatenate(
                    [src[half, pl.ds(r, n, stride=dil), :] for half in range(GROUP_W // LANES)],
                    axis=1).astype(BF16)

    cos = cos_ref[...]
    sin = sin_ref[...]
    qscale = (MLA_NOPE + MLA_ROPE) ** -0.5
    cqn = _row_norm(pcq_ref[0].astype(F32), qng_ref[...]).astype(BF16)
    rotate = lambda x: x * cos + pltpu.roll(x, LANES // 2, 1) * sin
    qf = jnp.dot(cqn, wuq_ref[...], preferred_element_type=F32)
    q_nope = _group_norm(qf[:, :W], nopeg_ref[0:1, :], g64) * (qscale * LOG2E)
    ckvn = _row_norm(pckv_ref[0].astype(F32), kvng_ref[...]).astype(BF16)
    kvf = jnp.dot(ckvn, wukv_ref[...], preferred_element_type=F32)
    k_nope = _group_norm(kvf[:, :W], nopeg_ref[1:2, :], g64)
    krs = pkr_ref[0].astype(F32)
    kr_ms = jnp.sum(krs * krs, axis=-1, keepdims=True) * (1.0 / (2 * MLA_ROPE))
    kr = rotate(krs * lax.rsqrt(kr_ms + EPS) * ropeg_ref[1:2, :])
    for g in range(MLA_GROUPS):
        slab = qf[:, W + g * LANES:W + (g + 1) * LANES]
        qr = rotate(_group_norm(slab, ropeg_ref[0:1, :], g32_ref[...])) * (qscale * LOG2E)
        lq = jnp.concatenate([q_nope[:, g * LANES:(g + 1) * LANES], qr], axis=1)
        lqT_ref[0, g] = lq.T.astype(BF16)
        lk_ref[0, g] = jnp.concatenate([k_nope[:, g * LANES:(g + 1) * LANES], kr], axis=1).astype(BF16)
    store_vT(lvT_ref, kvf[:, W:])


def _prep(proj3, ff3, consts, lp):
    B, S, _ = proj3.shape
    T = T_PREP
    NT = S // T
    assert S % T == 0 and S // MOBA_BLOCK == NT

    def pspec(width, col):
        return pl.BlockSpec((1, T, width), lambda b, t, c=col // width: (b, t, c))

    def cspec(shape):
        return pl.BlockSpec(shape, lambda b, t: (0,) * len(shape))

    in_specs = [pspec(768, COL_FOX), pspec(768, COL_MOBA), pspec(768, COL_DIL),
                pspec(256, COL_CQ), pspec(128, COL_CKV), pspec(128, COL_KR), pspec(128, 0),
                cspec((6, GROUP_W)), cspec((1, LANES)), cspec((1, MLA_Q_LORA)), cspec((1, MLA_KV_LORA)),
                cspec((2, GROUP_W)), cspec((2, LANES)),
                cspec((MLA_Q_LORA, MLA_GROUPS * GROUP_W)), cspec((MLA_KV_LORA, 2 * GROUP_W)),
                pl.BlockSpec((T, LANES), lambda b, t: (t, 0)), pl.BlockSpec((T, LANES), lambda b, t: (t, 0)),
                cspec((GROUP_W, GROUP_W)), cspec((LANES, LANES)), cspec((T, T))]

    def qT(groups):
        return (jax.ShapeDtypeStruct((B, groups, GROUP_W, S), BF16),
                pl.BlockSpec((1, groups, GROUP_W, T), lambda b, t: (b, 0, 0, t)))

    def keys(groups):
        return (jax.ShapeDtypeStruct((B, groups, S, GROUP_W), BF16),
                pl.BlockSpec((1, groups, T, GROUP_W), lambda b, t: (b, 0, t, 0)))

    def rows(w):
        return jax.ShapeDtypeStruct((B, S, w), BF16), pl.BlockSpec((1, T, w), lambda b, t: (b, t, 0))

    def vT():
        return (jax.ShapeDtypeStruct((B, NT, N_HEADS, V_ROWS, T), BF16),
                pl.BlockSpec((1, 1, N_HEADS, V_ROWS, T), lambda b, t: (b, t, 0, 0, 0)))

    def resid(dil):
        return (jax.ShapeDtypeStruct((B, dil, S // dil, GROUP_W), BF16),
                pl.BlockSpec((1, dil, T // dil, GROUP_W), lambda b, t: (b, 0, t, 0)))

    outs = [qT(1), keys(1), vT(),
            (jax.ShapeDtypeStruct((B, N_HEADS, S, LANES), F32),
             pl.BlockSpec((1, N_HEADS, T, LANES), lambda b, t: (b, 0, t, 0))),
            (jax.ShapeDtypeStruct((B, 8, S), F32), pl.BlockSpec((1, 8, T), lambda b, t: (b, 0, t))),
            qT(1), keys(1), vT(),
            (jax.ShapeDtypeStruct((B, NT, GROUP_W), F32), pl.BlockSpec((1, NT, GROUP_W), lambda b, t: (b, 0, 0))),
            qT(MLA_GROUPS), keys(MLA_GROUPS), vT(),
            rows(GROUP_W), rows(GROUP_W), rows(GROUP_W),
            resid(4), resid(4), resid(4), resid(16), resid(16), resid(16)]
    return pl.pallas_call(
        _prep_kernel,
        out_shape=tuple(o[0] for o in outs),
        grid=(B, NT),
        in_specs=in_specs,
        out_specs=tuple(o[1] for o in outs),
        scratch_shapes=[pltpu.VMEM((1, LANES), F32)] + [pltpu.VMEM((GROUP_W // LANES, T, LANES), F32)] * 3,
        compiler_params=_compiler_params(("arbitrary", "arbitrary")),
        name="prep",
    )(proj3, proj3, proj3, proj3, proj3, proj3, ff3,
      lp["gains"], lp["b_forget"], lp["q_norm"], lp["kv_norm"], lp["nope_gain"], lp["rope_gain"],
      lp["w_uq"], lp["w_ukv"], consts["cos"], consts["sin"], consts["g64"], consts["g32"], consts["tri"])


def _stage_head_queries(qT_ref, qm_ref, groups):
    r = lax.broadcasted_iota(jnp.int32, (GROUP_W, TQ), 0)
    for h in range(N_HEADS):
        if groups == 1:
            g, mask = 0, (r >= HEAD_DIM * h) & (r < HEAD_DIM * (h + 1))
        else:
            g, hl = divmod(h, N_HEADS // groups)
            mask = (r >= HEAD_DIM * hl) & (r < HEAD_DIM * (hl + 1))
            for base in (2 * HEAD_DIM, 2 * HEAD_DIM + LANES // 2):
                lo = base + MLA_HALF * hl
                mask = mask | ((r >= lo) & (r < lo + MLA_HALF))
        q = qT_ref[0, g]
        qm_ref[h] = jnp.where(mask, q, jnp.zeros_like(q))


def _flash_sweep(i, groups, qm_ref, k_ref, vT_ref, m_ref, acc_ref, u_ref, st_ref, o_ref, score_fn):
    kpos = lax.broadcasted_iota(jnp.int32, (TK, TQ), 0)
    qpos = lax.broadcasted_iota(jnp.int32, (TK, TQ), 1)
    causal = kpos <= qpos
    m_ref[...] = jnp.full(m_ref.shape, NEG, F32)
    acc_ref[...] = jnp.zeros(acc_ref.shape, F32)

    def scores(j, slot, diagonal):
        row0 = pl.multiple_of(j * TK, TK)
        for h in range(N_HEADS):
            kt = k_ref[0, h // (N_HEADS // groups), pl.ds(row0, TK), :]
            s = jnp.dot(kt, qm_ref[h], preferred_element_type=F32)
            u, row, live = score_fn(h, s, j, diagonal)
            if diagonal:
                u = jnp.where(causal, u, NEG)
            u_ref[slot, h] = u
            m_old = m_ref[h]
            top = jnp.max(u, axis=0, keepdims=True)
            if row is not None:
                top = top + row
            if live is not None:
                top = jnp.where(live, top, NEG)
            m_new = jnp.maximum(m_old, top)
            shift = m_new if row is None else m_new - row
            if live is not None:
                shift = jnp.where(live, shift, -NEG)
            st_ref[slot, h] = shift
            st_ref[slot, N_HEADS + h] = jnp.exp2(m_old - m_new)
            m_ref[h] = m_new

    def values(j, slot):
        for h in range(N_HEADS):
            p = jnp.exp2((u_ref[slot, h] - st_ref[slot, h]).astype(BF16))
            acc_ref[h] = (st_ref[slot, N_HEADS + h] * acc_ref[h]
                          + jnp.dot(vT_ref[0, j, h], p, preferred_element_type=F32))

    scores(i, 0, True)

    def step(n, slot):
        scores(n - 1, slot, False)
        values(jnp.where(n == 1, i, n - 2), 1 - slot)

    def unrolled(t, carry):
        n = 1 + SWEEP_UNROLL * t
        for k in range(SWEEP_UNROLL):
            step(n + k, (1 + k) % 2)
        return carry

    n_unrolled = i // SWEEP_UNROLL
    lax.fori_loop(0, n_unrolled, unrolled, 0)
    left = i - SWEEP_UNROLL * n_unrolled
    for k in range(SWEEP_UNROLL - 1):
        def leftover(_, carry, k=k):
            step(1 + SWEEP_UNROLL * n_unrolled + k, (1 + k) % 2)
            return carry

        lax.fori_loop(0, (left > k).astype(jnp.int32), leftover, 0)
    values(jnp.maximum(i - 1, 0), lax.rem(i, 2))
    outs = []
    for h in range(N_HEADS):
        acc = acc_ref[h]
        outs.append(acc[:HEAD_DIM] / acc[HEAD_DIM:HEAD_DIM + 1])
    o_ref[0] = jnp.concatenate(outs, axis=0).T.astype(BF16)


def _fox_kernel(qT_ref, k_ref, vT_ref, ck_ref, crow_ref, o_ref, qm_ref, m_ref, acc_ref, u_ref, st_ref):
    i = pl.program_id(1)
    _stage_head_queries(qT_ref, qm_ref, 1)

    def score_fn(h, s, j, diagonal):
        ck = ck_ref[0, h, pl.ds(pl.multiple_of(j * TK, TK), TK), :]
        return s - jnp.concatenate([ck] * (TQ // LANES), axis=1), crow_ref[0, h:h + 1, :], None

    _flash_sweep(i, 1, qm_ref, k_ref, vT_ref, m_ref, acc_ref, u_ref, st_ref, o_ref, score_fn)


def _mla_kernel(qT_ref, k_ref, vT_ref, o_ref, qm_ref, m_ref, acc_ref, u_ref, st_ref):
    i = pl.program_id(1)
    _stage_head_queries(qT_ref, qm_ref, MLA_GROUPS)
    _flash_sweep(i, MLA_GROUPS, qm_ref, k_ref, vT_ref, m_ref, acc_ref, u_ref, st_ref, o_ref,
                 lambda h, s, j, diagonal: (s, None, None))


def _moba_kernel(qT_ref, k_ref, vT_ref, kmean_ref, tb_ref, o_ref, qm_ref, m_ref, acc_ref, u_ref, st_ref, sel_ref):
    i = pl.program_id(1)
    _stage_head_queries(qT_ref, qm_ref, 1)
    kmean = kmean_ref[0].astype(BF16)
    nb = kmean.shape[0]
    blk = lax.broadcasted_iota(jnp.int32, (nb, TQ), 0)
    for h in range(N_HEADS):
        gate = jnp.dot(kmean, qm_ref[h], preferred_element_type=F32)
        gate = jnp.where(blk < i, gate, NEG)
        rank = jnp.zeros((nb, TQ), jnp.int32)
        for jp in range(nb):
            row = gate[jp:jp + 1, :]
            ahead = (row > gate) | ((row == gate) & (jp < blk))
            rank = rank + ahead.astype(jnp.int32)
        sel_ref[h] = ((rank < MOBA_TOPK) & (blk < i)).astype(F32)

    def score_fn(h, s, j, diagonal):
        u = s + tb_ref[jnp.minimum(i - j, MOBA_TB_ENTRIES - 1), h]
        live = None if diagonal else sel_ref[h, pl.ds(j, 1), :] > 0.5
        return u, None, live

    _flash_sweep(i, 1, qm_ref, k_ref, vT_ref, m_ref, acc_ref, u_ref, st_ref, o_ref, score_fn)


def _dense_mixer(kernel_fn, name, qT, k, vT, extra_args=(), extra_specs=(), extra_scratch=()):
    B, groups, _, S = qT.shape
    NT = S // TK
    assert TQ == TK and S % TQ == 0
    in_specs = [pl.BlockSpec((1, groups, GROUP_W, TQ), lambda b, i: (b, 0, 0, i)),
                pl.BlockSpec((1, groups, S, GROUP_W), lambda b, i: (b, 0, 0, 0)),
                pl.BlockSpec((1, NT, N_HEADS, V_ROWS, TK), lambda b, i: (b, 0, 0, 0, 0))] + list(extra_specs)
    scratch = [pltpu.VMEM((N_HEADS, GROUP_W, TQ), BF16), pltpu.VMEM((N_HEADS, 1, TQ), F32),
               pltpu.VMEM((N_HEADS, V_ROWS, TQ), F32), pltpu.VMEM((2, N_HEADS, TK, TQ), F32), pltpu.VMEM((2, 2 * N_HEADS, 1, TQ), F32)] + list(extra_scratch)
    return pl.pallas_call(
        kernel_fn,
        out_shape=jax.ShapeDtypeStruct((B, S, GROUP_W), BF16),
        grid=(B, S // TQ),
        in_specs=in_specs,
        out_specs=pl.BlockSpec((1, TQ, GROUP_W), lambda b, i: (b, i, 0)),
        scratch_shapes=scratch,
        compiler_params=_compiler_params(("arbitrary", "arbitrary")),
        name=name,
    )(qT, k, vT, *extra_args)


def _dil_kernel(q_ref, kp_ref, kc_ref, vp_ref, vc_ref, tb_ref, o_ref, lse_ref):
    n = pl.program_id(2)
    qb = q_ref.shape[2] // DIL_BLOCK
    k_all = jnp.concatenate([kp_ref[0, 0], kc_ref[0, 0]], axis=0)
    v_all = jnp.concatenate([vp_ref[0, 0], vc_ref[0, 0]], axis=0)
    qi = lax.broadcasted_iota(jnp.int32, (DIL_BLOCK, 2 * DIL_BLOCK), 0)
    kj = lax.broadcasted_iota(jnp.int32, (DIL_BLOCK, 2 * DIL_BLOCK), 1)
    rel = qi + DIL_BLOCK - kj
    in_band = (rel >= 0) & (rel <= DIL_BLOCK)
    qlane = lax.broadcasted_iota(jnp.int32, (DIL_BLOCK, GROUP_W), 1)
    in_head = [(qlane >= HEAD_DIM * h) & (qlane < HEAD_DIM * (h + 1)) for h in range(N_HEADS)]
    for c in range(qb):
        rows = slice(c * DIL_BLOCK, (c + 1) * DIL_BLOCK)
        q = q_ref[0, 0, rows, :]
        kb = k_all[c * DIL_BLOCK:(c + 2) * DIL_BLOCK]
        vb = v_all[c * DIL_BLOCK:(c + 2) * DIL_BLOCK]
        valid = in_band & ((kj >= DIL_BLOCK) | (n > 0)) if c == 0 else in_band
        q_heads = jnp.concatenate([jnp.where(msk, q, jnp.zeros_like(q)) for msk in in_head], axis=0)
        s = lax.dot_general(q_heads, kb, (((1,), (1,)), ((), ())), preferred_element_type=F32)
        s = s.reshape(N_HEADS, DIL_BLOCK, 2 * DIL_BLOCK)
        s = jnp.where(valid[None], s + tb_ref[0], NEG)
        m = jnp.max(s, axis=-1, keepdims=True)
        e = jnp.exp(s - m)
        l = jnp.sum(e, axis=-1, keepdims=True)
        lse = m + jnp.log(l)
        pr = (e * (1.0 / l)).astype(BF16)
        o_all = jnp.dot(pr.reshape(N_HEADS * DIL_BLOCK, 2 * DIL_BLOCK), vb, preferred_element_type=F32)
        o_all = o_all.reshape(N_HEADS, DIL_BLOCK, GROUP_W)
        o_acc = jnp.zeros((DIL_BLOCK, GROUP_W), F32)
        lse_map = jnp.zeros((DIL_BLOCK, GROUP_W), F32)
        for h in range(N_HEADS):
            o_acc = jnp.where(in_head[h], o_all[h], o_acc)
            lse_map = jnp.where(in_head[h], lse[h], lse_map)
        o_ref[0, 0, rows, :] = o_acc.astype(BF16)
        lse_ref[0, 0, rows, :] = lse_map


def _dilated_pattern(p, q, k, v, dil_tb):
    B, dil, L, _ = q.shape
    assert dil == DIL_PATTERNS[p][1] and L % DIL_BLOCK == 0
    nb = L // DIL_BLOCK
    qb = math.gcd(nb, DIL_BLOCKS_PER_STEP)
    cur = pl.BlockSpec((1, 1, qb * DIL_BLOCK, GROUP_W), lambda b, r, n: (b, r, n, 0))
    prev = pl.BlockSpec((1, 1, DIL_BLOCK, GROUP_W), lambda b, r, n: (b, r, jnp.maximum(n * qb - 1, 0), 0))
    return pl.pallas_call(
        _dil_kernel,
        out_shape=(jax.ShapeDtypeStruct((B, dil, L, GROUP_W), BF16),
                   jax.ShapeDtypeStruct((B, dil, L, GROUP_W), F32)),
        grid=(B, dil, nb // qb),
        in_specs=[cur, prev, cur, prev, cur,
                  pl.BlockSpec((1, N_HEADS, DIL_BLOCK, 2 * DIL_BLOCK), lambda b, r, n, p=p: (p, 0, 0, 0))],
        out_specs=(cur, cur),
        compiler_params=_compiler_params(("arbitrary", "arbitrary", "arbitrary")),
        name=f"dilated_{dil}",
    )(q, k, k, v, v, dil_tb)


def _mix_dilations(o_refs, lse_refs, scratch):
    halves = GROUP_W // LANES
    rows = o_refs[0].shape[1] * o_refs[0].shape[2]
    nat = []
    for ref, dst in zip(list(o_refs) + list(lse_refs), scratch):
        dil = ref.shape[1]
        if dil == 1:
            nat.append(ref[0, 0].astype(F32))
            continue
        for r in range(dil):
            for half in range(halves):
                dst[half, pl.ds(r, rows // dil, stride=dil), :] = (
                    ref[0, r, :, half * LANES:(half + 1) * LANES].astype(F32))
        nat.append(jnp.concatenate([dst[half] for half in range(halves)], axis=1))
    os, lses = nat[:len(o_refs)], nat[len(o_refs):]
    m = functools.reduce(jnp.maximum, lses)
    ws = [jnp.exp(l - m) for l in lses]
    tot = functools.reduce(jnp.add, ws)
    return functools.reduce(jnp.add, [(w / tot) * o for w, o in zip(ws, os)])


def _sigmoid(x):
    return 1.0 / (1.0 + jnp.exp(-x))


def _merge_kernel(x_ref, fo_ref, mo_ref, lo_ref, o1_ref, l1_ref, o4_ref, l4_ref, o16_ref, l16_ref, gate_ref, p_ref,
                  wout_ref, png_ref, wpg_ref, wpp_ref, o_ref, *dil_scratch):
    o_dil = _mix_dilations((o1_ref, o4_ref, o16_ref), (l1_ref, l4_ref, l16_ref), dil_scratch)
    mix = jnp.concatenate([fo_ref[...].astype(F32), mo_ref[...].astype(F32), o_dil, lo_ref[...].astype(F32)],
                          axis=1)
    g = gate_ref[...].astype(F32)
    y = jnp.dot((mix * (g * _sigmoid(g))).astype(BF16), wout_ref[...], preferred_element_type=F32)
    x1 = x_ref[...] + y
    hn = _row_norm(x1, png_ref[...]).astype(BF16)
    pg = _sigmoid(jnp.dot(hn, wpg_ref[...], preferred_element_type=F32))
    pp = jnp.dot(p_ref[0].astype(BF16), wpp_ref[...], preferred_element_type=F32)
    o_ref[...] = x1 + pg * pp


def _merge(layer, x2, fo, mo, lo, dil_outs, proj, p3, lp):
    rows = x2.shape[0]
    TM = TM_MERGE
    tiles_per_seq = dil_outs[0].shape[1] * dil_outs[0].shape[2] // TM

    def rspec(w, c=0):
        return pl.BlockSpec((TM, w), lambda i, c=c: (i, c))

    def cspec(shape):
        return pl.BlockSpec(shape, lambda i: (0, 0))

    def dspec(a):
        dil = a.shape[1]
        return pl.BlockSpec((1, dil, TM // dil, GROUP_W),
                            lambda i: (i // tiles_per_seq, 0, lax.rem(i, tiles_per_seq), 0))

    return pl.pallas_call(
        _merge_kernel,
        out_shape=jax.ShapeDtypeStruct((rows, D_MODEL), F32),
        grid=(rows // TM,),
        scratch_shapes=[pltpu.VMEM((GROUP_W // LANES, TM, LANES), F32)] * len(dil_outs),
        in_specs=[rspec(D_MODEL)] + [rspec(GROUP_W)] * 3 + [dspec(a) for a in dil_outs] + [
                  rspec(D_MODEL, COL_GATE // D_MODEL),
                  pl.BlockSpec((1, TM, PLE_DIM), lambda i: (layer, i, 0)),
                  cspec((D_MODEL, D_MODEL)), cspec((1, D_MODEL)), cspec((D_MODEL, D_MODEL)),
                  cspec((PLE_DIM, D_MODEL))],
        out_specs=rspec(D_MODEL),
        compiler_params=_compiler_params(("arbitrary",)),
        name="merge",
    )(x2, fo, mo, lo, *dil_outs, proj, p3, lp["w_out"], lp["ple_norm_g"], lp["w_ple_gate"], lp["w_ple_proj"])


def _layer_params(i, ln_g, b_forget, qk_gain, mla_q_norm, mla_kv_norm, mla_nope_gain,
                  mla_rope_gain, w_uq, w_ukv, w_out, ple_norm_g, w_ple_gate, w_ple_proj):
    def rope_slab(x1_parts, x2_parts, like):
        pad = jnp.zeros(like.shape[:-1] + (LANES // 2 - 2 * MLA_HALF,), like.dtype)
        return jnp.concatenate(list(x1_parts) + [pad] + list(x2_parts) + [pad], axis=-1)

    per_head = MLA_NOPE + MLA_ROPE
    uq = w_uq[i]
    uq_nope = [uq[:, h * per_head:h * per_head + MLA_NOPE] for h in range(N_HEADS)]
    uq_x1 = [uq[:, h * per_head + MLA_NOPE:h * per_head + MLA_NOPE + MLA_HALF] for h in range(N_HEADS)]
    uq_x2 = [uq[:, h * per_head + MLA_NOPE + MLA_HALF:(h + 1) * per_head] for h in range(N_HEADS)]
    hpg = N_HEADS // MLA_GROUPS
    uq_cols = uq_nope + [rope_slab(uq_x1[g * hpg:(g + 1) * hpg], uq_x2[g * hpg:(g + 1) * hpg], uq)
                         for g in range(MLA_GROUPS)]
    ukv = w_ukv[i]
    ukv_cols = ([ukv[:, h * 2 * HEAD_DIM:h * 2 * HEAD_DIM + HEAD_DIM] for h in range(N_HEADS)]
                + [ukv[:, h * 2 * HEAD_DIM + HEAD_DIM:(h + 1) * 2 * HEAD_DIM] for h in range(N_HEADS)])
    rg = mla_rope_gain[i]
    rope_gain = rope_slab([rg[:, :MLA_HALF]] * hpg, [rg[:, MLA_HALF:]] * hpg, rg)
    return {
        "ln_g": ln_g[i][None, :],
        "gains": jnp.tile(qk_gain[i], (1, N_HEADS)),
        "b_forget": jnp.pad(b_forget[i], (0, LANES - N_HEADS))[None, :],
        "q_norm": mla_q_norm[i][None, :],
        "kv_norm": mla_kv_norm[i][None, :],
        "nope_gain": jnp.tile(mla_nope_gain[i], (1, N_HEADS)),
        "rope_gain": rope_gain,
        "w_uq": jnp.concatenate(uq_cols, axis=1).astype(BF16),
        "w_ukv": jnp.concatenate(ukv_cols, axis=1).astype(BF16),
        "w_out": w_out[i].astype(BF16),
        "ple_norm_g": ple_norm_g[i][None, :],
        "w_ple_gate": w_ple_gate[i].astype(BF16),
        "w_ple_proj": w_ple_proj[i].astype(BF16),
    }


def _constants(S):
    inv = 1.0 / (ROPE_THETA ** (jnp.arange(MLA_HALF, dtype=F32) * 2.0 / MLA_ROPE))
    ang = jnp.arange(S).astype(F32)[:, None] * inv[None, :]
    cos = jnp.tile(jnp.cos(ang), (1, 2 * N_HEADS))
    sin = jnp.tile(jnp.sin(ang), (1, 2 * N_HEADS))
    sign = np.concatenate([-np.ones(LANES // 2, np.float32), np.ones(LANES // 2, np.float32)])
    lane = np.arange(GROUP_W)
    g64 = (lane[:, None] // HEAD_DIM == lane[None, :] // HEAD_DIM).astype(np.float32) / HEAD_DIM
    within = np.arange(LANES) % (LANES // 2)
    rope_head = np.where(within < 2 * MLA_HALF, within // MLA_HALF, -1)
    g32 = ((rope_head[:, None] == rope_head[None, :]) & (rope_head[:, None] >= 0)).astype(np.float32) / MLA_ROPE
    tri = np.tril(np.ones((T_PREP, T_PREP), np.float32))
    ekr = np.zeros((LANES, LANES), np.float32)
    kr_lane0 = IN_OFF["kr"] % LANES
    for c in range(MLA_ROPE):
        half, idx = divmod(c, MLA_HALF)
        for copy in range(N_HEADS // MLA_GROUPS):
            ekr[kr_lane0 + c, half * (LANES // 2) + copy * MLA_HALF + idx] = 1.0
    return {"cos": cos, "sin": sin * sign[None, :], "g64": jnp.asarray(g64, BF16),
            "g32": jnp.asarray(g32, BF16), "tri": jnp.asarray(tri, BF16), "ekr": jnp.asarray(ekr, BF16)}


def kernel(x, p, ln_g, w_in, b_forget, qk_gain, mla_q_norm, mla_kv_norm, mla_nope_gain, mla_rope_gain,
           w_uq, w_ukv, w_out, rel_bias, ple_norm_g, w_ple_gate, w_ple_proj):
    B, S, _ = x.shape
    depth = p.shape[0]
    consts = _constants(S)
    moba_tb, dil_tb = _bias_tables(rel_bias)
    x2 = x.reshape(B * S, D_MODEL)
    for i in range(depth):
        lp = _layer_params(i, ln_g, b_forget, qk_gain, mla_q_norm, mla_kv_norm, mla_nope_gain,
                           mla_rope_gain, w_uq, w_ukv, w_out, ple_norm_g, w_ple_gate, w_ple_proj)
        proj, ff = _in_projection(i, x2, lp["ln_g"], w_in, consts["ekr"])
        (fqT, fk, fvT, fck, fcrow, mqT, mk, mvT, mkmean, lqT, lk, lvT, *dil_qkv) = _prep(
            proj.reshape(B, S, PROJ_W), ff.reshape(B, S, LANES), consts, lp)
        S_ = S
        fo = _dense_mixer(
            _fox_kernel, "fox", fqT, fk, fvT, (fck, fcrow),
            (pl.BlockSpec((1, N_HEADS, S_, LANES), lambda b, i: (b, 0, 0, 0)),
             pl.BlockSpec((1, 8, TQ), lambda b, i: (b, 0, i))))
        mo = _dense_mixer(
            _moba_kernel, "moba", mqT, mk, mvT, (mkmean, moba_tb),
            (pl.BlockSpec((1, S_ // MOBA_BLOCK, GROUP_W), lambda b, i: (b, 0, 0)),
             pl.BlockSpec((MOBA_TB_ENTRIES, N_HEADS, MOBA_BLOCK, MOBA_BLOCK), lambda b, i: (0, 0, 0, 0))),
            (pltpu.VMEM((N_HEADS, S_ // MOBA_BLOCK, TQ), F32),))
        lo = _dense_mixer(_mla_kernel, "mla", lqT, lk, lvT)
        dil_outs = []
        for pat, (_, dil) in enumerate(DIL_PATTERNS):
            q, k, v = (a.reshape(B, dil, S // dil, GROUP_W) for a in dil_qkv[3 * pat:3 * pat + 3])
            dil_outs += list(_dilated_pattern(pat, q, k, v, dil_tb))
        rs = lambda a: a.reshape(B * S, GROUP_W)
        x2 = _merge(i, x2, rs(fo), rs(mo), rs(lo), dil_outs, proj, p.reshape(depth, B * S, PLE_DIM), lp)
    return x2.reshape(B, S, D_MODEL)
```

```python
import functools
import math

import numpy as np
import jax
import jax.numpy as jnp
from jax import lax
from jax.experimental import pallas as pl
from jax.experimental.pallas import tpu as pltpu

F32 = jnp.float32
BF16 = jnp.bfloat16

D_MODEL = 1024
N_HEADS = 4
HEAD_DIM = 64
GROUP_W = N_HEADS * HEAD_DIM
MLA_NOPE = 64
MLA_ROPE = 32
MLA_HALF = MLA_ROPE // 2
MLA_QK_W = N_HEADS * (MLA_NOPE + MLA_ROPE)
MLA_Q_LORA = 256
MLA_KV_LORA = 128
PLE_DIM = 256
MOBA_BLOCK = 256
MOBA_TOPK = 3
DIL_PATTERNS = ((128, 1), (512, 4), (2048, 16))
DIL_BLOCK = 128
DIL_BLOCKS_PER_STEP = 4
N_BUCKETS = 32
MAX_DISTANCE = 2048
ROPE_THETA = 10000.0
EPS = 1e-6
NEG = -1e30
LOG2E = math.log2(math.e)

LANES = 128
VMEM_LIMIT_BYTES = 56 * 1024 * 1024

COL_FOX, COL_MOBA, COL_DIL = 0, 768, 1536
COL_CQ, COL_CKV, COL_KR, COL_FF, COL_GATE = 2304, 2560, 2688, 2816, 3072
PROJ_W = 4096
PROJ_TN = 512
STAGE_ROWS = 128

IN_OFF = {}
IN_W = 0
for _name, _size in (("fq", GROUP_W), ("fk", GROUP_W), ("fv", GROUP_W), ("ff", N_HEADS),
                     ("mq", GROUP_W), ("mk", GROUP_W), ("mv", GROUP_W),
                     ("dq", GROUP_W), ("dk", GROUP_W), ("dv", GROUP_W),
                     ("cq", MLA_Q_LORA), ("ckv", MLA_KV_LORA), ("kr", MLA_ROPE), ("gate", D_MODEL)):
    IN_OFF[_name] = IN_W
    IN_W += _size

TM_PROJ = 512
T_PREP = 256
TQ = 256
TK = 256
SWEEP_UNROLL = 4
MOBA_TB_ENTRIES = 8
TM_MERGE = 512
V_ROWS = HEAD_DIM + 16
MLA_GROUPS = 2


def _bucket_np(d):
    d = np.maximum(np.asarray(d, np.int64), 0)
    max_exact = N_BUCKETS // 2
    d_f = np.maximum(d, 1).astype(np.float64)
    val = np.log(d_f / max_exact) / math.log(MAX_DISTANCE / max_exact) * (N_BUCKETS - max_exact)
    frac = np.abs(val - np.round(val))
    on_edge = (frac < 1e-5) & (d > max_exact) & (val < N_BUCKETS - max_exact - 0.5)
    assert not on_edge.any(), "distance too close to a bucket edge for a static table"
    large = np.minimum(max_exact + np.floor(val + 1e-9).astype(np.int64), N_BUCKETS - 1)
    return np.where(d < max_exact, d, large)


def _bucket_steps(lo, hi):
    ds = np.arange(lo, hi + 1)
    bs = _bucket_np(ds)
    steps = [(int(ds[i]), int(bs[i])) for i in range(1, len(ds)) if bs[i] != bs[i - 1]]
    return int(bs[0]), steps


def _compiler_params(sem):
    return pltpu.CompilerParams(dimension_semantics=sem, vmem_limit_bytes=VMEM_LIMIT_BYTES)


def _bias_from_steps(d, rb_ref, col, lo, hi):
    b0, steps = _bucket_steps(lo, hi)
    val = jnp.full(d.shape, rb_ref[b0, col], F32)
    for t, b in steps:
        val = jnp.where(d >= t, rb_ref[b, col], val)
    return val


def _bias_tables_kernel(rb_ref, moba_ref, dil_ref):
    kl = lax.broadcasted_iota(jnp.int32, (MOBA_BLOCK, MOBA_BLOCK), 0)
    ql = lax.broadcasted_iota(jnp.int32, (MOBA_BLOCK, MOBA_BLOCK), 1)
    for e in range(MOBA_TB_ENTRIES - 1):
        d = jnp.maximum(e * MOBA_BLOCK + ql - kl, 0)
        lo = max(e * MOBA_BLOCK - (MOBA_BLOCK - 1), 0)
        hi = e * MOBA_BLOCK + (MOBA_BLOCK - 1)
        for h in range(N_HEADS):
            moba_ref[e, h] = _bias_from_steps(d, rb_ref, h, lo, hi) * LOG2E
    for h in range(N_HEADS):
        moba_ref[MOBA_TB_ENTRIES - 1, h] = jnp.full((MOBA_BLOCK, MOBA_BLOCK), rb_ref[N_BUCKETS - 1, h], F32) * LOG2E
    qi = lax.broadcasted_iota(jnp.int32, (DIL_BLOCK, 2 * DIL_BLOCK), 0)
    kj = lax.broadcasted_iota(jnp.int32, (DIL_BLOCK, 2 * DIL_BLOCK), 1)
    rel = qi + DIL_BLOCK - kj
    for p, (_, dil) in enumerate(DIL_PATTERNS):
        d = jnp.maximum(rel * dil, 0)
        for h in range(N_HEADS):
            bias = _bias_from_steps(d, rb_ref, N_HEADS + h, 0, (2 * DIL_BLOCK - 1) * dil)
            dil_ref[p, h] = jnp.where((rel >= 0) & (rel <= DIL_BLOCK), bias, NEG)


def _bias_tables(rel_bias):
    assert _bucket_np((MOBA_TB_ENTRIES - 1) * MOBA_BLOCK - (MOBA_BLOCK - 1)) == N_BUCKETS - 1
    return pl.pallas_call(
        _bias_tables_kernel,
        out_shape=(jax.ShapeDtypeStruct((MOBA_TB_ENTRIES, N_HEADS, MOBA_BLOCK, MOBA_BLOCK), F32),
                   jax.ShapeDtypeStruct((len(DIL_PATTERNS), N_HEADS, DIL_BLOCK, 2 * DIL_BLOCK), F32)),
        in_specs=[pl.BlockSpec(memory_space=pltpu.SMEM)],
        out_specs=(pl.BlockSpec(memory_space=pltpu.VMEM), pl.BlockSpec(memory_space=pltpu.VMEM)),
        compiler_params=pltpu.CompilerParams(vmem_limit_bytes=VMEM_LIMIT_BYTES),
        name="bias_tables",
    )(rel_bias)


def _stage_in_weights(w_ref, ekr_ref, ws_ref):
    W = GROUP_W
    moves = ((COL_FOX, IN_OFF["fq"], 3 * W), (COL_MOBA, IN_OFF["mq"], 3 * W), (COL_DIL, IN_OFF["dq"], 3 * W),
             (COL_CQ, IN_OFF["cq"], MLA_Q_LORA), (COL_CKV, IN_OFF["ckv"], MLA_KV_LORA),
             (COL_GATE, IN_OFF["gate"], D_MODEL))
    kr_base = IN_OFF["kr"] // LANES * LANES
    assert IN_OFF["ff"] % LANES == 0 and IN_OFF["kr"] + MLA_ROPE <= kr_base + LANES
    lane = lax.broadcasted_iota(jnp.int32, (STAGE_ROWS, LANES), 1)
    for r0 in range(0, D_MODEL, STAGE_ROWS):
        rows = slice(r0, r0 + STAGE_ROWS)
        for dst, src, width in moves:
            ws_ref[rows, dst:dst + width] = w_ref[0, rows, src:src + width].astype(BF16)
        kr_win = w_ref[0, rows, kr_base:kr_base + LANES].astype(BF16)
        ws_ref[rows, COL_KR:COL_KR + LANES] = jnp.dot(kr_win, ekr_ref[...], preferred_element_type=F32).astype(BF16)
        ff_win = w_ref[0, rows, IN_OFF["ff"]:IN_OFF["ff"] + LANES]
        ws_ref[rows, COL_FF:COL_FF + LANES] = jnp.where(lane < N_HEADS, ff_win, 0.0).astype(BF16)
        ws_ref[rows, COL_FF + LANES:COL_GATE] = jnp.zeros((STAGE_ROWS, COL_GATE - COL_FF - LANES), BF16)


def _proj_kernel(x_ref, g_ref, w_ref, ekr_ref, o_ref, ff_ref, ws_ref):
    @pl.when(pl.program_id(0) == 0)
    def _():
        _stage_in_weights(w_ref, ekr_ref, ws_ref)

    x = x_ref[...]
    ms = jnp.mean(x * x, axis=-1, keepdims=True)
    h = (x * lax.rsqrt(ms + EPS) * g_ref[...]).astype(BF16)
    for c in range(PROJ_W // PROJ_TN):
        sl = slice(c * PROJ_TN, (c + 1) * PROJ_TN)
        o_ref[:, sl] = jnp.dot(h, ws_ref[:, sl], preferred_element_type=F32).astype(BF16)
    ff_ref[...] = jnp.dot(h, ws_ref[:, COL_FF:COL_FF + LANES], preferred_element_type=F32)


def _in_projection(layer, x2, ln_g, w_in, ekr):
    rows = x2.shape[0]
    return pl.pallas_call(
        _proj_kernel,
        out_shape=(jax.ShapeDtypeStruct((rows, PROJ_W), BF16), jax.ShapeDtypeStruct((rows, LANES), F32)),
        grid=(rows // TM_PROJ,),
        in_specs=[pl.BlockSpec((TM_PROJ, D_MODEL), lambda i: (i, 0)),
                  pl.BlockSpec((1, D_MODEL), lambda i: (0, 0)),
                  pl.BlockSpec((1, D_MODEL, IN_W), lambda i: (layer, 0, 0), pipeline_mode=pl.Buffered(1)),
                  pl.BlockSpec((LANES, LANES), lambda i: (0, 0))],
        out_specs=(pl.BlockSpec((TM_PROJ, PROJ_W), lambda i: (i, 0)),
                   pl.BlockSpec((TM_PROJ, LANES), lambda i: (i, 0))),
        scratch_shapes=[pltpu.VMEM((D_MODEL, PROJ_W), BF16)],
        compiler_params=_compiler_params(("arbitrary",)),
        name="in_projection",
    )(x2, ln_g, w_in, ekr)


def _group_mean_sq(x, g_mat):
    return jnp.dot((x * x).astype(BF16), g_mat, preferred_element_type=F32)


def _group_norm(x, gain, g_mat):
    return x * lax.rsqrt(_group_mean_sq(x, g_mat) + EPS) * gain


def _row_norm(x, gain):
    return x * lax.rsqrt(jnp.mean(x * x, axis=-1, keepdims=True) + EPS) * gain


def _log_sigmoid(x):
    return -(jnp.maximum(-x, 0.0) + jnp.log1p(jnp.exp(-jnp.abs(x))))


def _prep_kernel(pf_ref, pm_ref, pd_ref, pcq_ref, pckv_ref, pkr_ref, pff_ref,
                 gains_ref, bfor_ref, qng_ref, kvng_ref, nopeg_ref, ropeg_ref,
                 wuq_ref, wukv_ref, cos_ref, sin_ref, g64_ref, g32_ref, tri_ref,
                 fqT_ref, fk_ref, fvT_ref, fck_ref, fcrow_ref,
                 mqT_ref, mk_ref, mvT_ref, mkmean_ref,
                 lqT_ref, lk_ref, lvT_ref,
                 dq1_ref, dk1_ref, dv1_ref, dq4_ref, dk4_ref, dv4_ref, dq16_ref, dk16_ref, dv16_ref,
                 carry_ref, sq_ref, sk_ref, sv_ref):
    t = pl.program_id(1)
    g64 = g64_ref[...]
    scale = HEAD_DIM ** -0.5
    W = GROUP_W
    ones_rows = jnp.ones((V_ROWS - HEAD_DIM, T_PREP), BF16)

    def store_vT(dst, v):
        vT = v.T.astype(BF16)
        for h in range(N_HEADS):
            dst[0, 0, h] = jnp.concatenate([vT[HEAD_DIM * h:HEAD_DIM * (h + 1)], ones_rows], axis=0)

    pf = pf_ref[0].astype(F32)
    fq = _group_norm(pf[:, :W], gains_ref[0:1, :], g64) * (scale * LOG2E)
    fk = _group_norm(pf[:, W:2 * W], gains_ref[1:2, :], g64)
    fqT_ref[0, 0] = fq.T.astype(BF16)
    fk_ref[0, 0] = fk.astype(BF16)
    store_vT(fvT_ref, pf[:, 2 * W:])

    @pl.when(t == 0)
    def _():
        carry_ref[...] = jnp.zeros_like(carry_ref)

    log_f = _log_sigmoid(pff_ref[0] + bfor_ref[...])
    tri = tri_ref[...]
    c = carry_ref[...]
    rest = log_f
    for _ in range(3):
        part = rest.astype(BF16)
        c = c + jnp.dot(tri, part, preferred_element_type=F32)
        rest = rest - part.astype(F32)
    carry_ref[...] = c[T_PREP - 1:T_PREP, :]
    c2 = c * LOG2E
    for h in range(N_HEADS):
        fck_ref[0, h] = jnp.broadcast_to(c2[:, h:h + 1], (T_PREP, LANES))
    fcrow_ref[0] = c2.T[0:8, :]

    pm = pm_ref[0].astype(F32)
    mq = _group_norm(pm[:, :W], gains_ref[2:3, :], g64) * (scale * LOG2E)
    mk = _group_norm(pm[:, W:2 * W], gains_ref[3:4, :], g64)
    mqT_ref[0, 0] = mq.T.astype(BF16)
    mk_ref[0, 0] = mk.astype(BF16)
    store_vT(mvT_ref, pm[:, 2 * W:])
    mkmean_ref[0, pl.ds(t, 1), :] = jnp.mean(mk, axis=0, keepdims=True)

    pd = pd_ref[0].astype(F32)
    dq = _group_norm(pd[:, :W], gains_ref[4:5, :], g64) * scale
    dk = _group_norm(pd[:, W:2 * W], gains_ref[5:6, :], g64)
    dv = pd[:, 2 * W:]
    dq1_ref[0] = dq.astype(BF16)
    dk1_ref[0] = dk.astype(BF16)
    dv1_ref[0] = pd_ref[0, :, 2 * W:]
    for val, dst in ((dq, sq_ref), (dk, sk_ref), (dv, sv_ref)):
        for half in range(GROUP_W // LANES):
            dst[half] = val[:, half * LANES:(half + 1) * LANES]
    for dil, outs in ((4, (dq4_ref, dk4_ref, dv4_ref)), (16, (dq16_ref, dk16_ref, dv16_ref))):
        n = T_PREP // dil
        for r in range(dil):
            for src, dst in zip((sq_ref, sk_ref, sv_ref), outs):
                dst[0, r] = jnp.concatenate(
                    [src[half, pl.ds(r, n, stride=dil), :] for half in range(GROUP_W // LANES)],
                    axis=1).astype(BF16)

    cos = cos_ref[...]
    sin = sin_ref[...]
    qscale = (MLA_NOPE + MLA_ROPE) ** -0.5
    cqn = _row_norm(pcq_ref[0].astype(F32), qng_ref[...]).astype(BF16)
    rotate = lambda x: x * cos + pltpu.roll(x, LANES // 2, 1) * sin
    qf = jnp.dot(cqn, wuq_ref[...], preferred_element_type=F32)
    q_nope = _group_norm(qf[:, :W], nopeg_ref[0:1, :], g64) * (qscale * LOG2E)
    ckvn = _row_norm(pckv_ref[0].astype(F32), kvng_ref[...]).astype(BF16)
    kvf = jnp.dot(ckvn, wukv_ref[...], preferred_element_type=F32)
    k_nope = _group_norm(kvf[:, :W], nopeg_ref[1:2, :], g64)
    krs = pkr_ref[0].astype(F32)
    kr_ms = jnp.sum(krs * krs, axis=-1, keepdims=True) * (1.0 / (2 * MLA_ROPE))
    kr = rotate(krs * lax.rsqrt(kr_ms + EPS) * ropeg_ref[1:2, :])
    for g in range(MLA_GROUPS):
        slab = qf[:, W + g * LANES:W + (g + 1) * LANES]
        qr = rotate(_group_norm(slab, ropeg_ref[0:1, :], g32_ref[...])) * (qscale * LOG2E)
        lq = jnp.concatenate([q_nope[:, g * LANES:(g + 1) * LANES], qr], axis=1)
        lqT_ref[0, g] = lq.T.astype(BF16)
        lk_ref[0, g] = jnp.concatenate([k_nope[:, g * LANES:(g + 1) * LANES], kr], axis=1).astype(BF16)
    store_vT(lvT_ref, kvf[:, W:])


def _prep(proj3, ff3, consts, lp):
    B, S, _ = proj3.shape
    T = T_PREP
    NT = S // T
    assert S % T == 0 and S // MOBA_BLOCK == NT

    def pspec(width, col):
        return pl.BlockSpec((1, T, width), lambda b, t, c=col // width: (b, t, c))

    def cspec(shape):
        return pl.BlockSpec(shape, lambda b, t: (0,) * len(shape))

    in_specs = [pspec(768, COL_FOX), pspec(768, COL_MOBA), pspec(768, COL_DIL),
                pspec(256, COL_CQ), pspec(128, COL_CKV), pspec(128, COL_KR), pspec(128, 0),
                cspec((6, GROUP_W)), cspec((1, LANES)), cspec((1, MLA_Q_LORA)), cspec((1, MLA_KV_LORA)),
                cspec((2, GROUP_W)), cspec((2, LANES)),
                cspec((MLA_Q_LORA, MLA_GROUPS * GROUP_W)), cspec((MLA_KV_LORA, 2 * GROUP_W)),
                pl.BlockSpec((T, LANES), lambda b, t: (t, 0)), pl.BlockSpec((T, LANES), lambda b, t: (t, 0)),
                cspec((GROUP_W, GROUP_W)), cspec((LANES, LANES)), cspec((T, T))]

    def qT(groups):
        return (jax.ShapeDtypeStruct((B, groups, GROUP_W, S), BF16),
                pl.BlockSpec((1, groups, GROUP_W, T), lambda b, t: (b, 0, 0, t)))

    def keys(groups):
        return (jax.ShapeDtypeStruct((B, groups, S, GROUP_W), BF16),
                pl.BlockSpec((1, groups, T, GROUP_W), lambda b, t: (b, 0, t, 0)))

    def rows(w):
        return jax.ShapeDtypeStruct((B, S, w), BF16), pl.BlockSpec((1, T, w), lambda b, t: (b, t, 0))

    def vT():
        return (jax.ShapeDtypeStruct((B, NT, N_HEADS, V_ROWS, T), BF16),
                pl.BlockSpec((1, 1, N_HEADS, V_ROWS, T), lambda b, t: (b, t, 0, 0, 0)))

    def resid(dil):
        return (jax.ShapeDtypeStruct((B, dil, S // dil, GROUP_W), BF16),
                pl.BlockSpec((1, dil, T // dil, GROUP_W), lambda b, t: (b, 0, t, 0)))

    outs = [qT(1), keys(1), vT(),
            (jax.ShapeDtypeStruct((B, N_HEADS, S, LANES), F32),
             pl.BlockSpec((1, N_HEADS, T, LANES), lambda b, t: (b, 0, t, 0))),
            (jax.ShapeDtypeStruct((B, 8, S), F32), pl.BlockSpec((1, 8, T), lambda b, t: (b, 0, t))),
            qT(1), keys(1), vT(),
            (jax.ShapeDtypeStruct((B, NT, GROUP_W), F32), pl.BlockSpec((1, NT, GROUP_W), lambda b, t: (b, 0, 0))),
            qT(MLA_GROUPS), keys(MLA_GROUPS), vT(),
            rows(GROUP_W), rows(GROUP_W), rows(GROUP_W),
            resid(4), resid(4), resid(4), resid(16), resid(16), resid(16)]
    return pl.pallas_call(
        _prep_kernel,
        out_shape=tuple(o[0] for o in outs),
        grid=(B, NT),
        in_specs=in_specs,
        out_specs=tuple(o[1] for o in outs),
        scratch_shapes=[pltpu.VMEM((1, LANES), F32)] + [pltpu.VMEM((GROUP_W // LANES, T, LANES), F32)] * 3,
        compiler_params=_compiler_params(("arbitrary", "arbitrary")),
        name="prep",
    )(proj3, proj3, proj3, proj3, proj3, proj3, ff3,
      lp["gains"], lp["b_forget"], lp["q_norm"], lp["kv_norm"], lp["nope_gain"], lp["rope_gain"],
      lp["w_uq"], lp["w_ukv"], consts["cos"], consts["sin"], consts["g64"], consts["g32"], consts["tri"])


def _stage_head_queries(qT_ref, qm_ref, groups):
    r = lax.broadcasted_iota(jnp.int32, (GROUP_W, TQ), 0)
    for h in range(N_HEADS):
        if groups == 1:
            g, mask = 0, (r >= HEAD_DIM * h) & (r < HEAD_DIM * (h + 1))
        else:
            g, hl = divmod(h, N_HEADS // groups)
            mask = (r >= HEAD_DIM * hl) & (r < HEAD_DIM * (hl + 1))
            for base in (2 * HEAD_DIM, 2 * HEAD_DIM + LANES // 2):
                lo = base + MLA_HALF * hl
                mask = mask | ((r >= lo) & (r < lo + MLA_HALF))
        q = qT_ref[0, g]
        qm_ref[h] = jnp.where(mask, q, jnp.zeros_like(q))


def _flash_sweep(i, groups, qm_ref, k_ref, vT_ref, m_ref, acc_ref, u_ref, st_ref, o_ref, score_fn):
    kpos = lax.broadcasted_iota(jnp.int32, (TK, TQ), 0)
    qpos = lax.broadcasted_iota(jnp.int32, (TK, TQ), 1)
    causal = kpos <= qpos
    m_ref[...] = jnp.full(m_ref.shape, NEG, F32)
    acc_ref[...] = jnp.zeros(acc_ref.shape, F32)

    def scores(j, slot, diagonal):
        row0 = pl.multiple_of(j * TK, TK)
        for h in range(N_HEADS):
            kt = k_ref[0, h // (N_HEADS // groups), pl.ds(row0, TK), :]
            s = jnp.dot(kt, qm_ref[h], preferred_element_type=F32)
            u, row, live = score_fn(h, s, j, diagonal)
            if diagonal:
                u = jnp.where(causal, u, NEG)
            u_ref[slot, h] = u
            m_old = m_ref[h]
            top = jnp.max(u, axis=0, keepdims=True)
            if row is not None:
                top = top + row
            if live is not None:
                top = jnp.where(live, top, NEG)
            m_new = jnp.maximum(m_old, top)
            shift = m_new if row is None else m_new - row
            if live is not None:
                shift = jnp.where(live, shift, -NEG)
            st_ref[slot, h] = shift
            st_ref[slot, N_HEADS + h] = jnp.exp2(m_old - m_new)
            m_ref[h] = m_new

    def values(j, slot):
        for h in range(N_HEADS):
            p = jnp.exp2((u_ref[slot, h] - st_ref[slot, h]).astype(BF16))
            acc_ref[h] = (st_ref[slot, N_HEADS + h] * acc_ref[h]
                          + jnp.dot(vT_ref[0, j, h], p, preferred_element_type=F32))

    scores(i, 0, True)

    def step(n, slot):
        scores(n - 1, slot, False)
        values(jnp.where(n == 1, i, n - 2), 1 - slot)

    def unrolled(t, carry):
        n = 1 + SWEEP_UNROLL * t
        for k in range(SWEEP_UNROLL):
            step(n + k, (1 + k) % 2)
        return carry

    n_unrolled = i // SWEEP_UNROLL
    lax.fori_loop(0, n_unrolled, unrolled, 0)
    left = i - SWEEP_UNROLL * n_unrolled
    for k in range(SWEEP_UNROLL - 1):
        def leftover(_, carry, k=k):
            step(1 + SWEEP_UNROLL * n_unrolled + k, (1 + k) % 2)
            return carry

        lax.fori_loop(0, (left > k).astype(jnp.int32), leftover, 0)
    values(jnp.maximum(i - 1, 0), lax.rem(i, 2))
    outs = []
    for h in range(N_HEADS):
        acc = acc_ref[h]
        outs.append(acc[:HEAD_DIM] / acc[HEAD_DIM:HEAD_DIM + 1])
    o_ref[0] = jnp.concatenate(outs, axis=0).T.astype(BF16)


def _fox_kernel(qT_ref, k_ref, vT_ref, ck_ref, crow_ref, o_ref, qm_ref, m_ref, acc_ref, u_ref, st_ref):
    i = pl.program_id(1)
    _stage_head_queries(qT_ref, qm_ref, 1)

    def score_fn(h, s, j, diagonal):
        ck = ck_ref[0, h, pl.ds(pl.multiple_of(j * TK, TK), TK), :]
        return s - jnp.concatenate([ck] * (TQ // LANES), axis=1), crow_ref[0, h:h + 1, :], None

    _flash_sweep(i, 1, qm_ref, k_ref, vT_ref, m_ref, acc_ref, u_ref, st_ref, o_ref, score_fn)


def _mla_kernel(qT_ref, k_ref, vT_ref, o_ref, qm_ref, m_ref, acc_ref, u_ref, st_ref):
    i = pl.program_id(1)
    _stage_head_queries(qT_ref, qm_ref, MLA_GROUPS)
    _flash_sweep(i, MLA_GROUPS, qm_ref, k_ref, vT_ref, m_ref, acc_ref, u_ref, st_ref, o_ref,
                 lambda h, s, j, diagonal: (s, None, None))


def _moba_kernel(qT_ref, k_ref, vT_ref, kmean_ref, tb_ref, o_ref, qm_ref, m_ref, acc_ref, u_ref, st_ref, sel_ref):
    i = pl.program_id(1)
    _stage_head_queries(qT_ref, qm_ref, 1)
    kmean = kmean_ref[0].astype(BF16)
    nb = kmean.shape[0]
    blk = lax.broadcasted_iota(jnp.int32, (nb, TQ), 0)
    for h in range(N_HEADS):
        gate = jnp.dot(kmean, qm_ref[h], preferred_element_type=F32)
        gate = jnp.where(blk < i, gate, NEG)
        rank = jnp.zeros((nb, TQ), jnp.int32)
        for jp in range(nb):
            row = gate[jp:jp + 1, :]
            ahead = (row > gate) | ((row == gate) & (jp < blk))
            rank = rank + ahead.astype(jnp.int32)
        sel_ref[h] = ((rank < MOBA_TOPK) & (blk < i)).astype(F32)

    def score_fn(h, s, j, diagonal):
        u = s + tb_ref[jnp.minimum(i - j, MOBA_TB_ENTRIES - 1), h]
        live = None if diagonal else sel_ref[h, pl.ds(j, 1), :] > 0.5
        return u, None, live

    _flash_sweep(i, 1, qm_ref, k_ref, vT_ref, m_ref, acc_ref, u_ref, st_ref, o_ref, score_fn)


def _dense_mixer(kernel_fn, name, qT, k, vT, extra_args=(), extra_specs=(), extra_scratch=()):
    B, groups, _, S = qT.shape
    NT = S // TK
    assert TQ == TK and S % TQ == 0
    in_specs = [pl.BlockSpec((1, groups, GROUP_W, TQ), lambda b, i: (b, 0, 0, i)),
                pl.BlockSpec((1, groups, S, GROUP_W), lambda b, i: (b, 0, 0, 0)),
                pl.BlockSpec((1, NT, N_HEADS, V_ROWS, TK), lambda b, i: (b, 0, 0, 0, 0))] + list(extra_specs)
    scratch = [pltpu.VMEM((N_HEADS, GROUP_W, TQ), BF16), pltpu.VMEM((N_HEADS, 1, TQ), F32),
               pltpu.VMEM((N_HEADS, V_ROWS, TQ), F32), pltpu.VMEM((2, N_HEADS, TK, TQ), F32), pltpu.VMEM((2, 2 * N_HEADS, 1, TQ), F32)] + list(extra_scratch)
    return pl.pallas_call(
        kernel_fn,
        out_shape=jax.ShapeDtypeStruct((B, S, GROUP_W), BF16),
        grid=(B, S // TQ),
        in_specs=in_specs,
        out_specs=pl.BlockSpec((1, TQ, GROUP_W), lambda b, i: (b, i, 0)),
        scratch_shapes=scratch,
        compiler_params=_compiler_params(("arbitrary", "arbitrary")),
        name=name,
    )(qT, k, vT, *extra_args)


def _dil_kernel(q_ref, kp_ref, kc_ref, vp_ref, vc_ref, tb_ref, o_ref, lse_ref):
    n = pl.program_id(2)
    qb = q_ref.shape[2] // DIL_BLOCK
    kj = lax.broadcasted_iota(jnp.int32, (DIL_BLOCK, 2 * DIL_BLOCK), 1)
    qlane = lax.broadcasted_iota(jnp.int32, (DIL_BLOCK, GROUP_W), 1)
    in_head = [(qlane >= HEAD_DIM * h) & (qlane < HEAD_DIM * (h + 1)) for h in range(N_HEADS)]
    for seq, c in [(seq, c) for seq in range(q_ref.shape[1]) for c in range(qb)]:
        rows = slice(c * DIL_BLOCK, (c + 1) * DIL_BLOCK)
        q = q_ref[0, seq, rows, :]
        if c == 0:
            kb = jnp.concatenate([kp_ref[0, seq], kc_ref[0, seq, :DIL_BLOCK, :]], axis=0)
            vb = jnp.concatenate([vp_ref[0, seq], vc_ref[0, seq, :DIL_BLOCK, :]], axis=0)
        else:
            kb = kc_ref[0, seq, (c - 1) * DIL_BLOCK:(c + 1) * DIL_BLOCK, :]
            vb = vc_ref[0, seq, (c - 1) * DIL_BLOCK:(c + 1) * DIL_BLOCK, :]
        q_heads = jnp.concatenate([jnp.where(msk, q, jnp.zeros_like(q)) for msk in in_head], axis=0)
        s = lax.dot_general(q_heads, kb, (((1,), (1,)), ((), ())), preferred_element_type=F32)
        s = s.reshape(N_HEADS, DIL_BLOCK, 2 * DIL_BLOCK)
        s = s + tb_ref[0]
        if c == 0:
            s = jnp.where(((kj >= DIL_BLOCK) | (n > 0))[None], s, NEG)
        m = jnp.max(s, axis=-1, keepdims=True)
        e = jnp.exp(s - m)
        l = jnp.sum(e, axis=-1, keepdims=True)
        lse = m + jnp.log(l)
        pr = (e * (1.0 / l)).astype(BF16)
        o_all = jnp.dot(pr.reshape(N_HEADS * DIL_BLOCK, 2 * DIL_BLOCK), vb, preferred_element_type=F32)
        o_all = o_all.reshape(N_HEADS, DIL_BLOCK, GROUP_W)
        o_acc = jnp.zeros((DIL_BLOCK, GROUP_W), F32)
        lse_map = jnp.zeros((DIL_BLOCK, GROUP_W), F32)
        for h in range(N_HEADS):
            o_acc = jnp.where(in_head[h], o_all[h], o_acc)
            lse_map = jnp.where(in_head[h], lse[h], lse_map)
        o_ref[0, seq, rows, :] = o_acc.astype(BF16)
        lse_ref[0, seq, rows, :] = lse_map


def _dilated_pattern(p, q, k, v, dil_tb):
    B, dil, L, _ = q.shape
    assert dil == DIL_PATTERNS[p][1] and L % DIL_BLOCK == 0
    nb = L // DIL_BLOCK
    qb = math.gcd(nb, DIL_BLOCKS_PER_STEP)
    seqs = math.gcd(dil, DIL_BLOCKS_PER_STEP // qb)
    cur = pl.BlockSpec((1, seqs, qb * DIL_BLOCK, GROUP_W), lambda b, r, n: (b, r, n, 0))
    prev = pl.BlockSpec((1, seqs, DIL_BLOCK, GROUP_W), lambda b, r, n: (b, r, jnp.maximum(n * qb - 1, 0), 0))
    return pl.pallas_call(
        _dil_kernel,
        out_shape=(jax.ShapeDtypeStruct((B, dil, L, GROUP_W), BF16),
                   jax.ShapeDtypeStruct((B, dil, L, GROUP_W), F32)),
        grid=(B, dil // seqs, nb // qb),
        in_specs=[cur, prev, cur, prev, cur,
                  pl.BlockSpec((1, N_HEADS, DIL_BLOCK, 2 * DIL_BLOCK), lambda b, r, n, p=p: (p, 0, 0, 0))],
        out_specs=(cur, cur),
        compiler_params=_compiler_params(("arbitrary", "arbitrary", "arbitrary")),
        name=f"dilated_{dil}",
    )(q, k, k, v, v, dil_tb)


def _mix_dilations(o_refs, lse_refs, scratch):
    halves = GROUP_W // LANES
    rows = o_refs[0].shape[1] * o_refs[0].shape[2]
    nat = []
    for ref, dst in zip(list(o_refs) + list(lse_refs), scratch):
        dil = ref.shape[1]
        if dil == 1:
            nat.append(ref[0, 0].astype(F32))
            continue
        for r in range(dil):
            for half in range(halves):
                dst[half, pl.ds(r, rows // dil, stride=dil), :] = (
                    ref[0, r, :, half * LANES:(half + 1) * LANES].astype(F32))
        nat.append(jnp.concatenate([dst[half] for half in range(halves)], axis=1))
    os, lses = nat[:len(o_refs)], nat[len(o_refs):]
    m = functools.reduce(jnp.maximum, lses)
    ws = [jnp.exp(l - m) for l in lses]
    tot = functools.reduce(jnp.add, ws)
    return functools.reduce(jnp.add, [(w / tot) * o for w, o in zip(ws, os)])


def _sigmoid(x):
    return 1.0 / (1.0 + jnp.exp(-x))


def _merge_kernel(x_ref, fo_ref, mo_ref, lo_ref, o1_ref, l1_ref, o4_ref, l4_ref, o16_ref, l16_ref, gate_ref, p_ref,
                  wout_ref, png_ref, wpg_ref, wpp_ref, o_ref, *dil_scratch):
    o_dil = _mix_dilations((o1_ref, o4_ref, o16_ref), (l1_ref, l4_ref, l16_ref), dil_scratch)
    mix = jnp.concatenate([fo_ref[...].astype(F32), mo_ref[...].astype(F32), o_dil, lo_ref[...].astype(F32)],
                          axis=1)
    g = gate_ref[...].astype(F32)
    y = jnp.dot((mix * (g * _sigmoid(g))).astype(BF16), wout_ref[...], preferred_element_type=F32)
    x1 = x_ref[...] + y
    hn = _row_norm(x1, png_ref[...]).astype(BF16)
    pg = _sigmoid(jnp.dot(hn, wpg_ref[...], preferred_element_type=F32))
    pp = jnp.dot(p_ref[0].astype(BF16), wpp_ref[...], preferred_element_type=F32)
    o_ref[...] = x1 + pg * pp


def _merge(layer, x2, fo, mo, lo, dil_outs, proj, p3, lp):
    rows = x2.shape[0]
    TM = TM_MERGE
    tiles_per_seq = dil_outs[0].shape[1] * dil_outs[0].shape[2] // TM

    def rspec(w, c=0):
        return pl.BlockSpec((TM, w), lambda i, c=c: (i, c))

    def cspec(shape):
        return pl.BlockSpec(shape, lambda i: (0, 0))

    def dspec(a):
        dil = a.shape[1]
        return pl.BlockSpec((1, dil, TM // dil, GROUP_W),
                            lambda i: (i // tiles_per_seq, 0, lax.rem(i, tiles_per_seq), 0))

    return pl.pallas_call(
        _merge_kernel,
        out_shape=jax.ShapeDtypeStruct((rows, D_MODEL), F32),
        grid=(rows // TM,),
        scratch_shapes=[pltpu.VMEM((GROUP_W // LANES, TM, LANES), F32)] * len(dil_outs),
        in_specs=[rspec(D_MODEL)] + [rspec(GROUP_W)] * 3 + [dspec(a) for a in dil_outs] + [
                  rspec(D_MODEL, COL_GATE // D_MODEL),
                  pl.BlockSpec((1, TM, PLE_DIM), lambda i: (layer, i, 0)),
                  cspec((D_MODEL, D_MODEL)), cspec((1, D_MODEL)), cspec((D_MODEL, D_MODEL)),
                  cspec((PLE_DIM, D_MODEL))],
        out_specs=rspec(D_MODEL),
        compiler_params=_compiler_params(("arbitrary",)),
        name="merge",
    )(x2, fo, mo, lo, *dil_outs, proj, p3, lp["w_out"], lp["ple_norm_g"], lp["w_ple_gate"], lp["w_ple_proj"])


def _layer_params(i, ln_g, b_forget, qk_gain, mla_q_norm, mla_kv_norm, mla_nope_gain,
                  mla_rope_gain, w_uq, w_ukv, w_out, ple_norm_g, w_ple_gate, w_ple_proj):
    def rope_slab(x1_parts, x2_parts, like):
        pad = jnp.zeros(like.shape[:-1] + (LANES // 2 - 2 * MLA_HALF,), like.dtype)
        return jnp.concatenate(list(x1_parts) + [pad] + list(x2_parts) + [pad], axis=-1)

    per_head = MLA_NOPE + MLA_ROPE
    uq = w_uq[i]
    uq_nope = [uq[:, h * per_head:h * per_head + MLA_NOPE] for h in range(N_HEADS)]
    uq_x1 = [uq[:, h * per_head + MLA_NOPE:h * per_head + MLA_NOPE + MLA_HALF] for h in range(N_HEADS)]
    uq_x2 = [uq[:, h * per_head + MLA_NOPE + MLA_HALF:(h + 1) * per_head] for h in range(N_HEADS)]
    hpg = N_HEADS // MLA_GROUPS
    uq_cols = uq_nope + [rope_slab(uq_x1[g * hpg:(g + 1) * hpg], uq_x2[g * hpg:(g + 1) * hpg], uq)
                         for g in range(MLA_GROUPS)]
    ukv = w_ukv[i]
    ukv_cols = ([ukv[:, h * 2 * HEAD_DIM:h * 2 * HEAD_DIM + HEAD_DIM] for h in range(N_HEADS)]
                + [ukv[:, h * 2 * HEAD_DIM + HEAD_DIM:(h + 1) * 2 * HEAD_DIM] for h in range(N_HEADS)])
    rg = mla_rope_gain[i]
    rope_gain = rope_slab([rg[:, :MLA_HALF]] * hpg, [rg[:, MLA_HALF:]] * hpg, rg)
    return {
        "ln_g": ln_g[i][None, :],
        "gains": jnp.tile(qk_gain[i], (1, N_HEADS)),
        "b_forget": jnp.pad(b_forget[i], (0, LANES - N_HEADS))[None, :],
        "q_norm": mla_q_norm[i][None, :],
        "kv_norm": mla_kv_norm[i][None, :],
        "nope_gain": jnp.tile(mla_nope_gain[i], (1, N_HEADS)),
        "rope_gain": rope_gain,
        "w_uq": jnp.concatenate(uq_cols, axis=1).astype(BF16),
        "w_ukv": jnp.concatenate(ukv_cols, axis=1).astype(BF16),
        "w_out": w_out[i].astype(BF16),
        "ple_norm_g": ple_norm_g[i][None, :],
        "w_ple_gate": w_ple_gate[i].astype(BF16),
        "w_ple_proj": w_ple_proj[i].astype(BF16),
    }


def _constants(S):
    inv = 1.0 / (ROPE_THETA ** (jnp.arange(MLA_HALF, dtype=F32) * 2.0 / MLA_ROPE))
    ang = jnp.arange(S).astype(F32)[:, None] * inv[None, :]
    cos = jnp.tile(jnp.cos(ang), (1, 2 * N_HEADS))
    sin = jnp.tile(jnp.sin(ang), (1, 2 * N_HEADS))
    sign = np.concatenate([-np.ones(LANES // 2, np.float32), np.ones(LANES // 2, np.float32)])
    lane = np.arange(GROUP_W)
    g64 = (lane[:, None] // HEAD_DIM == lane[None, :] // HEAD_DIM).astype(np.float32) / HEAD_DIM
    within = np.arange(LANES) % (LANES // 2)
    rope_head = np.where(within < 2 * MLA_HALF, within // MLA_HALF, -1)
    g32 = ((rope_head[:, None] == rope_head[None, :]) & (rope_head[:, None] >= 0)).astype(np.float32) / MLA_ROPE
    tri = np.tril(np.ones((T_PREP, T_PREP), np.float32))
    ekr = np.zeros((LANES, LANES), np.float32)
    kr_lane0 = IN_OFF["kr"] % LANES
    for c in range(MLA_ROPE):
        half, idx = divmod(c, MLA_HALF)
        for copy in range(N_HEADS // MLA_GROUPS):
            ekr[kr_lane0 + c, half * (LANES // 2) + copy * MLA_HALF + idx] = 1.0
    return {"cos": cos, "sin": sin * sign[None, :], "g64": jnp.asarray(g64, BF16),
            "g32": jnp.asarray(g32, BF16), "tri": jnp.asarray(tri, BF16), "ekr": jnp.asarray(ekr, BF16)}


def kernel(x, p, ln_g, w_in, b_forget, qk_gain, mla_q_norm, mla_kv_norm, mla_nope_gain, mla_rope_gain,
           w_uq, w_ukv, w_out, rel_bias, ple_norm_g, w_ple_gate, w_ple_proj):
    B, S, _ = x.shape
    depth = p.shape[0]
    consts = _constants(S)
    moba_tb, dil_tb = _bias_tables(rel_bias)
    x2 = x.reshape(B * S, D_MODEL)
    for i in range(depth):
        lp = _layer_params(i, ln_g, b_forget, qk_gain, mla_q_norm, mla_kv_norm, mla_nope_gain,
                           mla_rope_gain, w_uq, w_ukv, w_out, ple_norm_g, w_ple_gate, w_ple_proj)
        proj, ff = _in_projection(i, x2, lp["ln_g"], w_in, consts["ekr"])
        (fqT, fk, fvT, fck, fcrow, mqT, mk, mvT, mkmean, lqT, lk, lvT, *dil_qkv) = _prep(
            proj.reshape(B, S, PROJ_W), ff.reshape(B, S, LANES), consts, lp)
        S_ = S
        fo = _dense_mixer(
            _fox_kernel, "fox", fqT, fk, fvT, (fck, fcrow),
            (pl.BlockSpec((1, N_HEADS, S_, LANES), lambda b, i: (b, 0, 0, 0)),
             pl.BlockSpec((1, 8, TQ), lambda b, i: (b, 0, i))))
        mo = _dense_mixer(
            _moba_kernel, "moba", mqT, mk, mvT, (mkmean, moba_tb),
            (pl.BlockSpec((1, S_ // MOBA_BLOCK, GROUP_W), lambda b, i: (b, 0, 0)),
             pl.BlockSpec((MOBA_TB_ENTRIES, N_HEADS, MOBA_BLOCK, MOBA_BLOCK), lambda b, i: (0, 0, 0, 0))),
            (pltpu.VMEM((N_HEADS, S_ // MOBA_BLOCK, TQ), F32),))
        lo = _dense_mixer(_mla_kernel, "mla", lqT, lk, lvT)
        dil_outs = []
        for pat, (_, dil) in enumerate(DIL_PATTERNS):
            q, k, v = (a.reshape(B, dil, S // dil, GROUP_W) for a in dil_qkv[3 * pat:3 * pat + 3])
            dil_outs += list(_dilated_pattern(pat, q, k, v, dil_tb))
        rs = lambda a: a.reshape(B * S, GROUP_W)
        x2 = _merge(i, x2, rs(fo), rs(mo), rs(lo), dil_outs, proj, p.reshape(depth, B * S, PLE_DIM), lp)
    return x2.reshape(B, S, D_MODEL)
```

```python
import functools
import math

import numpy as np
import jax
import jax.numpy as jnp
from jax import lax
from jax.experimental import pallas as pl
from jax.experimental.pallas import tpu as pltpu

F32 = jnp.float32
BF16 = jnp.bfloat16

D_MODEL = 1024
N_HEADS = 4
HEAD_DIM = 64
GROUP_W = N_HEADS * HEAD_DIM
MLA_NOPE = 64
MLA_ROPE = 32
MLA_HALF = MLA_ROPE // 2
MLA_QK_W = N_HEADS * (MLA_NOPE + MLA_ROPE)
MLA_Q_LORA = 256
MLA_KV_LORA = 128
PLE_DIM = 256
MOBA_BLOCK = 256
MOBA_TOPK = 3
DIL_PATTERNS = ((128, 1), (512, 4), (2048, 16))
DIL_BLOCK = 128
DIL_BLOCKS_PER_STEP = 4
N_BUCKETS = 32
MAX_DISTANCE = 2048
ROPE_THETA = 10000.0
EPS = 1e-6
NEG = -1e30
LOG2E = math.log2(math.e)

LANES = 128
VMEM_LIMIT_BYTES = 56 * 1024 * 1024

COL_FOX, COL_MOBA, COL_DIL = 0, 768, 1536
COL_CQ, COL_CKV, COL_KR, COL_FF, COL_GATE = 2304, 2560, 2688, 2816, 3072
PROJ_W = 4096
PROJ_TN = 512
STAGE_ROWS = 128

IN_OFF = {}
IN_W = 0
for _name, _size in (("fq", GROUP_W), ("fk", GROUP_W), ("fv", GROUP_W), ("ff", N_HEADS),
                     ("mq", GROUP_W), ("mk", GROUP_W), ("mv", GROUP_W),
                     ("dq", GROUP_W), ("dk", GROUP_W), ("dv", GROUP_W),
                     ("cq", MLA_Q_LORA), ("ckv", MLA_KV_LORA), ("kr", MLA_ROPE), ("gate", D_MODEL)):
    IN_OFF[_name] = IN_W
    IN_W += _size

TM_PROJ = 512
T_PREP = 256
TQ = 256
TK = 256
N_SUB = 2
MOBA_TB_ENTRIES = 8
TM_MERGE = 512
V_ROWS = HEAD_DIM + 16
MLA_GROUPS = 2


def _bucket_np(d):
    d = np.maximum(np.asarray(d, np.int64), 0)
    max_exact = N_BUCKETS // 2
    d_f = np.maximum(d, 1).astype(np.float64)
    val = np.log(d_f / max_exact) / math.log(MAX_DISTANCE / max_exact) * (N_BUCKETS - max_exact)
    frac = np.abs(val - np.round(val))
    on_edge = (frac < 1e-5) & (d > max_exact) & (val < N_BUCKETS - max_exact - 0.5)
    assert not on_edge.any(), "distance too close to a bucket edge for a static table"
    large = np.minimum(max_exact + np.floor(val + 1e-9).astype(np.int64), N_BUCKETS - 1)
    return np.where(d < max_exact, d, large)


def _bucket_steps(lo, hi):
    ds = np.arange(lo, hi + 1)
    bs = _bucket_np(ds)
    steps = [(int(ds[i]), int(bs[i])) for i in range(1, len(ds)) if bs[i] != bs[i - 1]]
    return int(bs[0]), steps


def _compiler_params(sem):
    return pltpu.CompilerParams(dimension_semantics=sem, vmem_limit_bytes=VMEM_LIMIT_BYTES)


def _bias_from_steps(d, rb_ref, col, lo, hi):
    b0, steps = _bucket_steps(lo, hi)
    val = jnp.full(d.shape, rb_ref[b0, col], F32)
    for t, b in steps:
        val = jnp.where(d >= t, rb_ref[b, col], val)
    return val


def _bias_tables_kernel(rb_ref, moba_ref, dil_ref):
    kl = lax.broadcasted_iota(jnp.int32, (MOBA_BLOCK, MOBA_BLOCK), 0)
    ql = lax.broadcasted_iota(jnp.int32, (MOBA_BLOCK, MOBA_BLOCK), 1)
    for e in range(MOBA_TB_ENTRIES - 1):
        d = jnp.maximum(e * MOBA_BLOCK + ql - kl, 0)
        lo = max(e * MOBA_BLOCK - (MOBA_BLOCK - 1), 0)
        hi = e * MOBA_BLOCK + (MOBA_BLOCK - 1)
        for h in range(N_HEADS):
            moba_ref[e, h] = _bias_from_steps(d, rb_ref, h, lo, hi) * LOG2E
    for h in range(N_HEADS):
        moba_ref[MOBA_TB_ENTRIES - 1, h] = jnp.full((MOBA_BLOCK, MOBA_BLOCK), rb_ref[N_BUCKETS - 1, h], F32) * LOG2E
    qi = lax.broadcasted_iota(jnp.int32, (DIL_BLOCK, 2 * DIL_BLOCK), 0)
    kj = lax.broadcasted_iota(jnp.int32, (DIL_BLOCK, 2 * DIL_BLOCK), 1)
    rel = qi + DIL_BLOCK - kj
    for p, (_, dil) in enumerate(DIL_PATTERNS):
        d = jnp.maximum(rel * dil, 0)
        for h in range(N_HEADS):
            bias = _bias_from_steps(d, rb_ref, N_HEADS + h, 0, (2 * DIL_BLOCK - 1) * dil)
            dil_ref[p, h] = jnp.where((rel >= 0) & (rel <= DIL_BLOCK), bias, NEG)


def _bias_tables(rel_bias):
    assert _bucket_np((MOBA_TB_ENTRIES - 1) * MOBA_BLOCK - (MOBA_BLOCK - 1)) == N_BUCKETS - 1
    return pl.pallas_call(
        _bias_tables_kernel,
        out_shape=(jax.ShapeDtypeStruct((MOBA_TB_ENTRIES, N_HEADS, MOBA_BLOCK, MOBA_BLOCK), F32),
                   jax.ShapeDtypeStruct((len(DIL_PATTERNS), N_HEADS, DIL_BLOCK, 2 * DIL_BLOCK), F32)),
        in_specs=[pl.BlockSpec(memory_space=pltpu.SMEM)],
        out_specs=(pl.BlockSpec(memory_space=pltpu.VMEM), pl.BlockSpec(memory_space=pltpu.VMEM)),
        compiler_params=pltpu.CompilerParams(vmem_limit_bytes=VMEM_LIMIT_BYTES),
        name="bias_tables",
    )(rel_bias)


def _stage_in_weights(w_ref, ekr_ref, ws_ref):
    W = GROUP_W
    moves = ((COL_FOX, IN_OFF["fq"], 3 * W), (COL_MOBA, IN_OFF["mq"], 3 * W), (COL_DIL, IN_OFF["dq"], 3 * W),
             (COL_CQ, IN_OFF["cq"], MLA_Q_LORA), (COL_CKV, IN_OFF["ckv"], MLA_KV_LORA),
             (COL_GATE, IN_OFF["gate"], D_MODEL))
    kr_base = IN_OFF["kr"] // LANES * LANES
    assert IN_OFF["ff"] % LANES == 0 and IN_OFF["kr"] + MLA_ROPE <= kr_base + LANES
    lane = lax.broadcasted_iota(jnp.int32, (STAGE_ROWS, LANES), 1)
    for r0 in range(0, D_MODEL, STAGE_ROWS):
        rows = slice(r0, r0 + STAGE_ROWS)
        for dst, src, width in moves:
            ws_ref[rows, dst:dst + width] = w_ref[0, rows, src:src + width].astype(BF16)
        kr_win = w_ref[0, rows, kr_base:kr_base + LANES].astype(BF16)
        ws_ref[rows, COL_KR:COL_KR + LANES] = jnp.dot(kr_win, ekr_ref[...], preferred_element_type=F32).astype(BF16)
        ff_win = w_ref[0, rows, IN_OFF["ff"]:IN_OFF["ff"] + LANES]
        ws_ref[rows, COL_FF:COL_FF + LANES] = jnp.where(lane < N_HEADS, ff_win, 0.0).astype(BF16)
        ws_ref[rows, COL_FF + LANES:COL_GATE] = jnp.zeros((STAGE_ROWS, COL_GATE - COL_FF - LANES), BF16)


def _proj_kernel(x_ref, g_ref, w_ref, ekr_ref, o_ref, ff_ref, ws_ref):
    @pl.when(pl.program_id(0) == 0)
    def _():
        _stage_in_weights(w_ref, ekr_ref, ws_ref)

    x = x_ref[...]
    ms = jnp.mean(x * x, axis=-1, keepdims=True)
    h = (x * lax.rsqrt(ms + EPS) * g_ref[...]).astype(BF16)
    for c in range(PROJ_W // PROJ_TN):
        sl = slice(c * PROJ_TN, (c + 1) * PROJ_TN)
        o_ref[:, sl] = jnp.dot(h, ws_ref[:, sl], preferred_element_type=F32).astype(BF16)
    ff_ref[...] = jnp.dot(h, ws_ref[:, COL_FF:COL_FF + LANES], preferred_element_type=F32)


def _in_projection(layer, x2, ln_g, w_in, ekr):
    rows = x2.shape[0]
    return pl.pallas_call(
        _proj_kernel,
        out_shape=(jax.ShapeDtypeStruct((rows, PROJ_W), BF16), jax.ShapeDtypeStruct((rows, LANES), F32)),
        grid=(rows // TM_PROJ,),
        in_specs=[pl.BlockSpec((TM_PROJ, D_MODEL), lambda i: (i, 0)),
                  pl.BlockSpec((1, D_MODEL), lambda i: (0, 0)),
                  pl.BlockSpec((1, D_MODEL, IN_W), lambda i: (layer, 0, 0), pipeline_mode=pl.Buffered(1)),
                  pl.BlockSpec((LANES, LANES), lambda i: (0, 0))],
        out_specs=(pl.BlockSpec((TM_PROJ, PROJ_W), lambda i: (i, 0)),
                   pl.BlockSpec((TM_PROJ, LANES), lambda i: (i, 0))),
        scratch_shapes=[pltpu.VMEM((D_MODEL, PROJ_W), BF16)],
        compiler_params=_compiler_params(("arbitrary",)),
        name="in_projection",
    )(x2, ln_g, w_in, ekr)


def _group_mean_sq(x, g_mat):
    return jnp.dot((x * x).astype(BF16), g_mat, preferred_element_type=F32)


def _group_norm(x, gain, g_mat):
    return x * lax.rsqrt(_group_mean_sq(x, g_mat) + EPS) * gain


def _row_norm(x, gain):
    return x * lax.rsqrt(jnp.mean(x * x, axis=-1, keepdims=True) + EPS) * gain


def _log_sigmoid(x):
    return -(jnp.maximum(-x, 0.0) + jnp.log1p(jnp.exp(-jnp.abs(x))))


def _prep_kernel(pf_ref, pm_ref, pd_ref, pcq_ref, pckv_ref, pkr_ref, pff_ref,
                 gains_ref, bfor_ref, qng_ref, kvng_ref, nopeg_ref, ropeg_ref,
                 wuq_ref, wukv_ref, cos_ref, sin_ref, g64_ref, g32_ref, tri_ref,
                 fqT_ref, fk_ref, fvT_ref, fck_ref, fcrow_ref,
                 mqT_ref, mk_ref, mvT_ref, mkmean_ref,
                 lqT_ref, lk_ref, lvT_ref,
                 dq1_ref, dk1_ref, dv1_ref, dq4_ref, dk4_ref, dv4_ref, dq16_ref, dk16_ref, dv16_ref,
                 carry_ref, sq_ref, sk_ref, sv_ref):
    t = pl.program_id(1)
    g64 = g64_ref[...]
    scale = HEAD_DIM ** -0.5
    W = GROUP_W
    ones_rows = jnp.ones((V_ROWS - HEAD_DIM, T_PREP), BF16)

    def store_vT(dst, v):
        vT = v.T.astype(BF16)
        for h in range(N_HEADS):
            dst[0, 0, h] = jnp.concatenate([vT[HEAD_DIM * h:HEAD_DIM * (h + 1)], ones_rows], axis=0)

    pf = pf_ref[0].astype(F32)
    fq = _group_norm(pf[:, :W], gains_ref[0:1, :], g64) * (scale * LOG2E)
    fk = _group_norm(pf[:, W:2 * W], gains_ref[1:2, :], g64)
    fqT_ref[0, 0] = fq.T.astype(BF16)
    fk_ref[0, 0] = fk.astype(BF16)
    store_vT(fvT_ref, pf[:, 2 * W:])

    @pl.when(t == 0)
    def _():
        carry_ref[...] = jnp.zeros_like(carry_ref)

    log_f = _log_sigmoid(pff_ref[0] + bfor_ref[...])
    tri = tri_ref[...]
    c = carry_ref[...]
    rest = log_f
    for _ in range(3):
        part = rest.astype(BF16)
        c = c + jnp.dot(tri, part, preferred_element_type=F32)
        rest = rest - part.astype(F32)
    carry_ref[...] = c[T_PREP - 1:T_PREP, :]
    c2 = c * LOG2E
    for h in range(N_HEADS):
        fck_ref[0, h] = jnp.broadcast_to(c2[:, h:h + 1], (T_PREP, LANES))
    fcrow_ref[0] = c2.T[0:8, :]

    pm = pm_ref[0].astype(F32)
    mq = _group_norm(pm[:, :W], gains_ref[2:3, :], g64) * (scale * LOG2E)
    mk = _group_norm(pm[:, W:2 * W], gains_ref[3:4, :], g64)
    mqT_ref[0, 0] = mq.T.astype(BF16)
    mk_ref[0, 0] = mk.astype(BF16)
    store_vT(mvT_ref, pm[:, 2 * W:])
    mkmean_ref[0, pl.ds(t, 1), :] = jnp.mean(mk, axis=0, keepdims=True)

    pd = pd_ref[0].astype(F32)
    dq = _group_norm(pd[:, :W], gains_ref[4:5, :], g64) * scale
    dk = _group_norm(pd[:, W:2 * W], gains_ref[5:6, :], g64)
    dv = pd[:, 2 * W:]
    dq1_ref[0] = dq.astype(BF16)
    dk1_ref[0] = dk.astype(BF16)
    dv1_ref[0] = pd_ref[0, :, 2 * W:]
    for val, dst in ((dq, sq_ref), (dk, sk_ref), (dv, sv_ref)):
        for half in range(GROUP_W // LANES):
            dst[half] = val[:, half * LANES:(half + 1) * LANES]
    for dil, outs in ((4, (dq4_ref, dk4_ref, dv4_ref)), (16, (dq16_ref, dk16_ref, dv16_ref))):
        n = T_PREP // dil
        for r in range(dil):
            for src, dst in zip((sq_ref, sk_ref, sv_ref), outs):
                dst[0, r] = jnp.concatenate(
                    [src[half, pl.ds(r, n, stride=dil), :] for half in range(GROUP_W // LANES)],
                    axis=1).astype(BF16)

    cos = cos_ref[...]
    sin = sin_ref[...]
    qscale = (MLA_NOPE + MLA_ROPE) ** -0.5
    cqn = _row_norm(pcq_ref[0].astype(F32), qng_ref[...]).astype(BF16)
    rotate = lambda x: x * cos + pltpu.roll(x, LANES // 2, 1) * sin
    qf = jnp.dot(cqn, wuq_ref[...], preferred_element_type=F32)
    q_nope = _group_norm(qf[:, :W], nopeg_ref[0:1, :], g64) * (qscale * LOG2E)
    ckvn = _row_norm(pckv_ref[0].astype(F32), kvng_ref[...]).astype(BF16)
    kvf = jnp.dot(ckvn, wukv_ref[...], preferred_element_type=F32)
    k_nope = _group_norm(kvf[:, :W], nopeg_ref[1:2, :], g64)
    krs = pkr_ref[0].astype(F32)
    kr_ms = jnp.sum(krs * krs, axis=-1, keepdims=True) * (1.0 / (2 * MLA_ROPE))
    kr = rotate(krs * lax.rsqrt(kr_ms + EPS) * ropeg_ref[1:2, :])
    for g in range(MLA_GROUPS):
        slab = qf[:, W + g * LANES:W + (g + 1) * LANES]
        qr = rotate(_group_norm(slab, ropeg_ref[0:1, :], g32_ref[...])) * (qscale * LOG2E)
        lq = jnp.concatenate([q_nope[:, g * LANES:(g + 1) * LANES], qr], axis=1)
        lqT_ref[0, g] = lq.T.astype(BF16)
        lk_ref[0, g] = jnp.concatenate([k_nope[:, g * LANES:(g + 1) * LANES], kr], axis=1).astype(BF16)
    store_vT(lvT_ref, kvf[:, W:])


def _prep(proj3, ff3, consts, lp):
    B, S, _ = proj3.shape
    T = T_PREP
    NT = S // T
    assert S % T == 0 and S // MOBA_BLOCK == NT

    def pspec(width, col):
        return pl.BlockSpec((1, T, width), lambda b, t, c=col // width: (b, t, c))

    def cspec(shape):
        return pl.BlockSpec(shape, lambda b, t: (0,) * len(shape))

    in_specs = [pspec(768, COL_FOX), pspec(768, COL_MOBA), pspec(768, COL_DIL),
                pspec(256, COL_CQ), pspec(128, COL_CKV), pspec(128, COL_KR), pspec(128, 0),
                cspec((6, GROUP_W)), cspec((1, LANES)), cspec((1, MLA_Q_LORA)), cspec((1, MLA_KV_LORA)),
                cspec((2, GROUP_W)), cspec((2, LANES)),
                cspec((MLA_Q_LORA, MLA_GROUPS * GROUP_W)), cspec((MLA_KV_LORA, 2 * GROUP_W)),
                pl.BlockSpec((T, LANES), lambda b, t: (t, 0)), pl.BlockSpec((T, LANES), lambda b, t: (t, 0)),
                cspec((GROUP_W, GROUP_W)), cspec((LANES, LANES)), cspec((T, T))]

    def qT(groups):
        return (jax.ShapeDtypeStruct((B, groups, GROUP_W, S), BF16),
                pl.BlockSpec((1, groups, GROUP_W, T), lambda b, t: (b, 0, 0, t)))

    def keys(groups):
        return (jax.ShapeDtypeStruct((B, groups, S, GROUP_W), BF16),
                pl.BlockSpec((1, groups, T, GROUP_W), lambda b, t: (b, 0, t, 0)))

    def rows(w):
        return jax.ShapeDtypeStruct((B, S, w), BF16), pl.BlockSpec((1, T, w), lambda b, t: (b, t, 0))

    def vT():
        return (jax.ShapeDtypeStruct((B, NT, N_HEADS, V_ROWS, T), BF16),
                pl.BlockSpec((1, 1, N_HEADS, V_ROWS, T), lambda b, t: (b, t, 0, 0, 0)))

    def resid(dil):
        return (jax.ShapeDtypeStruct((B, dil, S // dil, GROUP_W), BF16),
                pl.BlockSpec((1, dil, T // dil, GROUP_W), lambda b, t: (b, 0, t, 0)))

    outs = [qT(1), keys(1), vT(),
            (jax.ShapeDtypeStruct((B, N_HEADS, S, LANES), F32),
             pl.BlockSpec((1, N_HEADS, T, LANES), lambda b, t: (b, 0, t, 0))),
            (jax.ShapeDtypeStruct((B, 8, S), F32), pl.BlockSpec((1, 8, T), lambda b, t: (b, 0, t))),
            qT(1), keys(1), vT(),
            (jax.ShapeDtypeStruct((B, NT, GROUP_W), F32), pl.BlockSpec((1, NT, GROUP_W), lambda b, t: (b, 0, 0))),
            qT(MLA_GROUPS), keys(MLA_GROUPS), vT(),
            rows(GROUP_W), rows(GROUP_W), rows(GROUP_W),
            resid(4), resid(4), resid(4), resid(16), resid(16), resid(16)]
    return pl.pallas_call(
        _prep_kernel,
        out_shape=tuple(o[0] for o in outs),
        grid=(B, NT),
        in_specs=in_specs,
        out_specs=tuple(o[1] for o in outs),
        scratch_shapes=[pltpu.VMEM((1, LANES), F32)] + [pltpu.VMEM((GROUP_W // LANES, T, LANES), F32)] * 3,
        compiler_params=_compiler_params(("arbitrary", "arbitrary")),
        name="prep",
    )(proj3, proj3, proj3, proj3, proj3, proj3, ff3,
      lp["gains"], lp["b_forget"], lp["q_norm"], lp["kv_norm"], lp["nope_gain"], lp["rope_gain"],
      lp["w_uq"], lp["w_ukv"], consts["cos"], consts["sin"], consts["g64"], consts["g32"], consts["tri"])


def _stage_head_queries(qT_ref, qm_ref, groups):
    r = lax.broadcasted_iota(jnp.int32, (GROUP_W, TQ), 0)
    for h in range(N_HEADS):
        if groups == 1:
            g, mask = 0, (r >= HEAD_DIM * h) & (r < HEAD_DIM * (h + 1))
        else:
            g, hl = divmod(h, N_HEADS // groups)
            mask = (r >= HEAD_DIM * hl) & (r < HEAD_DIM * (hl + 1))
            for base in (2 * HEAD_DIM, 2 * HEAD_DIM + LANES // 2):
                lo = base + MLA_HALF * hl
                mask = mask | ((r >= lo) & (r < lo + MLA_HALF))
        for sub in range(N_SUB):
            q = qT_ref[0, g, :, sub * TQ:(sub + 1) * TQ]
            qm_ref[sub * N_HEADS + h] = jnp.where(mask, q, jnp.zeros_like(q))


def _flash_sweep(pair_idx, groups, qm_ref, k_ref, vT_ref, m_ref, acc_ref, u_ref, st_ref, o_ref, score_fn):
    kpos = lax.broadcasted_iota(jnp.int32, (TK, TQ), 0)
    qpos = lax.broadcasted_iota(jnp.int32, (TK, TQ), 1)
    causal = kpos <= qpos
    m_ref[...] = jnp.full(m_ref.shape, NEG, F32)
    acc_ref[...] = jnp.zeros(acc_ref.shape, F32)
    n_chain = N_SUB * N_HEADS
    both = tuple(range(N_SUB))
    q_tile = [N_SUB * pair_idx + sub for sub in both]

    def scores(j, slot, diagonal, subs):
        row0 = pl.multiple_of(j * TK, TK)
        kts = [k_ref[0, g, pl.ds(row0, TK), :] for g in range(groups)]
        for sub in subs:
            for h in range(N_HEADS):
                c = sub * N_HEADS + h
                s = jnp.dot(kts[h // (N_HEADS // groups)], qm_ref[c], preferred_element_type=F32)
                u, row, live = score_fn(sub, h, s, j, diagonal)
                if diagonal:
                    u = jnp.where(causal, u, NEG)
                u_ref[slot, c] = u
                m_old = m_ref[c]
                top = jnp.max(u, axis=0, keepdims=True)
                if row is not None:
                    top = top + row
                if live is not None:
                    top = jnp.where(live, top, NEG)
                m_new = jnp.maximum(m_old, top)
                shift = m_new if row is None else m_new - row
                if live is not None:
                    shift = jnp.where(live, shift, -NEG)
                st_ref[slot, c] = shift
                st_ref[slot, n_chain + c] = jnp.exp2(m_old - m_new)
                m_ref[c] = m_new

    def values(j, slot, subs):
        for sub in subs:
            for h in range(N_HEADS):
                c = sub * N_HEADS + h
                p = jnp.exp2((u_ref[slot, c] - st_ref[slot, c]).astype(BF16))
                acc_ref[c] = (st_ref[slot, n_chain + c] * acc_ref[c]
                              + jnp.dot(vT_ref[0, j, h], p, preferred_element_type=F32))

    for sub in both:
        scores(q_tile[sub], 0, True, (sub,))

    def step(n, slot):
        scores(n - 1, slot, False, both)
        for sub in both:
            values(jnp.where(n == 1, q_tile[sub], n - 2), 1 - slot, (sub,))

    def two_steps(t, carry):
        step(1 + 2 * t, 1)
        step(2 + 2 * t, 0)
        return carry

    lax.fori_loop(0, pair_idx, two_steps, 0)
    n_common = N_SUB * pair_idx
    pending = [jnp.where(pair_idx == 0, q_tile[sub], n_common - 1) for sub in both]
    scores(n_common, 1, False, (1,))
    values(pending[0], 0, (0,))
    values(pending[1], 0, (1,))
    values(n_common, 1, (1,))
    for sub in both:
        outs = []
        for h in range(N_HEADS):
            acc = acc_ref[sub * N_HEADS + h]
            outs.append(acc[:HEAD_DIM] / acc[HEAD_DIM:HEAD_DIM + 1])
        o_ref[0, sub * TQ:(sub + 1) * TQ, :] = jnp.concatenate(outs, axis=0).T.astype(BF16)


def _fox_kernel(qT_ref, k_ref, vT_ref, ck_ref, crow_ref, o_ref, qm_ref, m_ref, acc_ref, u_ref, st_ref):
    pair_idx = pl.program_id(1)
    _stage_head_queries(qT_ref, qm_ref, 1)

    def score_fn(sub, h, s, j, diagonal):
        ck = ck_ref[0, h, pl.ds(pl.multiple_of(j * TK, TK), TK), :]
        cq = crow_ref[0, h:h + 1, sub * TQ:(sub + 1) * TQ]
        return s - jnp.concatenate([ck] * (TQ // LANES), axis=1), cq, None

    _flash_sweep(pair_idx, 1, qm_ref, k_ref, vT_ref, m_ref, acc_ref, u_ref, st_ref, o_ref, score_fn)


def _mla_kernel(qT_ref, k_ref, vT_ref, o_ref, qm_ref, m_ref, acc_ref, u_ref, st_ref):
    pair_idx = pl.program_id(1)
    _stage_head_queries(qT_ref, qm_ref, MLA_GROUPS)
    _flash_sweep(pair_idx, MLA_GROUPS, qm_ref, k_ref, vT_ref, m_ref, acc_ref, u_ref, st_ref, o_ref,
                 lambda sub, h, s, j, diagonal: (s, None, None))


def _moba_kernel(qT_ref, k_ref, vT_ref, kmean_ref, tb_ref, o_ref, qm_ref, m_ref, acc_ref, u_ref, st_ref, sel_ref):
    pair_idx = pl.program_id(1)
    _stage_head_queries(qT_ref, qm_ref, 1)
    kmean = kmean_ref[0].astype(BF16)
    nb = kmean.shape[0]
    blk = lax.broadcasted_iota(jnp.int32, (nb, TQ), 0)
    own = [N_SUB * pair_idx + sub for sub in range(N_SUB)]
    for c in range(N_SUB * N_HEADS):
        i = own[c // N_HEADS]
        gate = jnp.dot(kmean, qm_ref[c], preferred_element_type=F32)
        gate = jnp.where(blk < i, gate, NEG)
        rank = jnp.zeros((nb, TQ), jnp.int32)
        for jp in range(nb):
            row = gate[jp:jp + 1, :]
            ahead = (row > gate) | ((row == gate) & (jp < blk))
            rank = rank + ahead.astype(jnp.int32)
        sel_ref[c] = ((rank < MOBA_TOPK) & (blk < i)).astype(F32)

    def score_fn(sub, h, s, j, diagonal):
        u = s + tb_ref[jnp.minimum(own[sub] - j, MOBA_TB_ENTRIES - 1), h]
        live = None if diagonal else sel_ref[sub * N_HEADS + h, pl.ds(j, 1), :] > 0.5
        return u, None, live

    _flash_sweep(pair_idx, 1, qm_ref, k_ref, vT_ref, m_ref, acc_ref, u_ref, st_ref, o_ref, score_fn)


def _dense_mixer(kernel_fn, name, qT, k, vT, extra_args=(), extra_specs=(), extra_scratch=()):
    B, groups, _, S = qT.shape
    NT = S // TK
    assert TQ == TK and N_SUB == 2 and S % (N_SUB * TQ) == 0
    n_chain = N_SUB * N_HEADS
    in_specs = [pl.BlockSpec((1, groups, GROUP_W, N_SUB * TQ), lambda b, i: (b, 0, 0, i)),
                pl.BlockSpec((1, groups, S, GROUP_W), lambda b, i: (b, 0, 0, 0)),
                pl.BlockSpec((1, NT, N_HEADS, V_ROWS, TK), lambda b, i: (b, 0, 0, 0, 0))] + list(extra_specs)
    scratch = [pltpu.VMEM((n_chain, GROUP_W, TQ), BF16),
               pltpu.VMEM((n_chain, 1, TQ), F32),
               pltpu.VMEM((n_chain, V_ROWS, TQ), F32),
               pltpu.VMEM((2, n_chain, TK, TQ), F32),
               pltpu.VMEM((2, 2 * n_chain, 1, TQ), F32)] + list(extra_scratch)
    return pl.pallas_call(
        kernel_fn,
        out_shape=jax.ShapeDtypeStruct((B, S, GROUP_W), BF16),
        grid=(B, S // (N_SUB * TQ)),
        in_specs=in_specs,
        out_specs=pl.BlockSpec((1, N_SUB * TQ, GROUP_W), lambda b, i: (b, i, 0)),
        scratch_shapes=scratch,
        compiler_params=_compiler_params(("arbitrary", "arbitrary")),
        name=name,
    )(qT, k, vT, *extra_args)


def _dil_kernel(q_ref, kp_ref, kc_ref, vp_ref, vc_ref, tb_ref, o_ref, lse_ref):
    n = pl.program_id(2)
    qb = q_ref.shape[2] // DIL_BLOCK
    kj = lax.broadcasted_iota(jnp.int32, (DIL_BLOCK, 2 * DIL_BLOCK), 1)
    qlane = lax.broadcasted_iota(jnp.int32, (DIL_BLOCK, GROUP_W), 1)
    in_head = [(qlane >= HEAD_DIM * h) & (qlane < HEAD_DIM * (h + 1)) for h in range(N_HEADS)]
    for seq, c in [(seq, c) for seq in range(q_ref.shape[1]) for c in range(qb)]:
        rows = slice(c * DIL_BLOCK, (c + 1) * DIL_BLOCK)
        q = q_ref[0, seq, rows, :]
        if c == 0:
            kb = jnp.concatenate([kp_ref[0, seq], kc_ref[0, seq, :DIL_BLOCK, :]], axis=0)
            vb = jnp.concatenate([vp_ref[0, seq], vc_ref[0, seq, :DIL_BLOCK, :]], axis=0)
        else:
            kb = kc_ref[0, seq, (c - 1) * DIL_BLOCK:(c + 1) * DIL_BLOCK, :]
            vb = vc_ref[0, seq, (c - 1) * DIL_BLOCK:(c + 1) * DIL_BLOCK, :]
        q_heads = jnp.concatenate([jnp.where(msk, q, jnp.zeros_like(q)) for msk in in_head], axis=0)
        s = lax.dot_general(q_heads, kb, (((1,), (1,)), ((), ())), preferred_element_type=F32)
        s = s.reshape(N_HEADS, DIL_BLOCK, 2 * DIL_BLOCK)
        s = s + tb_ref[0]
        if c == 0:
            s = jnp.where(((kj >= DIL_BLOCK) | (n > 0))[None], s, NEG)
        m = jnp.max(s, axis=-1, keepdims=True)
        e = jnp.exp(s - m)
        l = jnp.sum(e, axis=-1, keepdims=True)
        lse = m + jnp.log(l)
        pr = (e * (1.0 / l)).astype(BF16)
        o_all = jnp.dot(pr.reshape(N_HEADS * DIL_BLOCK, 2 * DIL_BLOCK), vb, preferred_element_type=F32)
        o_all = o_all.reshape(N_HEADS, DIL_BLOCK, GROUP_W)
        o_acc = jnp.zeros((DIL_BLOCK, GROUP_W), F32)
        lse_map = jnp.zeros((DIL_BLOCK, GROUP_W), F32)
        for h in range(N_HEADS):
            o_acc = jnp.where(in_head[h], o_all[h], o_acc)
            lse_map = jnp.where(in_head[h], lse[h], lse_map)
        o_ref[0, seq, rows, :] = o_acc.astype(BF16)
        lse_ref[0, seq, rows, :] = lse_map


def _dilated_pattern(p, q, k, v, dil_tb):
    B, dil, L, _ = q.shape
    assert dil == DIL_PATTERNS[p][1] and L % DIL_BLOCK == 0
    nb = L // DIL_BLOCK
    qb = math.gcd(nb, DIL_BLOCKS_PER_STEP)
    seqs = math.gcd(dil, DIL_BLOCKS_PER_STEP // qb)
    cur = pl.BlockSpec((1, seqs, qb * DIL_BLOCK, GROUP_W), lambda b, r, n: (b, r, n, 0))
    prev = pl.BlockSpec((1, seqs, DIL_BLOCK, GROUP_W), lambda b, r, n: (b, r, jnp.maximum(n * qb - 1, 0), 0))
    return pl.pallas_call(
        _dil_kernel,
        out_shape=(jax.ShapeDtypeStruct((B, dil, L, GROUP_W), BF16),
                   jax.ShapeDtypeStruct((B, dil, L, GROUP_W), F32)),
        grid=(B, dil // seqs, nb // qb),
        in_specs=[cur, prev, cur, prev, cur,
                  pl.BlockSpec((1, N_HEADS, DIL_BLOCK, 2 * DIL_BLOCK), lambda b, r, n, p=p: (p, 0, 0, 0))],
        out_specs=(cur, cur),
        compiler_params=_compiler_params(("arbitrary", "arbitrary", "arbitrary")),
        name=f"dilated_{dil}",
    )(q, k, k, v, v, dil_tb)


def _mix_dilations(o_refs, lse_refs, scratch):
    halves = GROUP_W // LANES
    rows = o_refs[0].shape[1] * o_refs[0].shape[2]
    nat = []
    for ref, dst in zip(list(o_refs) + list(lse_refs), scratch):
        dil = ref.shape[1]
        if dil == 1:
            nat.append(ref[0, 0].astype(F32))
            continue
        for r in range(dil):
            for half in range(halves):
                dst[half, pl.ds(r, rows // dil, stride=dil), :] = (
                    ref[0, r, :, half * LANES:(half + 1) * LANES].astype(F32))
        nat.append(jnp.concatenate([dst[half] for half in range(halves)], axis=1))
    os, lses = nat[:len(o_refs)], nat[len(o_refs):]
    m = functools.reduce(jnp.maximum, lses)
    ws = [jnp.exp(l - m) for l in lses]
    tot = functools.reduce(jnp.add, ws)
    return functools.reduce(jnp.add, [(w / tot) * o for w, o in zip(ws, os)])


def _sigmoid(x):
    return 1.0 / (1.0 + jnp.exp(-x))


def _merge_kernel(x_ref, fo_ref, mo_ref, lo_ref, o1_ref, l1_ref, o4_ref, l4_ref, o16_ref, l16_ref, gate_ref, p_ref,
                  wout_ref, png_ref, wpg_ref, wpp_ref, o_ref, *dil_scratch):
    o_dil = _mix_dilations((o1_ref, o4_ref, o16_ref), (l1_ref, l4_ref, l16_ref), dil_scratch)
    mix = jnp.concatenate([fo_ref[...].astype(F32), mo_ref[...].astype(F32), o_dil, lo_ref[...].astype(F32)],
                          axis=1)
    g = gate_ref[...].astype(F32)
    y = jnp.dot((mix * (g * _sigmoid(g))).astype(BF16), wout_ref[...], preferred_element_type=F32)
    x1 = x_ref[...] + y
    hn = _row_norm(x1, png_ref[...]).astype(BF16)
    pg = _sigmoid(jnp.dot(hn, wpg_ref[...], preferred_element_type=F32))
    pp = jnp.dot(p_ref[0].astype(BF16), wpp_ref[...], preferred_element_type=F32)
    o_ref[...] = x1 + pg * pp


def _merge(layer, x2, fo, mo, lo, dil_outs, proj, p3, lp):
    rows = x2.shape[0]
    TM = TM_MERGE
    tiles_per_seq = dil_outs[0].shape[1] * dil_outs[0].shape[2] // TM

    def rspec(w, c=0):
        return pl.BlockSpec((TM, w), lambda i, c=c: (i, c))

    def cspec(shape):
        return pl.BlockSpec(shape, lambda i: (0, 0))

    def dspec(a):
        dil = a.shape[1]
        return pl.BlockSpec((1, dil, TM // dil, GROUP_W),
                            lambda i: (i // tiles_per_seq, 0, lax.rem(i, tiles_per_seq), 0))

    return pl.pallas_call(
        _merge_kernel,
        out_shape=jax.ShapeDtypeStruct((rows, D_MODEL), F32),
        grid=(rows // TM,),
        scratch_shapes=[pltpu.VMEM((GROUP_W // LANES, TM, LANES), F32)] * len(dil_outs),
        in_specs=[rspec(D_MODEL)] + [rspec(GROUP_W)] * 3 + [dspec(a) for a in dil_outs] + [
                  rspec(D_MODEL, COL_GATE // D_MODEL),
                  pl.BlockSpec((1, TM, PLE_DIM), lambda i: (layer, i, 0)),
                  cspec((D_MODEL, D_MODEL)), cspec((1, D_MODEL)), cspec((D_MODEL, D_MODEL)),
                  cspec((PLE_DIM, D_MODEL))],
        out_specs=rspec(D_MODEL),
        compiler_params=_compiler_params(("arbitrary",)),
        name="merge",
    )(x2, fo, mo, lo, *dil_outs, proj, p3, lp["w_out"], lp["ple_norm_g"], lp["w_ple_gate"], lp["w_ple_proj"])


def _layer_params(i, ln_g, b_forget, qk_gain, mla_q_norm, mla_kv_norm, mla_nope_gain,
                  mla_rope_gain, w_uq, w_ukv, w_out, ple_norm_g, w_ple_gate, w_ple_proj):
    def rope_slab(x1_parts, x2_parts, like):
        pad = jnp.zeros(like.shape[:-1] + (LANES // 2 - 2 * MLA_HALF,), like.dtype)
        return jnp.concatenate(list(x1_parts) + [pad] + list(x2_parts) + [pad], axis=-1)

    per_head = MLA_NOPE + MLA_ROPE
    uq = w_uq[i]
    uq_nope = [uq[:, h * per_head:h * per_head + MLA_NOPE] for h in range(N_HEADS)]
    uq_x1 = [uq[:, h * per_head + MLA_NOPE:h * per_head + MLA_NOPE + MLA_HALF] for h in range(N_HEADS)]
    uq_x2 = [uq[:, h * per_head + MLA_NOPE + MLA_HALF:(h + 1) * per_head] for h in range(N_HEADS)]
    hpg = N_HEADS // MLA_GROUPS
    uq_cols = uq_nope + [rope_slab(uq_x1[g * hpg:(g + 1) * hpg], uq_x2[g * hpg:(g + 1) * hpg], uq)
                         for g in range(MLA_GROUPS)]
    ukv = w_ukv[i]
    ukv_cols = ([ukv[:, h * 2 * HEAD_DIM:h * 2 * HEAD_DIM + HEAD_DIM] for h in range(N_HEADS)]
                + [ukv[:, h * 2 * HEAD_DIM + HEAD_DIM:(h + 1) * 2 * HEAD_DIM] for h in range(N_HEADS)])
    rg = mla_rope_gain[i]
    rope_gain = rope_slab([rg[:, :MLA_HALF]] * hpg, [rg[:, MLA_HALF:]] * hpg, rg)
    return {
        "ln_g": ln_g[i][None, :],
        "gains": jnp.tile(qk_gain[i], (1, N_HEADS)),
        "b_forget": jnp.pad(b_forget[i], (0, LANES - N_HEADS))[None, :],
        "q_norm": mla_q_norm[i][None, :],
        "kv_norm": mla_kv_norm[i][None, :],
        "nope_gain": jnp.tile(mla_nope_gain[i], (1, N_HEADS)),
        "rope_gain": rope_gain,
        "w_uq": jnp.concatenate(uq_cols, axis=1).astype(BF16),
        "w_ukv": jnp.concatenate(ukv_cols, axis=1).astype(BF16),
        "w_out": w_out[i].astype(BF16),
        "ple_norm_g": ple_norm_g[i][None, :],
        "w_ple_gate": w_ple_gate[i].astype(BF16),
        "w_ple_proj": w_ple_proj[i].astype(BF16),
    }


def _constants(S):
    inv = 1.0 / (ROPE_THETA ** (jnp.arange(MLA_HALF, dtype=F32) * 2.0 / MLA_ROPE))
    ang = jnp.arange(S).astype(F32)[:, None] * inv[None, :]
    cos = jnp.tile(jnp.cos(ang), (1, 2 * N_HEADS))
    sin = jnp.tile(jnp.sin(ang), (1, 2 * N_HEADS))
    sign = np.concatenate([-np.ones(LANES // 2, np.float32), np.ones(LANES // 2, np.float32)])
    lane = np.arange(GROUP_W)
    g64 = (lane[:, None] // HEAD_DIM == lane[None, :] // HEAD_DIM).astype(np.float32) / HEAD_DIM
    within = np.arange(LANES) % (LANES // 2)
    rope_head = np.where(within < 2 * MLA_HALF, within // MLA_HALF, -1)
    g32 = ((rope_head[:, None] == rope_head[None, :]) & (rope_head[:, None] >= 0)).astype(np.float32) / MLA_ROPE
    tri = np.tril(np.ones((T_PREP, T_PREP), np.float32))
    ekr = np.zeros((LANES, LANES), np.float32)
    kr_lane0 = IN_OFF["kr"] % LANES
    for c in range(MLA_ROPE):
        half, idx = divmod(c, MLA_HALF)
        for copy in range(N_HEADS // MLA_GROUPS):
            ekr[kr_lane0 + c, half * (LANES // 2) + copy * MLA_HALF + idx] = 1.0
    return {"cos": cos, "sin": sin * sign[None, :], "g64": jnp.asarray(g64, BF16),
            "g32": jnp.asarray(g32, BF16), "tri": jnp.asarray(tri, BF16), "ekr": jnp.asarray(ekr, BF16)}


def kernel(x, p, ln_g, w_in, b_forget, qk_gain, mla_q_norm, mla_kv_norm, mla_nope_gain, mla_rope_gain,
           w_uq, w_ukv, w_out, rel_bias, ple_norm_g, w_ple_gate, w_ple_proj):
    B, S, _ = x.shape
    depth = p.shape[0]
    consts = _constants(S)
    moba_tb, dil_tb = _bias_tables(rel_bias)
    x2 = x.reshape(B * S, D_MODEL)
    for i in range(depth):
        lp = _layer_params(i, ln_g, b_forget, qk_gain, mla_q_norm, mla_kv_norm, mla_nope_gain,
                           mla_rope_gain, w_uq, w_ukv, w_out, ple_norm_g, w_ple_gate, w_ple_proj)
        proj, ff = _in_projection(i, x2, lp["ln_g"], w_in, consts["ekr"])
        (fqT, fk, fvT, fck, fcrow, mqT, mk, mvT, mkmean, lqT, lk, lvT, *dil_qkv) = _prep(
            proj.reshape(B, S, PROJ_W), ff.reshape(B, S, LANES), consts, lp)
        S_ = S
        fo = _dense_mixer(
            _fox_kernel, "fox", fqT, fk, fvT, (fck, fcrow),
            (pl.BlockSpec((1, N_HEADS, S_, LANES), lambda b, i: (b, 0, 0, 0)),
             pl.BlockSpec((1, 8, N_SUB * TQ), lambda b, i: (b, 0, i))))
        mo = _dense_mixer(
            _moba_kernel, "moba", mqT, mk, mvT, (mkmean, moba_tb),
            (pl.BlockSpec((1, S_ // MOBA_BLOCK, GROUP_W), lambda b, i: (b, 0, 0)),
             pl.BlockSpec((MOBA_TB_ENTRIES, N_HEADS, MOBA_BLOCK, MOBA_BLOCK), lambda b, i: (0, 0, 0, 0))),
            (pltpu.VMEM((N_SUB * N_HEADS, S_ // MOBA_BLOCK, TQ), F32),))
        lo = _dense_mixer(_mla_kernel, "mla", lqT, lk, lvT)
        dil_outs = []
        for pat, (_, dil) in enumerate(DIL_PATTERNS):
            q, k, v = (a.reshape(B, dil, S // dil, GROUP_W) for a in dil_qkv[3 * pat:3 * pat + 3])
            dil_outs += list(_dilated_pattern(pat, q, k, v, dil_tb))
        rs = lambda a: a.reshape(B * S, GROUP_W)
        x2 = _merge(i, x2, rs(fo), rs(mo), rs(lo), dil_outs, proj, p.reshape(depth, B * S, PLE_DIM), lp)
    return x2.reshape(B, S, D_MODEL)
```

```python
import functools
import math

import numpy as np
import jax
import jax.numpy as jnp
from jax import lax
from jax.experimental import pallas as pl
from jax.experimental.pallas import tpu as pltpu

F32 = jnp.float32
BF16 = jnp.bfloat16

D_MODEL = 1024
N_HEADS = 4
HEAD_DIM = 64
GROUP_W = N_HEADS * HEAD_DIM
MLA_NOPE = 64
MLA_ROPE = 32
MLA_HALF = MLA_ROPE // 2
MLA_QK_W = N_HEADS * (MLA_NOPE + MLA_ROPE)
MLA_Q_LORA = 256
MLA_KV_LORA = 128
PLE_DIM = 256
MOBA_BLOCK = 256
MOBA_TOPK = 3
DIL_PATTERNS = ((128, 1), (512, 4), (2048, 16))
DIL_BLOCK = 128
DIL_BLOCKS_PER_STEP = 4
N_BUCKETS = 32
MAX_DISTANCE = 2048
ROPE_THETA = 10000.0
EPS = 1e-6
NEG = -1e30
LOG2E = math.log2(math.e)

LANES = 128
VMEM_LIMIT_BYTES = 56 * 1024 * 1024

COL_FOX, COL_MOBA, COL_DIL = 0, 768, 1536
COL_CQ, COL_CKV, COL_KR, COL_FF, COL_GATE = 2304, 2560, 2688, 2816, 3072
PROJ_W = 4096
PROJ_TN = 512
STAGE_ROWS = 128

IN_OFF = {}
IN_W = 0
for _name, _size in (("fq", GROUP_W), ("fk", GROUP_W), ("fv", GROUP_W), ("ff", N_HEADS),
                     ("mq", GROUP_W), ("mk", GROUP_W), ("mv", GROUP_W),
                     ("dq", GROUP_W), ("dk", GROUP_W), ("dv", GROUP_W),
                     ("cq", MLA_Q_LORA), ("ckv", MLA_KV_LORA), ("kr", MLA_ROPE), ("gate", D_MODEL)):
    IN_OFF[_name] = IN_W
    IN_W += _size

TM_PROJ = 512
T_PREP = 256
TQ = 256
TK = 256
N_SUB = 2
MOBA_TB_ENTRIES = 8
TM_MERGE = 512
V_ROWS = HEAD_DIM + 16
MLA_GROUPS = 2


def _bucket_np(d):
    d = np.maximum(np.asarray(d, np.int64), 0)
    max_exact = N_BUCKETS // 2
    d_f = np.maximum(d, 1).astype(np.float64)
    val = np.log(d_f / max_exact) / math.log(MAX_DISTANCE / max_exact) * (N_BUCKETS - max_exact)
    frac = np.abs(val - np.round(val))
    on_edge = (frac < 1e-5) & (d > max_exact) & (val < N_BUCKETS - max_exact - 0.5)
    assert not on_edge.any(), "distance too close to a bucket edge for a static table"
    large = np.minimum(max_exact + np.floor(val + 1e-9).astype(np.int64), N_BUCKETS - 1)
    return np.where(d < max_exact, d, large)


def _bucket_steps(lo, hi):
    ds = np.arange(lo, hi + 1)
    bs = _bucket_np(ds)
    steps = [(int(ds[i]), int(bs[i])) for i in range(1, len(ds)) if bs[i] != bs[i - 1]]
    return int(bs[0]), steps


def _compiler_params(sem):
    return pltpu.CompilerParams(dimension_semantics=sem, vmem_limit_bytes=VMEM_LIMIT_BYTES)


def _bias_from_steps(d, rb_ref, col, lo, hi):
    b0, steps = _bucket_steps(lo, hi)
    val = jnp.full(d.shape, rb_ref[b0, col], F32)
    for t, b in steps:
        val = jnp.where(d >= t, rb_ref[b, col], val)
    return val


def _bias_tables_kernel(rb_ref, moba_ref, dil_ref):
    kl = lax.broadcasted_iota(jnp.int32, (MOBA_BLOCK, MOBA_BLOCK), 0)
    ql = lax.broadcasted_iota(jnp.int32, (MOBA_BLOCK, MOBA_BLOCK), 1)
    for e in range(MOBA_TB_ENTRIES - 1):
        d = jnp.maximum(e * MOBA_BLOCK + ql - kl, 0)
        lo = max(e * MOBA_BLOCK - (MOBA_BLOCK - 1), 0)
        hi = e * MOBA_BLOCK + (MOBA_BLOCK - 1)
        for h in range(N_HEADS):
            moba_ref[e, h] = _bias_from_steps(d, rb_ref, h, lo, hi) * LOG2E
    for h in range(N_HEADS):
        moba_ref[MOBA_TB_ENTRIES - 1, h] = jnp.full((MOBA_BLOCK, MOBA_BLOCK), rb_ref[N_BUCKETS - 1, h], F32) * LOG2E
    qi = lax.broadcasted_iota(jnp.int32, (DIL_BLOCK, 2 * DIL_BLOCK), 0)
    kj = lax.broadcasted_iota(jnp.int32, (DIL_BLOCK, 2 * DIL_BLOCK), 1)
    rel = qi + DIL_BLOCK - kj
    for p, (_, dil) in enumerate(DIL_PATTERNS):
        d = jnp.maximum(rel * dil, 0)
        for h in range(N_HEADS):
            bias = _bias_from_steps(d, rb_ref, N_HEADS + h, 0, (2 * DIL_BLOCK - 1) * dil)
            dil_ref[p, h] = jnp.where((rel >= 0) & (rel <= DIL_BLOCK), bias, NEG)


def _bias_tables(rel_bias):
    assert _bucket_np((MOBA_TB_ENTRIES - 1) * MOBA_BLOCK - (MOBA_BLOCK - 1)) == N_BUCKETS - 1
    return pl.pallas_call(
        _bias_tables_kernel,
        out_shape=(jax.ShapeDtypeStruct((MOBA_TB_ENTRIES, N_HEADS, MOBA_BLOCK, MOBA_BLOCK), F32),
                   jax.ShapeDtypeStruct((len(DIL_PATTERNS), N_HEADS, DIL_BLOCK, 2 * DIL_BLOCK), F32)),
        in_specs=[pl.BlockSpec(memory_space=pltpu.SMEM)],
        out_specs=(pl.BlockSpec(memory_space=pltpu.VMEM), pl.BlockSpec(memory_space=pltpu.VMEM)),
        compiler_params=pltpu.CompilerParams(vmem_limit_bytes=VMEM_LIMIT_BYTES),
        name="bias_tables",
    )(rel_bias)


def _stage_in_weights(w_ref, ekr_ref, ws_ref):
    W = GROUP_W
    moves = ((COL_FOX, IN_OFF["fq"], 3 * W), (COL_MOBA, IN_OFF["mq"], 3 * W), (COL_DIL, IN_OFF["dq"], 3 * W),
             (COL_CQ, IN_OFF["cq"], MLA_Q_LORA), (COL_CKV, IN_OFF["ckv"], MLA_KV_LORA),
             (COL_GATE, IN_OFF["gate"], D_MODEL))
    kr_base = IN_OFF["kr"] // LANES * LANES
    assert IN_OFF["ff"] % LANES == 0 and IN_OFF["kr"] + MLA_ROPE <= kr_base + LANES
    lane = lax.broadcasted_iota(jnp.int32, (STAGE_ROWS, LANES), 1)
    for r0 in range(0, D_MODEL, STAGE_ROWS):
        rows = slice(r0, r0 + STAGE_ROWS)
        for dst, src, width in moves:
            ws_ref[rows, dst:dst + width] = w_ref[0, rows, src:src + width].astype(BF16)
        kr_win = w_ref[0, rows, kr_base:kr_base + LANES].astype(BF16)
        ws_ref[rows, COL_KR:COL_KR + LANES] = jnp.dot(kr_win, ekr_ref[...], preferred_element_type=F32).astype(BF16)
        ff_win = w_ref[0, rows, IN_OFF["ff"]:IN_OFF["ff"] + LANES]
        ws_ref[rows, COL_FF:COL_FF + LANES] = jnp.where(lane < N_HEADS, ff_win, 0.0).astype(BF16)
        ws_ref[rows, COL_FF + LANES:COL_GATE] = jnp.zeros((STAGE_ROWS, COL_GATE - COL_FF - LANES), BF16)


def _group_mean_sq(x, g_mat):
    return jnp.dot((x * x).astype(BF16), g_mat, preferred_element_type=F32)


def _group_norm(x, gain, g_mat):
    return x * lax.rsqrt(_group_mean_sq(x, g_mat) + EPS) * gain


def _row_norm(x, gain):
    return x * lax.rsqrt(jnp.mean(x * x, axis=-1, keepdims=True) + EPS) * gain


def _log_sigmoid(x):
    return -(jnp.maximum(-x, 0.0) + jnp.log1p(jnp.exp(-jnp.abs(x))))


def _front_kernel(x_ref, lng_ref, w_ref, ekr_ref,
                  gains_ref, bfor_ref, qng_ref, kvng_ref, nopeg_ref, ropeg_ref,
                  wuq_ref, wukv_ref, cos_ref, sin_ref, g64_ref, g32_ref, tri_ref,
                  fqT_ref, fk_ref, fvT_ref, fck_ref, fcrow_ref,
                  mqT_ref, mk_ref, mvT_ref, mkmean_ref,
                  lqT_ref, lk_ref, lvT_ref,
                  dq1_ref, dk1_ref, dv1_ref, dq4_ref, dk4_ref, dv4_ref, dq16_ref, dk16_ref, dv16_ref,
                  gate_ref,
                  ws_ref, carry_ref, sq_ref, sk_ref, sv_ref):
    t = pl.program_id(1)

    @pl.when((pl.program_id(0) == 0) & (t == 0))
    def _():
        _stage_in_weights(w_ref, ekr_ref, ws_ref)

    g64 = g64_ref[...]
    scale = HEAD_DIM ** -0.5
    W = GROUP_W
    ones_rows = jnp.ones((V_ROWS - HEAD_DIM, T_PREP), BF16)

    def store_vT(dst, v):
        vT = v.T.astype(BF16)
        for h in range(N_HEADS):
            dst[0, 0, h] = jnp.concatenate([vT[HEAD_DIM * h:HEAD_DIM * (h + 1)], ones_rows], axis=0)

    x = x_ref[0]
    hn = (x * lax.rsqrt(jnp.mean(x * x, axis=-1, keepdims=True) + EPS) * lng_ref[...]).astype(BF16)
    project = lambda lo, width: jnp.dot(hn, ws_ref[:, lo:lo + width], preferred_element_type=F32)
    pf = project(COL_FOX, 3 * W)
    p_ff = project(COL_FF, LANES)
    pm = project(COL_MOBA, 3 * W)

    fq = _group_norm(pf[:, :W], gains_ref[0:1, :], g64) * (scale * LOG2E)
    fk = _group_norm(pf[:, W:2 * W], gains_ref[1:2, :], g64)
    fqT_ref[0, 0] = fq.T.astype(BF16)
    fk_ref[0, 0] = fk.astype(BF16)
    store_vT(fvT_ref, pf[:, 2 * W:])

    @pl.when(t == 0)
    def _():
        carry_ref[...] = jnp.zeros_like(carry_ref)

    log_f = _log_sigmoid(p_ff + bfor_ref[...])
    tri = tri_ref[...]
    c = carry_ref[...]
    rest = log_f
    for _ in range(3):
        part = rest.astype(BF16)
        c = c + jnp.dot(tri, part, preferred_element_type=F32)
        rest = rest - part.astype(F32)
    carry_ref[...] = c[T_PREP - 1:T_PREP, :]
    c2 = c * LOG2E
    for h in range(N_HEADS):
        fck_ref[0, h] = jnp.broadcast_to(c2[:, h:h + 1], (T_PREP, LANES))
    fcrow_ref[0] = c2.T[0:8, :]

    pd = project(COL_DIL, 3 * W)
    mq = _group_norm(pm[:, :W], gains_ref[2:3, :], g64) * (scale * LOG2E)
    mk = _group_norm(pm[:, W:2 * W], gains_ref[3:4, :], g64)
    mqT_ref[0, 0] = mq.T.astype(BF16)
    mk_ref[0, 0] = mk.astype(BF16)
    store_vT(mvT_ref, pm[:, 2 * W:])
    mkmean_ref[0, pl.ds(t, 1), :] = jnp.mean(mk, axis=0, keepdims=True)

    p_mla = project(COL_CQ, MLA_Q_LORA + MLA_KV_LORA + LANES)
    dq = _group_norm(pd[:, :W], gains_ref[4:5, :], g64) * scale
    dk = _group_norm(pd[:, W:2 * W], gains_ref[5:6, :], g64)
    dv = pd[:, 2 * W:]
    dq1_ref[0] = dq.astype(BF16)
    dk1_ref[0] = dk.astype(BF16)
    dv1_ref[0] = dv.astype(BF16)
    for val, dst in ((dq, sq_ref), (dk, sk_ref), (dv, sv_ref)):
        for half in range(GROUP_W // LANES):
            dst[half] = val[:, half * LANES:(half + 1) * LANES]
    for dil, outs in ((4, (dq4_ref, dk4_ref, dv4_ref)), (16, (dq16_ref, dk16_ref, dv16_ref))):
        n = T_PREP // dil
        for r in range(dil):
            for src, dst in zip((sq_ref, sk_ref, sv_ref), outs):
                dst[0, r] = jnp.concatenate(
                    [src[half, pl.ds(r, n, stride=dil), :] for half in range(GROUP_W // LANES)],
                    axis=1).astype(BF16)

    half_gate = D_MODEL // 2
    gate_ref[0, :, :half_gate] = project(COL_GATE, half_gate).astype(BF16)
    cos = cos_ref[...]
    sin = sin_ref[...]
    qscale = (MLA_NOPE + MLA_ROPE) ** -0.5
    cqn = _row_norm(p_mla[:, :MLA_Q_LORA], qng_ref[...]).astype(BF16)
    rotate = lambda x: x * cos + pltpu.roll(x, LANES // 2, 1) * sin
    qf = jnp.dot(cqn, wuq_ref[...], preferred_element_type=F32)
    q_nope = _group_norm(qf[:, :W], nopeg_ref[0:1, :], g64) * (qscale * LOG2E)
    ckvn = _row_norm(p_mla[:, MLA_Q_LORA:MLA_Q_LORA + MLA_KV_LORA], kvng_ref[...]).astype(BF16)
    kvf = jnp.dot(ckvn, wukv_ref[...], preferred_element_type=F32)
    k_nope = _group_norm(kvf[:, :W], nopeg_ref[1:2, :], g64)
    krs = p_mla[:, MLA_Q_LORA + MLA_KV_LORA:]
    kr_ms = jnp.sum(krs * krs, axis=-1, keepdims=True) * (1.0 / (2 * MLA_ROPE))
    kr = rotate(krs * lax.rsqrt(kr_ms + EPS) * ropeg_ref[1:2, :])
    for g in range(MLA_GROUPS):
        slab = qf[:, W + g * LANES:W + (g + 1) * LANES]
        qr = rotate(_group_norm(slab, ropeg_ref[0:1, :], g32_ref[...])) * (qscale * LOG2E)
        lq = jnp.concatenate([q_nope[:, g * LANES:(g + 1) * LANES], qr], axis=1)
        lqT_ref[0, g] = lq.T.astype(BF16)
        lk_ref[0, g] = jnp.concatenate([k_nope[:, g * LANES:(g + 1) * LANES], kr], axis=1).astype(BF16)
    store_vT(lvT_ref, kvf[:, W:])
    gate_ref[0, :, half_gate:] = project(COL_GATE + half_gate, half_gate).astype(BF16)


def _front(layer, x3, w_in, consts, lp):
    B, S, _ = x3.shape
    T = T_PREP
    NT = S // T
    assert S % T == 0 and S // MOBA_BLOCK == NT

    def cspec(shape):
        return pl.BlockSpec(shape, lambda b, t: (0,) * len(shape))

    in_specs = [pl.BlockSpec((1, T, D_MODEL), lambda b, t: (b, t, 0)), cspec((1, D_MODEL)),
                pl.BlockSpec((1, D_MODEL, IN_W), lambda b, t: (layer, 0, 0), pipeline_mode=pl.Buffered(1)),
                cspec((LANES, LANES)),
                cspec((6, GROUP_W)), cspec((1, LANES)), cspec((1, MLA_Q_LORA)), cspec((1, MLA_KV_LORA)),
                cspec((2, GROUP_W)), cspec((2, LANES)),
                cspec((MLA_Q_LORA, MLA_GROUPS * GROUP_W)), cspec((MLA_KV_LORA, 2 * GROUP_W)),
                pl.BlockSpec((T, LANES), lambda b, t: (t, 0)), pl.BlockSpec((T, LANES), lambda b, t: (t, 0)),
                cspec((GROUP_W, GROUP_W)), cspec((LANES, LANES)), cspec((T, T))]

    def qT(groups):
        return (jax.ShapeDtypeStruct((B, groups, GROUP_W, S), BF16),
                pl.BlockSpec((1, groups, GROUP_W, T), lambda b, t: (b, 0, 0, t)))

    def keys(groups):
        return (jax.ShapeDtypeStruct((B, groups, S, GROUP_W), BF16),
                pl.BlockSpec((1, groups, T, GROUP_W), lambda b, t: (b, 0, t, 0)))

    def rows(w):
        return jax.ShapeDtypeStruct((B, S, w), BF16), pl.BlockSpec((1, T, w), lambda b, t: (b, t, 0))

    def vT():
        return (jax.ShapeDtypeStruct((B, NT, N_HEADS, V_ROWS, T), BF16),
                pl.BlockSpec((1, 1, N_HEADS, V_ROWS, T), lambda b, t: (b, t, 0, 0, 0)))

    def resid(dil):
        return (jax.ShapeDtypeStruct((B, dil, S // dil, GROUP_W), BF16),
                pl.BlockSpec((1, dil, T // dil, GROUP_W), lambda b, t: (b, 0, t, 0)))

    outs = [qT(1), keys(1), vT(),
            (jax.ShapeDtypeStruct((B, N_HEADS, S, LANES), F32),
             pl.BlockSpec((1, N_HEADS, T, LANES), lambda b, t: (b, 0, t, 0))),
            (jax.ShapeDtypeStruct((B, 8, S), F32), pl.BlockSpec((1, 8, T), lambda b, t: (b, 0, t))),
            qT(1), keys(1), vT(),
            (jax.ShapeDtypeStruct((B, NT, GROUP_W), F32), pl.BlockSpec((1, NT, GROUP_W), lambda b, t: (b, 0, 0))),
            qT(MLA_GROUPS), keys(MLA_GROUPS), vT(),
            rows(GROUP_W), rows(GROUP_W), rows(GROUP_W),
            resid(4), resid(4), resid(4), resid(16), resid(16), resid(16),
            rows(D_MODEL)]
    return pl.pallas_call(
        _front_kernel,
        out_shape=tuple(o[0] for o in outs),
        grid=(B, NT),
        in_specs=in_specs,
        out_specs=tuple(o[1] for o in outs),
        scratch_shapes=[pltpu.VMEM((D_MODEL, PROJ_W), BF16), pltpu.VMEM((1, LANES), F32)]
        + [pltpu.VMEM((GROUP_W // LANES, T, LANES), F32)] * 3,
        compiler_params=_compiler_params(("arbitrary", "arbitrary")),
        name="front",
    )(x3, lp["ln_g"], w_in, consts["ekr"],
      lp["gains"], lp["b_forget"], lp["q_norm"], lp["kv_norm"], lp["nope_gain"], lp["rope_gain"],
      lp["w_uq"], lp["w_ukv"], consts["cos"], consts["sin"], consts["g64"], consts["g32"], consts["tri"])


def _stage_head_queries(qT_ref, qm_ref, groups):
    r = lax.broadcasted_iota(jnp.int32, (GROUP_W, TQ), 0)
    for h in range(N_HEADS):
        if groups == 1:
            g, mask = 0, (r >= HEAD_DIM * h) & (r < HEAD_DIM * (h + 1))
        else:
            g, hl = divmod(h, N_HEADS // groups)
            mask = (r >= HEAD_DIM * hl) & (r < HEAD_DIM * (hl + 1))
            for base in (2 * HEAD_DIM, 2 * HEAD_DIM + LANES // 2):
                lo = base + MLA_HALF * hl
                mask = mask | ((r >= lo) & (r < lo + MLA_HALF))
        for sub in range(N_SUB):
            q = qT_ref[0, g, :, sub * TQ:(sub + 1) * TQ]
            qm_ref[sub * N_HEADS + h] = jnp.where(mask, q, jnp.zeros_like(q))


def _flash_sweep(pair_idx, groups, qm_ref, k_ref, vT_ref, m_ref, acc_ref, u_ref, st_ref, o_ref, score_fn):
    kpos = lax.broadcasted_iota(jnp.int32, (TK, TQ), 0)
    qpos = lax.broadcasted_iota(jnp.int32, (TK, TQ), 1)
    causal = kpos <= qpos
    m_ref[...] = jnp.full(m_ref.shape, NEG, F32)
    acc_ref[...] = jnp.zeros(acc_ref.shape, F32)
    n_chain = N_SUB * N_HEADS
    both = tuple(range(N_SUB))
    q_tile = [N_SUB * pair_idx + sub for sub in both]

    def scores(j, slot, diagonal, subs):
        row0 = pl.multiple_of(j * TK, TK)
        kts = [k_ref[0, g, pl.ds(row0, TK), :] for g in range(groups)]
        for sub in subs:
            for h in range(N_HEADS):
                c = sub * N_HEADS + h
                s = jnp.dot(kts[h // (N_HEADS // groups)], qm_ref[c], preferred_element_type=F32)
                u, row, live = score_fn(sub, h, s, j, diagonal)
                if diagonal:
                    u = jnp.where(causal, u, NEG)
                u_ref[slot, c] = u
                m_old = m_ref[c]
                top = jnp.max(u, axis=0, keepdims=True)
                if row is not None:
                    top = top + row
                if live is not None:
                    top = jnp.where(live, top, NEG)
                m_new = jnp.maximum(m_old, top)
                shift = m_new if row is None else m_new - row
                if live is not None:
                    shift = jnp.where(live, shift, -NEG)
                st_ref[slot, c] = shift
                st_ref[slot, n_chain + c] = jnp.exp2(m_old - m_new)
                m_ref[c] = m_new

    def values(j, slot, subs):
        for sub in subs:
            for h in range(N_HEADS):
                c = sub * N_HEADS + h
                p = jnp.exp2((u_ref[slot, c] - st_ref[slot, c]).astype(BF16))
                acc_ref[c] = (st_ref[slot, n_chain + c] * acc_ref[c]
                              + jnp.dot(vT_ref[0, j, h], p, preferred_element_type=F32))

    for sub in both:
        scores(q_tile[sub], 0, True, (sub,))

    def step(n, slot):
        scores(n - 1, slot, False, both)
        for sub in both:
            values(jnp.where(n == 1, q_tile[sub], n - 2), 1 - slot, (sub,))

    def two_steps(t, carry):
        step(1 + 2 * t, 1)
        step(2 + 2 * t, 0)
        return carry

    lax.fori_loop(0, pair_idx, two_steps, 0)
    n_common = N_SUB * pair_idx
    pending = [jnp.where(pair_idx == 0, q_tile[sub], n_common - 1) for sub in both]
    scores(n_common, 1, False, (1,))
    values(pending[0], 0, (0,))
    values(pending[1], 0, (1,))
    values(n_common, 1, (1,))
    for sub in both:
        outs = []
        for h in range(N_HEADS):
            acc = acc_ref[sub * N_HEADS + h]
            outs.append(acc[:HEAD_DIM] / acc[HEAD_DIM:HEAD_DIM + 1])
        o_ref[0, sub * TQ:(sub + 1) * TQ, :] = jnp.concatenate(outs, axis=0).T.astype(BF16)


def _fox_kernel(qT_ref, k_ref, vT_ref, ck_ref, crow_ref, o_ref, qm_ref, m_ref, acc_ref, u_ref, st_ref):
    pair_idx = pl.program_id(1)
    _stage_head_queries(qT_ref, qm_ref, 1)

    def score_fn(sub, h, s, j, diagonal):
        ck = ck_ref[0, h, pl.ds(pl.multiple_of(j * TK, TK), TK), :]
        cq = crow_ref[0, h:h + 1, sub * TQ:(sub + 1) * TQ]
        return s - jnp.concatenate([ck] * (TQ // LANES), axis=1), cq, None

    _flash_sweep(pair_idx, 1, qm_ref, k_ref, vT_ref, m_ref, acc_ref, u_ref, st_ref, o_ref, score_fn)


def _mla_kernel(qT_ref, k_ref, vT_ref, o_ref, qm_ref, m_ref, acc_ref, u_ref, st_ref):
    pair_idx = pl.program_id(1)
    _stage_head_queries(qT_ref, qm_ref, MLA_GROUPS)
    _flash_sweep(pair_idx, MLA_GROUPS, qm_ref, k_ref, vT_ref, m_ref, acc_ref, u_ref, st_ref, o_ref,
                 lambda sub, h, s, j, diagonal: (s, None, None))


def _moba_kernel(qT_ref, k_ref, vT_ref, kmean_ref, tb_ref, o_ref, qm_ref, m_ref, acc_ref, u_ref, st_ref, sel_ref):
    pair_idx = pl.program_id(1)
    _stage_head_queries(qT_ref, qm_ref, 1)
    kmean = kmean_ref[0].astype(BF16)
    nb = kmean.shape[0]
    blk = lax.broadcasted_iota(jnp.int32, (nb, TQ), 0)
    own = [N_SUB * pair_idx + sub for sub in range(N_SUB)]
    for c in range(N_SUB * N_HEADS):
        i = own[c // N_HEADS]
        gate = jnp.dot(kmean, qm_ref[c], preferred_element_type=F32)
        gate = jnp.where(blk < i, gate, NEG)
        rank = jnp.zeros((nb, TQ), jnp.int32)
        for jp in range(nb):
            row = gate[jp:jp + 1, :]
            ahead = (row > gate) | ((row == gate) & (jp < blk))
            rank = rank + ahead.astype(jnp.int32)
        sel_ref[c] = ((rank < MOBA_TOPK) & (blk < i)).astype(F32)

    def score_fn(sub, h, s, j, diagonal):
        u = s + tb_ref[jnp.minimum(own[sub] - j, MOBA_TB_ENTRIES - 1), h]
        live = None if diagonal else sel_ref[sub * N_HEADS + h, pl.ds(j, 1), :] > 0.5
        return u, None, live

    _flash_sweep(pair_idx, 1, qm_ref, k_ref, vT_ref, m_ref, acc_ref, u_ref, st_ref, o_ref, score_fn)


def _dense_mixer(kernel_fn, name, qT, k, vT, extra_args=(), extra_specs=(), extra_scratch=()):
    B, groups, _, S = qT.shape
    NT = S // TK
    assert TQ == TK and N_SUB == 2 and S % (N_SUB * TQ) == 0
    n_chain = N_SUB * N_HEADS
    in_specs = [pl.BlockSpec((1, groups, GROUP_W, N_SUB * TQ), lambda b, i: (b, 0, 0, i)),
                pl.BlockSpec((1, groups, S, GROUP_W), lambda b, i: (b, 0, 0, 0)),
                pl.BlockSpec((1, NT, N_HEADS, V_ROWS, TK), lambda b, i: (b, 0, 0, 0, 0))] + list(extra_specs)
    scratch = [pltpu.VMEM((n_chain, GROUP_W, TQ), BF16),
               pltpu.VMEM((n_chain, 1, TQ), F32),
               pltpu.VMEM((n_chain, V_ROWS, TQ), F32),
               pltpu.VMEM((2, n_chain, TK, TQ), F32),
               pltpu.VMEM((2, 2 * n_chain, 1, TQ), F32)] + list(extra_scratch)
    return pl.pallas_call(
        kernel_fn,
        out_shape=jax.ShapeDtypeStruct((B, S, GROUP_W), BF16),
        grid=(B, S // (N_SUB * TQ)),
        in_specs=in_specs,
        out_specs=pl.BlockSpec((1, N_SUB * TQ, GROUP_W), lambda b, i: (b, i, 0)),
        scratch_shapes=scratch,
        compiler_params=_compiler_params(("arbitrary", "arbitrary")),
        name=name,
    )(qT, k, vT, *extra_args)


def _dil_kernel(q_ref, kp_ref, kc_ref, vp_ref, vc_ref, tb_ref, o_ref, lse_ref):
    n = pl.program_id(2)
    qb = q_ref.shape[2] // DIL_BLOCK
    kj = lax.broadcasted_iota(jnp.int32, (DIL_BLOCK, 2 * DIL_BLOCK), 1)
    qlane = lax.broadcasted_iota(jnp.int32, (DIL_BLOCK, GROUP_W), 1)
    in_head = [(qlane >= HEAD_DIM * h) & (qlane < HEAD_DIM * (h + 1)) for h in range(N_HEADS)]
    for seq, c in [(seq, c) for seq in range(q_ref.shape[1]) for c in range(qb)]:
        rows = slice(c * DIL_BLOCK, (c + 1) * DIL_BLOCK)
        q = q_ref[0, seq, rows, :]
        if c == 0:
            kb = jnp.concatenate([kp_ref[0, seq], kc_ref[0, seq, :DIL_BLOCK, :]], axis=0)
            vb = jnp.concatenate([vp_ref[0, seq], vc_ref[0, seq, :DIL_BLOCK, :]], axis=0)
        else:
            kb = kc_ref[0, seq, (c - 1) * DIL_BLOCK:(c + 1) * DIL_BLOCK, :]
            vb = vc_ref[0, seq, (c - 1) * DIL_BLOCK:(c + 1) * DIL_BLOCK, :]
        q_heads = jnp.concatenate([jnp.where(msk, q, jnp.zeros_like(q)) for msk in in_head], axis=0)
        s = lax.dot_general(q_heads, kb, (((1,), (1,)), ((), ())), preferred_element_type=F32)
        s = s.reshape(N_HEADS, DIL_BLOCK, 2 * DIL_BLOCK)
        s = s + tb_ref[0]
        if c == 0:
            s = jnp.where(((kj >= DIL_BLOCK) | (n > 0))[None], s, NEG)
        m = jnp.max(s, axis=-1, keepdims=True)
        e = jnp.exp(s - m)
        l = jnp.sum(e, axis=-1, keepdims=True)
        lse = m + jnp.log(l)
        pr = (e * (1.0 / l)).astype(BF16)
        o_all = jnp.dot(pr.reshape(N_HEADS * DIL_BLOCK, 2 * DIL_BLOCK), vb, preferred_element_type=F32)
        o_all = o_all.reshape(N_HEADS, DIL_BLOCK, GROUP_W)
        o_acc = jnp.zeros((DIL_BLOCK, GROUP_W), F32)
        lse_map = jnp.zeros((DIL_BLOCK, GROUP_W), F32)
        for h in range(N_HEADS):
            o_acc = jnp.where(in_head[h], o_all[h], o_acc)
            lse_map = jnp.where(in_head[h], lse[h], lse_map)
        o_ref[0, seq, rows, :] = o_acc.astype(BF16)
        lse_ref[0, seq, rows, :] = lse_map


def _dilated_pattern(p, q, k, v, dil_tb):
    B, dil, L, _ = q.shape
    assert dil == DIL_PATTERNS[p][1] and L % DIL_BLOCK == 0
    nb = L // DIL_BLOCK
    qb = math.gcd(nb, DIL_BLOCKS_PER_STEP)
    seqs = math.gcd(dil, DIL_BLOCKS_PER_STEP // qb)
    cur = pl.BlockSpec((1, seqs, qb * DIL_BLOCK, GROUP_W), lambda b, r, n: (b, r, n, 0))
    prev = pl.BlockSpec((1, seqs, DIL_BLOCK, GROUP_W), lambda b, r, n: (b, r, jnp.maximum(n * qb - 1, 0), 0))
    return pl.pallas_call(
        _dil_kernel,
        out_shape=(jax.ShapeDtypeStruct((B, dil, L, GROUP_W), BF16),
                   jax.ShapeDtypeStruct((B, dil, L, GROUP_W), F32)),
        grid=(B, dil // seqs, nb // qb),
        in_specs=[cur, prev, cur, prev, cur,
                  pl.BlockSpec((1, N_HEADS, DIL_BLOCK, 2 * DIL_BLOCK), lambda b, r, n, p=p: (p, 0, 0, 0))],
        out_specs=(cur, cur),
        compiler_params=_compiler_params(("arbitrary", "arbitrary", "arbitrary")),
        name=f"dilated_{dil}",
    )(q, k, k, v, v, dil_tb)


def _mix_dilations(o_refs, lse_refs, scratch):
    halves = GROUP_W // LANES
    rows = o_refs[0].shape[1] * o_refs[0].shape[2]
    nat = []
    for ref, dst in zip(list(o_refs) + list(lse_refs), scratch):
        dil = ref.shape[1]
        if dil == 1:
            nat.append(ref[0, 0].astype(F32))
            continue
        for r in range(dil):
            for half in range(halves):
                dst[half, pl.ds(r, rows // dil, stride=dil), :] = (
                    ref[0, r, :, half * LANES:(half + 1) * LANES].astype(F32))
        nat.append(jnp.concatenate([dst[half] for half in range(halves)], axis=1))
    os, lses = nat[:len(o_refs)], nat[len(o_refs):]
    m = functools.reduce(jnp.maximum, lses)
    ws = [jnp.exp(l - m) for l in lses]
    tot = functools.reduce(jnp.add, ws)
    return functools.reduce(jnp.add, [(w / tot) * o for w, o in zip(ws, os)])


def _sigmoid(x):
    return 1.0 / (1.0 + jnp.exp(-x))


def _merge_kernel(x_ref, fo_ref, mo_ref, lo_ref, o1_ref, l1_ref, o4_ref, l4_ref, o16_ref, l16_ref, gate_ref, p_ref,
                  wout_ref, png_ref, wpg_ref, wpp_ref, o_ref, *dil_scratch):
    o_dil = _mix_dilations((o1_ref, o4_ref, o16_ref), (l1_ref, l4_ref, l16_ref), dil_scratch)
    mix = jnp.concatenate([fo_ref[...].astype(F32), mo_ref[...].astype(F32), o_dil, lo_ref[...].astype(F32)],
                          axis=1)
    g = gate_ref[...].astype(F32)
    y = jnp.dot((mix * (g * _sigmoid(g))).astype(BF16), wout_ref[...], preferred_element_type=F32)
    x1 = x_ref[...] + y
    hn = _row_norm(x1, png_ref[...]).astype(BF16)
    pg = _sigmoid(jnp.dot(hn, wpg_ref[...], preferred_element_type=F32))
    pp = jnp.dot(p_ref[0].astype(BF16), wpp_ref[...], preferred_element_type=F32)
    o_ref[...] = x1 + pg * pp


def _merge(layer, x2, fo, mo, lo, dil_outs, proj, p3, lp):
    rows = x2.shape[0]
    TM = TM_MERGE
    tiles_per_seq = dil_outs[0].shape[1] * dil_outs[0].shape[2] // TM

    def rspec(w, c=0):
        return pl.BlockSpec((TM, w), lambda i, c=c: (i, c))

    def cspec(shape):
        return pl.BlockSpec(shape, lambda i: (0, 0))

    def dspec(a):
        dil = a.shape[1]
        return pl.BlockSpec((1, dil, TM // dil, GROUP_W),
                            lambda i: (i // tiles_per_seq, 0, lax.rem(i, tiles_per_seq), 0))

    return pl.pallas_call(
        _merge_kernel,
        out_shape=jax.ShapeDtypeStruct((rows, D_MODEL), F32),
        grid=(rows // TM,),
        scratch_shapes=[pltpu.VMEM((GROUP_W // LANES, TM, LANES), F32)] * len(dil_outs),
        in_specs=[rspec(D_MODEL)] + [rspec(GROUP_W)] * 3 + [dspec(a) for a in dil_outs] + [
                  rspec(D_MODEL),
                  pl.BlockSpec((1, TM, PLE_DIM), lambda i: (layer, i, 0)),
                  cspec((D_MODEL, D_MODEL)), cspec((1, D_MODEL)), cspec((D_MODEL, D_MODEL)),
                  cspec((PLE_DIM, D_MODEL))],
        out_specs=rspec(D_MODEL),
        compiler_params=_compiler_params(("arbitrary",)),
        name="merge",
    )(x2, fo, mo, lo, *dil_outs, proj, p3, lp["w_out"], lp["ple_norm_g"], lp["w_ple_gate"], lp["w_ple_proj"])


def _layer_params(i, ln_g, b_forget, qk_gain, mla_q_norm, mla_kv_norm, mla_nope_gain,
                  mla_rope_gain, w_uq, w_ukv, w_out, ple_norm_g, w_ple_gate, w_ple_proj):
    def rope_slab(x1_parts, x2_parts, like):
        pad = jnp.zeros(like.shape[:-1] + (LANES // 2 - 2 * MLA_HALF,), like.dtype)
        return jnp.concatenate(list(x1_parts) + [pad] + list(x2_parts) + [pad], axis=-1)

    per_head = MLA_NOPE + MLA_ROPE
    uq = w_uq[i]
    uq_nope = [uq[:, h * per_head:h * per_head + MLA_NOPE] for h in range(N_HEADS)]
    uq_x1 = [uq[:, h * per_head + MLA_NOPE:h * per_head + MLA_NOPE + MLA_HALF] for h in range(N_HEADS)]
    uq_x2 = [uq[:, h * per_head + MLA_NOPE + MLA_HALF:(h + 1) * per_head] for h in range(N_HEADS)]
    hpg = N_HEADS // MLA_GROUPS
    uq_cols = uq_nope + [rope_slab(uq_x1[g * hpg:(g + 1) * hpg], uq_x2[g * hpg:(g + 1) * hpg], uq)
                         for g in range(MLA_GROUPS)]
    ukv = w_ukv[i]
    ukv_cols = ([ukv[:, h * 2 * HEAD_DIM:h * 2 * HEAD_DIM + HEAD_DIM] for h in range(N_HEADS)]
                + [ukv[:, h * 2 * HEAD_DIM + HEAD_DIM:(h + 1) * 2 * HEAD_DIM] for h in range(N_HEADS)])
    rg = mla_rope_gain[i]
    rope_gain = rope_slab([rg[:, :MLA_HALF]] * hpg, [rg[:, MLA_HALF:]] * hpg, rg)
    return {
        "ln_g": ln_g[i][None, :],
        "gains": jnp.tile(qk_gain[i], (1, N_HEADS)),
        "b_forget": jnp.pad(b_forget[i], (0, LANES - N_HEADS))[None, :],
        "q_norm": mla_q_norm[i][None, :],
        "kv_norm": mla_kv_norm[i][None, :],
        "nope_gain": jnp.tile(mla_nope_gain[i], (1, N_HEADS)),
        "rope_gain": rope_gain,
        "w_uq": jnp.concatenate(uq_cols, axis=1).astype(BF16),
        "w_ukv": jnp.concatenate(ukv_cols, axis=1).astype(BF16),
        "w_out": w_out[i].astype(BF16),
        "ple_norm_g": ple_norm_g[i][None, :],
        "w_ple_gate": w_ple_gate[i].astype(BF16),
        "w_ple_proj": w_ple_proj[i].astype(BF16),
    }


def _constants(S):
    inv = 1.0 / (ROPE_THETA ** (jnp.arange(MLA_HALF, dtype=F32) * 2.0 / MLA_ROPE))
    ang = jnp.arange(S).astype(F32)[:, None] * inv[None, :]
    cos = jnp.tile(jnp.cos(ang), (1, 2 * N_HEADS))
    sin = jnp.tile(jnp.sin(ang), (1, 2 * N_HEADS))
    sign = np.concatenate([-np.ones(LANES // 2, np.float32), np.ones(LANES // 2, np.float32)])
    lane = np.arange(GROUP_W)
    g64 = (lane[:, None] // HEAD_DIM == lane[None, :] // HEAD_DIM).astype(np.float32) / HEAD_DIM
    within = np.arange(LANES) % (LANES // 2)
    rope_head = np.where(within < 2 * MLA_HALF, within // MLA_HALF, -1)
    g32 = ((rope_head[:, None] == rope_head[None, :]) & (rope_head[:, None] >= 0)).astype(np.float32) / MLA_ROPE
    tri = np.tril(np.ones((T_PREP, T_PREP), np.float32))
    ekr = np.zeros((LANES, LANES), np.float32)
    kr_lane0 = IN_OFF["kr"] % LANES
    for c in range(MLA_ROPE):
        half, idx = divmod(c, MLA_HALF)
        for copy in range(N_HEADS // MLA_GROUPS):
            ekr[kr_lane0 + c, half * (LANES // 2) + copy * MLA_HALF + idx] = 1.0
    return {"cos": cos, "sin": sin * sign[None, :], "g64": jnp.asarray(g64, BF16),
            "g32": jnp.asarray(g32, BF16), "tri": jnp.asarray(tri, BF16), "ekr": jnp.asarray(ekr, BF16)}


def kernel(x, p, ln_g, w_in, b_forget, qk_gain, mla_q_norm, mla_kv_norm, mla_nope_gain, mla_rope_gain,
           w_uq, w_ukv, w_out, rel_bias, ple_norm_g, w_ple_gate, w_ple_proj):
    B, S, _ = x.shape
    depth = p.shape[0]
    consts = _constants(S)
    moba_tb, dil_tb = _bias_tables(rel_bias)
    x2 = x.reshape(B * S, D_MODEL)
    for i in range(depth):
        lp = _layer_params(i, ln_g, b_forget, qk_gain, mla_q_norm, mla_kv_norm, mla_nope_gain,
                           mla_rope_gain, w_uq, w_ukv, w_out, ple_norm_g, w_ple_gate, w_ple_proj)
        (fqT, fk, fvT, fck, fcrow, mqT, mk, mvT, mkmean, lqT, lk, lvT, *dil_qkv, gate) = _front(
            i, x2.reshape(B, S, D_MODEL), w_in, consts, lp)
        S_ = S
        fo = _dense_mixer(
            _fox_kernel, "fox", fqT, fk, fvT, (fck, fcrow),
            (pl.BlockSpec((1, N_HEADS, S_, LANES), lambda b, i: (b, 0, 0, 0)),
             pl.BlockSpec((1, 8, N_SUB * TQ), lambda b, i: (b, 0, i))))
        mo = _dense_mixer(
            _moba_kernel, "moba", mqT, mk, mvT, (mkmean, moba_tb),
            (pl.BlockSpec((1, S_ // MOBA_BLOCK, GROUP_W), lambda b, i: (b, 0, 0)),
             pl.BlockSpec((MOBA_TB_ENTRIES, N_HEADS, MOBA_BLOCK, MOBA_BLOCK), lambda b, i: (0, 0, 0, 0))),
            (pltpu.VMEM((N_SUB * N_HEADS, S_ // MOBA_BLOCK, TQ), F32),))
        lo = _dense_mixer(_mla_kernel, "mla", lqT, lk, lvT)
        dil_outs = []
        for pat, (_, dil) in enumerate(DIL_PATTERNS):
            q, k, v = (a.reshape(B, dil, S // dil, GROUP_W) for a in dil_qkv[3 * pat:3 * pat + 3])
            dil_outs += list(_dilated_pattern(pat, q, k, v, dil_tb))
        rs = lambda a: a.reshape(B * S, GROUP_W)
        x2 = _merge(i, x2, rs(fo), rs(mo), rs(lo), dil_outs, gate.reshape(B * S, D_MODEL),
                    p.reshape(depth, B * S, PLE_DIM), lp)
    return x2.reshape(B, S, D_MODEL)
```

```python
import functools
import math

import numpy as np
import jax
import jax.numpy as jnp
from jax import lax
from jax.experimental import pallas as pl
from jax.experimental.pallas import tpu as pltpu

F32 = jnp.float32
BF16 = jnp.bfloat16

D_MODEL = 1024
N_HEADS = 4
HEAD_DIM = 64
GROUP_W = N_HEADS * HEAD_DIM
MLA_NOPE = 64
MLA_ROPE = 32
MLA_HALF = MLA_ROPE // 2
MLA_QK_W = N_HEADS * (MLA_NOPE + MLA_ROPE)
MLA_Q_LORA = 256
MLA_KV_LORA = 128
PLE_DIM = 256
MOBA_BLOCK = 256
MOBA_TOPK = 3
DIL_PATTERNS = ((128, 1), (512, 4), (2048, 16))
DIL_BLOCK = 128
DIL_BLOCKS_PER_STEP = 4
N_BUCKETS = 32
MAX_DISTANCE = 2048
ROPE_THETA = 10000.0
EPS = 1e-6
NEG = -1e30
LOG2E = math.log2(math.e)

LANES = 128
VMEM_LIMIT_BYTES = 56 * 1024 * 1024

COL_FOX, COL_MOBA, COL_DIL = 0, 768, 1536
COL_CQ, COL_CKV, COL_KR, COL_FF, COL_GATE = 2304, 2560, 2688, 2816, 3072
PROJ_W = 4096
PROJ_TN = 512
STAGE_ROWS = 128

IN_OFF = {}
IN_W = 0
for _name, _size in (("fq", GROUP_W), ("fk", GROUP_W), ("fv", GROUP_W), ("ff", N_HEADS),
                     ("mq", GROUP_W), ("mk", GROUP_W), ("mv", GROUP_W),
                     ("dq", GROUP_W), ("dk", GROUP_W), ("dv", GROUP_W),
                     ("cq", MLA_Q_LORA), ("ckv", MLA_KV_LORA), ("kr", MLA_ROPE), ("gate", D_MODEL)):
    IN_OFF[_name] = IN_W
    IN_W += _size

TM_PROJ = 512
T_PREP = 256
TQ = 256
TK = 256
N_SUB = 2
MOBA_TB_ENTRIES = 8
TM_MERGE = 512
V_ROWS = HEAD_DIM + 16
MLA_GROUPS = 2


def _bucket_np(d):
    d = np.maximum(np.asarray(d, np.int64), 0)
    max_exact = N_BUCKETS // 2
    d_f = np.maximum(d, 1).astype(np.float64)
    val = np.log(d_f / max_exact) / math.log(MAX_DISTANCE / max_exact) * (N_BUCKETS - max_exact)
    frac = np.abs(val - np.round(val))
    on_edge = (frac < 1e-5) & (d > max_exact) & (val < N_BUCKETS - max_exact - 0.5)
    assert not on_edge.any(), "distance too close to a bucket edge for a static table"
    large = np.minimum(max_exact + np.floor(val + 1e-9).astype(np.int64), N_BUCKETS - 1)
    return np.where(d < max_exact, d, large)


def _bucket_steps(lo, hi):
    ds = np.arange(lo, hi + 1)
    bs = _bucket_np(ds)
    steps = [(int(ds[i]), int(bs[i])) for i in range(1, len(ds)) if bs[i] != bs[i - 1]]
    return int(bs[0]), steps


def _compiler_params(sem):
    return pltpu.CompilerParams(dimension_semantics=sem, vmem_limit_bytes=VMEM_LIMIT_BYTES)


def _bias_from_steps(d, rb_ref, col, lo, hi):
    b0, steps = _bucket_steps(lo, hi)
    val = jnp.full(d.shape, rb_ref[b0, col], F32)
    for t, b in steps:
        val = jnp.where(d >= t, rb_ref[b, col], val)
    return val


def _bias_tables_kernel(rb_ref, moba_ref, dil_ref):
    kl = lax.broadcasted_iota(jnp.int32, (MOBA_BLOCK, MOBA_BLOCK), 0)
    ql = lax.broadcasted_iota(jnp.int32, (MOBA_BLOCK, MOBA_BLOCK), 1)
    for e in range(MOBA_TB_ENTRIES - 1):
        d = jnp.maximum(e * MOBA_BLOCK + ql - kl, 0)
        lo = max(e * MOBA_BLOCK - (MOBA_BLOCK - 1), 0)
        hi = e * MOBA_BLOCK + (MOBA_BLOCK - 1)
        for h in range(N_HEADS):
            moba_ref[e, h] = _bias_from_steps(d, rb_ref, h, lo, hi) * LOG2E
    for h in range(N_HEADS):
        moba_ref[MOBA_TB_ENTRIES - 1, h] = jnp.full((MOBA_BLOCK, MOBA_BLOCK), rb_ref[N_BUCKETS - 1, h], F32) * LOG2E
    qi = lax.broadcasted_iota(jnp.int32, (DIL_BLOCK, 2 * DIL_BLOCK), 0)
    kj = lax.broadcasted_iota(jnp.int32, (DIL_BLOCK, 2 * DIL_BLOCK), 1)
    rel = qi + DIL_BLOCK - kj
    for p, (_, dil) in enumerate(DIL_PATTERNS):
        d = jnp.maximum(rel * dil, 0)
        for h in range(N_HEADS):
            bias = _bias_from_steps(d, rb_ref, N_HEADS + h, 0, (2 * DIL_BLOCK - 1) * dil)
            dil_ref[p, h] = jnp.where((rel >= 0) & (rel <= DIL_BLOCK), bias, NEG)


def _bias_tables(rel_bias):
    assert _bucket_np((MOBA_TB_ENTRIES - 1) * MOBA_BLOCK - (MOBA_BLOCK - 1)) == N_BUCKETS - 1
    return pl.pallas_call(
        _bias_tables_kernel,
        out_shape=(jax.ShapeDtypeStruct((MOBA_TB_ENTRIES, N_HEADS, MOBA_BLOCK, MOBA_BLOCK), F32),
                   jax.ShapeDtypeStruct((len(DIL_PATTERNS), N_HEADS, DIL_BLOCK, 2 * DIL_BLOCK), F32)),
        in_specs=[pl.BlockSpec(memory_space=pltpu.SMEM)],
        out_specs=(pl.BlockSpec(memory_space=pltpu.VMEM), pl.BlockSpec(memory_space=pltpu.VMEM)),
        compiler_params=pltpu.CompilerParams(vmem_limit_bytes=VMEM_LIMIT_BYTES),
        name="bias_tables",
    )(rel_bias)


def _stage_in_weights(w_ref, ekr_ref, ws_ref):
    W = GROUP_W
    moves = ((COL_FOX, IN_OFF["fq"], 3 * W), (COL_MOBA, IN_OFF["mq"], 3 * W), (COL_DIL, IN_OFF["dq"], 3 * W),
             (COL_CQ, IN_OFF["cq"], MLA_Q_LORA), (COL_CKV, IN_OFF["ckv"], MLA_KV_LORA),
             (COL_GATE, IN_OFF["gate"], D_MODEL))
    kr_base = IN_OFF["kr"] // LANES * LANES
    assert IN_OFF["ff"] % LANES == 0 and IN_OFF["kr"] + MLA_ROPE <= kr_base + LANES
    lane = lax.broadcasted_iota(jnp.int32, (STAGE_ROWS, LANES), 1)
    for r0 in range(0, D_MODEL, STAGE_ROWS):
        rows = slice(r0, r0 + STAGE_ROWS)
        for dst, src, width in moves:
            ws_ref[rows, dst:dst + width] = w_ref[0, rows, src:src + width].astype(BF16)
        kr_win = w_ref[0, rows, kr_base:kr_base + LANES].astype(BF16)
        ws_ref[rows, COL_KR:COL_KR + LANES] = jnp.dot(kr_win, ekr_ref[...], preferred_element_type=F32).astype(BF16)
        ff_win = w_ref[0, rows, IN_OFF["ff"]:IN_OFF["ff"] + LANES]
        ws_ref[rows, COL_FF:COL_FF + LANES] = jnp.where(lane < N_HEADS, ff_win, 0.0).astype(BF16)
        ws_ref[rows, COL_FF + LANES:COL_GATE] = jnp.zeros((STAGE_ROWS, COL_GATE - COL_FF - LANES), BF16)


def _group_mean_sq(x, g_mat):
    return jnp.dot((x * x).astype(BF16), g_mat, preferred_element_type=F32)


def _group_norm(x, gain, g_mat):
    return x * lax.rsqrt(_group_mean_sq(x, g_mat) + EPS) * gain


def _row_norm(x, gain):
    return x * lax.rsqrt(jnp.mean(x * x, axis=-1, keepdims=True) + EPS) * gain


def _log_sigmoid(x):
    return -(jnp.maximum(-x, 0.0) + jnp.log1p(jnp.exp(-jnp.abs(x))))


def _front_kernel(x_ref, lng_ref, w_ref, ekr_ref,
                  gains_ref, bfor_ref, qng_ref, kvng_ref, nopeg_ref, ropeg_ref,
                  wuq_ref, wukv_ref, cos_ref, sin_ref, g64_ref, g32_ref, tri_ref,
                  fqT_ref, fk_ref, fvT_ref, fck_ref, fcrow_ref,
                  mqT_ref, mk_ref, mvT_ref, mkmean_ref,
                  lqT_ref, lk_ref, lvT_ref,
                  dq1_ref, dk1_ref, dv1_ref, dq4_ref, dk4_ref, dv4_ref, dq16_ref, dk16_ref, dv16_ref,
                  gate_ref,
                  ws_ref, carry_ref, sq_ref, sk_ref, sv_ref):
    t = pl.program_id(1)

    @pl.when((pl.program_id(0) == 0) & (t == 0))
    def _():
        _stage_in_weights(w_ref, ekr_ref, ws_ref)

    g64 = g64_ref[...]
    scale = HEAD_DIM ** -0.5
    W = GROUP_W
    ones_rows = jnp.ones((V_ROWS - HEAD_DIM, T_PREP), BF16)

    def store_vT(dst, v):
        vT = v.T.astype(BF16)
        for h in range(N_HEADS):
            dst[0, 0, h] = jnp.concatenate([vT[HEAD_DIM * h:HEAD_DIM * (h + 1)], ones_rows], axis=0)

    x = x_ref[0]
    hn = (x * lax.rsqrt(jnp.mean(x * x, axis=-1, keepdims=True) + EPS) * lng_ref[...]).astype(BF16)
    project = lambda lo, width: jnp.dot(hn, ws_ref[:, lo:lo + width], preferred_element_type=F32)
    pf = project(COL_FOX, 3 * W)
    p_ff = project(COL_FF, LANES)
    pm = project(COL_MOBA, 3 * W)

    fq = _group_norm(pf[:, :W], gains_ref[0:1, :], g64) * (scale * LOG2E)
    fk = _group_norm(pf[:, W:2 * W], gains_ref[1:2, :], g64)
    fqT_ref[0, 0] = fq.T.astype(BF16)
    fk_ref[0, 0] = fk.astype(BF16)
    store_vT(fvT_ref, pf[:, 2 * W:])

    @pl.when(t == 0)
    def _():
        carry_ref[...] = jnp.zeros_like(carry_ref)

    log_f = _log_sigmoid(p_ff + bfor_ref[...])
    tri = tri_ref[...]
    c = carry_ref[...]
    rest = log_f
    for _ in range(3):
        part = rest.astype(BF16)
        c = c + jnp.dot(tri, part, preferred_element_type=F32)
        rest = rest - part.astype(F32)
    carry_ref[...] = c[T_PREP - 1:T_PREP, :]
    c2 = c * LOG2E
    for h in range(N_HEADS):
        fck_ref[0, h] = jnp.broadcast_to(c2[:, h:h + 1], (T_PREP, LANES))
    fcrow_ref[0] = c2.T[0:8, :]

    pd = project(COL_DIL, 3 * W)
    mq = _group_norm(pm[:, :W], gains_ref[2:3, :], g64) * (scale * LOG2E)
    mk = _group_norm(pm[:, W:2 * W], gains_ref[3:4, :], g64)
    mqT_ref[0, 0] = mq.T.astype(BF16)
    mk_ref[0, 0] = mk.astype(BF16)
    store_vT(mvT_ref, pm[:, 2 * W:])
    mkmean_ref[0, pl.ds(t, 1), :] = jnp.mean(mk, axis=0, keepdims=True)

    p_mla = project(COL_CQ, MLA_Q_LORA + MLA_KV_LORA + LANES)
    dq = _group_norm(pd[:, :W], gains_ref[4:5, :], g64) * scale
    dk = _group_norm(pd[:, W:2 * W], gains_ref[5:6, :], g64)
    dv = pd[:, 2 * W:]
    dq1_ref[0] = dq.astype(BF16)
    dk1_ref[0] = dk.astype(BF16)
    dv1_ref[0] = dv.astype(BF16)
    for val, dst in ((dq, sq_ref), (dk, sk_ref), (dv, sv_ref)):
        for half in range(GROUP_W // LANES):
            dst[half] = val[:, half * LANES:(half + 1) * LANES]
    for dil, outs in ((4, (dq4_ref, dk4_ref, dv4_ref)), (16, (dq16_ref, dk16_ref, dv16_ref))):
        n = T_PREP // dil
        for r in range(dil):
            for src, dst in zip((sq_ref, sk_ref, sv_ref), outs):
                dst[0, r] = jnp.concatenate(
                    [src[half, pl.ds(r, n, stride=dil), :] for half in range(GROUP_W // LANES)],
                    axis=1).astype(BF16)

    half_gate = D_MODEL // 2
    gate_ref[0, :, :half_gate] = project(COL_GATE, half_gate).astype(BF16)
    cos = cos_ref[...]
    sin = sin_ref[...]
    qscale = (MLA_NOPE + MLA_ROPE) ** -0.5
    cqn = _row_norm(p_mla[:, :MLA_Q_LORA], qng_ref[...]).astype(BF16)
    rotate = lambda x: x * cos + pltpu.roll(x, LANES // 2, 1) * sin
    qf = jnp.dot(cqn, wuq_ref[...], preferred_element_type=F32)
    q_nope = _group_norm(qf[:, :W], nopeg_ref[0:1, :], g64) * (qscale * LOG2E)
    ckvn = _row_norm(p_mla[:, MLA_Q_LORA:MLA_Q_LORA + MLA_KV_LORA], kvng_ref[...]).astype(BF16)
    kvf = jnp.dot(ckvn, wukv_ref[...], preferred_element_type=F32)
    k_nope = _group_norm(kvf[:, :W], nopeg_ref[1:2, :], g64)
    krs = p_mla[:, MLA_Q_LORA + MLA_KV_LORA:]
    kr_ms = jnp.sum(krs * krs, axis=-1, keepdims=True) * (1.0 / (2 * MLA_ROPE))
    kr = rotate(krs * lax.rsqrt(kr_ms + EPS) * ropeg_ref[1:2, :])
    for g in range(MLA_GROUPS):
        slab = qf[:, W + g * LANES:W + (g + 1) * LANES]
        qr = rotate(_group_norm(slab, ropeg_ref[0:1, :], g32_ref[...])) * (qscale * LOG2E)
        lq = jnp.concatenate([q_nope[:, g * LANES:(g + 1) * LANES], qr], axis=1)
        lqT_ref[0, g] = lq.T.astype(BF16)
        lk_ref[0, g] = jnp.concatenate([k_nope[:, g * LANES:(g + 1) * LANES], kr], axis=1).astype(BF16)
    store_vT(lvT_ref, kvf[:, W:])
    gate_ref[0, :, half_gate:] = project(COL_GATE + half_gate, half_gate).astype(BF16)


def _front(layer, x3, w_in, consts, lp):
    B, S, _ = x3.shape
    T = T_PREP
    NT = S // T
    assert S % T == 0 and S // MOBA_BLOCK == NT

    def cspec(shape):
        return pl.BlockSpec(shape, lambda b, t: (0,) * len(shape))

    in_specs = [pl.BlockSpec((1, T, D_MODEL), lambda b, t: (b, t, 0)), cspec((1, D_MODEL)),
                pl.BlockSpec((1, D_MODEL, IN_W), lambda b, t: (layer, 0, 0), pipeline_mode=pl.Buffered(1)),
                cspec((LANES, LANES)),
                cspec((6, GROUP_W)), cspec((1, LANES)), cspec((1, MLA_Q_LORA)), cspec((1, MLA_KV_LORA)),
                cspec((2, GROUP_W)), cspec((2, LANES)),
                cspec((MLA_Q_LORA, MLA_GROUPS * GROUP_W)), cspec((MLA_KV_LORA, 2 * GROUP_W)),
                pl.BlockSpec((T, LANES), lambda b, t: (t, 0)), pl.BlockSpec((T, LANES), lambda b, t: (t, 0)),
                cspec((GROUP_W, GROUP_W)), cspec((LANES, LANES)), cspec((T, T))]

    def qT(groups):
        return (jax.ShapeDtypeStruct((B, groups, GROUP_W, S), BF16),
                pl.BlockSpec((1, groups, GROUP_W, T), lambda b, t: (b, 0, 0, t)))

    def keys(groups):
        return (jax.ShapeDtypeStruct((B, groups, S, GROUP_W), BF16),
                pl.BlockSpec((1, groups, T, GROUP_W), lambda b, t: (b, 0, t, 0)))

    def rows(w):
        return jax.ShapeDtypeStruct((B, S, w), BF16), pl.BlockSpec((1, T, w), lambda b, t: (b, t, 0))

    def vT():
        return (jax.ShapeDtypeStruct((B, NT, N_HEADS, V_ROWS, T), BF16),
                pl.BlockSpec((1, 1, N_HEADS, V_ROWS, T), lambda b, t: (b, t, 0, 0, 0)))

    def resid(dil):
        return (jax.ShapeDtypeStruct((B, dil, S // dil, GROUP_W), BF16),
                pl.BlockSpec((1, dil, T // dil, GROUP_W), lambda b, t: (b, 0, t, 0)))

    outs = [qT(1), keys(1), vT(),
            (jax.ShapeDtypeStruct((B, N_HEADS, S, LANES), F32),
             pl.BlockSpec((1, N_HEADS, T, LANES), lambda b, t: (b, 0, t, 0))),
            (jax.ShapeDtypeStruct((B, 8, S), F32), pl.BlockSpec((1, 8, T), lambda b, t: (b, 0, t))),
            qT(1), keys(1), vT(),
            (jax.ShapeDtypeStruct((B, NT, GROUP_W), F32), pl.BlockSpec((1, NT, GROUP_W), lambda b, t: (b, 0, 0))),
            qT(MLA_GROUPS), keys(MLA_GROUPS), vT(),
            rows(GROUP_W), rows(GROUP_W), rows(GROUP_W),
            resid(4), resid(4), resid(4), resid(16), resid(16), resid(16),
            rows(D_MODEL)]
    return pl.pallas_call(
        _front_kernel,
        out_shape=tuple(o[0] for o in outs),
        grid=(B, NT),
        in_specs=in_specs,
        out_specs=tuple(o[1] for o in outs),
        scratch_shapes=[pltpu.VMEM((D_MODEL, PROJ_W), BF16), pltpu.VMEM((1, LANES), F32)]
        + [pltpu.VMEM((GROUP_W // LANES, T, LANES), F32)] * 3,
        compiler_params=_compiler_params(("arbitrary", "arbitrary")),
        name="front",
    )(x3, lp["ln_g"], w_in, consts["ekr"],
      lp["gains"], lp["b_forget"], lp["q_norm"], lp["kv_norm"], lp["nope_gain"], lp["rope_gain"],
      lp["w_uq"], lp["w_ukv"], consts["cos"], consts["sin"], consts["g64"], consts["g32"], consts["tri"])


def _stage_head_queries(qT_ref, qm_ref, groups):
    r = lax.broadcasted_iota(jnp.int32, (GROUP_W, TQ), 0)
    for h in range(N_HEADS):
        if groups == 1:
            g, mask = 0, (r >= HEAD_DIM * h) & (r < HEAD_DIM * (h + 1))
        else:
            g, hl = divmod(h, N_HEADS // groups)
            mask = (r >= HEAD_DIM * hl) & (r < HEAD_DIM * (hl + 1))
            for base in (2 * HEAD_DIM, 2 * HEAD_DIM + LANES // 2):
                lo = base + MLA_HALF * hl
                mask = mask | ((r >= lo) & (r < lo + MLA_HALF))
        for sub in range(N_SUB):
            q = qT_ref[0, g, :, sub * TQ:(sub + 1) * TQ]
            qm_ref[sub * N_HEADS + h] = jnp.where(mask, q, jnp.zeros_like(q))


def _flash_sweep(pair_idx, groups, qm_ref, k_ref, vT_ref, m_ref, acc_ref, u_ref, o_ref, score_fn):
    kpos = lax.broadcasted_iota(jnp.int32, (TK, TQ), 0)
    qpos = lax.broadcasted_iota(jnp.int32, (TK, TQ), 1)
    causal = kpos <= qpos
    m_ref[...] = jnp.full(m_ref.shape, NEG, F32)
    acc_ref[...] = jnp.zeros(acc_ref.shape, F32)
    both = tuple(range(N_SUB))

    def scores(j0, nblk, diag_last, subs):
        row0 = pl.multiple_of(j0 * TK, TK)
        kts = [k_ref[0, g, pl.ds(row0, nblk * TK), :] for g in range(groups)]
        pending = []
        for sub in subs:
            for h in range(N_HEADS):
                c = sub * N_HEADS + h
                s = jnp.dot(kts[h // (N_HEADS // groups)], qm_ref[c], preferred_element_type=F32)
                tops, lives, row = [], [], None
                for blk in range(nblk):
                    rows = slice(blk * TK, (blk + 1) * TK)
                    diagonal = diag_last and blk == nblk - 1
                    u, row, live = score_fn(sub, h, s[rows], j0 + blk, diagonal)
                    if diagonal:
                        u = jnp.where(causal, u, NEG)
                    u_ref[c, rows, :] = u
                    top = jnp.max(u, axis=0, keepdims=True)
                    if row is not None:
                        top = top + row
                    if live is not None:
                        top = jnp.where(live, top, NEG)
                    tops.append(top)
                    lives.append(live)
                m_old = m_ref[c]
                m_new = functools.reduce(jnp.maximum, tops, m_old)
                shift = m_new if row is None else m_new - row
                shifts = [shift if live is None else jnp.where(live, shift, -NEG) for live in lives]
                m_ref[c] = m_new
                pending.append((c, h, shifts, jnp.exp2(m_old - m_new)))
        return pending

    def values(j0, pending):
        for c, h, shifts, alpha in pending:
            pv = None
            for blk, shift in enumerate(shifts):
                p = jnp.exp2((u_ref[c, blk * TK:(blk + 1) * TK, :] - shift).astype(BF16))
                term = jnp.dot(vT_ref[0, j0 + blk, h], p, preferred_element_type=F32)
                pv = term if pv is None else pv + term
            acc_ref[c] = alpha * acc_ref[c] + pv

    def common(t, carry):
        values(2 * t, scores(2 * t, 2, False, both))
        return carry

    lax.fori_loop(0, pair_idx, common, 0)
    n_common = N_SUB * pair_idx
    pend_a = scores(n_common, 1, True, (0,))
    pend_b = scores(n_common, 2, True, (1,))
    values(n_common, pend_a)
    values(n_common, pend_b)
    for sub in both:
        outs = []
        for h in range(N_HEADS):
            acc = acc_ref[sub * N_HEADS + h]
            outs.append(acc[:HEAD_DIM] / acc[HEAD_DIM:HEAD_DIM + 1])
        o_ref[0, sub * TQ:(sub + 1) * TQ, :] = jnp.concatenate(outs, axis=0).T.astype(BF16)


def _fox_kernel(qT_ref, k_ref, vT_ref, ck_ref, crow_ref, o_ref, qm_ref, m_ref, acc_ref, u_ref):
    pair_idx = pl.program_id(1)
    _stage_head_queries(qT_ref, qm_ref, 1)

    def score_fn(sub, h, s, j, diagonal):
        ck = ck_ref[0, h, pl.ds(pl.multiple_of(j * TK, TK), TK), :]
        cq = crow_ref[0, h:h + 1, sub * TQ:(sub + 1) * TQ]
        return s - jnp.concatenate([ck] * (TQ // LANES), axis=1), cq, None

    _flash_sweep(pair_idx, 1, qm_ref, k_ref, vT_ref, m_ref, acc_ref, u_ref, o_ref, score_fn)


def _mla_kernel(qT_ref, k_ref, vT_ref, o_ref, qm_ref, m_ref, acc_ref, u_ref):
    pair_idx = pl.program_id(1)
    _stage_head_queries(qT_ref, qm_ref, MLA_GROUPS)
    _flash_sweep(pair_idx, MLA_GROUPS, qm_ref, k_ref, vT_ref, m_ref, acc_ref, u_ref, o_ref,
                 lambda sub, h, s, j, diagonal: (s, None, None))


def _moba_kernel(qT_ref, k_ref, vT_ref, kmean_ref, tb_ref, o_ref, qm_ref, m_ref, acc_ref, u_ref, sel_ref):
    pair_idx = pl.program_id(1)
    _stage_head_queries(qT_ref, qm_ref, 1)
    kmean = kmean_ref[0].astype(BF16)
    nb = kmean.shape[0]
    blk = lax.broadcasted_iota(jnp.int32, (nb, TQ), 0)
    own = [N_SUB * pair_idx + sub for sub in range(N_SUB)]
    for c in range(N_SUB * N_HEADS):
        i = own[c // N_HEADS]
        gate = jnp.dot(kmean, qm_ref[c], preferred_element_type=F32)
        gate = jnp.where(blk < i, gate, NEG)
        rank = jnp.zeros((nb, TQ), jnp.int32)
        for jp in range(nb):
            row = gate[jp:jp + 1, :]
            ahead = (row > gate) | ((row == gate) & (jp < blk))
            rank = rank + ahead.astype(jnp.int32)
        sel_ref[c] = ((rank < MOBA_TOPK) & (blk < i)).astype(F32)

    def score_fn(sub, h, s, j, diagonal):
        u = s + tb_ref[jnp.minimum(own[sub] - j, MOBA_TB_ENTRIES - 1), h]
        live = None if diagonal else sel_ref[sub * N_HEADS + h, pl.ds(j, 1), :] > 0.5
        return u, None, live

    _flash_sweep(pair_idx, 1, qm_ref, k_ref, vT_ref, m_ref, acc_ref, u_ref, o_ref, score_fn)


def _dense_mixer(kernel_fn, name, qT, k, vT, extra_args=(), extra_specs=(), extra_scratch=()):
    B, groups, _, S = qT.shape
    NT = S // TK
    assert TQ == TK and N_SUB == 2 and S % (N_SUB * TQ) == 0
    n_chain = N_SUB * N_HEADS
    in_specs = [pl.BlockSpec((1, groups, GROUP_W, N_SUB * TQ), lambda b, i: (b, 0, 0, i)),
                pl.BlockSpec((1, groups, S, GROUP_W), lambda b, i: (b, 0, 0, 0)),
                pl.BlockSpec((1, NT, N_HEADS, V_ROWS, TK), lambda b, i: (b, 0, 0, 0, 0))] + list(extra_specs)
    scratch = [pltpu.VMEM((n_chain, GROUP_W, TQ), BF16),
               pltpu.VMEM((n_chain, 1, TQ), F32),
               pltpu.VMEM((n_chain, V_ROWS, TQ), F32),
               pltpu.VMEM((n_chain, 2 * TK, TQ), F32)] + list(extra_scratch)
    return pl.pallas_call(
        kernel_fn,
        out_shape=jax.ShapeDtypeStruct((B, S, GROUP_W), BF16),
        grid=(B, S // (N_SUB * TQ)),
        in_specs=in_specs,
        out_specs=pl.BlockSpec((1, N_SUB * TQ, GROUP_W), lambda b, i: (b, i, 0)),
        scratch_shapes=scratch,
        compiler_params=_compiler_params(("arbitrary", "arbitrary")),
        name=name,
    )(qT, k, vT, *extra_args)


def _dil_kernel(q_ref, kp_ref, kc_ref, vp_ref, vc_ref, tb_ref, o_ref, lse_ref):
    n = pl.program_id(2)
    qb = q_ref.shape[2] // DIL_BLOCK
    kj = lax.broadcasted_iota(jnp.int32, (DIL_BLOCK, 2 * DIL_BLOCK), 1)
    qlane = lax.broadcasted_iota(jnp.int32, (DIL_BLOCK, GROUP_W), 1)
    in_head = [(qlane >= HEAD_DIM * h) & (qlane < HEAD_DIM * (h + 1)) for h in range(N_HEADS)]
    for seq, c in [(seq, c) for seq in range(q_ref.shape[1]) for c in range(qb)]:
        rows = slice(c * DIL_BLOCK, (c + 1) * DIL_BLOCK)
        q = q_ref[0, seq, rows, :]
        if c == 0:
            kb = jnp.concatenate([kp_ref[0, seq], kc_ref[0, seq, :DIL_BLOCK, :]], axis=0)
            vb = jnp.concatenate([vp_ref[0, seq], vc_ref[0, seq, :DIL_BLOCK, :]], axis=0)
        else:
            kb = kc_ref[0, seq, (c - 1) * DIL_BLOCK:(c + 1) * DIL_BLOCK, :]
            vb = vc_ref[0, seq, (c - 1) * DIL_BLOCK:(c + 1) * DIL_BLOCK, :]
        q_heads = jnp.concatenate([jnp.where(msk, q, jnp.zeros_like(q)) for msk in in_head], axis=0)
        s = lax.dot_general(q_heads, kb, (((1,), (1,)), ((), ())), preferred_element_type=F32)
        s = s.reshape(N_HEADS, DIL_BLOCK, 2 * DIL_BLOCK)
        s = s + tb_ref[0]
        if c == 0:
            s = jnp.where(((kj >= DIL_BLOCK) | (n > 0))[None], s, NEG)
        m = jnp.max(s, axis=-1, keepdims=True)
        e = jnp.exp(s - m)
        l = jnp.sum(e, axis=-1, keepdims=True)
        lse = m + jnp.log(l)
        pr = (e * (1.0 / l)).astype(BF16)
        o_all = jnp.dot(pr.reshape(N_HEADS * DIL_BLOCK, 2 * DIL_BLOCK), vb, preferred_element_type=F32)
        o_all = o_all.reshape(N_HEADS, DIL_BLOCK, GROUP_W)
        o_acc = jnp.zeros((DIL_BLOCK, GROUP_W), F32)
        lse_map = jnp.zeros((DIL_BLOCK, GROUP_W), F32)
        for h in range(N_HEADS):
            o_acc = jnp.where(in_head[h], o_all[h], o_acc)
            lse_map = jnp.where(in_head[h], lse[h], lse_map)
        o_ref[0, seq, rows, :] = o_acc.astype(BF16)
        lse_ref[0, seq, rows, :] = lse_map


def _dilated_pattern(p, q, k, v, dil_tb):
    B, dil, L, _ = q.shape
    assert dil == DIL_PATTERNS[p][1] and L % DIL_BLOCK == 0
    nb = L // DIL_BLOCK
    qb = math.gcd(nb, DIL_BLOCKS_PER_STEP)
    seqs = math.gcd(dil, DIL_BLOCKS_PER_STEP // qb)
    cur = pl.BlockSpec((1, seqs, qb * DIL_BLOCK, GROUP_W), lambda b, r, n: (b, r, n, 0))
    prev = pl.BlockSpec((1, seqs, DIL_BLOCK, GROUP_W), lambda b, r, n: (b, r, jnp.maximum(n * qb - 1, 0), 0))
    return pl.pallas_call(
        _dil_kernel,
        out_shape=(jax.ShapeDtypeStruct((B, dil, L, GROUP_W), BF16),
                   jax.ShapeDtypeStruct((B, dil, L, GROUP_W), F32)),
        grid=(B, dil // seqs, nb // qb),
        in_specs=[cur, prev, cur, prev, cur,
                  pl.BlockSpec((1, N_HEADS, DIL_BLOCK, 2 * DIL_BLOCK), lambda b, r, n, p=p: (p, 0, 0, 0))],
        out_specs=(cur, cur),
        compiler_params=_compiler_params(("arbitrary", "arbitrary", "arbitrary")),
        name=f"dilated_{dil}",
    )(q, k, k, v, v, dil_tb)


def _mix_dilations(o_refs, lse_refs, scratch):
    halves = GROUP_W // LANES
    rows = o_refs[0].shape[1] * o_refs[0].shape[2]
    nat = []
    for ref, dst in zip(list(o_refs) + list(lse_refs), scratch):
        dil = ref.shape[1]
        if dil == 1:
            nat.append(ref[0, 0].astype(F32))
            continue
        for r in range(dil):
            for half in range(halves):
                dst[half, pl.ds(r, rows // dil, stride=dil), :] = (
                    ref[0, r, :, half * LANES:(half + 1) * LANES].astype(F32))
        nat.append(jnp.concatenate([dst[half] for half in range(halves)], axis=1))
    os, lses = nat[:len(o_refs)], nat[len(o_refs):]
    m = functools.reduce(jnp.maximum, lses)
    ws = [jnp.exp(l - m) for l in lses]
    tot = functools.reduce(jnp.add, ws)
    return functools.reduce(jnp.add, [(w / tot) * o for w, o in zip(ws, os)])


def _sigmoid(x):
    return 1.0 / (1.0 + jnp.exp(-x))


def _merge_kernel(x_ref, fo_ref, mo_ref, lo_ref, o1_ref, l1_ref, o4_ref, l4_ref, o16_ref, l16_ref, gate_ref, p_ref,
                  wout_ref, png_ref, wpg_ref, wpp_ref, o_ref, *dil_scratch):
    o_dil = _mix_dilations((o1_ref, o4_ref, o16_ref), (l1_ref, l4_ref, l16_ref), dil_scratch)
    mix = jnp.concatenate([fo_ref[...].astype(F32), mo_ref[...].astype(F32), o_dil, lo_ref[...].astype(F32)],
                          axis=1)
    g = gate_ref[...].astype(F32)
    y = jnp.dot((mix * (g * _sigmoid(g))).astype(BF16), wout_ref[...], preferred_element_type=F32)
    x1 = x_ref[...] + y
    hn = _row_norm(x1, png_ref[...]).astype(BF16)
    pg = _sigmoid(jnp.dot(hn, wpg_ref[...], preferred_element_type=F32))
    pp = jnp.dot(p_ref[0].astype(BF16), wpp_ref[...], preferred_element_type=F32)
    o_ref[...] = x1 + pg * pp


def _merge(layer, x2, fo, mo, lo, dil_outs, proj, p3, lp):
    rows = x2.shape[0]
    TM = TM_MERGE
    tiles_per_seq = dil_outs[0].shape[1] * dil_outs[0].shape[2] // TM

    def rspec(w, c=0):
        return pl.BlockSpec((TM, w), lambda i, c=c: (i, c))

    def cspec(shape):
        return pl.BlockSpec(shape, lambda i: (0, 0))

    def dspec(a):
        dil = a.shape[1]
        return pl.BlockSpec((1, dil, TM // dil, GROUP_W),
                            lambda i: (i // tiles_per_seq, 0, lax.rem(i, tiles_per_seq), 0))

    return pl.pallas_call(
        _merge_kernel,
        out_shape=jax.ShapeDtypeStruct((rows, D_MODEL), F32),
        grid=(rows // TM,),
        scratch_shapes=[pltpu.VMEM((GROUP_W // LANES, TM, LANES), F32)] * len(dil_outs),
        in_specs=[rspec(D_MODEL)] + [rspec(GROUP_W)] * 3 + [dspec(a) for a in dil_outs] + [
                  rspec(D_MODEL),
                  pl.BlockSpec((1, TM, PLE_DIM), lambda i: (layer, i, 0)),
                  cspec((D_MODEL, D_MODEL)), cspec((1, D_MODEL)), cspec((D_MODEL, D_MODEL)),
                  cspec((PLE_DIM, D_MODEL))],
        out_specs=rspec(D_MODEL),
        compiler_params=_compiler_params(("arbitrary",)),
        name="merge",
    )(x2, fo, mo, lo, *dil_outs, proj, p3, lp["w_out"], lp["ple_norm_g"], lp["w_ple_gate"], lp["w_ple_proj"])


def _layer_params(i, ln_g, b_forget, qk_gain, mla_q_norm, mla_kv_norm, mla_nope_gain,
                  mla_rope_gain, w_uq, w_ukv, w_out, ple_norm_g, w_ple_gate, w_ple_proj):
    def rope_slab(x1_parts, x2_parts, like):
        pad = jnp.zeros(like.shape[:-1] + (LANES // 2 - 2 * MLA_HALF,), like.dtype)
        return jnp.concatenate(list(x1_parts) + [pad] + list(x2_parts) + [pad], axis=-1)

    per_head = MLA_NOPE + MLA_ROPE
    uq = w_uq[i]
    uq_nope = [uq[:, h * per_head:h * per_head + MLA_NOPE] for h in range(N_HEADS)]
    uq_x1 = [uq[:, h * per_head + MLA_NOPE:h * per_head + MLA_NOPE + MLA_HALF] for h in range(N_HEADS)]
    uq_x2 = [uq[:, h * per_head + MLA_NOPE + MLA_HALF:(h + 1) * per_head] for h in range(N_HEADS)]
    hpg = N_HEADS // MLA_GROUPS
    uq_cols = uq_nope + [rope_slab(uq_x1[g * hpg:(g + 1) * hpg], uq_x2[g * hpg:(g + 1) * hpg], uq)
                         for g in range(MLA_GROUPS)]
    ukv = w_ukv[i]
    ukv_cols = ([ukv[:, h * 2 * HEAD_DIM:h * 2 * HEAD_DIM + HEAD_DIM] for h in range(N_HEADS)]
                + [ukv[:, h * 2 * HEAD_DIM + HEAD_DIM:(h + 1) * 2 * HEAD_DIM] for h in range(N_HEADS)])
    rg = mla_rope_gain[i]
    rope_gain = rope_slab([rg[:, :MLA_HALF]] * hpg, [rg[:, MLA_HALF:]] * hpg, rg)
    return {
        "ln_g": ln_g[i][None, :],
        "gains": jnp.tile(qk_gain[i], (1, N_HEADS)),
        "b_forget": jnp.pad(b_forget[i], (0, LANES - N_HEADS))[None, :],
        "q_norm": mla_q_norm[i][None, :],
        "kv_norm": mla_kv_norm[i][None, :],
        "nope_gain": jnp.tile(mla_nope_gain[i], (1, N_HEADS)),
        "rope_gain": rope_gain,
        "w_uq": jnp.concatenate(uq_cols, axis=1).astype(BF16),
        "w_ukv": jnp.concatenate(ukv_cols, axis=1).astype(BF16),
        "w_out": w_out[i].astype(BF16),
        "ple_norm_g": ple_norm_g[i][None, :],
        "w_ple_gate": w_ple_gate[i].astype(BF16),
        "w_ple_proj": w_ple_proj[i].astype(BF16),
    }


def _constants(S):
    inv = 1.0 / (ROPE_THETA ** (jnp.arange(MLA_HALF, dtype=F32) * 2.0 / MLA_ROPE))
    ang = jnp.arange(S).astype(F32)[:, None] * inv[None, :]
    cos = jnp.tile(jnp.cos(ang), (1, 2 * N_HEADS))
    sin = jnp.tile(jnp.sin(ang), (1, 2 * N_HEADS))
    sign = np.concatenate([-np.ones(LANES // 2, np.float32), np.ones(LANES // 2, np.float32)])
    lane = np.arange(GROUP_W)
    g64 = (lane[:, None] // HEAD_DIM == lane[None, :] // HEAD_DIM).astype(np.float32) / HEAD_DIM
    within = np.arange(LANES) % (LANES // 2)
    rope_head = np.where(within < 2 * MLA_HALF, within // MLA_HALF, -1)
    g32 = ((rope_head[:, None] == rope_head[None, :]) & (rope_head[:, None] >= 0)).astype(np.float32) / MLA_ROPE
    tri = np.tril(np.ones((T_PREP, T_PREP), np.float32))
    ekr = np.zeros((LANES, LANES), np.float32)
    kr_lane0 = IN_OFF["kr"] % LANES
    for c in range(MLA_ROPE):
        half, idx = divmod(c, MLA_HALF)
        for copy in range(N_HEADS // MLA_GROUPS):
            ekr[kr_lane0 + c, half * (LANES // 2) + copy * MLA_HALF + idx] = 1.0
    return {"cos": cos, "sin": sin * sign[None, :], "g64": jnp.asarray(g64, BF16),
            "g32": jnp.asarray(g32, BF16), "tri": jnp.asarray(tri, BF16), "ekr": jnp.asarray(ekr, BF16)}


def kernel(x, p, ln_g, w_in, b_forget, qk_gain, mla_q_norm, mla_kv_norm, mla_nope_gain, mla_rope_gain,
           w_uq, w_ukv, w_out, rel_bias, ple_norm_g, w_ple_gate, w_ple_proj):
    B, S, _ = x.shape
    depth = p.shape[0]
    consts = _constants(S)
    moba_tb, dil_tb = _bias_tables(rel_bias)
    x2 = x.reshape(B * S, D_MODEL)
    for i in range(depth):
        lp = _layer_params(i, ln_g, b_forget, qk_gain, mla_q_norm, mla_kv_norm, mla_nope_gain,
                           mla_rope_gain, w_uq, w_ukv, w_out, ple_norm_g, w_ple_gate, w_ple_proj)
        (fqT, fk, fvT, fck, fcrow, mqT, mk, mvT, mkmean, lqT, lk, lvT, *dil_qkv, gate) = _front(
            i, x2.reshape(B, S, D_MODEL), w_in, consts, lp)
        S_ = S
        fo = _dense_mixer(
            _fox_kernel, "fox", fqT, fk, fvT, (fck, fcrow),
            (pl.BlockSpec((1, N_HEADS, S_, LANES), lambda b, i: (b, 0, 0, 0)),
             pl.BlockSpec((1, 8, N_SUB * TQ), lambda b, i: (b, 0, i))))
        mo = _dense_mixer(
            _moba_kernel, "moba", mqT, mk, mvT, (mkmean, moba_tb),
            (pl.BlockSpec((1, S_ // MOBA_BLOCK, GROUP_W), lambda b, i: (b, 0, 0)),
             pl.BlockSpec((MOBA_TB_ENTRIES, N_HEADS, MOBA_BLOCK, MOBA_BLOCK), lambda b, i: (0, 0, 0, 0))),
            (pltpu.VMEM((N_SUB * N_HEADS, S_ // MOBA_BLOCK, TQ), F32),))
        lo = _dense_mixer(_mla_kernel, "mla", lqT, lk, lvT)
        dil_outs = []
        for pat, (_, dil) in enumerate(DIL_PATTERNS):
            q, k, v = (a.reshape(B, dil, S // dil, GROUP_W) for a in dil_qkv[3 * pat:3 * pat + 3])
            dil_outs += list(_dilated_pattern(pat, q, k, v, dil_tb))
        rs = lambda a: a.reshape(B * S, GROUP_W)
        x2 = _merge(i, x2, rs(fo), rs(mo), rs(lo), dil_outs, gate.reshape(B * S, D_MODEL),
                    p.reshape(depth, B * S, PLE_DIM), lp)
    return x2.reshape(B, S, D_MODEL)
```

```python
import functools
import math

import numpy as np
import jax
import jax.numpy as jnp
from jax import lax
from jax.experimental import pallas as pl
from jax.experimental.pallas import tpu as pltpu

F32 = jnp.float32
BF16 = jnp.bfloat16

D_MODEL = 1024
N_HEADS = 4
HEAD_DIM = 64
GROUP_W = N_HEADS * HEAD_DIM
MLA_NOPE = 64
MLA_ROPE = 32
MLA_HALF = MLA_ROPE // 2
MLA_QK_W = N_HEADS * (MLA_NOPE + MLA_ROPE)
MLA_Q_LORA = 256
MLA_KV_LORA = 128
PLE_DIM = 256
MOBA_BLOCK = 256
MOBA_TOPK = 3
DIL_PATTERNS = ((128, 1), (512, 4), (2048, 16))
DIL_BLOCK = 128
DIL_BLOCKS_PER_STEP = 4
N_BUCKETS = 32
MAX_DISTANCE = 2048
ROPE_THETA = 10000.0
EPS = 1e-6
NEG = -1e30
LOG2E = math.log2(math.e)

LANES = 128
VMEM_LIMIT_BYTES = 56 * 1024 * 1024

COL_FOX, COL_MOBA, COL_DIL = 0, 768, 1536
COL_CQ, COL_CKV, COL_KR, COL_FF, COL_GATE = 2304, 2560, 2688, 2816, 3072
PROJ_W = 4096
PROJ_TN = 512
STAGE_ROWS = 128

IN_OFF = {}
IN_W = 0
for _name, _size in (("fq", GROUP_W), ("fk", GROUP_W), ("fv", GROUP_W), ("ff", N_HEADS),
                     ("mq", GROUP_W), ("mk", GROUP_W), ("mv", GROUP_W),
                     ("dq", GROUP_W), ("dk", GROUP_W), ("dv", GROUP_W),
                     ("cq", MLA_Q_LORA), ("ckv", MLA_KV_LORA), ("kr", MLA_ROPE), ("gate", D_MODEL)):
    IN_OFF[_name] = IN_W
    IN_W += _size

TM_PROJ = 512
T_PREP = 256
FRONT_CHAINS = 2
TQ = 256
TK = 256
N_SUB = 2
MOBA_TB_ENTRIES = 8
TM_MERGE = 512
V_ROWS = HEAD_DIM + 16
MLA_GROUPS = 2


def _bucket_np(d):
    d = np.maximum(np.asarray(d, np.int64), 0)
    max_exact = N_BUCKETS // 2
    d_f = np.maximum(d, 1).astype(np.float64)
    val = np.log(d_f / max_exact) / math.log(MAX_DISTANCE / max_exact) * (N_BUCKETS - max_exact)
    frac = np.abs(val - np.round(val))
    on_edge = (frac < 1e-5) & (d > max_exact) & (val < N_BUCKETS - max_exact - 0.5)
    assert not on_edge.any(), "distance too close to a bucket edge for a static table"
    large = np.minimum(max_exact + np.floor(val + 1e-9).astype(np.int64), N_BUCKETS - 1)
    return np.where(d < max_exact, d, large)


def _bucket_steps(lo, hi):
    ds = np.arange(lo, hi + 1)
    bs = _bucket_np(ds)
    steps = [(int(ds[i]), int(bs[i])) for i in range(1, len(ds)) if bs[i] != bs[i - 1]]
    return int(bs[0]), steps


def _compiler_params(sem):
    return pltpu.CompilerParams(dimension_semantics=sem, vmem_limit_bytes=VMEM_LIMIT_BYTES)


def _bias_from_steps(d, rb_ref, col, lo, hi):
    b0, steps = _bucket_steps(lo, hi)
    val = jnp.full(d.shape, rb_ref[b0, col], F32)
    for t, b in steps:
        val = jnp.where(d >= t, rb_ref[b, col], val)
    return val


def _bias_tables_kernel(rb_ref, moba_ref, dil_ref):
    kl = lax.broadcasted_iota(jnp.int32, (MOBA_BLOCK, MOBA_BLOCK), 0)
    ql = lax.broadcasted_iota(jnp.int32, (MOBA_BLOCK, MOBA_BLOCK), 1)
    for e in range(MOBA_TB_ENTRIES - 1):
        d = jnp.maximum(e * MOBA_BLOCK + ql - kl, 0)
        lo = max(e * MOBA_BLOCK - (MOBA_BLOCK - 1), 0)
        hi = e * MOBA_BLOCK + (MOBA_BLOCK - 1)
        for h in range(N_HEADS):
            moba_ref[e, h] = _bias_from_steps(d, rb_ref, h, lo, hi) * LOG2E
    for h in range(N_HEADS):
        moba_ref[MOBA_TB_ENTRIES - 1, h] = jnp.full((MOBA_BLOCK, MOBA_BLOCK), rb_ref[N_BUCKETS - 1, h], F32) * LOG2E
    qi = lax.broadcasted_iota(jnp.int32, (DIL_BLOCK, 2 * DIL_BLOCK), 0)
    kj = lax.broadcasted_iota(jnp.int32, (DIL_BLOCK, 2 * DIL_BLOCK), 1)
    rel = qi + DIL_BLOCK - kj
    for p, (_, dil) in enumerate(DIL_PATTERNS):
        d = jnp.maximum(rel * dil, 0)
        for h in range(N_HEADS):
            bias = _bias_from_steps(d, rb_ref, N_HEADS + h, 0, (2 * DIL_BLOCK - 1) * dil)
            dil_ref[p, h] = jnp.where((rel >= 0) & (rel <= DIL_BLOCK), bias, NEG)


def _bias_tables(rel_bias):
    assert _bucket_np((MOBA_TB_ENTRIES - 1) * MOBA_BLOCK - (MOBA_BLOCK - 1)) == N_BUCKETS - 1
    return pl.pallas_call(
        _bias_tables_kernel,
        out_shape=(jax.ShapeDtypeStruct((MOBA_TB_ENTRIES, N_HEADS, MOBA_BLOCK, MOBA_BLOCK), F32),
                   jax.ShapeDtypeStruct((len(DIL_PATTERNS), N_HEADS, DIL_BLOCK, 2 * DIL_BLOCK), F32)),
        in_specs=[pl.BlockSpec(memory_space=pltpu.SMEM)],
        out_specs=(pl.BlockSpec(memory_space=pltpu.VMEM), pl.BlockSpec(memory_space=pltpu.VMEM)),
        compiler_params=pltpu.CompilerParams(vmem_limit_bytes=VMEM_LIMIT_BYTES),
        name="bias_tables",
    )(rel_bias)


def _stage_in_weights(w_ref, ekr_ref, ws_ref):
    W = GROUP_W
    moves = ((COL_FOX, IN_OFF["fq"], 3 * W), (COL_MOBA, IN_OFF["mq"], 3 * W), (COL_DIL, IN_OFF["dq"], 3 * W),
             (COL_CQ, IN_OFF["cq"], MLA_Q_LORA), (COL_CKV, IN_OFF["ckv"], MLA_KV_LORA),
             (COL_GATE, IN_OFF["gate"], D_MODEL))
    kr_base = IN_OFF["kr"] // LANES * LANES
    assert IN_OFF["ff"] % LANES == 0 and IN_OFF["kr"] + MLA_ROPE <= kr_base + LANES
    lane = lax.broadcasted_iota(jnp.int32, (STAGE_ROWS, LANES), 1)
    for r0 in range(0, D_MODEL, STAGE_ROWS):
        rows = slice(r0, r0 + STAGE_ROWS)
        for dst, src, width in moves:
            ws_ref[rows, dst:dst + width] = w_ref[0, rows, src:src + width].astype(BF16)
        kr_win = w_ref[0, rows, kr_base:kr_base + LANES].astype(BF16)
        ws_ref[rows, COL_KR:COL_KR + LANES] = jnp.dot(kr_win, ekr_ref[...], preferred_element_type=F32).astype(BF16)
        ff_win = w_ref[0, rows, IN_OFF["ff"]:IN_OFF["ff"] + LANES]
        ws_ref[rows, COL_FF:COL_FF + LANES] = jnp.where(lane < N_HEADS, ff_win, 0.0).astype(BF16)
        ws_ref[rows, COL_FF + LANES:COL_GATE] = jnp.zeros((STAGE_ROWS, COL_GATE - COL_FF - LANES), BF16)


def _group_mean_sq(x, g_mat):
    return jnp.dot((x * x).astype(BF16), g_mat, preferred_element_type=F32)


def _group_norm(x, gain, g_mat):
    return x * lax.rsqrt(_group_mean_sq(x, g_mat) + EPS) * gain


def _row_norm(x, gain):
    return x * lax.rsqrt(jnp.mean(x * x, axis=-1, keepdims=True) + EPS) * gain


def _log_sigmoid(x):
    return -(jnp.maximum(-x, 0.0) + jnp.log1p(jnp.exp(-jnp.abs(x))))


def _front_kernel(x_ref, lng_ref, w_ref, ekr_ref,
                  gains_ref, bfor_ref, qng_ref, kvng_ref, nopeg_ref, ropeg_ref,
                  wuq_ref, wukv_ref, cos_ref, sin_ref, g64_ref, g32_ref, tri_ref,
                  fqT_ref, fk_ref, fvT_ref, fck_ref, fcrow_ref,
                  mqT_ref, mk_ref, mvT_ref, mkmean_ref,
                  lqT_ref, lk_ref, lvT_ref,
                  dq1_ref, dk1_ref, dv1_ref, dq4_ref, dk4_ref, dv4_ref, dq16_ref, dk16_ref, dv16_ref,
                  gate_ref,
                  ws_ref, carry_ref, sq_ref, sk_ref, sv_ref):
    t = pl.program_id(1)

    @pl.when((pl.program_id(0) == 0) & (t == 0))
    def _():
        _stage_in_weights(w_ref, ekr_ref, ws_ref)

    g64 = g64_ref[...]
    scale = HEAD_DIM ** -0.5
    qscale = (MLA_NOPE + MLA_ROPE) ** -0.5
    W = GROUP_W
    T = T_PREP
    ones_rows = jnp.ones((V_ROWS - HEAD_DIM, T), BF16)

    @pl.when(t == 0)
    def _():
        carry_ref[...] = jnp.zeros_like(carry_ref)

    for ch in range(FRONT_CHAINS):
        rows = slice(ch * T, (ch + 1) * T)

        def store_vT(dst, v):
            vT = v.T.astype(BF16)
            for h in range(N_HEADS):
                dst[0, ch, h] = jnp.concatenate([vT[HEAD_DIM * h:HEAD_DIM * (h + 1)], ones_rows], axis=0)

        x = x_ref[0, rows, :]
        hn = (x * lax.rsqrt(jnp.mean(x * x, axis=-1, keepdims=True) + EPS) * lng_ref[...]).astype(BF16)
        project = lambda lo, width: jnp.dot(hn, ws_ref[:, lo:lo + width], preferred_element_type=F32)
        pf = project(COL_FOX, 3 * W)
        p_ff = project(COL_FF, LANES)
        pm = project(COL_MOBA, 3 * W)

        fq = _group_norm(pf[:, :W], gains_ref[0:1, :], g64) * (scale * LOG2E)
        fk = _group_norm(pf[:, W:2 * W], gains_ref[1:2, :], g64)
        fqT_ref[0, 0, :, rows] = fq.T.astype(BF16)
        fk_ref[0, 0, rows, :] = fk.astype(BF16)
        store_vT(fvT_ref, pf[:, 2 * W:])
        log_f = _log_sigmoid(p_ff + bfor_ref[...])
        tri = tri_ref[...]
        c = carry_ref[...]
        rest = log_f
        for _ in range(3):
            part = rest.astype(BF16)
            c = c + jnp.dot(tri, part, preferred_element_type=F32)
            rest = rest - part.astype(F32)
        carry_ref[...] = c[T - 1:T, :]
        c2 = c * LOG2E
        for h in range(N_HEADS):
            fck_ref[0, h, rows, :] = jnp.broadcast_to(c2[:, h:h + 1], (T, LANES))
        fcrow_ref[0, :, rows] = c2.T[0:8, :]

        pd = project(COL_DIL, 3 * W)
        mq = _group_norm(pm[:, :W], gains_ref[2:3, :], g64) * (scale * LOG2E)
        mk = _group_norm(pm[:, W:2 * W], gains_ref[3:4, :], g64)
        mqT_ref[0, 0, :, rows] = mq.T.astype(BF16)
        mk_ref[0, 0, rows, :] = mk.astype(BF16)
        store_vT(mvT_ref, pm[:, 2 * W:])
        mkmean_ref[0, pl.ds(FRONT_CHAINS * t + ch, 1), :] = jnp.mean(mk, axis=0, keepdims=True)

        p_mla = project(COL_CQ, MLA_Q_LORA + MLA_KV_LORA + LANES)
        dq = _group_norm(pd[:, :W], gains_ref[4:5, :], g64) * scale
        dk = _group_norm(pd[:, W:2 * W], gains_ref[5:6, :], g64)
        dv = pd[:, 2 * W:]
        dq1_ref[0, rows, :] = dq.astype(BF16)
        dk1_ref[0, rows, :] = dk.astype(BF16)
        dv1_ref[0, rows, :] = dv.astype(BF16)
        for val, dst in ((dq, sq_ref), (dk, sk_ref), (dv, sv_ref)):
            for half in range(GROUP_W // LANES):
                dst[ch, half] = val[:, half * LANES:(half + 1) * LANES]
        for dil, outs in ((4, (dq4_ref, dk4_ref, dv4_ref)), (16, (dq16_ref, dk16_ref, dv16_ref))):
            n = T // dil
            for r in range(dil):
                for src, dst in zip((sq_ref, sk_ref, sv_ref), outs):
                    dst[0, r, ch * n:(ch + 1) * n, :] = jnp.concatenate(
                        [src[ch, half, pl.ds(r, n, stride=dil), :] for half in range(GROUP_W // LANES)],
                        axis=1).astype(BF16)

        half_gate = D_MODEL // 2
        gate_ref[0, rows, :half_gate] = project(COL_GATE, half_gate).astype(BF16)
        cos = cos_ref[rows, :]
        sin = sin_ref[rows, :]
        cqn = _row_norm(p_mla[:, :MLA_Q_LORA], qng_ref[...]).astype(BF16)
        rotate = lambda v, cos=cos, sin=sin: v * cos + pltpu.roll(v, LANES // 2, 1) * sin
        qf = jnp.dot(cqn, wuq_ref[...], preferred_element_type=F32)
        q_nope = _group_norm(qf[:, :W], nopeg_ref[0:1, :], g64) * (qscale * LOG2E)
        ckvn = _row_norm(p_mla[:, MLA_Q_LORA:MLA_Q_LORA + MLA_KV_LORA], kvng_ref[...]).astype(BF16)
        kvf = jnp.dot(ckvn, wukv_ref[...], preferred_element_type=F32)
        k_nope = _group_norm(kvf[:, :W], nopeg_ref[1:2, :], g64)
        krs = p_mla[:, MLA_Q_LORA + MLA_KV_LORA:]
        kr_ms = jnp.sum(krs * krs, axis=-1, keepdims=True) * (1.0 / (2 * MLA_ROPE))
        kr = rotate(krs * lax.rsqrt(kr_ms + EPS) * ropeg_ref[1:2, :])
        for g in range(MLA_GROUPS):
            slab = qf[:, W + g * LANES:W + (g + 1) * LANES]
            qr = rotate(_group_norm(slab, ropeg_ref[0:1, :], g32_ref[...])) * (qscale * LOG2E)
            lq = jnp.concatenate([q_nope[:, g * LANES:(g + 1) * LANES], qr], axis=1)
            lqT_ref[0, g, :, rows] = lq.T.astype(BF16)
            lk_ref[0, g, rows, :] = jnp.concatenate([k_nope[:, g * LANES:(g + 1) * LANES], kr],
                                                    axis=1).astype(BF16)
        store_vT(lvT_ref, kvf[:, W:])
        gate_ref[0, rows, half_gate:] = project(COL_GATE + half_gate, half_gate).astype(BF16)


def _front(layer, x3, w_in, consts, lp):
    B, S, _ = x3.shape
    T = T_PREP
    TS = FRONT_CHAINS * T
    NT = S // T
    assert S % TS == 0 and S // MOBA_BLOCK == NT

    def cspec(shape):
        return pl.BlockSpec(shape, lambda b, t: (0,) * len(shape))

    in_specs = [pl.BlockSpec((1, TS, D_MODEL), lambda b, t: (b, t, 0)), cspec((1, D_MODEL)),
                pl.BlockSpec((1, D_MODEL, IN_W), lambda b, t: (layer, 0, 0), pipeline_mode=pl.Buffered(1)),
                cspec((LANES, LANES)),
                cspec((6, GROUP_W)), cspec((1, LANES)), cspec((1, MLA_Q_LORA)), cspec((1, MLA_KV_LORA)),
                cspec((2, GROUP_W)), cspec((2, LANES)),
                cspec((MLA_Q_LORA, MLA_GROUPS * GROUP_W)), cspec((MLA_KV_LORA, 2 * GROUP_W)),
                pl.BlockSpec((TS, LANES), lambda b, t: (t, 0)), pl.BlockSpec((TS, LANES), lambda b, t: (t, 0)),
                cspec((GROUP_W, GROUP_W)), cspec((LANES, LANES)), cspec((T, T))]

    def qT(groups):
        return (jax.ShapeDtypeStruct((B, groups, GROUP_W, S), BF16),
                pl.BlockSpec((1, groups, GROUP_W, TS), lambda b, t: (b, 0, 0, t)))

    def keys(groups):
        return (jax.ShapeDtypeStruct((B, groups, S, GROUP_W), BF16),
                pl.BlockSpec((1, groups, TS, GROUP_W), lambda b, t: (b, 0, t, 0)))

    def rows(w):
        return jax.ShapeDtypeStruct((B, S, w), BF16), pl.BlockSpec((1, TS, w), lambda b, t: (b, t, 0))

    def vT():
        return (jax.ShapeDtypeStruct((B, NT, N_HEADS, V_ROWS, T), BF16),
                pl.BlockSpec((1, FRONT_CHAINS, N_HEADS, V_ROWS, T), lambda b, t: (b, t, 0, 0, 0)))

    def resid(dil):
        return (jax.ShapeDtypeStruct((B, dil, S // dil, GROUP_W), BF16),
                pl.BlockSpec((1, dil, TS // dil, GROUP_W), lambda b, t: (b, 0, t, 0)))

    outs = [qT(1), keys(1), vT(),
            (jax.ShapeDtypeStruct((B, N_HEADS, S, LANES), F32),
             pl.BlockSpec((1, N_HEADS, TS, LANES), lambda b, t: (b, 0, t, 0))),
            (jax.ShapeDtypeStruct((B, 8, S), F32), pl.BlockSpec((1, 8, TS), lambda b, t: (b, 0, t))),
            qT(1), keys(1), vT(),
            (jax.ShapeDtypeStruct((B, NT, GROUP_W), F32), pl.BlockSpec((1, NT, GROUP_W), lambda b, t: (b, 0, 0))),
            qT(MLA_GROUPS), keys(MLA_GROUPS), vT(),
            rows(GROUP_W), rows(GROUP_W), rows(GROUP_W),
            resid(4), resid(4), resid(4), resid(16), resid(16), resid(16),
            rows(D_MODEL)]
    return pl.pallas_call(
        _front_kernel,
        out_shape=tuple(o[0] for o in outs),
        grid=(B, NT // FRONT_CHAINS),
        in_specs=in_specs,
        out_specs=tuple(o[1] for o in outs),
        scratch_shapes=[pltpu.VMEM((D_MODEL, PROJ_W), BF16), pltpu.VMEM((1, LANES), F32)]
        + [pltpu.VMEM((FRONT_CHAINS, GROUP_W // LANES, T, LANES), F32)] * 3,
        compiler_params=_compiler_params(("arbitrary", "arbitrary")),
        name="front",
    )(x3, lp["ln_g"], w_in, consts["ekr"],
      lp["gains"], lp["b_forget"], lp["q_norm"], lp["kv_norm"], lp["nope_gain"], lp["rope_gain"],
      lp["w_uq"], lp["w_ukv"], consts["cos"], consts["sin"], consts["g64"], consts["g32"], consts["tri"])


def _stage_head_queries(qT_ref, qm_ref, groups):
    r = lax.broadcasted_iota(jnp.int32, (GROUP_W, TQ), 0)
    for h in range(N_HEADS):
        if groups == 1:
            g, mask = 0, (r >= HEAD_DIM * h) & (r < HEAD_DIM * (h + 1))
        else:
            g, hl = divmod(h, N_HEADS // groups)
            mask = (r >= HEAD_DIM * hl) & (r < HEAD_DIM * (hl + 1))
            for base in (2 * HEAD_DIM, 2 * HEAD_DIM + LANES // 2):
                lo = base + MLA_HALF * hl
                mask = mask | ((r >= lo) & (r < lo + MLA_HALF))
        for sub in range(N_SUB):
            q = qT_ref[0, g, :, sub * TQ:(sub + 1) * TQ]
            qm_ref[sub * N_HEADS + h] = jnp.where(mask, q, jnp.zeros_like(q))


def _flash_sweep(pair_idx, groups, qm_ref, k_ref, vT_ref, m_ref, acc_ref, u_ref, o_ref, score_fn):
    kpos = lax.broadcasted_iota(jnp.int32, (TK, TQ), 0)
    qpos = lax.broadcasted_iota(jnp.int32, (TK, TQ), 1)
    causal = kpos <= qpos
    m_ref[...] = jnp.full(m_ref.shape, NEG, F32)
    acc_ref[...] = jnp.zeros(acc_ref.shape, F32)
    both = tuple(range(N_SUB))

    def scores(j0, nblk, diag_last, subs):
        row0 = pl.multiple_of(j0 * TK, TK)
        kts = [k_ref[0, g, pl.ds(row0, nblk * TK), :] for g in range(groups)]
        pending = []
        for sub in subs:
            for h in range(N_HEADS):
                c = sub * N_HEADS + h
                s = jnp.dot(kts[h // (N_HEADS // groups)], qm_ref[c], preferred_element_type=F32)
                tops, lives, row = [], [], None
                for blk in range(nblk):
                    rows = slice(blk * TK, (blk + 1) * TK)
                    diagonal = diag_last and blk == nblk - 1
                    u, row, live = score_fn(sub, h, s[rows], j0 + blk, diagonal)
                    if diagonal:
                        u = jnp.where(causal, u, NEG)
                    u_ref[c, rows, :] = u
                    top = jnp.max(u, axis=0, keepdims=True)
                    if row is not None:
                        top = top + row
                    if live is not None:
                        top = jnp.where(live, top, NEG)
                    tops.append(top)
                    lives.append(live)
                m_old = m_ref[c]
                m_new = functools.reduce(jnp.maximum, tops, m_old)
                shift = m_new if row is None else m_new - row
                shifts = [shift if live is None else jnp.where(live, shift, -NEG) for live in lives]
                m_ref[c] = m_new
                pending.append((c, h, shifts, jnp.exp2(m_old - m_new)))
        return pending

    def values(j0, pending):
        for c, h, shifts, alpha in pending:
            pv = None
            for blk, shift in enumerate(shifts):
                p = jnp.exp2((u_ref[c, blk * TK:(blk + 1) * TK, :] - shift).astype(BF16))
                term = jnp.dot(vT_ref[0, j0 + blk, h], p, preferred_element_type=F32)
                pv = term if pv is None else pv + term
            acc_ref[c] = alpha * acc_ref[c] + pv

    def common(t, carry):
        values(2 * t, scores(2 * t, 2, False, both))
        return carry

    lax.fori_loop(0, pair_idx, common, 0)
    n_common = N_SUB * pair_idx
    pend_a = scores(n_common, 1, True, (0,))
    pend_b = scores(n_common, 2, True, (1,))
    values(n_common, pend_a)
    values(n_common, pend_b)
    for sub in both:
        outs = []
        for h in range(N_HEADS):
            acc = acc_ref[sub * N_HEADS + h]
            outs.append(acc[:HEAD_DIM] / acc[HEAD_DIM:HEAD_DIM + 1])
        o_ref[0, sub * TQ:(sub + 1) * TQ, :] = jnp.concatenate(outs, axis=0).T.astype(BF16)


def _fox_kernel(qT_ref, k_ref, vT_ref, ck_ref, crow_ref, o_ref, qm_ref, m_ref, acc_ref, u_ref):
    pair_idx = pl.program_id(1)
    _stage_head_queries(qT_ref, qm_ref, 1)

    def score_fn(sub, h, s, j, diagonal):
        ck = ck_ref[0, h, pl.ds(pl.multiple_of(j * TK, TK), TK), :]
        cq = crow_ref[0, h:h + 1, sub * TQ:(sub + 1) * TQ]
        return s - jnp.concatenate([ck] * (TQ // LANES), axis=1), cq, None

    _flash_sweep(pair_idx, 1, qm_ref, k_ref, vT_ref, m_ref, acc_ref, u_ref, o_ref, score_fn)


def _mla_kernel(qT_ref, k_ref, vT_ref, o_ref, qm_ref, m_ref, acc_ref, u_ref):
    pair_idx = pl.program_id(1)
    _stage_head_queries(qT_ref, qm_ref, MLA_GROUPS)
    _flash_sweep(pair_idx, MLA_GROUPS, qm_ref, k_ref, vT_ref, m_ref, acc_ref, u_ref, o_ref,
                 lambda sub, h, s, j, diagonal: (s, None, None))


def _moba_kernel(qT_ref, k_ref, vT_ref, kmean_ref, tb_ref, o_ref, qm_ref, m_ref, acc_ref, u_ref, sel_ref):
    pair_idx = pl.program_id(1)
    _stage_head_queries(qT_ref, qm_ref, 1)
    kmean = kmean_ref[0].astype(BF16)
    nb = kmean.shape[0]
    blk = lax.broadcasted_iota(jnp.int32, (nb, TQ), 0)
    own = [N_SUB * pair_idx + sub for sub in range(N_SUB)]
    for c in range(N_SUB * N_HEADS):
        i = own[c // N_HEADS]
        gate = jnp.dot(kmean, qm_ref[c], preferred_element_type=F32)
        gate = jnp.where(blk < i, gate, NEG)
        rank = jnp.zeros((nb, TQ), jnp.int32)
        for jp in range(nb):
            row = gate[jp:jp + 1, :]
            ahead = (row > gate) | ((row == gate) & (jp < blk))
            rank = rank + ahead.astype(jnp.int32)
        sel_ref[c] = ((rank < MOBA_TOPK) & (blk < i)).astype(F32)

    def score_fn(sub, h, s, j, diagonal):
        u = s + tb_ref[jnp.minimum(own[sub] - j, MOBA_TB_ENTRIES - 1), h]
        live = None if diagonal else sel_ref[sub * N_HEADS + h, pl.ds(j, 1), :] > 0.5
        return u, None, live

    _flash_sweep(pair_idx, 1, qm_ref, k_ref, vT_ref, m_ref, acc_ref, u_ref, o_ref, score_fn)


def _dense_mixer(kernel_fn, name, qT, k, vT, extra_args=(), extra_specs=(), extra_scratch=()):
    B, groups, _, S = qT.shape
    NT = S // TK
    assert TQ == TK and N_SUB == 2 and S % (N_SUB * TQ) == 0
    n_chain = N_SUB * N_HEADS
    in_specs = [pl.BlockSpec((1, groups, GROUP_W, N_SUB * TQ), lambda b, i: (b, 0, 0, i)),
                pl.BlockSpec((1, groups, S, GROUP_W), lambda b, i: (b, 0, 0, 0)),
                pl.BlockSpec((1, NT, N_HEADS, V_ROWS, TK), lambda b, i: (b, 0, 0, 0, 0))] + list(extra_specs)
    scratch = [pltpu.VMEM((n_chain, GROUP_W, TQ), BF16),
               pltpu.VMEM((n_chain, 1, TQ), F32),
               pltpu.VMEM((n_chain, V_ROWS, TQ), F32),
               pltpu.VMEM((n_chain, 2 * TK, TQ), F32)] + list(extra_scratch)
    return pl.pallas_call(
        kernel_fn,
        out_shape=jax.ShapeDtypeStruct((B, S, GROUP_W), BF16),
        grid=(B, S // (N_SUB * TQ)),
        in_specs=in_specs,
        out_specs=pl.BlockSpec((1, N_SUB * TQ, GROUP_W), lambda b, i: (b, i, 0)),
        scratch_shapes=scratch,
        compiler_params=_compiler_params(("arbitrary", "arbitrary")),
        name=name,
    )(qT, k, vT, *extra_args)


def _dil_kernel(q_ref, kp_ref, kc_ref, vp_ref, vc_ref, tb_ref, o_ref, lse_ref):
    n = pl.program_id(2)
    qb = q_ref.shape[2] // DIL_BLOCK
    kj = lax.broadcasted_iota(jnp.int32, (DIL_BLOCK, 2 * DIL_BLOCK), 1)
    qlane = lax.broadcasted_iota(jnp.int32, (DIL_BLOCK, GROUP_W), 1)
    in_head = [(qlane >= HEAD_DIM * h) & (qlane < HEAD_DIM * (h + 1)) for h in range(N_HEADS)]
    for seq, c in [(seq, c) for seq in range(q_ref.shape[1]) for c in range(qb)]:
        rows = slice(c * DIL_BLOCK, (c + 1) * DIL_BLOCK)
        q = q_ref[0, seq, rows, :]
        if c == 0:
            kb = jnp.concatenate([kp_ref[0, seq], kc_ref[0, seq, :DIL_BLOCK, :]], axis=0)
            vb = jnp.concatenate([vp_ref[0, seq], vc_ref[0, seq, :DIL_BLOCK, :]], axis=0)
        else:
            kb = kc_ref[0, seq, (c - 1) * DIL_BLOCK:(c + 1) * DIL_BLOCK, :]
            vb = vc_ref[0, seq, (c - 1) * DIL_BLOCK:(c + 1) * DIL_BLOCK, :]
        q_heads = jnp.concatenate([jnp.where(msk, q, jnp.zeros_like(q)) for msk in in_head], axis=0)
        s = lax.dot_general(q_heads, kb, (((1,), (1,)), ((), ())), preferred_element_type=F32)
        s = s.reshape(N_HEADS, DIL_BLOCK, 2 * DIL_BLOCK)
        s = s + tb_ref[0]
        if c == 0:
            s = jnp.where(((kj >= DIL_BLOCK) | (n > 0))[None], s, NEG)
        m = jnp.max(s, axis=-1, keepdims=True)
        e = jnp.exp(s - m)
        l = jnp.sum(e, axis=-1, keepdims=True)
        lse = m + jnp.log(l)
        pr = (e * (1.0 / l)).astype(BF16)
        o_all = jnp.dot(pr.reshape(N_HEADS * DIL_BLOCK, 2 * DIL_BLOCK), vb, preferred_element_type=F32)
        o_all = o_all.reshape(N_HEADS, DIL_BLOCK, GROUP_W)
        o_acc = jnp.zeros((DIL_BLOCK, GROUP_W), F32)
        lse_map = jnp.zeros((DIL_BLOCK, GROUP_W), F32)
        for h in range(N_HEADS):
            o_acc = jnp.where(in_head[h], o_all[h], o_acc)
            lse_map = jnp.where(in_head[h], lse[h], lse_map)
        o_ref[0, seq, rows, :] = o_acc.astype(BF16)
        lse_ref[0, seq, rows, :] = lse_map


def _dilated_pattern(p, q, k, v, dil_tb):
    B, dil, L, _ = q.shape
    assert dil == DIL_PATTERNS[p][1] and L % DIL_BLOCK == 0
    nb = L // DIL_BLOCK
    qb = math.gcd(nb, DIL_BLOCKS_PER_STEP)
    seqs = math.gcd(dil, DIL_BLOCKS_PER_STEP // qb)
    cur = pl.BlockSpec((1, seqs, qb * DIL_BLOCK, GROUP_W), lambda b, r, n: (b, r, n, 0))
    prev = pl.BlockSpec((1, seqs, DIL_BLOCK, GROUP_W), lambda b, r, n: (b, r, jnp.maximum(n * qb - 1, 0), 0))
    return pl.pallas_call(
        _dil_kernel,
        out_shape=(jax.ShapeDtypeStruct((B, dil, L, GROUP_W), BF16),
                   jax.ShapeDtypeStruct((B, dil, L, GROUP_W), F32)),
        grid=(B, dil // seqs, nb // qb),
        in_specs=[cur, prev, cur, prev, cur,
                  pl.BlockSpec((1, N_HEADS, DIL_BLOCK, 2 * DIL_BLOCK), lambda b, r, n, p=p: (p, 0, 0, 0))],
        out_specs=(cur, cur),
        compiler_params=_compiler_params(("arbitrary", "arbitrary", "arbitrary")),
        name=f"dilated_{dil}",
    )(q, k, k, v, v, dil_tb)


def _mix_dilations(o_refs, lse_refs, scratch):
    halves = GROUP_W // LANES
    rows = o_refs[0].shape[1] * o_refs[0].shape[2]
    nat = []
    for ref, dst in zip(list(o_refs) + list(lse_refs), scratch):
        dil = ref.shape[1]
        if dil == 1:
            nat.append(ref[0, 0].astype(F32))
            continue
        for r in range(dil):
            for half in range(halves):
                dst[half, pl.ds(r, rows // dil, stride=dil), :] = (
                    ref[0, r, :, half * LANES:(half + 1) * LANES].astype(F32))
        nat.append(jnp.concatenate([dst[half] for half in range(halves)], axis=1))
    os, lses = nat[:len(o_refs)], nat[len(o_refs):]
    m = functools.reduce(jnp.maximum, lses)
    ws = [jnp.exp(l - m) for l in lses]
    tot = functools.reduce(jnp.add, ws)
    return functools.reduce(jnp.add, [(w / tot) * o for w, o in zip(ws, os)])


def _sigmoid(x):
    return 1.0 / (1.0 + jnp.exp(-x))


def _merge_kernel(x_ref, fo_ref, mo_ref, lo_ref, o1_ref, l1_ref, o4_ref, l4_ref, o16_ref, l16_ref, gate_ref, p_ref,
                  wout_ref, png_ref, wpg_ref, wpp_ref, o_ref, *dil_scratch):
    o_dil = _mix_dilations((o1_ref, o4_ref, o16_ref), (l1_ref, l4_ref, l16_ref), dil_scratch)
    mix = jnp.concatenate([fo_ref[...].astype(F32), mo_ref[...].astype(F32), o_dil, lo_ref[...].astype(F32)],
                          axis=1)
    g = gate_ref[...].astype(F32)
    y = jnp.dot((mix * (g * _sigmoid(g))).astype(BF16), wout_ref[...], preferred_element_type=F32)
    x1 = x_ref[...] + y
    hn = _row_norm(x1, png_ref[...]).astype(BF16)
    pg = _sigmoid(jnp.dot(hn, wpg_ref[...], preferred_element_type=F32))
    pp = jnp.dot(p_ref[0].astype(BF16), wpp_ref[...], preferred_element_type=F32)
    o_ref[...] = x1 + pg * pp


def _merge(layer, x2, fo, mo, lo, dil_outs, proj, p3, lp):
    rows = x2.shape[0]
    TM = TM_MERGE
    tiles_per_seq = dil_outs[0].shape[1] * dil_outs[0].shape[2] // TM

    def rspec(w, c=0):
        return pl.BlockSpec((TM, w), lambda i, c=c: (i, c))

    def cspec(shape):
        return pl.BlockSpec(shape, lambda i: (0, 0))

    def dspec(a):
        dil = a.shape[1]
        return pl.BlockSpec((1, dil, TM // dil, GROUP_W),
                            lambda i: (i // tiles_per_seq, 0, lax.rem(i, tiles_per_seq), 0))

    return pl.pallas_call(
        _merge_kernel,
        out_shape=jax.ShapeDtypeStruct((rows, D_MODEL), F32),
        grid=(rows // TM,),
        scratch_shapes=[pltpu.VMEM((GROUP_W // LANES, TM, LANES), F32)] * len(dil_outs),
        in_specs=[rspec(D_MODEL)] + [rspec(GROUP_W)] * 3 + [dspec(a) for a in dil_outs] + [
                  rspec(D_MODEL),
                  pl.BlockSpec((1, TM, PLE_DIM), lambda i: (layer, i, 0)),
                  cspec((D_MODEL, D_MODEL)), cspec((1, D_MODEL)), cspec((D_MODEL, D_MODEL)),
                  cspec((PLE_DIM, D_MODEL))],
        out_specs=rspec(D_MODEL),
        compiler_params=_compiler_params(("arbitrary",)),
        name="merge",
    )(x2, fo, mo, lo, *dil_outs, proj, p3, lp["w_out"], lp["ple_norm_g"], lp["w_ple_gate"], lp["w_ple_proj"])


def _layer_params(i, ln_g, b_forget, qk_gain, mla_q_norm, mla_kv_norm, mla_nope_gain,
                  mla_rope_gain, w_uq, w_ukv, w_out, ple_norm_g, w_ple_gate, w_ple_proj):
    def rope_slab(x1_parts, x2_parts, like):
        pad = jnp.zeros(like.shape[:-1] + (LANES // 2 - 2 * MLA_HALF,), like.dtype)
        return jnp.concatenate(list(x1_parts) + [pad] + list(x2_parts) + [pad], axis=-1)

    per_head = MLA_NOPE + MLA_ROPE
    uq = w_uq[i]
    uq_nope = [uq[:, h * per_head:h * per_head + MLA_NOPE] for h in range(N_HEADS)]
    uq_x1 = [uq[:, h * per_head + MLA_NOPE:h * per_head + MLA_NOPE + MLA_HALF] for h in range(N_HEADS)]
    uq_x2 = [uq[:, h * per_head + MLA_NOPE + MLA_HALF:(h + 1) * per_head] for h in range(N_HEADS)]
    hpg = N_HEADS // MLA_GROUPS
    uq_cols = uq_nope + [rope_slab(uq_x1[g * hpg:(g + 1) * hpg], uq_x2[g * hpg:(g + 1) * hpg], uq)
                         for g in range(MLA_GROUPS)]
    ukv = w_ukv[i]
    ukv_cols = ([ukv[:, h * 2 * HEAD_DIM:h * 2 * HEAD_DIM + HEAD_DIM] for h in range(N_HEADS)]
                + [ukv[:, h * 2 * HEAD_DIM + HEAD_DIM:(h + 1) * 2 * HEAD_DIM] for h in range(N_HEADS)])
    rg = mla_rope_gain[i]
    rope_gain = rope_slab([rg[:, :MLA_HALF]] * hpg, [rg[:, MLA_HALF:]] * hpg, rg)
    return {
        "ln_g": ln_g[i][None, :],
        "gains": jnp.tile(qk_gain[i], (1, N_HEADS)),
        "b_forget": jnp.pad(b_forget[i], (0, LANES - N_HEADS))[None, :],
        "q_norm": mla_q_norm[i][None, :],
        "kv_norm": mla_kv_norm[i][None, :],
        "nope_gain": jnp.tile(mla_nope_gain[i], (1, N_HEADS)),
        "rope_gain": rope_gain,
        "w_uq": jnp.concatenate(uq_cols, axis=1).astype(BF16),
        "w_ukv": jnp.concatenate(ukv_cols, axis=1).astype(BF16),
        "w_out": w_out[i].astype(BF16),
        "ple_norm_g": ple_norm_g[i][None, :],
        "w_ple_gate": w_ple_gate[i].astype(BF16),
        "w_ple_proj": w_ple_proj[i].astype(BF16),
    }


def _constants(S):
    inv = 1.0 / (ROPE_THETA ** (jnp.arange(MLA_HALF, dtype=F32) * 2.0 / MLA_ROPE))
    ang = jnp.arange(S).astype(F32)[:, None] * inv[None, :]
    cos = jnp.tile(jnp.cos(ang), (1, 2 * N_HEADS))
    sin = jnp.tile(jnp.sin(ang), (1, 2 * N_HEADS))
    sign = np.concatenate([-np.ones(LANES // 2, np.float32), np.ones(LANES // 2, np.float32)])
    lane = np.arange(GROUP_W)
    g64 = (lane[:, None] // HEAD_DIM == lane[None, :] // HEAD_DIM).astype(np.float32) / HEAD_DIM
    within = np.arange(LANES) % (LANES // 2)
    rope_head = np.where(within < 2 * MLA_HALF, within // MLA_HALF, -1)
    g32 = ((rope_head[:, None] == rope_head[None, :]) & (rope_head[:, None] >= 0)).astype(np.float32) / MLA_ROPE
    tri = np.tril(np.ones((T_PREP, T_PREP), np.float32))
    ekr = np.zeros((LANES, LANES), np.float32)
    kr_lane0 = IN_OFF["kr"] % LANES
    for c in range(MLA_ROPE):
        half, idx = divmod(c, MLA_HALF)
        for copy in range(N_HEADS // MLA_GROUPS):
            ekr[kr_lane0 + c, half * (LANES // 2) + copy * MLA_HALF + idx] = 1.0
    return {"cos": cos, "sin": sin * sign[None, :], "g64": jnp.asarray(g64, BF16),
            "g32": jnp.asarray(g32, BF16), "tri": jnp.asarray(tri, BF16), "ekr": jnp.asarray(ekr, BF16)}


def kernel(x, p, ln_g, w_in, b_forget, qk_gain, mla_q_norm, mla_kv_norm, mla_nope_gain, mla_rope_gain,
           w_uq, w_ukv, w_out, rel_bias, ple_norm_g, w_ple_gate, w_ple_proj):
    B, S, _ = x.shape
    depth = p.shape[0]
    consts = _constants(S)
    moba_tb, dil_tb = _bias_tables(rel_bias)
    x2 = x.reshape(B * S, D_MODEL)
    for i in range(depth):
        lp = _layer_params(i, ln_g, b_forget, qk_gain, mla_q_norm, mla_kv_norm, mla_nope_gain,
                           mla_rope_gain, w_uq, w_ukv, w_out, ple_norm_g, w_ple_gate, w_ple_proj)
        (fqT, fk, fvT, fck, fcrow, mqT, mk, mvT, mkmean, lqT, lk, lvT, *dil_qkv, gate) = _front(
            i, x2.reshape(B, S, D_MODEL), w_in, consts, lp)
        S_ = S
        fo = _dense_mixer(
            _fox_kernel, "fox", fqT, fk, fvT, (fck, fcrow),
            (pl.BlockSpec((1, N_HEADS, S_, LANES), lambda b, i: (b, 0, 0, 0)),
             pl.BlockSpec((1, 8, N_SUB * TQ), lambda b, i: (b, 0, i))))
        mo = _dense_mixer(
            _moba_kernel, "moba", mqT, mk, mvT, (mkmean, moba_tb),
            (pl.BlockSpec((1, S_ // MOBA_BLOCK, GROUP_W), lambda b, i: (b, 0, 0)),
             pl.BlockSpec((MOBA_TB_ENTRIES, N_HEADS, MOBA_BLOCK, MOBA_BLOCK), lambda b, i: (0, 0, 0, 0))),
            (pltpu.VMEM((N_SUB * N_HEADS, S_ // MOBA_BLOCK, TQ), F32),))
        lo = _dense_mixer(_mla_kernel, "mla", lqT, lk, lvT)
        dil_outs = []
        for pat, (_, dil) in enumerate(DIL_PATTERNS):
            q, k, v = (a.reshape(B, dil, S // dil, GROUP_W) for a in dil_qkv[3 * pat:3 * pat + 3])
            dil_outs += list(_dilated_pattern(pat, q, k, v, dil_tb))
        rs = lambda a: a.reshape(B * S, GROUP_W)
        x2 = _merge(i, x2, rs(fo), rs(mo), rs(lo), dil_outs, gate.reshape(B * S, D_MODEL),
                    p.reshape(depth, B * S, PLE_DIM), lp)
    return x2.reshape(B, S, D_MODEL)
```

```python
import functools
import math

import numpy as np
import jax
import jax.numpy as jnp
from jax import lax
from jax.experimental import pallas as pl
from jax.experimental.pallas import tpu as pltpu

F32 = jnp.float32
BF16 = jnp.bfloat16

D_MODEL = 1024
N_HEADS = 4
HEAD_DIM = 64
GROUP_W = N_HEADS * HEAD_DIM
MLA_NOPE = 64
MLA_ROPE = 32
MLA_HALF = MLA_ROPE // 2
MLA_QK_W = N_HEADS * (MLA_NOPE + MLA_ROPE)
MLA_Q_LORA = 256
MLA_KV_LORA = 128
PLE_DIM = 256
MOBA_BLOCK = 256
MOBA_TOPK = 3
DIL_PATTERNS = ((128, 1), (512, 4), (2048, 16))
DIL_BLOCK = 128
DIL_BLOCKS_PER_STEP = 4
N_BUCKETS = 32
MAX_DISTANCE = 2048
ROPE_THETA = 10000.0
EPS = 1e-6
NEG = -1e30
LOG2E = math.log2(math.e)

LANES = 128
VMEM_LIMIT_BYTES = 56 * 1024 * 1024

COL_FOX, COL_MOBA, COL_DIL = 0, 768, 1536
COL_CQ, COL_CKV, COL_KR, COL_FF, COL_GATE = 2304, 2560, 2688, 2816, 3072
PROJ_W = 4096
PROJ_TN = 512
STAGE_ROWS = 128

IN_OFF = {}
IN_W = 0
for _name, _size in (("fq", GROUP_W), ("fk", GROUP_W), ("fv", GROUP_W), ("ff", N_HEADS),
                     ("mq", GROUP_W), ("mk", GROUP_W), ("mv", GROUP_W),
                     ("dq", GROUP_W), ("dk", GROUP_W), ("dv", GROUP_W),
                     ("cq", MLA_Q_LORA), ("ckv", MLA_KV_LORA), ("kr", MLA_ROPE), ("gate", D_MODEL)):
    IN_OFF[_name] = IN_W
    IN_W += _size

TM_PROJ = 512
T_PREP = 256
FRONT_CHAINS = 2
TQ = 256
TK = 256
N_SUB = 2
MOBA_TB_ENTRIES = 8
TM_MERGE = 512
V_ROWS = HEAD_DIM + 16
MLA_GROUPS = 2


def _bucket_np(d):
    d = np.maximum(np.asarray(d, np.int64), 0)
    max_exact = N_BUCKETS // 2
    d_f = np.maximum(d, 1).astype(np.float64)
    val = np.log(d_f / max_exact) / math.log(MAX_DISTANCE / max_exact) * (N_BUCKETS - max_exact)
    frac = np.abs(val - np.round(val))
    on_edge = (frac < 1e-5) & (d > max_exact) & (val < N_BUCKETS - max_exact - 0.5)
    assert not on_edge.any(), "distance too close to a bucket edge for a static table"
    large = np.minimum(max_exact + np.floor(val + 1e-9).astype(np.int64), N_BUCKETS - 1)
    return np.where(d < max_exact, d, large)


def _bucket_steps(lo, hi):
    ds = np.arange(lo, hi + 1)
    bs = _bucket_np(ds)
    steps = [(int(ds[i]), int(bs[i])) for i in range(1, len(ds)) if bs[i] != bs[i - 1]]
    return int(bs[0]), steps


def _compiler_params(sem):
    return pltpu.CompilerParams(dimension_semantics=sem, vmem_limit_bytes=VMEM_LIMIT_BYTES)


def _bias_from_steps(d, rb_ref, col, lo, hi):
    b0, steps = _bucket_steps(lo, hi)
    val = jnp.full(d.shape, rb_ref[b0, col], F32)
    for t, b in steps:
        val = jnp.where(d >= t, rb_ref[b, col], val)
    return val


def _bias_tables_kernel(rb_ref, moba_ref, dil_ref):
    kl = lax.broadcasted_iota(jnp.int32, (MOBA_BLOCK, MOBA_BLOCK), 0)
    ql = lax.broadcasted_iota(jnp.int32, (MOBA_BLOCK, MOBA_BLOCK), 1)
    for e in range(MOBA_TB_ENTRIES - 1):
        d = jnp.maximum(e * MOBA_BLOCK + ql - kl, 0)
        lo = max(e * MOBA_BLOCK - (MOBA_BLOCK - 1), 0)
        hi = e * MOBA_BLOCK + (MOBA_BLOCK - 1)
        for h in range(N_HEADS):
            moba_ref[e, h] = _bias_from_steps(d, rb_ref, h, lo, hi) * LOG2E
    for h in range(N_HEADS):
        moba_ref[MOBA_TB_ENTRIES - 1, h] = jnp.full((MOBA_BLOCK, MOBA_BLOCK), rb_ref[N_BUCKETS - 1, h], F32) * LOG2E
    qi = lax.broadcasted_iota(jnp.int32, (DIL_BLOCK, 2 * DIL_BLOCK), 0)
    kj = lax.broadcasted_iota(jnp.int32, (DIL_BLOCK, 2 * DIL_BLOCK), 1)
    rel = qi + DIL_BLOCK - kj
    for p, (_, dil) in enumerate(DIL_PATTERNS):
        d = jnp.maximum(rel * dil, 0)
        for h in range(N_HEADS):
            bias = _bias_from_steps(d, rb_ref, N_HEADS + h, 0, (2 * DIL_BLOCK - 1) * dil)
            dil_ref[p, h] = jnp.where((rel >= 0) & (rel <= DIL_BLOCK), bias, NEG)


def _bias_tables(rel_bias):
    assert _bucket_np((MOBA_TB_ENTRIES - 1) * MOBA_BLOCK - (MOBA_BLOCK - 1)) == N_BUCKETS - 1
    return pl.pallas_call(
        _bias_tables_kernel,
        out_shape=(jax.ShapeDtypeStruct((MOBA_TB_ENTRIES, N_HEADS, MOBA_BLOCK, MOBA_BLOCK), F32),
                   jax.ShapeDtypeStruct((len(DIL_PATTERNS), N_HEADS, DIL_BLOCK, 2 * DIL_BLOCK), F32)),
        in_specs=[pl.BlockSpec(memory_space=pltpu.SMEM)],
        out_specs=(pl.BlockSpec(memory_space=pltpu.VMEM), pl.BlockSpec(memory_space=pltpu.VMEM)),
        compiler_params=pltpu.CompilerParams(vmem_limit_bytes=VMEM_LIMIT_BYTES),
        name="bias_tables",
    )(rel_bias)


def _stage_in_weights(w_ref, ekr_ref, ws_ref):
    W = GROUP_W
    moves = ((COL_FOX, IN_OFF["fq"], 3 * W), (COL_MOBA, IN_OFF["mq"], 3 * W), (COL_DIL, IN_OFF["dq"], 3 * W),
             (COL_CQ, IN_OFF["cq"], MLA_Q_LORA), (COL_CKV, IN_OFF["ckv"], MLA_KV_LORA),
             (COL_GATE, IN_OFF["gate"], D_MODEL))
    kr_base = IN_OFF["kr"] // LANES * LANES
    assert IN_OFF["ff"] % LANES == 0 and IN_OFF["kr"] + MLA_ROPE <= kr_base + LANES
    lane = lax.broadcasted_iota(jnp.int32, (STAGE_ROWS, LANES), 1)
    for r0 in range(0, D_MODEL, STAGE_ROWS):
        rows = slice(r0, r0 + STAGE_ROWS)
        for dst, src, width in moves:
            ws_ref[rows, dst:dst + width] = w_ref[0, rows, src:src + width].astype(BF16)
        kr_win = w_ref[0, rows, kr_base:kr_base + LANES].astype(BF16)
        ws_ref[rows, COL_KR:COL_KR + LANES] = jnp.dot(kr_win, ekr_ref[...], preferred_element_type=F32).astype(BF16)
        ff_win = w_ref[0, rows, IN_OFF["ff"]:IN_OFF["ff"] + LANES]
        ws_ref[rows, COL_FF:COL_FF + LANES] = jnp.where(lane < N_HEADS, ff_win, 0.0).astype(BF16)
        ws_ref[rows, COL_FF + LANES:COL_GATE] = jnp.zeros((STAGE_ROWS, COL_GATE - COL_FF - LANES), BF16)


def _group_mean_sq(x, g_mat):
    return jnp.dot((x * x).astype(BF16), g_mat, preferred_element_type=F32)


def _group_norm(x, gain, g_mat):
    return x * lax.rsqrt(_group_mean_sq(x, g_mat) + EPS) * gain


def _row_norm(x, gain):
    return x * lax.rsqrt(jnp.mean(x * x, axis=-1, keepdims=True) + EPS) * gain


def _log_sigmoid(x):
    return -(jnp.maximum(-x, 0.0) + jnp.log1p(jnp.exp(-jnp.abs(x))))


def _front_kernel(x_ref, lng_ref, w_ref, ekr_ref,
                  gains_ref, bfor_ref, qng_ref, kvng_ref, nopeg_ref, ropeg_ref,
                  wuq_ref, wukv_ref, cos_ref, sin_ref, g64_ref, g32_ref, tri_ref,
                  fqT_ref, fk_ref, fvT_ref, fck_ref, fcrow_ref,
                  mqT_ref, mk_ref, mvT_ref, mkmean_ref,
                  lqT_ref, lk_ref, lvT_ref,
                  dq1_ref, dk1_ref, dv1_ref, dq4_ref, dk4_ref, dv4_ref, dq16_ref, dk16_ref, dv16_ref,
                  gate_ref,
                  ws_ref, carry_ref, sq_ref, sk_ref, sv_ref):
    t = pl.program_id(1)

    @pl.when((pl.program_id(0) == 0) & (t == 0))
    def _():
        _stage_in_weights(w_ref, ekr_ref, ws_ref)

    g64 = g64_ref[...]
    scale = HEAD_DIM ** -0.5
    qscale = (MLA_NOPE + MLA_ROPE) ** -0.5
    W = GROUP_W
    T = T_PREP
    ones_rows = jnp.ones((V_ROWS - HEAD_DIM, T), BF16)

    @pl.when(t == 0)
    def _():
        carry_ref[...] = jnp.zeros_like(carry_ref)

    for ch in range(FRONT_CHAINS):
        rows = slice(ch * T, (ch + 1) * T)

        def store_vT(dst, v):
            vT = v.T.astype(BF16)
            for h in range(N_HEADS):
                dst[0, ch, h] = jnp.concatenate([vT[HEAD_DIM * h:HEAD_DIM * (h + 1)], ones_rows], axis=0)

        x = x_ref[0, rows, :]
        hn = (x * lax.rsqrt(jnp.mean(x * x, axis=-1, keepdims=True) + EPS) * lng_ref[...]).astype(BF16)
        project = lambda lo, width: jnp.dot(hn, ws_ref[:, lo:lo + width], preferred_element_type=F32)
        pf = project(COL_FOX, 3 * W)
        p_ff = project(COL_FF, LANES)
        pm = project(COL_MOBA, 3 * W)

        fq = _group_norm(pf[:, :W], gains_ref[0:1, :], g64) * (scale * LOG2E)
        fk = _group_norm(pf[:, W:2 * W], gains_ref[1:2, :], g64)
        fqT_ref[0, 0, :, rows] = fq.T.astype(BF16)
        fk_ref[0, 0, rows, :] = fk.astype(BF16)
        store_vT(fvT_ref, pf[:, 2 * W:])
        log_f = _log_sigmoid(p_ff + bfor_ref[...])
        tri = tri_ref[...]
        c = carry_ref[...]
        rest = log_f
        for _ in range(3):
            part = rest.astype(BF16)
            c = c + jnp.dot(tri, part, preferred_element_type=F32)
            rest = rest - part.astype(F32)
        carry_ref[...] = c[T - 1:T, :]
        c2 = c * LOG2E
        for h in range(N_HEADS):
            fck_ref[0, h, rows, :] = jnp.broadcast_to(c2[:, h:h + 1], (T, LANES))
        fcrow_ref[0, :, rows] = c2.T[0:8, :]

        pd = project(COL_DIL, 3 * W)
        mq = _group_norm(pm[:, :W], gains_ref[2:3, :], g64) * (scale * LOG2E)
        mk = _group_norm(pm[:, W:2 * W], gains_ref[3:4, :], g64)
        mqT_ref[0, 0, :, rows] = mq.T.astype(BF16)
        mk_ref[0, 0, rows, :] = mk.astype(BF16)
        store_vT(mvT_ref, pm[:, 2 * W:])
        mkmean_ref[0, pl.ds(FRONT_CHAINS * t + ch, 1), :] = jnp.mean(mk, axis=0, keepdims=True)

        p_mla = project(COL_CQ, MLA_Q_LORA + MLA_KV_LORA + LANES)
        dq = _group_norm(pd[:, :W], gains_ref[4:5, :], g64) * scale
        dk = _group_norm(pd[:, W:2 * W], gains_ref[5:6, :], g64)
        dv = pd[:, 2 * W:]
        dq1_ref[0, rows, :] = dq.astype(BF16)
        dk1_ref[0, rows, :] = dk.astype(BF16)
        dv1_ref[0, rows, :] = dv.astype(BF16)
        for val, dst in ((dq, sq_ref), (dk, sk_ref), (dv, sv_ref)):
            for half in range(GROUP_W // LANES):
                dst[ch, half] = val[:, half * LANES:(half + 1) * LANES]
        for dil, outs in ((4, (dq4_ref, dk4_ref, dv4_ref)), (16, (dq16_ref, dk16_ref, dv16_ref))):
            n = T // dil
            for r in range(dil):
                for src, dst in zip((sq_ref, sk_ref, sv_ref), outs):
                    dst[0, r, ch * n:(ch + 1) * n, :] = jnp.concatenate(
                        [src[ch, half, pl.ds(r, n, stride=dil), :] for half in range(GROUP_W // LANES)],
                        axis=1).astype(BF16)

        half_gate = D_MODEL // 2
        gate_ref[0, rows, :half_gate] = project(COL_GATE, half_gate).astype(BF16)
        cos = cos_ref[rows, :]
        sin = sin_ref[rows, :]
        cqn = _row_norm(p_mla[:, :MLA_Q_LORA], qng_ref[...]).astype(BF16)
        rotate = lambda v, cos=cos, sin=sin: v * cos + pltpu.roll(v, LANES // 2, 1) * sin
        qf = jnp.dot(cqn, wuq_ref[...], preferred_element_type=F32)
        q_nope = _group_norm(qf[:, :W], nopeg_ref[0:1, :], g64) * (qscale * LOG2E)
        ckvn = _row_norm(p_mla[:, MLA_Q_LORA:MLA_Q_LORA + MLA_KV_LORA], kvng_ref[...]).astype(BF16)
        kvf = jnp.dot(ckvn, wukv_ref[...], preferred_element_type=F32)
        k_nope = _group_norm(kvf[:, :W], nopeg_ref[1:2, :], g64)
        krs = p_mla[:, MLA_Q_LORA + MLA_KV_LORA:]
        kr_ms = jnp.sum(krs * krs, axis=-1, keepdims=True) * (1.0 / (2 * MLA_ROPE))
        kr = rotate(krs * lax.rsqrt(kr_ms + EPS) * ropeg_ref[1:2, :])
        for g in range(MLA_GROUPS):
            slab = qf[:, W + g * LANES:W + (g + 1) * LANES]
            qr = rotate(_group_norm(slab, ropeg_ref[0:1, :], g32_ref[...])) * (qscale * LOG2E)
            lq = jnp.concatenate([q_nope[:, g * LANES:(g + 1) * LANES], qr], axis=1)
            lqT_ref[0, g, :, rows] = lq.T.astype(BF16)
            lk_ref[0, g, rows, :] = jnp.concatenate([k_nope[:, g * LANES:(g + 1) * LANES], kr],
                                                    axis=1).astype(BF16)
        store_vT(lvT_ref, kvf[:, W:])
        gate_ref[0, rows, half_gate:] = project(COL_GATE + half_gate, half_gate).astype(BF16)


def _front(layer, x3, w_in, consts, lp):
    B, S, _ = x3.shape
    T = T_PREP
    TS = FRONT_CHAINS * T
    NT = S // T
    assert S % TS == 0 and S // MOBA_BLOCK == NT

    def cspec(shape):
        return pl.BlockSpec(shape, lambda b, t: (0,) * len(shape))

    in_specs = [pl.BlockSpec((1, TS, D_MODEL), lambda b, t: (b, t, 0)), cspec((1, D_MODEL)),
                pl.BlockSpec((1, D_MODEL, IN_W), lambda b, t: (layer, 0, 0), pipeline_mode=pl.Buffered(1)),
                cspec((LANES, LANES)),
                cspec((6, GROUP_W)), cspec((1, LANES)), cspec((1, MLA_Q_LORA)), cspec((1, MLA_KV_LORA)),
                cspec((2, GROUP_W)), cspec((2, LANES)),
                cspec((MLA_Q_LORA, MLA_GROUPS * GROUP_W)), cspec((MLA_KV_LORA, 2 * GROUP_W)),
                pl.BlockSpec((TS, LANES), lambda b, t: (t, 0)), pl.BlockSpec((TS, LANES), lambda b, t: (t, 0)),
                cspec((GROUP_W, GROUP_W)), cspec((LANES, LANES)), cspec((T, T))]

    def qT(groups):
        return (jax.ShapeDtypeStruct((B, groups, GROUP_W, S), BF16),
                pl.BlockSpec((1, groups, GROUP_W, TS), lambda b, t: (b, 0, 0, t)))

    def keys(groups):
        return (jax.ShapeDtypeStruct((B, groups, S, GROUP_W), BF16),
                pl.BlockSpec((1, groups, TS, GROUP_W), lambda b, t: (b, 0, t, 0)))

    def rows(w):
        return jax.ShapeDtypeStruct((B, S, w), BF16), pl.BlockSpec((1, TS, w), lambda b, t: (b, t, 0))

    def vT():
        return (jax.ShapeDtypeStruct((B, NT, N_HEADS, V_ROWS, T), BF16),
                pl.BlockSpec((1, FRONT_CHAINS, N_HEADS, V_ROWS, T), lambda b, t: (b, t, 0, 0, 0)))

    def resid(dil):
        return (jax.ShapeDtypeStruct((B, dil, S // dil, GROUP_W), BF16),
                pl.BlockSpec((1, dil, TS // dil, GROUP_W), lambda b, t: (b, 0, t, 0)))

    outs = [qT(1), keys(1), vT(),
            (jax.ShapeDtypeStruct((B, N_HEADS, S, LANES), F32),
             pl.BlockSpec((1, N_HEADS, TS, LANES), lambda b, t: (b, 0, t, 0))),
            (jax.ShapeDtypeStruct((B, 8, S), F32), pl.BlockSpec((1, 8, TS), lambda b, t: (b, 0, t))),
            qT(1), keys(1), vT(),
            (jax.ShapeDtypeStruct((B, NT, GROUP_W), F32), pl.BlockSpec((1, NT, GROUP_W), lambda b, t: (b, 0, 0))),
            qT(MLA_GROUPS), keys(MLA_GROUPS), vT(),
            rows(GROUP_W), rows(GROUP_W), rows(GROUP_W),
            resid(4), resid(4), resid(4), resid(16), resid(16), resid(16),
            rows(D_MODEL)]
    return pl.pallas_call(
        _front_kernel,
        out_shape=tuple(o[0] for o in outs),
        grid=(B, NT // FRONT_CHAINS),
        in_specs=in_specs,
        out_specs=tuple(o[1] for o in outs),
        scratch_shapes=[pltpu.VMEM((D_MODEL, PROJ_W), BF16), pltpu.VMEM((1, LANES), F32)]
        + [pltpu.VMEM((FRONT_CHAINS, GROUP_W // LANES, T, LANES), F32)] * 3,
        compiler_params=_compiler_params(("arbitrary", "arbitrary")),
        name="front",
    )(x3, lp["ln_g"], w_in, consts["ekr"],
      lp["gains"], lp["b_forget"], lp["q_norm"], lp["kv_norm"], lp["nope_gain"], lp["rope_gain"],
      lp["w_uq"], lp["w_ukv"], consts["cos"], consts["sin"], consts["g64"], consts["g32"], consts["tri"])


def _stage_head_queries(qT_ref, qm_ref, groups):
    r = lax.broadcasted_iota(jnp.int32, (GROUP_W, TQ), 0)
    for h in range(N_HEADS):
        if groups == 1:
            g, mask = 0, (r >= HEAD_DIM * h) & (r < HEAD_DIM * (h + 1))
        else:
            g, hl = divmod(h, N_HEADS // groups)
            mask = (r >= HEAD_DIM * hl) & (r < HEAD_DIM * (hl + 1))
            for base in (2 * HEAD_DIM, 2 * HEAD_DIM + LANES // 2):
                lo = base + MLA_HALF * hl
                mask = mask | ((r >= lo) & (r < lo + MLA_HALF))
        for sub in range(N_SUB):
            q = qT_ref[0, g, :, sub * TQ:(sub + 1) * TQ]
            qm_ref[sub * N_HEADS + h] = jnp.where(mask, q, jnp.zeros_like(q))


def _flash_sweep(pair_idx, groups, qm_ref, k_ref, vT_ref, m_ref, acc_ref, u_ref, o_ref, score_fn, n_far=None):
    kpos = lax.broadcasted_iota(jnp.int32, (TK, TQ), 0)
    qpos = lax.broadcasted_iota(jnp.int32, (TK, TQ), 1)
    causal = kpos <= qpos
    m_ref[...] = jnp.full(m_ref.shape, NEG, F32)
    acc_ref[...] = jnp.zeros(acc_ref.shape, F32)
    both = tuple(range(N_SUB))

    def scores(j0, nblk, diag_last, subs, far=False):
        row0 = pl.multiple_of(j0 * TK, TK)
        kts = [k_ref[0, g, pl.ds(row0, nblk * TK), :] for g in range(groups)]
        pending = []
        for sub in subs:
            for h in range(N_HEADS):
                c = sub * N_HEADS + h
                s = jnp.dot(kts[h // (N_HEADS // groups)], qm_ref[c], preferred_element_type=F32)
                tops, lives, row = [], [], None
                for blk in range(nblk):
                    rows = slice(blk * TK, (blk + 1) * TK)
                    diagonal = diag_last and blk == nblk - 1
                    u, row, live = score_fn(sub, h, s[rows], j0 + blk, diagonal, far)
                    if diagonal:
                        u = jnp.where(causal, u, NEG)
                    u_ref[c, rows, :] = u
                    top = jnp.max(u, axis=0, keepdims=True)
                    if row is not None:
                        top = top + row
                    if live is not None:
                        top = jnp.where(live, top, NEG)
                    tops.append(top)
                    lives.append(live)
                m_old = m_ref[c]
                m_new = functools.reduce(jnp.maximum, tops, m_old)
                shift = m_new if row is None else m_new - row
                shifts = [shift if live is None else jnp.where(live, shift, -NEG) for live in lives]
                m_ref[c] = m_new
                pending.append((c, h, shifts, jnp.exp2(m_old - m_new)))
        return pending

    def values(j0, pending):
        for c, h, shifts, alpha in pending:
            pv = None
            for blk, shift in enumerate(shifts):
                p = jnp.exp2((u_ref[c, blk * TK:(blk + 1) * TK, :] - shift).astype(BF16))
                term = jnp.dot(vT_ref[0, j0 + blk, h], p, preferred_element_type=F32)
                pv = term if pv is None else pv + term
            acc_ref[c] = alpha * acc_ref[c] + pv

    def common(t, carry, far=False):
        values(2 * t, scores(2 * t, 2, False, both, far))
        return carry

    first_near = 0
    if n_far is not None:
        lax.fori_loop(0, n_far, functools.partial(common, far=True), 0)
        first_near = n_far
    lax.fori_loop(first_near, pair_idx, common, 0)
    n_common = N_SUB * pair_idx
    pend_a = scores(n_common, 1, True, (0,))
    pend_b = scores(n_common, 2, True, (1,))
    values(n_common, pend_a)
    values(n_common, pend_b)
    for sub in both:
        outs = []
        for h in range(N_HEADS):
            acc = acc_ref[sub * N_HEADS + h]
            outs.append(acc[:HEAD_DIM] / acc[HEAD_DIM:HEAD_DIM + 1])
        o_ref[0, sub * TQ:(sub + 1) * TQ, :] = jnp.concatenate(outs, axis=0).T.astype(BF16)


def _fox_kernel(qT_ref, k_ref, vT_ref, ck_ref, crow_ref, o_ref, qm_ref, m_ref, acc_ref, u_ref):
    pair_idx = pl.program_id(1)
    _stage_head_queries(qT_ref, qm_ref, 1)

    def score_fn(sub, h, s, j, diagonal, far):
        ck = ck_ref[0, h, pl.ds(pl.multiple_of(j * TK, TK), TK), :]
        cq = crow_ref[0, h:h + 1, sub * TQ:(sub + 1) * TQ]
        return s - jnp.concatenate([ck] * (TQ // LANES), axis=1), cq, None

    _flash_sweep(pair_idx, 1, qm_ref, k_ref, vT_ref, m_ref, acc_ref, u_ref, o_ref, score_fn)


def _mla_kernel(qT_ref, k_ref, vT_ref, o_ref, qm_ref, m_ref, acc_ref, u_ref):
    pair_idx = pl.program_id(1)
    _stage_head_queries(qT_ref, qm_ref, MLA_GROUPS)
    _flash_sweep(pair_idx, MLA_GROUPS, qm_ref, k_ref, vT_ref, m_ref, acc_ref, u_ref, o_ref,
                 lambda sub, h, s, j, diagonal, far: (s, None, None))


def _moba_kernel(qT_ref, k_ref, vT_ref, kmean_ref, tb_ref, o_ref, qm_ref, m_ref, acc_ref, u_ref, sel_ref):
    pair_idx = pl.program_id(1)
    _stage_head_queries(qT_ref, qm_ref, 1)
    kmean = kmean_ref[0].astype(BF16)
    nb = kmean.shape[0]
    blk = lax.broadcasted_iota(jnp.int32, (nb, TQ), 0)
    own = [N_SUB * pair_idx + sub for sub in range(N_SUB)]
    for c in range(N_SUB * N_HEADS):
        i = own[c // N_HEADS]
        gate = jnp.dot(kmean, qm_ref[c], preferred_element_type=F32)
        gate = jnp.where(blk < i, gate, NEG)
        rank = jnp.zeros((nb, TQ), jnp.int32)
        for jp in range(nb):
            row = gate[jp:jp + 1, :]
            ahead = (row > gate) | ((row == gate) & (jp < blk))
            rank = rank + ahead.astype(jnp.int32)
        sel_ref[c] = ((rank < MOBA_TOPK) & (blk < i)).astype(F32)

    def score_fn(sub, h, s, j, diagonal, far):
        live = None if diagonal else sel_ref[sub * N_HEADS + h, pl.ds(j, 1), :] > 0.5
        if far:
            return s, tb_ref[MOBA_TB_ENTRIES - 1, h, 0:1, :], live
        return s + tb_ref[jnp.minimum(own[sub] - j, MOBA_TB_ENTRIES - 1), h], None, live

    n_far = jnp.maximum(pair_idx - (MOBA_TB_ENTRIES - 1) // 2, 0)
    _flash_sweep(pair_idx, 1, qm_ref, k_ref, vT_ref, m_ref, acc_ref, u_ref, o_ref, score_fn, n_far)


def _dense_mixer(kernel_fn, name, qT, k, vT, extra_args=(), extra_specs=(), extra_scratch=()):
    B, groups, _, S = qT.shape
    NT = S // TK
    assert TQ == TK and N_SUB == 2 and S % (N_SUB * TQ) == 0
    n_chain = N_SUB * N_HEADS
    in_specs = [pl.BlockSpec((1, groups, GROUP_W, N_SUB * TQ), lambda b, i: (b, 0, 0, i)),
                pl.BlockSpec((1, groups, S, GROUP_W), lambda b, i: (b, 0, 0, 0)),
                pl.BlockSpec((1, NT, N_HEADS, V_ROWS, TK), lambda b, i: (b, 0, 0, 0, 0))] + list(extra_specs)
    scratch = [pltpu.VMEM((n_chain, GROUP_W, TQ), BF16),
               pltpu.VMEM((n_chain, 1, TQ), F32),
               pltpu.VMEM((n_chain, V_ROWS, TQ), F32),
               pltpu.VMEM((n_chain, 2 * TK, TQ), F32)] + list(extra_scratch)
    return pl.pallas_call(
        kernel_fn,
        out_shape=jax.ShapeDtypeStruct((B, S, GROUP_W), BF16),
        grid=(B, S // (N_SUB * TQ)),
        in_specs=in_specs,
        out_specs=pl.BlockSpec((1, N_SUB * TQ, GROUP_W), lambda b, i: (b, i, 0)),
        scratch_shapes=scratch,
        compiler_params=_compiler_params(("arbitrary", "arbitrary")),
        name=name,
    )(qT, k, vT, *extra_args)


def _dil_kernel(q_ref, kp_ref, kc_ref, vp_ref, vc_ref, tb_ref, o_ref, lse_ref):
    n = pl.program_id(2)
    qb = q_ref.shape[2] // DIL_BLOCK
    kj = lax.broadcasted_iota(jnp.int32, (DIL_BLOCK, 2 * DIL_BLOCK), 1)
    qlane = lax.broadcasted_iota(jnp.int32, (DIL_BLOCK, GROUP_W), 1)
    in_head = [(qlane >= HEAD_DIM * h) & (qlane < HEAD_DIM * (h + 1)) for h in range(N_HEADS)]
    for seq, c in [(seq, c) for seq in range(q_ref.shape[1]) for c in range(qb)]:
        rows = slice(c * DIL_BLOCK, (c + 1) * DIL_BLOCK)
        q = q_ref[0, seq, rows, :]
        if c == 0:
            kb = jnp.concatenate([kp_ref[0, seq], kc_ref[0, seq, :DIL_BLOCK, :]], axis=0)
            vb = jnp.concatenate([vp_ref[0, seq], vc_ref[0, seq, :DIL_BLOCK, :]], axis=0)
        else:
            kb = kc_ref[0, seq, (c - 1) * DIL_BLOCK:(c + 1) * DIL_BLOCK, :]
            vb = vc_ref[0, seq, (c - 1) * DIL_BLOCK:(c + 1) * DIL_BLOCK, :]
        q_heads = jnp.concatenate([jnp.where(msk, q, jnp.zeros_like(q)) for msk in in_head], axis=0)
        s = lax.dot_general(q_heads, kb, (((1,), (1,)), ((), ())), preferred_element_type=F32)
        s = s.reshape(N_HEADS, DIL_BLOCK, 2 * DIL_BLOCK)
        s = s + tb_ref[0]
        if c == 0:
            s = jnp.where(((kj >= DIL_BLOCK) | (n > 0))[None], s, NEG)
        m = jnp.max(s, axis=-1, keepdims=True)
        e = jnp.exp(s - m)
        l = jnp.sum(e, axis=-1, keepdims=True)
        lse = m + jnp.log(l)
        pr = (e * (1.0 / l)).astype(BF16)
        o_all = jnp.dot(pr.reshape(N_HEADS * DIL_BLOCK, 2 * DIL_BLOCK), vb, preferred_element_type=F32)
        o_all = o_all.reshape(N_HEADS, DIL_BLOCK, GROUP_W)
        o_acc = jnp.zeros((DIL_BLOCK, GROUP_W), F32)
        lse_map = jnp.zeros((DIL_BLOCK, GROUP_W), F32)
        for h in range(N_HEADS):
            o_acc = jnp.where(in_head[h], o_all[h], o_acc)
            lse_map = jnp.where(in_head[h], lse[h], lse_map)
        o_ref[0, seq, rows, :] = o_acc.astype(BF16)
        lse_ref[0, seq, rows, :] = lse_map


def _dilated_pattern(p, q, k, v, dil_tb):
    B, dil, L, _ = q.shape
    assert dil == DIL_PATTERNS[p][1] and L % DIL_BLOCK == 0
    nb = L // DIL_BLOCK
    qb = math.gcd(nb, DIL_BLOCKS_PER_STEP)
    seqs = math.gcd(dil, DIL_BLOCKS_PER_STEP // qb)
    cur = pl.BlockSpec((1, seqs, qb * DIL_BLOCK, GROUP_W), lambda b, r, n: (b, r, n, 0))
    prev = pl.BlockSpec((1, seqs, DIL_BLOCK, GROUP_W), lambda b, r, n: (b, r, jnp.maximum(n * qb - 1, 0), 0))
    return pl.pallas_call(
        _dil_kernel,
        out_shape=(jax.ShapeDtypeStruct((B, dil, L, GROUP_W), BF16),
                   jax.ShapeDtypeStruct((B, dil, L, GROUP_W), F32)),
        grid=(B, dil // seqs, nb // qb),
        in_specs=[cur, prev, cur, prev, cur,
                  pl.BlockSpec((1, N_HEADS, DIL_BLOCK, 2 * DIL_BLOCK), lambda b, r, n, p=p: (p, 0, 0, 0))],
        out_specs=(cur, cur),
        compiler_params=_compiler_params(("arbitrary", "arbitrary", "arbitrary")),
        name=f"dilated_{dil}",
    )(q, k, k, v, v, dil_tb)


def _mix_dilations(o_refs, lse_refs, scratch):
    halves = GROUP_W // LANES
    rows = o_refs[0].shape[1] * o_refs[0].shape[2]
    nat = []
    for ref, dst in zip(list(o_refs) + list(lse_refs), scratch):
        dil = ref.shape[1]
        if dil == 1:
            nat.append(ref[0, 0].astype(F32))
            continue
        for r in range(dil):
            for half in range(halves):
                dst[half, pl.ds(r, rows // dil, stride=dil), :] = (
                    ref[0, r, :, half * LANES:(half + 1) * LANES].astype(F32))
        nat.append(jnp.concatenate([dst[half] for half in range(halves)], axis=1))
    os, lses = nat[:len(o_refs)], nat[len(o_refs):]
    m = functools.reduce(jnp.maximum, lses)
    ws = [jnp.exp(l - m) for l in lses]
    tot = functools.reduce(jnp.add, ws)
    return functools.reduce(jnp.add, [(w / tot) * o for w, o in zip(ws, os)])


def _sigmoid(x):
    return 1.0 / (1.0 + jnp.exp(-x))


def _merge_kernel(x_ref, fo_ref, mo_ref, lo_ref, o1_ref, l1_ref, o4_ref, l4_ref, o16_ref, l16_ref, gate_ref, p_ref,
                  wout_ref, png_ref, wpg_ref, wpp_ref, o_ref, *dil_scratch):
    o_dil = _mix_dilations((o1_ref, o4_ref, o16_ref), (l1_ref, l4_ref, l16_ref), dil_scratch)
    mix = jnp.concatenate([fo_ref[...].astype(F32), mo_ref[...].astype(F32), o_dil, lo_ref[...].astype(F32)],
                          axis=1)
    g = gate_ref[...].astype(F32)
    y = jnp.dot((mix * (g * _sigmoid(g))).astype(BF16), wout_ref[...], preferred_element_type=F32)
    x1 = x_ref[...] + y
    hn = _row_norm(x1, png_ref[...]).astype(BF16)
    pg = _sigmoid(jnp.dot(hn, wpg_ref[...], preferred_element_type=F32))
    pp = jnp.dot(p_ref[0].astype(BF16), wpp_ref[...], preferred_element_type=F32)
    o_ref[...] = x1 + pg * pp


def _merge(layer, x2, fo, mo, lo, dil_outs, proj, p3, lp):
    rows = x2.shape[0]
    TM = TM_MERGE
    tiles_per_seq = dil_outs[0].shape[1] * dil_outs[0].shape[2] // TM

    def rspec(w, c=0):
        return pl.BlockSpec((TM, w), lambda i, c=c: (i, c))

    def cspec(shape):
        return pl.BlockSpec(shape, lambda i: (0, 0))

    def dspec(a):
        dil = a.shape[1]
        return pl.BlockSpec((1, dil, TM // dil, GROUP_W),
                            lambda i: (i // tiles_per_seq, 0, lax.rem(i, tiles_per_seq), 0))

    return pl.pallas_call(
        _merge_kernel,
        out_shape=jax.ShapeDtypeStruct((rows, D_MODEL), F32),
        grid=(rows // TM,),
        scratch_shapes=[pltpu.VMEM((GROUP_W // LANES, TM, LANES), F32)] * len(dil_outs),
        in_specs=[rspec(D_MODEL)] + [rspec(GROUP_W)] * 3 + [dspec(a) for a in dil_outs] + [
                  rspec(D_MODEL),
                  pl.BlockSpec((1, TM, PLE_DIM), lambda i: (layer, i, 0)),
                  cspec((D_MODEL, D_MODEL)), cspec((1, D_MODEL)), cspec((D_MODEL, D_MODEL)),
                  cspec((PLE_DIM, D_MODEL))],
        out_specs=rspec(D_MODEL),
        compiler_params=_compiler_params(("arbitrary",)),
        name="merge",
    )(x2, fo, mo, lo, *dil_outs, proj, p3, lp["w_out"], lp["ple_norm_g"], lp["w_ple_gate"], lp["w_ple_proj"])


def _layer_params(i, ln_g, b_forget, qk_gain, mla_q_norm, mla_kv_norm, mla_nope_gain,
                  mla_rope_gain, w_uq, w_ukv, w_out, ple_norm_g, w_ple_gate, w_ple_proj):
    def rope_slab(x1_parts, x2_parts, like):
        pad = jnp.zeros(like.shape[:-1] + (LANES // 2 - 2 * MLA_HALF,), like.dtype)
        return jnp.concatenate(list(x1_parts) + [pad] + list(x2_parts) + [pad], axis=-1)

    per_head = MLA_NOPE + MLA_ROPE
    uq = w_uq[i]
    uq_nope = [uq[:, h * per_head:h * per_head + MLA_NOPE] for h in range(N_HEADS)]
    uq_x1 = [uq[:, h * per_head + MLA_NOPE:h * per_head + MLA_NOPE + MLA_HALF] for h in range(N_HEADS)]
    uq_x2 = [uq[:, h * per_head + MLA_NOPE + MLA_HALF:(h + 1) * per_head] for h in range(N_HEADS)]
    hpg = N_HEADS // MLA_GROUPS
    uq_cols = uq_nope + [rope_slab(uq_x1[g * hpg:(g + 1) * hpg], uq_x2[g * hpg:(g + 1) * hpg], uq)
                         for g in range(MLA_GROUPS)]
    ukv = w_ukv[i]
    ukv_cols = ([ukv[:, h * 2 * HEAD_DIM:h * 2 * HEAD_DIM + HEAD_DIM] for h in range(N_HEADS)]
                + [ukv[:, h * 2 * HEAD_DIM + HEAD_DIM:(h + 1) * 2 * HEAD_DIM] for h in range(N_HEADS)])
    rg = mla_rope_gain[i]
    rope_gain = rope_slab([rg[:, :MLA_HALF]] * hpg, [rg[:, MLA_HALF:]] * hpg, rg)
    return {
        "ln_g": ln_g[i][None, :],
        "gains": jnp.tile(qk_gain[i], (1, N_HEADS)),
        "b_forget": jnp.pad(b_forget[i], (0, LANES - N_HEADS))[None, :],
        "q_norm": mla_q_norm[i][None, :],
        "kv_norm": mla_kv_norm[i][None, :],
        "nope_gain": jnp.tile(mla_nope_gain[i], (1, N_HEADS)),
        "rope_gain": rope_gain,
        "w_uq": jnp.concatenate(uq_cols, axis=1).astype(BF16),
        "w_ukv": jnp.concatenate(ukv_cols, axis=1).astype(BF16),
        "w_out": w_out[i].astype(BF16),
        "ple_norm_g": ple_norm_g[i][None, :],
        "w_ple_gate": w_ple_gate[i].astype(BF16),
        "w_ple_proj": w_ple_proj[i].astype(BF16),
    }


def _constants(S):
    inv = 1.0 / (ROPE_THETA ** (jnp.arange(MLA_HALF, dtype=F32) * 2.0 / MLA_ROPE))
    ang = jnp.arange(S).astype(F32)[:, None] * inv[None, :]
    cos = jnp.tile(jnp.cos(ang), (1, 2 * N_HEADS))
    sin = jnp.tile(jnp.sin(ang), (1, 2 * N_HEADS))
    sign = np.concatenate([-np.ones(LANES // 2, np.float32), np.ones(LANES // 2, np.float32)])
    lane = np.arange(GROUP_W)
    g64 = (lane[:, None] // HEAD_DIM == lane[None, :] // HEAD_DIM).astype(np.float32) / HEAD_DIM
    within = np.arange(LANES) % (LANES // 2)
    rope_head = np.where(within < 2 * MLA_HALF, within // MLA_HALF, -1)
    g32 = ((rope_head[:, None] == rope_head[None, :]) & (rope_head[:, None] >= 0)).astype(np.float32) / MLA_ROPE
    tri = np.tril(np.ones((T_PREP, T_PREP), np.float32))
    ekr = np.zeros((LANES, LANES), np.float32)
    kr_lane0 = IN_OFF["kr"] % LANES
    for c in range(MLA_ROPE):
        half, idx = divmod(c, MLA_HALF)
        for copy in range(N_HEADS // MLA_GROUPS):
            ekr[kr_lane0 + c, half * (LANES // 2) + copy * MLA_HALF + idx] = 1.0
    return {"cos": cos, "sin": sin * sign[None, :], "g64": jnp.asarray(g64, BF16),
            "g32": jnp.asarray(g32, BF16), "tri": jnp.asarray(tri, BF16), "ekr": jnp.asarray(ekr, BF16)}


def kernel(x, p, ln_g, w_in, b_forget, qk_gain, mla_q_norm, mla_kv_norm, mla_nope_gain, mla_rope_gain,
           w_uq, w_ukv, w_out, rel_bias, ple_norm_g, w_ple_gate, w_ple_proj):
    B, S, _ = x.shape
    depth = p.shape[0]
    consts = _constants(S)
    moba_tb, dil_tb = _bias_tables(rel_bias)
    x2 = x.reshape(B * S, D_MODEL)
    for i in range(depth):
        lp = _layer_params(i, ln_g, b_forget, qk_gain, mla_q_norm, mla_kv_norm, mla_nope_gain,
                           mla_rope_gain, w_uq, w_ukv, w_out, ple_norm_g, w_ple_gate, w_ple_proj)
        (fqT, fk, fvT, fck, fcrow, mqT, mk, mvT, mkmean, lqT, lk, lvT, *dil_qkv, gate) = _front(
            i, x2.reshape(B, S, D_MODEL), w_in, consts, lp)
        S_ = S
        fo = _dense_mixer(
            _fox_kernel, "fox", fqT, fk, fvT, (fck, fcrow),
            (pl.BlockSpec((1, N_HEADS, S_, LANES), lambda b, i: (b, 0, 0, 0)),
             pl.BlockSpec((1, 8, N_SUB * TQ), lambda b, i: (b, 0, i))))
        mo = _dense_mixer(
            _moba_kernel, "moba", mqT, mk, mvT, (mkmean, moba_tb),
            (pl.BlockSpec((1, S_ // MOBA_BLOCK, GROUP_W), lambda b, i: (b, 0, 0)),
             pl.BlockSpec((MOBA_TB_ENTRIES, N_HEADS, MOBA_BLOCK, MOBA_BLOCK), lambda b, i: (0, 0, 0, 0))),
            (pltpu.VMEM((N_SUB * N_HEADS, S_ // MOBA_BLOCK, TQ), F32),))
        lo = _dense_mixer(_mla_kernel, "mla", lqT, lk, lvT)
        dil_outs = []
        for pat, (_, dil) in enumerate(DIL_PATTERNS):
            q, k, v = (a.reshape(B, dil, S // dil, GROUP_W) for a in dil_qkv[3 * pat:3 * pat + 3])
            dil_outs += list(_dilated_pattern(pat, q, k, v, dil_tb))
        rs = lambda a: a.reshape(B * S, GROUP_W)
        x2 = _merge(i, x2, rs(fo), rs(mo), rs(lo), dil_outs, gate.reshape(B * S, D_MODEL),
                    p.reshape(depth, B * S, PLE_DIM), lp)
    return x2.reshape(B, S, D_MODEL)
```

```python
import functools
import math

import numpy as np
import jax
import jax.numpy as jnp
from jax import lax
from jax.experimental import pallas as pl
from jax.experimental.pallas import tpu as pltpu

F32 = jnp.float32
BF16 = jnp.bfloat16

D_MODEL = 1024
N_HEADS = 4
HEAD_DIM = 64
GROUP_W = N_HEADS * HEAD_DIM
MLA_NOPE = 64
MLA_ROPE = 32
MLA_HALF = MLA_ROPE // 2
MLA_QK_W = N_HEADS * (MLA_NOPE + MLA_ROPE)
MLA_Q_LORA = 256
MLA_KV_LORA = 128
PLE_DIM = 256
MOBA_BLOCK = 256
MOBA_TOPK = 3
DIL_PATTERNS = ((128, 1), (512, 4), (2048, 16))
DIL_BLOCK = 128
DIL_BLOCKS_PER_STEP = 16
N_BUCKETS = 32
MAX_DISTANCE = 2048
ROPE_THETA = 10000.0
EPS = 1e-6
NEG = -1e30
LOG2E = math.log2(math.e)

LANES = 128
VMEM_LIMIT_BYTES = 56 * 1024 * 1024

COL_FOX, COL_MOBA, COL_DIL = 0, 768, 1536
COL_CQ, COL_CKV, COL_KR, COL_FF, COL_GATE = 2304, 2560, 2688, 2816, 3072
PROJ_W = 4096
PROJ_TN = 512
STAGE_ROWS = 128

IN_OFF = {}
IN_W = 0
for _name, _size in (("fq", GROUP_W), ("fk", GROUP_W), ("fv", GROUP_W), ("ff", N_HEADS),
                     ("mq", GROUP_W), ("mk", GROUP_W), ("mv", GROUP_W),
                     ("dq", GROUP_W), ("dk", GROUP_W), ("dv", GROUP_W),
                     ("cq", MLA_Q_LORA), ("ckv", MLA_KV_LORA), ("kr", MLA_ROPE), ("gate", D_MODEL)):
    IN_OFF[_name] = IN_W
    IN_W += _size

TM_PROJ = 512
T_PREP = 256
FRONT_CHAINS = 2
TQ = 256
TK = 256
N_SUB = 2
MOBA_TB_ENTRIES = 8
TM_MERGE = 1024
V_ROWS = HEAD_DIM + 16
MLA_GROUPS = 2


def _bucket_np(d):
    d = np.maximum(np.asarray(d, np.int64), 0)
    max_exact = N_BUCKETS // 2
    d_f = np.maximum(d, 1).astype(np.float64)
    val = np.log(d_f / max_exact) / math.log(MAX_DISTANCE / max_exact) * (N_BUCKETS - max_exact)
    frac = np.abs(val - np.round(val))
    on_edge = (frac < 1e-5) & (d > max_exact) & (val < N_BUCKETS - max_exact - 0.5)
    assert not on_edge.any(), "distance too close to a bucket edge for a static table"
    large = np.minimum(max_exact + np.floor(val + 1e-9).astype(np.int64), N_BUCKETS - 1)
    return np.where(d < max_exact, d, large)


def _bucket_steps(lo, hi):
    ds = np.arange(lo, hi + 1)
    bs = _bucket_np(ds)
    steps = [(int(ds[i]), int(bs[i])) for i in range(1, len(ds)) if bs[i] != bs[i - 1]]
    return int(bs[0]), steps


def _compiler_params(sem):
    return pltpu.CompilerParams(dimension_semantics=sem, vmem_limit_bytes=VMEM_LIMIT_BYTES)


def _bias_from_steps(d, rb_ref, col, lo, hi):
    b0, steps = _bucket_steps(lo, hi)
    val = jnp.full(d.shape, rb_ref[b0, col], F32)
    for t, b in steps:
        val = jnp.where(d >= t, rb_ref[b, col], val)
    return val


def _bias_tables_kernel(rb_ref, moba_ref, dil_ref):
    kl = lax.broadcasted_iota(jnp.int32, (MOBA_BLOCK, MOBA_BLOCK), 0)
    ql = lax.broadcasted_iota(jnp.int32, (MOBA_BLOCK, MOBA_BLOCK), 1)
    for e in range(MOBA_TB_ENTRIES - 1):
        d = jnp.maximum(e * MOBA_BLOCK + ql - kl, 0)
        lo = max(e * MOBA_BLOCK - (MOBA_BLOCK - 1), 0)
        hi = e * MOBA_BLOCK + (MOBA_BLOCK - 1)
        for h in range(N_HEADS):
            moba_ref[e, h] = _bias_from_steps(d, rb_ref, h, lo, hi) * LOG2E
    for h in range(N_HEADS):
        moba_ref[MOBA_TB_ENTRIES - 1, h] = jnp.full((MOBA_BLOCK, MOBA_BLOCK), rb_ref[N_BUCKETS - 1, h], F32) * LOG2E
    qi = lax.broadcasted_iota(jnp.int32, (DIL_BLOCK, 2 * DIL_BLOCK), 0)
    kj = lax.broadcasted_iota(jnp.int32, (DIL_BLOCK, 2 * DIL_BLOCK), 1)
    rel = qi + DIL_BLOCK - kj
    for p, (_, dil) in enumerate(DIL_PATTERNS):
        d = jnp.maximum(rel * dil, 0)
        for h in range(N_HEADS):
            bias = _bias_from_steps(d, rb_ref, N_HEADS + h, 0, (2 * DIL_BLOCK - 1) * dil)
            dil_ref[p, h] = jnp.where((rel >= 0) & (rel <= DIL_BLOCK), bias, NEG)


def _bias_tables(rel_bias):
    assert _bucket_np((MOBA_TB_ENTRIES - 1) * MOBA_BLOCK - (MOBA_BLOCK - 1)) == N_BUCKETS - 1
    return pl.pallas_call(
        _bias_tables_kernel,
        out_shape=(jax.ShapeDtypeStruct((MOBA_TB_ENTRIES, N_HEADS, MOBA_BLOCK, MOBA_BLOCK), F32),
                   jax.ShapeDtypeStruct((len(DIL_PATTERNS), N_HEADS, DIL_BLOCK, 2 * DIL_BLOCK), F32)),
        in_specs=[pl.BlockSpec(memory_space=pltpu.SMEM)],
        out_specs=(pl.BlockSpec(memory_space=pltpu.VMEM), pl.BlockSpec(memory_space=pltpu.VMEM)),
        compiler_params=pltpu.CompilerParams(vmem_limit_bytes=VMEM_LIMIT_BYTES),
        name="bias_tables",
    )(rel_bias)


def _stage_in_weights(w_ref, ekr_ref, ws_ref):
    W = GROUP_W
    moves = ((COL_FOX, IN_OFF["fq"], 3 * W), (COL_MOBA, IN_OFF["mq"], 3 * W), (COL_DIL, IN_OFF["dq"], 3 * W),
             (COL_CQ, IN_OFF["cq"], MLA_Q_LORA), (COL_CKV, IN_OFF["ckv"], MLA_KV_LORA),
             (COL_GATE, IN_OFF["gate"], D_MODEL))
    kr_base = IN_OFF["kr"] // LANES * LANES
    assert IN_OFF["ff"] % LANES == 0 and IN_OFF["kr"] + MLA_ROPE <= kr_base + LANES
    lane = lax.broadcasted_iota(jnp.int32, (STAGE_ROWS, LANES), 1)
    for r0 in range(0, D_MODEL, STAGE_ROWS):
        rows = slice(r0, r0 + STAGE_ROWS)
        for dst, src, width in moves:
            ws_ref[rows, dst:dst + width] = w_ref[0, rows, src:src + width].astype(BF16)
        kr_win = w_ref[0, rows, kr_base:kr_base + LANES].astype(BF16)
        ws_ref[rows, COL_KR:COL_KR + LANES] = jnp.dot(kr_win, ekr_ref[...], preferred_element_type=F32).astype(BF16)
        ff_win = w_ref[0, rows, IN_OFF["ff"]:IN_OFF["ff"] + LANES]
        ws_ref[rows, COL_FF:COL_FF + LANES] = jnp.where(lane < N_HEADS, ff_win, 0.0).astype(BF16)
        ws_ref[rows, COL_FF + LANES:COL_GATE] = jnp.zeros((STAGE_ROWS, COL_GATE - COL_FF - LANES), BF16)


def _group_mean_sq(x, g_mat):
    return jnp.dot((x * x).astype(BF16), g_mat, preferred_element_type=F32)


def _group_norm(x, gain, g_mat):
    return x * lax.rsqrt(_group_mean_sq(x, g_mat) + EPS) * gain


def _row_norm(x, gain):
    return x * lax.rsqrt(jnp.mean(x * x, axis=-1, keepdims=True) + EPS) * gain


def _log_sigmoid(x):
    return -(jnp.maximum(-x, 0.0) + jnp.log1p(jnp.exp(-jnp.abs(x))))


def _front_kernel(x_ref, lng_ref, w_ref, ekr_ref,
                  gains_ref, bfor_ref, qng_ref, kvng_ref, nopeg_ref, ropeg_ref,
                  wuq_ref, wukv_ref, cos_ref, sin_ref, g64_ref, g32_ref, tri_ref,
                  fqT_ref, fk_ref, fvT_ref, fck_ref, fcrow_ref,
                  mqT_ref, mk_ref, mvT_ref, mkmean_ref,
                  lqT_ref, lk_ref, lvT_ref,
                  dq1_ref, dk1_ref, dv1_ref, dq4_ref, dk4_ref, dv4_ref, dq16_ref, dk16_ref, dv16_ref,
                  gate_ref,
                  ws_ref, carry_ref, sq_ref, sk_ref, sv_ref):
    t = pl.program_id(1)

    @pl.when((pl.program_id(0) == 0) & (t == 0))
    def _():
        _stage_in_weights(w_ref, ekr_ref, ws_ref)

    g64 = g64_ref[...]
    scale = HEAD_DIM ** -0.5
    qscale = (MLA_NOPE + MLA_ROPE) ** -0.5
    W = GROUP_W
    T = T_PREP
    ones_rows = jnp.ones((V_ROWS - HEAD_DIM, T), BF16)

    @pl.when(t == 0)
    def _():
        carry_ref[...] = jnp.zeros_like(carry_ref)

    for ch in range(FRONT_CHAINS):
        rows = slice(ch * T, (ch + 1) * T)

        def store_vT(dst, v):
            vT = v.T.astype(BF16)
            for h in range(N_HEADS):
                dst[0, ch, h] = jnp.concatenate([vT[HEAD_DIM * h:HEAD_DIM * (h + 1)], ones_rows], axis=0)

        x = x_ref[0, rows, :]
        hn = (x * lax.rsqrt(jnp.mean(x * x, axis=-1, keepdims=True) + EPS) * lng_ref[...]).astype(BF16)
        project = lambda lo, width: jnp.dot(hn, ws_ref[:, lo:lo + width], preferred_element_type=F32)
        pf = project(COL_FOX, 3 * W)
        p_ff = project(COL_FF, LANES)
        pm = project(COL_MOBA, 3 * W)

        fq = _group_norm(pf[:, :W], gains_ref[0:1, :], g64) * (scale * LOG2E)
        fk = _group_norm(pf[:, W:2 * W], gains_ref[1:2, :], g64)
        fqT_ref[0, 0, :, rows] = fq.T.astype(BF16)
        fk_ref[0, 0, rows, :] = fk.astype(BF16)
        store_vT(fvT_ref, pf[:, 2 * W:])
        log_f = _log_sigmoid(p_ff + bfor_ref[...])
        tri = tri_ref[...]
        c = carry_ref[...]
        rest = log_f
        for _ in range(3):
            part = rest.astype(BF16)
            c = c + jnp.dot(tri, part, preferred_element_type=F32)
            rest = rest - part.astype(F32)
        carry_ref[...] = c[T - 1:T, :]
        c2 = c * LOG2E
        for h in range(N_HEADS):
            fck_ref[0, h, rows, :] = jnp.broadcast_to(c2[:, h:h + 1], (T, LANES))
        fcrow_ref[0, :, rows] = c2.T[0:8, :]

        pd = project(COL_DIL, 3 * W)
        mq = _group_norm(pm[:, :W], gains_ref[2:3, :], g64) * (scale * LOG2E)
        mk = _group_norm(pm[:, W:2 * W], gains_ref[3:4, :], g64)
        mqT_ref[0, 0, :, rows] = mq.T.astype(BF16)
        mk_ref[0, 0, rows, :] = mk.astype(BF16)
        store_vT(mvT_ref, pm[:, 2 * W:])
        mkmean_ref[0, pl.ds(FRONT_CHAINS * t + ch, 1), :] = jnp.mean(mk, axis=0, keepdims=True)

        p_mla = project(COL_CQ, MLA_Q_LORA + MLA_KV_LORA + LANES)
        dq = _group_norm(pd[:, :W], gains_ref[4:5, :], g64) * scale
        dk = _group_norm(pd[:, W:2 * W], gains_ref[5:6, :], g64)
        dv = pd[:, 2 * W:]
        dq1_ref[0, rows, :] = dq.astype(BF16)
        dk1_ref[0, rows, :] = dk.astype(BF16)
        dv1_ref[0, rows, :] = dv.astype(BF16)
        for val, dst in ((dq, sq_ref), (dk, sk_ref), (dv, sv_ref)):
            for half in range(GROUP_W // LANES):
                dst[ch, half] = val[:, half * LANES:(half + 1) * LANES]
        for dil, outs in ((4, (dq4_ref, dk4_ref, dv4_ref)), (16, (dq16_ref, dk16_ref, dv16_ref))):
            n = T // dil
            for r in range(dil):
                for src, dst in zip((sq_ref, sk_ref, sv_ref), outs):
                    dst[0, r, ch * n:(ch + 1) * n, :] = jnp.concatenate(
                        [src[ch, half, pl.ds(r, n, stride=dil), :] for half in range(GROUP_W // LANES)],
                        axis=1).astype(BF16)

        half_gate = D_MODEL // 2
        gate_ref[0, rows, :half_gate] = project(COL_GATE, half_gate).astype(BF16)
        cos = cos_ref[rows, :]
        sin = sin_ref[rows, :]
        cqn = _row_norm(p_mla[:, :MLA_Q_LORA], qng_ref[...]).astype(BF16)
        rotate = lambda v, cos=cos, sin=sin: v * cos + pltpu.roll(v, LANES // 2, 1) * sin
        qf = jnp.dot(cqn, wuq_ref[...], preferred_element_type=F32)
        q_nope = _group_norm(qf[:, :W], nopeg_ref[0:1, :], g64) * (qscale * LOG2E)
        ckvn = _row_norm(p_mla[:, MLA_Q_LORA:MLA_Q_LORA + MLA_KV_LORA], kvng_ref[...]).astype(BF16)
        kvf = jnp.dot(ckvn, wukv_ref[...], preferred_element_type=F32)
        k_nope = _group_norm(kvf[:, :W], nopeg_ref[1:2, :], g64)
        krs = p_mla[:, MLA_Q_LORA + MLA_KV_LORA:]
        kr_ms = jnp.sum(krs * krs, axis=-1, keepdims=True) * (1.0 / (2 * MLA_ROPE))
        kr = rotate(krs * lax.rsqrt(kr_ms + EPS) * ropeg_ref[1:2, :])
        for g in range(MLA_GROUPS):
            slab = qf[:, W + g * LANES:W + (g + 1) * LANES]
            qr = rotate(_group_norm(slab, ropeg_ref[0:1, :], g32_ref[...])) * (qscale * LOG2E)
            lq = jnp.concatenate([q_nope[:, g * LANES:(g + 1) * LANES], qr], axis=1)
            lqT_ref[0, g, :, rows] = lq.T.astype(BF16)
            lk_ref[0, g, rows, :] = jnp.concatenate([k_nope[:, g * LANES:(g + 1) * LANES], kr],
                                                    axis=1).astype(BF16)
        store_vT(lvT_ref, kvf[:, W:])
        gate_ref[0, rows, half_gate:] = project(COL_GATE + half_gate, half_gate).astype(BF16)


def _front(layer, x3, w_in, consts, lp):
    B, S, _ = x3.shape
    T = T_PREP
    TS = FRONT_CHAINS * T
    NT = S // T
    assert S % TS == 0 and S // MOBA_BLOCK == NT

    def cspec(shape):
        return pl.BlockSpec(shape, lambda b, t: (0,) * len(shape))

    in_specs = [pl.BlockSpec((1, TS, D_MODEL), lambda b, t: (b, t, 0)), cspec((1, D_MODEL)),
                pl.BlockSpec((1, D_MODEL, IN_W), lambda b, t: (layer, 0, 0), pipeline_mode=pl.Buffered(1)),
                cspec((LANES, LANES)),
                cspec((6, GROUP_W)), cspec((1, LANES)), cspec((1, MLA_Q_LORA)), cspec((1, MLA_KV_LORA)),
                cspec((2, GROUP_W)), cspec((2, LANES)),
                cspec((MLA_Q_LORA, MLA_GROUPS * GROUP_W)), cspec((MLA_KV_LORA, 2 * GROUP_W)),
                pl.BlockSpec((TS, LANES), lambda b, t: (t, 0)), pl.BlockSpec((TS, LANES), lambda b, t: (t, 0)),
                cspec((GROUP_W, GROUP_W)), cspec((LANES, LANES)), cspec((T, T))]

    def qT(groups):
        return (jax.ShapeDtypeStruct((B, groups, GROUP_W, S), BF16),
                pl.BlockSpec((1, groups, GROUP_W, TS), lambda b, t: (b, 0, 0, t)))

    def keys(groups):
        return (jax.ShapeDtypeStruct((B, groups, S, GROUP_W), BF16),
                pl.BlockSpec((1, groups, TS, GROUP_W), lambda b, t: (b, 0, t, 0)))

    def rows(w):
        return jax.ShapeDtypeStruct((B, S, w), BF16), pl.BlockSpec((1, TS, w), lambda b, t: (b, t, 0))

    def vT():
        return (jax.ShapeDtypeStruct((B, NT, N_HEADS, V_ROWS, T), BF16),
                pl.BlockSpec((1, FRONT_CHAINS, N_HEADS, V_ROWS, T), lambda b, t: (b, t, 0, 0, 0)))

    def resid(dil):
        return (jax.ShapeDtypeStruct((B, dil, S // dil, GROUP_W), BF16),
                pl.BlockSpec((1, dil, TS // dil, GROUP_W), lambda b, t: (b, 0, t, 0)))

    outs = [qT(1), keys(1), vT(),
            (jax.ShapeDtypeStruct((B, N_HEADS, S, LANES), F32),
             pl.BlockSpec((1, N_HEADS, TS, LANES), lambda b, t: (b, 0, t, 0))),
            (jax.ShapeDtypeStruct((B, 8, S), F32), pl.BlockSpec((1, 8, TS), lambda b, t: (b, 0, t))),
            qT(1), keys(1), vT(),
            (jax.ShapeDtypeStruct((B, NT, GROUP_W), F32), pl.BlockSpec((1, NT, GROUP_W), lambda b, t: (b, 0, 0))),
            qT(MLA_GROUPS), keys(MLA_GROUPS), vT(),
            rows(GROUP_W), rows(GROUP_W), rows(GROUP_W),
            resid(4), resid(4), resid(4), resid(16), resid(16), resid(16),
            rows(D_MODEL)]
    return pl.pallas_call(
        _front_kernel,
        out_shape=tuple(o[0] for o in outs),
        grid=(B, NT // FRONT_CHAINS),
        in_specs=in_specs,
        out_specs=tuple(o[1] for o in outs),
        scratch_shapes=[pltpu.VMEM((D_MODEL, PROJ_W), BF16), pltpu.VMEM((1, LANES), F32)]
        + [pltpu.VMEM((FRONT_CHAINS, GROUP_W // LANES, T, LANES), F32)] * 3,
        compiler_params=_compiler_params(("arbitrary", "arbitrary")),
        name="front",
    )(x3, lp["ln_g"], w_in, consts["ekr"],
      lp["gains"], lp["b_forget"], lp["q_norm"], lp["kv_norm"], lp["nope_gain"], lp["rope_gain"],
      lp["w_uq"], lp["w_ukv"], consts["cos"], consts["sin"], consts["g64"], consts["g32"], consts["tri"])


def _stage_head_queries(qT_ref, qm_ref, groups):
    r = lax.broadcasted_iota(jnp.int32, (GROUP_W, TQ), 0)
    for h in range(N_HEADS):
        if groups == 1:
            g, mask = 0, (r >= HEAD_DIM * h) & (r < HEAD_DIM * (h + 1))
        else:
            g, hl = divmod(h, N_HEADS // groups)
            mask = (r >= HEAD_DIM * hl) & (r < HEAD_DIM * (hl + 1))
            for base in (2 * HEAD_DIM, 2 * HEAD_DIM + LANES // 2):
                lo = base + MLA_HALF * hl
                mask = mask | ((r >= lo) & (r < lo + MLA_HALF))
        for sub in range(N_SUB):
            q = qT_ref[0, g, :, sub * TQ:(sub + 1) * TQ]
            qm_ref[sub * N_HEADS + h] = jnp.where(mask, q, jnp.zeros_like(q))


def _flash_sweep(pair_idx, groups, qm_ref, k_ref, vT_ref, m_ref, acc_ref, u_ref, o_ref, score_fn):
    kpos = lax.broadcasted_iota(jnp.int32, (TK, TQ), 0)
    qpos = lax.broadcasted_iota(jnp.int32, (TK, TQ), 1)
    causal = kpos <= qpos
    m_ref[...] = jnp.full(m_ref.shape, NEG, F32)
    acc_ref[...] = jnp.zeros(acc_ref.shape, F32)
    both = tuple(range(N_SUB))

    def scores(j0, nblk, diag_last, subs):
        row0 = pl.multiple_of(j0 * TK, TK)
        kts = [k_ref[0, g, pl.ds(row0, nblk * TK), :] for g in range(groups)]
        pending = []
        for sub in subs:
            for h in range(N_HEADS):
                c = sub * N_HEADS + h
                s = jnp.dot(kts[h // (N_HEADS // groups)], qm_ref[c], preferred_element_type=F32)
                tops, lives, row = [], [], None
                for blk in range(nblk):
                    rows = slice(blk * TK, (blk + 1) * TK)
                    diagonal = diag_last and blk == nblk - 1
                    u, row, live = score_fn(sub, h, s[rows], j0 + blk, diagonal)
                    if diagonal:
                        u = jnp.where(causal, u, NEG)
                    u_ref[c, rows, :] = u
                    top = jnp.max(u, axis=0, keepdims=True)
                    if row is not None:
                        top = top + row
                    if live is not None:
                        top = jnp.where(live, top, NEG)
                    tops.append(top)
                    lives.append(live)
                m_old = m_ref[c]
                m_new = functools.reduce(jnp.maximum, tops, m_old)
                shift = m_new if row is None else m_new - row
                shifts = [shift if live is None else jnp.where(live, shift, -NEG) for live in lives]
                m_ref[c] = m_new
                pending.append((c, h, shifts, jnp.exp2(m_old - m_new)))
        return pending

    def values(j0, pending):
        for c, h, shifts, alpha in pending:
            pv = None
            for blk, shift in enumerate(shifts):
                p = jnp.exp2((u_ref[c, blk * TK:(blk + 1) * TK, :] - shift).astype(BF16))
                term = jnp.dot(vT_ref[0, j0 + blk, h], p, preferred_element_type=F32)
                pv = term if pv is None else pv + term
            acc_ref[c] = alpha * acc_ref[c] + pv

    def common(t, carry):
        values(2 * t, scores(2 * t, 2, False, both))
        return carry

    lax.fori_loop(0, pair_idx, common, 0)
    n_common = N_SUB * pair_idx
    pend_a = scores(n_common, 1, True, (0,))
    pend_b = scores(n_common, 2, True, (1,))
    values(n_common, pend_a)
    values(n_common, pend_b)
    for sub in both:
        outs = []
        for h in range(N_HEADS):
            acc = acc_ref[sub * N_HEADS + h]
            outs.append(acc[:HEAD_DIM] / acc[HEAD_DIM:HEAD_DIM + 1])
        o_ref[0, sub * TQ:(sub + 1) * TQ, :] = jnp.concatenate(outs, axis=0).T.astype(BF16)


def _fox_kernel(qT_ref, k_ref, vT_ref, ck_ref, crow_ref, o_ref, qm_ref, m_ref, acc_ref, u_ref):
    pair_idx = pl.program_id(1)
    _stage_head_queries(qT_ref, qm_ref, 1)

    def score_fn(sub, h, s, j, diagonal):
        ck = ck_ref[0, h, pl.ds(pl.multiple_of(j * TK, TK), TK), :]
        cq = crow_ref[0, h:h + 1, sub * TQ:(sub + 1) * TQ]
        return s - jnp.concatenate([ck] * (TQ // LANES), axis=1), cq, None

    _flash_sweep(pair_idx, 1, qm_ref, k_ref, vT_ref, m_ref, acc_ref, u_ref, o_ref, score_fn)


def _mla_kernel(qT_ref, k_ref, vT_ref, o_ref, qm_ref, m_ref, acc_ref, u_ref):
    pair_idx = pl.program_id(1)
    _stage_head_queries(qT_ref, qm_ref, MLA_GROUPS)
    _flash_sweep(pair_idx, MLA_GROUPS, qm_ref, k_ref, vT_ref, m_ref, acc_ref, u_ref, o_ref,
                 lambda sub, h, s, j, diagonal: (s, None, None))


def _moba_kernel(qT_ref, k_ref, vT_ref, kmean_ref, tb_ref, o_ref, qm_ref, m_ref, acc_ref, u_ref, sel_ref):
    pair_idx = pl.program_id(1)
    _stage_head_queries(qT_ref, qm_ref, 1)
    kmean = kmean_ref[0].astype(BF16)
    nb = kmean.shape[0]
    blk = lax.broadcasted_iota(jnp.int32, (nb, TQ), 0)
    own = [N_SUB * pair_idx + sub for sub in range(N_SUB)]
    for c in range(N_SUB * N_HEADS):
        i = own[c // N_HEADS]
        gate = jnp.dot(kmean, qm_ref[c], preferred_element_type=F32)
        gate = jnp.where(blk < i, gate, NEG)
        rank = jnp.zeros((nb, TQ), jnp.int32)
        for jp in range(nb):
            row = gate[jp:jp + 1, :]
            ahead = (row > gate) | ((row == gate) & (jp < blk))
            rank = rank + ahead.astype(jnp.int32)
        sel_ref[c] = ((rank < MOBA_TOPK) & (blk < i)).astype(F32)

    def score_fn(sub, h, s, j, diagonal):
        u = s + tb_ref[jnp.minimum(own[sub] - j, MOBA_TB_ENTRIES - 1), h]
        live = None if diagonal else sel_ref[sub * N_HEADS + h, pl.ds(j, 1), :] > 0.5
        return u, None, live

    _flash_sweep(pair_idx, 1, qm_ref, k_ref, vT_ref, m_ref, acc_ref, u_ref, o_ref, score_fn)


def _dense_mixer(kernel_fn, name, qT, k, vT, extra_args=(), extra_specs=(), extra_scratch=()):
    B, groups, _, S = qT.shape
    NT = S // TK
    assert TQ == TK and N_SUB == 2 and S % (N_SUB * TQ) == 0
    n_chain = N_SUB * N_HEADS
    in_specs = [pl.BlockSpec((1, groups, GROUP_W, N_SUB * TQ), lambda b, i: (b, 0, 0, i)),
                pl.BlockSpec((1, groups, S, GROUP_W), lambda b, i: (b, 0, 0, 0)),
                pl.BlockSpec((1, NT, N_HEADS, V_ROWS, TK), lambda b, i: (b, 0, 0, 0, 0))] + list(extra_specs)
    scratch = [pltpu.VMEM((n_chain, GROUP_W, TQ), BF16),
               pltpu.VMEM((n_chain, 1, TQ), F32),
               pltpu.VMEM((n_chain, V_ROWS, TQ), F32),
               pltpu.VMEM((n_chain, 2 * TK, TQ), F32)] + list(extra_scratch)
    return pl.pallas_call(
        kernel_fn,
        out_shape=jax.ShapeDtypeStruct((B, S, GROUP_W), BF16),
        grid=(B, S // (N_SUB * TQ)),
        in_specs=in_specs,
        out_specs=pl.BlockSpec((1, N_SUB * TQ, GROUP_W), lambda b, i: (b, i, 0)),
        scratch_shapes=scratch,
        compiler_params=_compiler_params(("arbitrary", "arbitrary")),
        name=name,
    )(qT, k, vT, *extra_args)


def _dil_kernel(q_ref, kp_ref, kc_ref, vp_ref, vc_ref, tb_ref, o_ref, lse_ref):
    n = pl.program_id(2)
    qb = q_ref.shape[2] // DIL_BLOCK
    kj = lax.broadcasted_iota(jnp.int32, (DIL_BLOCK, 2 * DIL_BLOCK), 1)
    qlane = lax.broadcasted_iota(jnp.int32, (DIL_BLOCK, GROUP_W), 1)
    in_head = [(qlane >= HEAD_DIM * h) & (qlane < HEAD_DIM * (h + 1)) for h in range(N_HEADS)]
    for seq, c in [(seq, c) for seq in range(q_ref.shape[1]) for c in range(qb)]:
        rows = slice(c * DIL_BLOCK, (c + 1) * DIL_BLOCK)
        q = q_ref[0, seq, rows, :]
        if c == 0:
            kb = jnp.concatenate([kp_ref[0, seq], kc_ref[0, seq, :DIL_BLOCK, :]], axis=0)
            vb = jnp.concatenate([vp_ref[0, seq], vc_ref[0, seq, :DIL_BLOCK, :]], axis=0)
        else:
            kb = kc_ref[0, seq, (c - 1) * DIL_BLOCK:(c + 1) * DIL_BLOCK, :]
            vb = vc_ref[0, seq, (c - 1) * DIL_BLOCK:(c + 1) * DIL_BLOCK, :]
        q_heads = jnp.concatenate([jnp.where(msk, q, jnp.zeros_like(q)) for msk in in_head], axis=0)
        s = lax.dot_general(q_heads, kb, (((1,), (1,)), ((), ())), preferred_element_type=F32)
        s = s.reshape(N_HEADS, DIL_BLOCK, 2 * DIL_BLOCK)
        s = s + tb_ref[0]
        if c == 0:
            s = jnp.where(((kj >= DIL_BLOCK) | (n > 0))[None], s, NEG)
        m = jnp.max(s, axis=-1, keepdims=True)
        e = jnp.exp(s - m)
        l = jnp.sum(e, axis=-1, keepdims=True)
        lse = m + jnp.log(l)
        pr = (e * (1.0 / l)).astype(BF16)
        o_all = jnp.dot(pr.reshape(N_HEADS * DIL_BLOCK, 2 * DIL_BLOCK), vb, preferred_element_type=F32)
        o_all = o_all.reshape(N_HEADS, DIL_BLOCK, GROUP_W)
        o_acc = jnp.zeros((DIL_BLOCK, GROUP_W), F32)
        lse_map = jnp.zeros((DIL_BLOCK, GROUP_W), F32)
        for h in range(N_HEADS):
            o_acc = jnp.where(in_head[h], o_all[h], o_acc)
            lse_map = jnp.where(in_head[h], lse[h], lse_map)
        o_ref[0, seq, rows, :] = o_acc.astype(BF16)
        lse_ref[0, seq, rows, :] = lse_map


def _dilated_pattern(p, q, k, v, dil_tb):
    B, dil, L, _ = q.shape
    assert dil == DIL_PATTERNS[p][1] and L % DIL_BLOCK == 0
    nb = L // DIL_BLOCK
    qb = math.gcd(nb, DIL_BLOCKS_PER_STEP)
    seqs = math.gcd(dil, DIL_BLOCKS_PER_STEP // qb)
    cur = pl.BlockSpec((1, seqs, qb * DIL_BLOCK, GROUP_W), lambda b, r, n: (b, r, n, 0))
    prev = pl.BlockSpec((1, seqs, DIL_BLOCK, GROUP_W), lambda b, r, n: (b, r, jnp.maximum(n * qb - 1, 0), 0))
    return pl.pallas_call(
        _dil_kernel,
        out_shape=(jax.ShapeDtypeStruct((B, dil, L, GROUP_W), BF16),
                   jax.ShapeDtypeStruct((B, dil, L, GROUP_W), F32)),
        grid=(B, dil // seqs, nb // qb),
        in_specs=[cur, prev, cur, prev, cur,
                  pl.BlockSpec((1, N_HEADS, DIL_BLOCK, 2 * DIL_BLOCK), lambda b, r, n, p=p: (p, 0, 0, 0))],
        out_specs=(cur, cur),
        compiler_params=_compiler_params(("arbitrary", "arbitrary", "arbitrary")),
        name=f"dilated_{dil}",
    )(q, k, k, v, v, dil_tb)


def _mix_dilations(o_refs, lse_refs, scratch):
    halves = GROUP_W // LANES
    rows = o_refs[0].shape[1] * o_refs[0].shape[2]
    nat = []
    for ref, dst in zip(list(o_refs) + list(lse_refs), scratch):
        dil = ref.shape[1]
        if dil == 1:
            nat.append(ref[0, 0].astype(F32))
            continue
        for r in range(dil):
            for half in range(halves):
                dst[half, pl.ds(r, rows // dil, stride=dil), :] = (
                    ref[0, r, :, half * LANES:(half + 1) * LANES].astype(F32))
        nat.append(jnp.concatenate([dst[half] for half in range(halves)], axis=1))
    os, lses = nat[:len(o_refs)], nat[len(o_refs):]
    m = functools.reduce(jnp.maximum, lses)
    ws = [jnp.exp(l - m) for l in lses]
    tot = functools.reduce(jnp.add, ws)
    return functools.reduce(jnp.add, [(w / tot) * o for w, o in zip(ws, os)])


def _sigmoid(x):
    return 1.0 / (1.0 + jnp.exp(-x))


def _merge_kernel(x_ref, fo_ref, mo_ref, lo_ref, o1_ref, l1_ref, o4_ref, l4_ref, o16_ref, l16_ref, gate_ref, p_ref,
                  wout_ref, png_ref, wpg_ref, wpp_ref, o_ref, *dil_scratch):
    o_dil = _mix_dilations((o1_ref, o4_ref, o16_ref), (l1_ref, l4_ref, l16_ref), dil_scratch)
    mix = jnp.concatenate([fo_ref[...].astype(F32), mo_ref[...].astype(F32), o_dil, lo_ref[...].astype(F32)],
                          axis=1)
    g = gate_ref[...].astype(F32)
    y = jnp.dot((mix * (g * _sigmoid(g))).astype(BF16), wout_ref[...], preferred_element_type=F32)
    x1 = x_ref[...] + y
    hn = _row_norm(x1, png_ref[...]).astype(BF16)
    pg = _sigmoid(jnp.dot(hn, wpg_ref[...], preferred_element_type=F32))
    pp = jnp.dot(p_ref[0].astype(BF16), wpp_ref[...], preferred_element_type=F32)
    o_ref[...] = x1 + pg * pp


def _merge(layer, x2, fo, mo, lo, dil_outs, proj, p3, lp):
    rows = x2.shape[0]
    TM = TM_MERGE
    tiles_per_seq = dil_outs[0].shape[1] * dil_outs[0].shape[2] // TM

    def rspec(w, c=0):
        return pl.BlockSpec((TM, w), lambda i, c=c: (i, c))

    def cspec(shape):
        return pl.BlockSpec(shape, lambda i: (0, 0))

    def dspec(a):
        dil = a.shape[1]
        return pl.BlockSpec((1, dil, TM // dil, GROUP_W),
                            lambda i: (i // tiles_per_seq, 0, lax.rem(i, tiles_per_seq), 0))

    return pl.pallas_call(
        _merge_kernel,
        out_shape=jax.ShapeDtypeStruct((rows, D_MODEL), F32),
        grid=(rows // TM,),
        scratch_shapes=[pltpu.VMEM((GROUP_W // LANES, TM, LANES), F32)] * len(dil_outs),
        in_specs=[rspec(D_MODEL)] + [rspec(GROUP_W)] * 3 + [dspec(a) for a in dil_outs] + [
                  rspec(D_MODEL),
                  pl.BlockSpec((1, TM, PLE_DIM), lambda i: (layer, i, 0)),
                  cspec((D_MODEL, D_MODEL)), cspec((1, D_MODEL)), cspec((D_MODEL, D_MODEL)),
                  cspec((PLE_DIM, D_MODEL))],
        out_specs=rspec(D_MODEL),
        compiler_params=_compiler_params(("arbitrary",)),
        name="merge",
    )(x2, fo, mo, lo, *dil_outs, proj, p3, lp["w_out"], lp["ple_norm_g"], lp["w_ple_gate"], lp["w_ple_proj"])


def _layer_params(i, ln_g, b_forget, qk_gain, mla_q_norm, mla_kv_norm, mla_nope_gain,
                  mla_rope_gain, w_uq, w_ukv, w_out, ple_norm_g, w_ple_gate, w_ple_proj):
    def rope_slab(x1_parts, x2_parts, like):
        pad = jnp.zeros(like.shape[:-1] + (LANES // 2 - 2 * MLA_HALF,), like.dtype)
        return jnp.concatenate(list(x1_parts) + [pad] + list(x2_parts) + [pad], axis=-1)

    per_head = MLA_NOPE + MLA_ROPE
    uq = w_uq[i]
    uq_nope = [uq[:, h * per_head:h * per_head + MLA_NOPE] for h in range(N_HEADS)]
    uq_x1 = [uq[:, h * per_head + MLA_NOPE:h * per_head + MLA_NOPE + MLA_HALF] for h in range(N_HEADS)]
    uq_x2 = [uq[:, h * per_head + MLA_NOPE + MLA_HALF:(h + 1) * per_head] for h in range(N_HEADS)]
    hpg = N_HEADS // MLA_GROUPS
    uq_cols = uq_nope + [rope_slab(uq_x1[g * hpg:(g + 1) * hpg], uq_x2[g * hpg:(g + 1) * hpg], uq)
                         for g in range(MLA_GROUPS)]
    ukv = w_ukv[i]
    ukv_cols = ([ukv[:, h * 2 * HEAD_DIM:h * 2 * HEAD_DIM + HEAD_DIM] for h in range(N_HEADS)]
                + [ukv[:, h * 2 * HEAD_DIM + HEAD_DIM:(h + 1) * 2 * HEAD_DIM] for h in range(N_HEADS)])
    rg = mla_rope_gain[i]
    rope_gain = rope_slab([rg[:, :MLA_HALF]] * hpg, [rg[:, MLA_HALF:]] * hpg, rg)
    return {
        "ln_g": ln_g[i][None, :],
        "gains": jnp.tile(qk_gain[i], (1, N_HEADS)),
        "b_forget": jnp.pad(b_forget[i], (0, LANES - N_HEADS))[None, :],
        "q_norm": mla_q_norm[i][None, :],
        "kv_norm": mla_kv_norm[i][None, :],
        "nope_gain": jnp.tile(mla_nope_gain[i], (1, N_HEADS)),
        "rope_gain": rope_gain,
        "w_uq": jnp.concatenate(uq_cols, axis=1).astype(BF16),
        "w_ukv": jnp.concatenate(ukv_cols, axis=1).astype(BF16),
        "w_out": w_out[i].astype(BF16),
        "ple_norm_g": ple_norm_g[i][None, :],
        "w_ple_gate": w_ple_gate[i].astype(BF16),
        "w_ple_proj": w_ple_proj[i].astype(BF16),
    }


def _constants(S):
    inv = 1.0 / (ROPE_THETA ** (jnp.arange(MLA_HALF, dtype=F32) * 2.0 / MLA_ROPE))
    ang = jnp.arange(S).astype(F32)[:, None] * inv[None, :]
    cos = jnp.tile(jnp.cos(ang), (1, 2 * N_HEADS))
    sin = jnp.tile(jnp.sin(ang), (1, 2 * N_HEADS))
    sign = np.concatenate([-np.ones(LANES // 2, np.float32), np.ones(LANES // 2, np.float32)])
    lane = np.arange(GROUP_W)
    g64 = (lane[:, None] // HEAD_DIM == lane[None, :] // HEAD_DIM).astype(np.float32) / HEAD_DIM
    within = np.arange(LANES) % (LANES // 2)
    rope_head = np.where(within < 2 * MLA_HALF, within // MLA_HALF, -1)
    g32 = ((rope_head[:, None] == rope_head[None, :]) & (rope_head[:, None] >= 0)).astype(np.float32) / MLA_ROPE
    tri = np.tril(np.ones((T_PREP, T_PREP), np.float32))
    ekr = np.zeros((LANES, LANES), np.float32)
    kr_lane0 = IN_OFF["kr"] % LANES
    for c in range(MLA_ROPE):
        half, idx = divmod(c, MLA_HALF)
        for copy in range(N_HEADS // MLA_GROUPS):
            ekr[kr_lane0 + c, half * (LANES // 2) + copy * MLA_HALF + idx] = 1.0
    return {"cos": cos, "sin": sin * sign[None, :], "g64": jnp.asarray(g64, BF16),
            "g32": jnp.asarray(g32, BF16), "tri": jnp.asarray(tri, BF16), "ekr": jnp.asarray(ekr, BF16)}


def kernel(x, p, ln_g, w_in, b_forget, qk_gain, mla_q_norm, mla_kv_norm, mla_nope_gain, mla_rope_gain,
           w_uq, w_ukv, w_out, rel_bias, ple_norm_g, w_ple_gate, w_ple_proj):
    B, S, _ = x.shape
    depth = p.shape[0]
    consts = _constants(S)
    moba_tb, dil_tb = _bias_tables(rel_bias)
    x2 = x.reshape(B * S, D_MODEL)
    for i in range(depth):
        lp = _layer_params(i, ln_g, b_forget, qk_gain, mla_q_norm, mla_kv_norm, mla_nope_gain,
                           mla_rope_gain, w_uq, w_ukv, w_out, ple_norm_g, w_ple_gate, w_ple_proj)
        (fqT, fk, fvT, fck, fcrow, mqT, mk, mvT, mkmean, lqT, lk, lvT, *dil_qkv, gate) = _front(
            i, x2.reshape(B, S, D_MODEL), w_in, consts, lp)
        S_ = S
        fo = _dense_mixer(
            _fox_kernel, "fox", fqT, fk, fvT, (fck, fcrow),
            (pl.BlockSpec((1, N_HEADS, S_, LANES), lambda b, i: (b, 0, 0, 0)),
             pl.BlockSpec((1, 8, N_SUB * TQ), lambda b, i: (b, 0, i))))
        mo = _dense_mixer(
            _moba_kernel, "moba", mqT, mk, mvT, (mkmean, moba_tb),
            (pl.BlockSpec((1, S_ // MOBA_BLOCK, GROUP_W), lambda b, i: (b, 0, 0)),
             pl.BlockSpec((MOBA_TB_ENTRIES, N_HEADS, MOBA_BLOCK, MOBA_BLOCK), lambda b, i: (0, 0, 0, 0))),
            (pltpu.VMEM((N_SUB * N_HEADS, S_ // MOBA_BLOCK, TQ), F32),))
        lo = _dense_mixer(_mla_kernel, "mla", lqT, lk, lvT)
        dil_outs = []
        for pat, (_, dil) in enumerate(DIL_PATTERNS):
            q, k, v = (a.reshape(B, dil, S // dil, GROUP_W) for a in dil_qkv[3 * pat:3 * pat + 3])
            dil_outs += list(_dilated_pattern(pat, q, k, v, dil_tb))
        rs = lambda a: a.reshape(B * S, GROUP_W)
        x2 = _merge(i, x2, rs(fo), rs(mo), rs(lo), dil_outs, gate.reshape(B * S, D_MODEL),
                    p.reshape(depth, B * S, PLE_DIM), lp)
    return x2.reshape(B, S, D_MODEL)
```

```python
import functools
import math

import numpy as np
import jax
import jax.numpy as jnp
from jax import lax
from jax.experimental import pallas as pl
from jax.experimental.pallas import tpu as pltpu

F32 = jnp.float32
BF16 = jnp.bfloat16

D_MODEL = 1024
N_HEADS = 4
HEAD_DIM = 64
GROUP_W = N_HEADS * HEAD_DIM
MLA_NOPE = 64
MLA_ROPE = 32
MLA_HALF = MLA_ROPE // 2
MLA_QK_W = N_HEADS * (MLA_NOPE + MLA_ROPE)
MLA_Q_LORA = 256
MLA_KV_LORA = 128
PLE_DIM = 256
MOBA_BLOCK = 256
MOBA_TOPK = 3
DIL_PATTERNS = ((128, 1), (512, 4), (2048, 16))
DIL_BLOCK = 128
DIL_BLOCKS_PER_STEP = 32
N_BUCKETS = 32
MAX_DISTANCE = 2048
ROPE_THETA = 10000.0
EPS = 1e-6
NEG = -1e30
LOG2E = math.log2(math.e)

LANES = 128
VMEM_LIMIT_BYTES = 56 * 1024 * 1024

COL_FOX, COL_MOBA, COL_DIL = 0, 768, 1536
COL_CQ, COL_CKV, COL_KR, COL_FF, COL_GATE = 2304, 2560, 2688, 2816, 3072
PROJ_W = 4096
PROJ_TN = 512
STAGE_ROWS = 128

IN_OFF = {}
IN_W = 0
for _name, _size in (("fq", GROUP_W), ("fk", GROUP_W), ("fv", GROUP_W), ("ff", N_HEADS),
                     ("mq", GROUP_W), ("mk", GROUP_W), ("mv", GROUP_W),
                     ("dq", GROUP_W), ("dk", GROUP_W), ("dv", GROUP_W),
                     ("cq", MLA_Q_LORA), ("ckv", MLA_KV_LORA), ("kr", MLA_ROPE), ("gate", D_MODEL)):
    IN_OFF[_name] = IN_W
    IN_W += _size

TM_PROJ = 512
T_PREP = 256
FRONT_CHAINS = 2
TQ = 256
TK = 256
N_SUB = 2
MOBA_TB_ENTRIES = 8
TM_MERGE = 1024
V_ROWS = HEAD_DIM + 16
MLA_GROUPS = 2


def _bucket_np(d):
    d = np.maximum(np.asarray(d, np.int64), 0)
    max_exact = N_BUCKETS // 2
    d_f = np.maximum(d, 1).astype(np.float64)
    val = np.log(d_f / max_exact) / math.log(MAX_DISTANCE / max_exact) * (N_BUCKETS - max_exact)
    frac = np.abs(val - np.round(val))
    on_edge = (frac < 1e-5) & (d > max_exact) & (val < N_BUCKETS - max_exact - 0.5)
    assert not on_edge.any(), "distance too close to a bucket edge for a static table"
    large = np.minimum(max_exact + np.floor(val + 1e-9).astype(np.int64), N_BUCKETS - 1)
    return np.where(d < max_exact, d, large)


def _bucket_steps(lo, hi):
    ds = np.arange(lo, hi + 1)
    bs = _bucket_np(ds)
    steps = [(int(ds[i]), int(bs[i])) for i in range(1, len(ds)) if bs[i] != bs[i - 1]]
    return int(bs[0]), steps


def _compiler_params(sem):
    return pltpu.CompilerParams(dimension_semantics=sem, vmem_limit_bytes=VMEM_LIMIT_BYTES)


def _bias_from_steps(d, rb_ref, col, lo, hi):
    b0, steps = _bucket_steps(lo, hi)
    val = jnp.full(d.shape, rb_ref[b0, col], F32)
    for t, b in steps:
        val = jnp.where(d >= t, rb_ref[b, col], val)
    return val


def _bias_tables_kernel(rb_ref, moba_ref, dil_ref):
    kl = lax.broadcasted_iota(jnp.int32, (MOBA_BLOCK, MOBA_BLOCK), 0)
    ql = lax.broadcasted_iota(jnp.int32, (MOBA_BLOCK, MOBA_BLOCK), 1)
    for e in range(MOBA_TB_ENTRIES - 1):
        d = jnp.maximum(e * MOBA_BLOCK + ql - kl, 0)
        lo = max(e * MOBA_BLOCK - (MOBA_BLOCK - 1), 0)
        hi = e * MOBA_BLOCK + (MOBA_BLOCK - 1)
        for h in range(N_HEADS):
            moba_ref[e, h] = _bias_from_steps(d, rb_ref, h, lo, hi) * LOG2E
    for h in range(N_HEADS):
        moba_ref[MOBA_TB_ENTRIES - 1, h] = jnp.full((MOBA_BLOCK, MOBA_BLOCK), rb_ref[N_BUCKETS - 1, h], F32) * LOG2E
    qi = lax.broadcasted_iota(jnp.int32, (DIL_BLOCK, 2 * DIL_BLOCK), 0)
    kj = lax.broadcasted_iota(jnp.int32, (DIL_BLOCK, 2 * DIL_BLOCK), 1)
    rel = qi + DIL_BLOCK - kj
    for p, (_, dil) in enumerate(DIL_PATTERNS):
        d = jnp.maximum(rel * dil, 0)
        for h in range(N_HEADS):
            bias = _bias_from_steps(d, rb_ref, N_HEADS + h, 0, (2 * DIL_BLOCK - 1) * dil)
            dil_ref[p, h] = jnp.where((rel >= 0) & (rel <= DIL_BLOCK), bias, NEG)


def _bias_tables(rel_bias):
    assert _bucket_np((MOBA_TB_ENTRIES - 1) * MOBA_BLOCK - (MOBA_BLOCK - 1)) == N_BUCKETS - 1
    return pl.pallas_call(
        _bias_tables_kernel,
        out_shape=(jax.ShapeDtypeStruct((MOBA_TB_ENTRIES, N_HEADS, MOBA_BLOCK, MOBA_BLOCK), F32),
                   jax.ShapeDtypeStruct((len(DIL_PATTERNS), N_HEADS, DIL_BLOCK, 2 * DIL_BLOCK), F32)),
        in_specs=[pl.BlockSpec(memory_space=pltpu.SMEM)],
        out_specs=(pl.BlockSpec(memory_space=pltpu.VMEM), pl.BlockSpec(memory_space=pltpu.VMEM)),
        compiler_params=pltpu.CompilerParams(vmem_limit_bytes=VMEM_LIMIT_BYTES),
        name="bias_tables",
    )(rel_bias)


def _stage_in_weights(w_ref, ekr_ref, ws_ref):
    W = GROUP_W
    moves = ((COL_FOX, IN_OFF["fq"], 3 * W), (COL_MOBA, IN_OFF["mq"], 3 * W), (COL_DIL, IN_OFF["dq"], 3 * W),
             (COL_CQ, IN_OFF["cq"], MLA_Q_LORA), (COL_CKV, IN_OFF["ckv"], MLA_KV_LORA),
             (COL_GATE, IN_OFF["gate"], D_MODEL))
    kr_base = IN_OFF["kr"] // LANES * LANES
    assert IN_OFF["ff"] % LANES == 0 and IN_OFF["kr"] + MLA_ROPE <= kr_base + LANES
    lane = lax.broadcasted_iota(jnp.int32, (STAGE_ROWS, LANES), 1)
    for r0 in range(0, D_MODEL, STAGE_ROWS):
        rows = slice(r0, r0 + STAGE_ROWS)
        for dst, src, width in moves:
            ws_ref[rows, dst:dst + width] = w_ref[0, rows, src:src + width].astype(BF16)
        kr_win = w_ref[0, rows, kr_base:kr_base + LANES].astype(BF16)
        ws_ref[rows, COL_KR:COL_KR + LANES] = jnp.dot(kr_win, ekr_ref[...], preferred_element_type=F32).astype(BF16)
        ff_win = w_ref[0, rows, IN_OFF["ff"]:IN_OFF["ff"] + LANES]
        ws_ref[rows, COL_FF:COL_FF + LANES] = jnp.where(lane < N_HEADS, ff_win, 0.0).astype(BF16)
        ws_ref[rows, COL_FF + LANES:COL_GATE] = jnp.zeros((STAGE_ROWS, COL_GATE - COL_FF - LANES), BF16)


def _group_mean_sq(x, g_mat):
    return jnp.dot((x * x).astype(BF16), g_mat, preferred_element_type=F32)


def _group_norm(x, gain, g_mat):
    return x * lax.rsqrt(_group_mean_sq(x, g_mat) + EPS) * gain


def _row_norm(x, gain):
    return x * lax.rsqrt(jnp.mean(x * x, axis=-1, keepdims=True) + EPS) * gain


def _log_sigmoid(x):
    return -(jnp.maximum(-x, 0.0) + jnp.log1p(jnp.exp(-jnp.abs(x))))


def _front_kernel(x_ref, lng_ref, w_ref, ekr_ref,
                  gains_ref, bfor_ref, qng_ref, kvng_ref, nopeg_ref, ropeg_ref,
                  wuq_ref, wukv_ref, cos_ref, sin_ref, g64_ref, g32_ref, tri_ref,
                  fqT_ref, fk_ref, fvT_ref, fck_ref, fcrow_ref,
                  mqT_ref, mk_ref, mvT_ref, mkmean_ref,
                  lqT_ref, lk_ref, lvT_ref,
                  dq1_ref, dk1_ref, dv1_ref, dq4_ref, dk4_ref, dv4_ref, dq16_ref, dk16_ref, dv16_ref,
                  gate_ref,
                  ws_ref, carry_ref, sq_ref, sk_ref, sv_ref):
    t = pl.program_id(1)

    @pl.when((pl.program_id(0) == 0) & (t == 0))
    def _():
        _stage_in_weights(w_ref, ekr_ref, ws_ref)

    g64 = g64_ref[...]
    scale = HEAD_DIM ** -0.5
    qscale = (MLA_NOPE + MLA_ROPE) ** -0.5
    W = GROUP_W
    T = T_PREP
    ones_rows = jnp.ones((V_ROWS - HEAD_DIM, T), BF16)

    @pl.when(t == 0)
    def _():
        carry_ref[...] = jnp.zeros_like(carry_ref)

    for ch in range(FRONT_CHAINS):
        rows = slice(ch * T, (ch + 1) * T)

        def store_vT(dst, v):
            vT = v.T.astype(BF16)
            for h in range(N_HEADS):
                dst[0, ch, h] = jnp.concatenate([vT[HEAD_DIM * h:HEAD_DIM * (h + 1)], ones_rows], axis=0)

        x = x_ref[0, rows, :]
        hn = (x * lax.rsqrt(jnp.mean(x * x, axis=-1, keepdims=True) + EPS) * lng_ref[...]).astype(BF16)
        project = lambda lo, width: jnp.dot(hn, ws_ref[:, lo:lo + width], preferred_element_type=F32)
        pf = project(COL_FOX, 3 * W)
        p_ff = project(COL_FF, LANES)
        pm = project(COL_MOBA, 3 * W)

        fq = _group_norm(pf[:, :W], gains_ref[0:1, :], g64) * (scale * LOG2E)
        fk = _group_norm(pf[:, W:2 * W], gains_ref[1:2, :], g64)
        fqT_ref[0, 0, :, rows] = fq.T.astype(BF16)
        fk_ref[0, 0, rows, :] = fk.astype(BF16)
        store_vT(fvT_ref, pf[:, 2 * W:])
        log_f = _log_sigmoid(p_ff + bfor_ref[...])
        tri = tri_ref[...]
        c = carry_ref[...]
        rest = log_f
        for _ in range(3):
            part = rest.astype(BF16)
            c = c + jnp.dot(tri, part, preferred_element_type=F32)
            rest = rest - part.astype(F32)
        carry_ref[...] = c[T - 1:T, :]
        c2 = c * LOG2E
        for h in range(N_HEADS):
            fck_ref[0, h, rows, :] = jnp.broadcast_to(c2[:, h:h + 1], (T, LANES))
        fcrow_ref[0, :, rows] = c2.T[0:8, :]

        pd = project(COL_DIL, 3 * W)
        mq = _group_norm(pm[:, :W], gains_ref[2:3, :], g64) * (scale * LOG2E)
        mk = _group_norm(pm[:, W:2 * W], gains_ref[3:4, :], g64)
        mqT_ref[0, 0, :, rows] = mq.T.astype(BF16)
        mk_ref[0, 0, rows, :] = mk.astype(BF16)
        store_vT(mvT_ref, pm[:, 2 * W:])
        mkmean_ref[0, pl.ds(FRONT_CHAINS * t + ch, 1), :] = jnp.mean(mk, axis=0, keepdims=True)

        p_mla = project(COL_CQ, MLA_Q_LORA + MLA_KV_LORA + LANES)
        dq = _group_norm(pd[:, :W], gains_ref[4:5, :], g64) * scale
        dk = _group_norm(pd[:, W:2 * W], gains_ref[5:6, :], g64)
        dv = pd[:, 2 * W:]
        dq1_ref[0, rows, :] = dq.astype(BF16)
        dk1_ref[0, rows, :] = dk.astype(BF16)
        dv1_ref[0, rows, :] = dv.astype(BF16)
        for val, dst in ((dq, sq_ref), (dk, sk_ref), (dv, sv_ref)):
            for half in range(GROUP_W // LANES):
                dst[ch, half] = val[:, half * LANES:(half + 1) * LANES]
        for dil, outs in ((4, (dq4_ref, dk4_ref, dv4_ref)), (16, (dq16_ref, dk16_ref, dv16_ref))):
            n = T // dil
            for r in range(dil):
                for src, dst in zip((sq_ref, sk_ref, sv_ref), outs):
                    dst[0, r, ch * n:(ch + 1) * n, :] = jnp.concatenate(
                        [src[ch, half, pl.ds(r, n, stride=dil), :] for half in range(GROUP_W // LANES)],
                        axis=1).astype(BF16)

        half_gate = D_MODEL // 2
        gate_ref[0, rows, :half_gate] = project(COL_GATE, half_gate).astype(BF16)
        cos = cos_ref[rows, :]
        sin = sin_ref[rows, :]
        cqn = _row_norm(p_mla[:, :MLA_Q_LORA], qng_ref[...]).astype(BF16)
        rotate = lambda v, cos=cos, sin=sin: v * cos + pltpu.roll(v, LANES // 2, 1) * sin
        qf = jnp.dot(cqn, wuq_ref[...], preferred_element_type=F32)
        q_nope = _group_norm(qf[:, :W], nopeg_ref[0:1, :], g64) * (qscale * LOG2E)
        ckvn = _row_norm(p_mla[:, MLA_Q_LORA:MLA_Q_LORA + MLA_KV_LORA], kvng_ref[...]).astype(BF16)
        kvf = jnp.dot(ckvn, wukv_ref[...], preferred_element_type=F32)
        k_nope = _group_norm(kvf[:, :W], nopeg_ref[1:2, :], g64)
        krs = p_mla[:, MLA_Q_LORA + MLA_KV_LORA:]
        kr_ms = jnp.sum(krs * krs, axis=-1, keepdims=True) * (1.0 / (2 * MLA_ROPE))
        kr = rotate(krs * lax.rsqrt(kr_ms + EPS) * ropeg_ref[1:2, :])
        for g in range(MLA_GROUPS):
            slab = qf[:, W + g * LANES:W + (g + 1) * LANES]
            qr = rotate(_group_norm(slab, ropeg_ref[0:1, :], g32_ref[...])) * (qscale * LOG2E)
            lq = jnp.concatenate([q_nope[:, g * LANES:(g + 1) * LANES], qr], axis=1)
            lqT_ref[0, g, :, rows] = lq.T.astype(BF16)
            lk_ref[0, g, rows, :] = jnp.concatenate([k_nope[:, g * LANES:(g + 1) * LANES], kr],
                                                    axis=1).astype(BF16)
        store_vT(lvT_ref, kvf[:, W:])
        gate_ref[0, rows, half_gate:] = project(COL_GATE + half_gate, half_gate).astype(BF16)


def _front(layer, x3, w_in, consts, lp):
    B, S, _ = x3.shape
    T = T_PREP
    TS = FRONT_CHAINS * T
    NT = S // T
    assert S % TS == 0 and S // MOBA_BLOCK == NT

    def cspec(shape):
        return pl.BlockSpec(shape, lambda b, t: (0,) * len(shape))

    in_specs = [pl.BlockSpec((1, TS, D_MODEL), lambda b, t: (b, t, 0)), cspec((1, D_MODEL)),
                pl.BlockSpec((1, D_MODEL, IN_W), lambda b, t: (layer, 0, 0), pipeline_mode=pl.Buffered(1)),
                cspec((LANES, LANES)),
                cspec((6, GROUP_W)), cspec((1, LANES)), cspec((1, MLA_Q_LORA)), cspec((1, MLA_KV_LORA)),
                cspec((2, GROUP_W)), cspec((2, LANES)),
                cspec((MLA_Q_LORA, MLA_GROUPS * GROUP_W)), cspec((MLA_KV_LORA, 2 * GROUP_W)),
                pl.BlockSpec((TS, LANES), lambda b, t: (t, 0)), pl.BlockSpec((TS, LANES), lambda b, t: (t, 0)),
                cspec((GROUP_W, GROUP_W)), cspec((LANES, LANES)), cspec((T, T))]

    def qT(groups):
        return (jax.ShapeDtypeStruct((B, groups, GROUP_W, S), BF16),
                pl.BlockSpec((1, groups, GROUP_W, TS), lambda b, t: (b, 0, 0, t)))

    def keys(groups):
        return (jax.ShapeDtypeStruct((B, groups, S, GROUP_W), BF16),
                pl.BlockSpec((1, groups, TS, GROUP_W), lambda b, t: (b, 0, t, 0)))

    def rows(w):
        return jax.ShapeDtypeStruct((B, S, w), BF16), pl.BlockSpec((1, TS, w), lambda b, t: (b, t, 0))

    def vT():
        return (jax.ShapeDtypeStruct((B, NT, N_HEADS, V_ROWS, T), BF16),
                pl.BlockSpec((1, FRONT_CHAINS, N_HEADS, V_ROWS, T), lambda b, t: (b, t, 0, 0, 0)))

    def resid(dil):
        return (jax.ShapeDtypeStruct((B, dil, S // dil, GROUP_W), BF16),
                pl.BlockSpec((1, dil, TS // dil, GROUP_W), lambda b, t: (b, 0, t, 0)))

    outs = [qT(1), keys(1), vT(),
            (jax.ShapeDtypeStruct((B, N_HEADS, S, LANES), F32),
             pl.BlockSpec((1, N_HEADS, TS, LANES), lambda b, t: (b, 0, t, 0))),
            (jax.ShapeDtypeStruct((B, 8, S), F32), pl.BlockSpec((1, 8, TS), lambda b, t: (b, 0, t))),
            qT(1), keys(1), vT(),
            (jax.ShapeDtypeStruct((B, NT, GROUP_W), F32), pl.BlockSpec((1, NT, GROUP_W), lambda b, t: (b, 0, 0))),
            qT(MLA_GROUPS), keys(MLA_GROUPS), vT(),
            rows(GROUP_W), rows(GROUP_W), rows(GROUP_W),
            resid(4), resid(4), resid(4), resid(16), resid(16), resid(16),
            rows(D_MODEL)]
    return pl.pallas_call(
        _front_kernel,
        out_shape=tuple(o[0] for o in outs),
        grid=(B, NT // FRONT_CHAINS),
        in_specs=in_specs,
        out_specs=tuple(o[1] for o in outs),
        scratch_shapes=[pltpu.VMEM((D_MODEL, PROJ_W), BF16), pltpu.VMEM((1, LANES), F32)]
        + [pltpu.VMEM((FRONT_CHAINS, GROUP_W // LANES, T, LANES), F32)] * 3,
        compiler_params=_compiler_params(("arbitrary", "arbitrary")),
        name="front",
    )(x3, lp["ln_g"], w_in, consts["ekr"],
      lp["gains"], lp["b_forget"], lp["q_norm"], lp["kv_norm"], lp["nope_gain"], lp["rope_gain"],
      lp["w_uq"], lp["w_ukv"], consts["cos"], consts["sin"], consts["g64"], consts["g32"], consts["tri"])


def _stage_head_queries(qT_ref, qm_ref, groups):
    r = lax.broadcasted_iota(jnp.int32, (GROUP_W, TQ), 0)
    for h in range(N_HEADS):
        if groups == 1:
            g, mask = 0, (r >= HEAD_DIM * h) & (r < HEAD_DIM * (h + 1))
        else:
            g, hl = divmod(h, N_HEADS // groups)
            mask = (r >= HEAD_DIM * hl) & (r < HEAD_DIM * (hl + 1))
            for base in (2 * HEAD_DIM, 2 * HEAD_DIM + LANES // 2):
                lo = base + MLA_HALF * hl
                mask = mask | ((r >= lo) & (r < lo + MLA_HALF))
        for sub in range(N_SUB):
            q = qT_ref[0, g, :, sub * TQ:(sub + 1) * TQ]
            qm_ref[sub * N_HEADS + h] = jnp.where(mask, q, jnp.zeros_like(q))


def _flash_sweep(pair_idx, groups, qm_ref, k_ref, vT_ref, m_ref, acc_ref, u_ref, o_ref, score_fn):
    kpos = lax.broadcasted_iota(jnp.int32, (TK, TQ), 0)
    qpos = lax.broadcasted_iota(jnp.int32, (TK, TQ), 1)
    causal = kpos <= qpos
    m_ref[...] = jnp.full(m_ref.shape, NEG, F32)
    acc_ref[...] = jnp.zeros(acc_ref.shape, F32)
    both = tuple(range(N_SUB))

    def scores(j0, nblk, diag_last, subs):
        row0 = pl.multiple_of(j0 * TK, TK)
        kts = [k_ref[0, g, pl.ds(row0, nblk * TK), :] for g in range(groups)]
        pending = []
        for sub in subs:
            for h in range(N_HEADS):
                c = sub * N_HEADS + h
                s = jnp.dot(kts[h // (N_HEADS // groups)], qm_ref[c], preferred_element_type=F32)
                tops, lives, row = [], [], None
                for blk in range(nblk):
                    rows = slice(blk * TK, (blk + 1) * TK)
                    diagonal = diag_last and blk == nblk - 1
                    u, row, live = score_fn(sub, h, s[rows], j0 + blk, diagonal)
                    if diagonal:
                        u = jnp.where(causal, u, NEG)
                    u_ref[c, rows, :] = u
                    top = jnp.max(u, axis=0, keepdims=True)
                    if row is not None:
                        top = top + row
                    if live is not None:
                        top = jnp.where(live, top, NEG)
                    tops.append(top)
                    lives.append(live)
                m_old = m_ref[c]
                m_new = functools.reduce(jnp.maximum, tops, m_old)
                shift = m_new if row is None else m_new - row
                shifts = [shift if live is None else jnp.where(live, shift, -NEG) for live in lives]
                m_ref[c] = m_new
                pending.append((c, h, shifts, jnp.exp2(m_old - m_new)))
        return pending

    def values(j0, pending):
        for c, h, shifts, alpha in pending:
            pv = None
            for blk, shift in enumerate(shifts):
                p = jnp.exp2((u_ref[c, blk * TK:(blk + 1) * TK, :] - shift).astype(BF16))
                term = jnp.dot(vT_ref[0, j0 + blk, h], p, preferred_element_type=F32)
                pv = term if pv is None else pv + term
            acc_ref[c] = alpha * acc_ref[c] + pv

    def common(t, carry):
        values(2 * t, scores(2 * t, 2, False, both))
        return carry

    lax.fori_loop(0, pair_idx, common, 0)
    n_common = N_SUB * pair_idx
    pend_a = scores(n_common, 1, True, (0,))
    pend_b = scores(n_common, 2, True, (1,))
    values(n_common, pend_a)
    values(n_common, pend_b)
    for sub in both:
        outs = []
        for h in range(N_HEADS):
            acc = acc_ref[sub * N_HEADS + h]
            outs.append(acc[:HEAD_DIM] / acc[HEAD_DIM:HEAD_DIM + 1])
        o_ref[0, sub * TQ:(sub + 1) * TQ, :] = jnp.concatenate(outs, axis=0).T.astype(BF16)


def _fox_kernel(qT_ref, k_ref, vT_ref, ck_ref, crow_ref, o_ref, qm_ref, m_ref, acc_ref, u_ref):
    pair_idx = pl.program_id(1)
    _stage_head_queries(qT_ref, qm_ref, 1)

    def score_fn(sub, h, s, j, diagonal):
        ck = ck_ref[0, h, pl.ds(pl.multiple_of(j * TK, TK), TK), :]
        cq = crow_ref[0, h:h + 1, sub * TQ:(sub + 1) * TQ]
        return s - jnp.concatenate([ck] * (TQ // LANES), axis=1), cq, None

    _flash_sweep(pair_idx, 1, qm_ref, k_ref, vT_ref, m_ref, acc_ref, u_ref, o_ref, score_fn)


def _mla_kernel(qT_ref, k_ref, vT_ref, o_ref, qm_ref, m_ref, acc_ref, u_ref):
    pair_idx = pl.program_id(1)
    _stage_head_queries(qT_ref, qm_ref, MLA_GROUPS)
    _flash_sweep(pair_idx, MLA_GROUPS, qm_ref, k_ref, vT_ref, m_ref, acc_ref, u_ref, o_ref,
                 lambda sub, h, s, j, diagonal: (s, None, None))


def _moba_kernel(qT_ref, k_ref, vT_ref, kmean_ref, tb_ref, o_ref, qm_ref, m_ref, acc_ref, u_ref, sel_ref):
    pair_idx = pl.program_id(1)
    _stage_head_queries(qT_ref, qm_ref, 1)
    kmean = kmean_ref[0].astype(BF16)
    nb = kmean.shape[0]
    blk = lax.broadcasted_iota(jnp.int32, (nb, TQ), 0)
    own = [N_SUB * pair_idx + sub for sub in range(N_SUB)]
    for c in range(N_SUB * N_HEADS):
        i = own[c // N_HEADS]
        gate = jnp.dot(kmean, qm_ref[c], preferred_element_type=F32)
        gate = jnp.where(blk < i, gate, NEG)
        rank = jnp.zeros((nb, TQ), jnp.int32)
        for jp in range(nb):
            row = gate[jp:jp + 1, :]
            ahead = (row > gate) | ((row == gate) & (jp < blk))
            rank = rank + ahead.astype(jnp.int32)
        sel_ref[c] = ((rank < MOBA_TOPK) & (blk < i)).astype(F32)

    def score_fn(sub, h, s, j, diagonal):
        u = s + tb_ref[jnp.minimum(own[sub] - j, MOBA_TB_ENTRIES - 1), h]
        live = None if diagonal else sel_ref[sub * N_HEADS + h, pl.ds(j, 1), :] > 0.5
        return u, None, live

    _flash_sweep(pair_idx, 1, qm_ref, k_ref, vT_ref, m_ref, acc_ref, u_ref, o_ref, score_fn)


def _dense_mixer(kernel_fn, name, qT, k, vT, extra_args=(), extra_specs=(), extra_scratch=()):
    B, groups, _, S = qT.shape
    NT = S // TK
    assert TQ == TK and N_SUB == 2 and S % (N_SUB * TQ) == 0
    n_chain = N_SUB * N_HEADS
    in_specs = [pl.BlockSpec((1, groups, GROUP_W, N_SUB * TQ), lambda b, i: (b, 0, 0, i)),
                pl.BlockSpec((1, groups, S, GROUP_W), lambda b, i: (b, 0, 0, 0)),
                pl.BlockSpec((1, NT, N_HEADS, V_ROWS, TK), lambda b, i: (b, 0, 0, 0, 0))] + list(extra_specs)
    scratch = [pltpu.VMEM((n_chain, GROUP_W, TQ), BF16),
               pltpu.VMEM((n_chain, 1, TQ), F32),
               pltpu.VMEM((n_chain, V_ROWS, TQ), F32),
               pltpu.VMEM((n_chain, 2 * TK, TQ), F32)] + list(extra_scratch)
    return pl.pallas_call(
        kernel_fn,
        out_shape=jax.ShapeDtypeStruct((B, S, GROUP_W), BF16),
        grid=(B, S // (N_SUB * TQ)),
        in_specs=in_specs,
        out_specs=pl.BlockSpec((1, N_SUB * TQ, GROUP_W), lambda b, i: (b, i, 0)),
        scratch_shapes=scratch,
        compiler_params=_compiler_params(("arbitrary", "arbitrary")),
        name=name,
    )(qT, k, vT, *extra_args)


def _dil_kernel(q_ref, kp_ref, kc_ref, vp_ref, vc_ref, tb_ref, o_ref, lse_ref):
    n = pl.program_id(2)
    qb = q_ref.shape[2] // DIL_BLOCK
    kj = lax.broadcasted_iota(jnp.int32, (DIL_BLOCK, 2 * DIL_BLOCK), 1)
    qlane = lax.broadcasted_iota(jnp.int32, (DIL_BLOCK, GROUP_W), 1)
    in_head = [(qlane >= HEAD_DIM * h) & (qlane < HEAD_DIM * (h + 1)) for h in range(N_HEADS)]
    for seq, c in [(seq, c) for seq in range(q_ref.shape[1]) for c in range(qb)]:
        rows = slice(c * DIL_BLOCK, (c + 1) * DIL_BLOCK)
        q = q_ref[0, seq, rows, :]
        if c == 0:
            kb = jnp.concatenate([kp_ref[0, seq], kc_ref[0, seq, :DIL_BLOCK, :]], axis=0)
            vb = jnp.concatenate([vp_ref[0, seq], vc_ref[0, seq, :DIL_BLOCK, :]], axis=0)
        else:
            kb = kc_ref[0, seq, (c - 1) * DIL_BLOCK:(c + 1) * DIL_BLOCK, :]
            vb = vc_ref[0, seq, (c - 1) * DIL_BLOCK:(c + 1) * DIL_BLOCK, :]
        q_heads = jnp.concatenate([jnp.where(msk, q, jnp.zeros_like(q)) for msk in in_head], axis=0)
        s = lax.dot_general(q_heads, kb, (((1,), (1,)), ((), ())), preferred_element_type=F32)
        s = s.reshape(N_HEADS, DIL_BLOCK, 2 * DIL_BLOCK)
        s = s + tb_ref[0]
        if c == 0:
            s = jnp.where(((kj >= DIL_BLOCK) | (n > 0))[None], s, NEG)
        m = jnp.max(s, axis=-1, keepdims=True)
        e = jnp.exp(s - m)
        l = jnp.sum(e, axis=-1, keepdims=True)
        lse = m + jnp.log(l)
        pr = (e * (1.0 / l)).astype(BF16)
        o_all = jnp.dot(pr.reshape(N_HEADS * DIL_BLOCK, 2 * DIL_BLOCK), vb, preferred_element_type=F32)
        o_all = o_all.reshape(N_HEADS, DIL_BLOCK, GROUP_W)
        o_acc = jnp.zeros((DIL_BLOCK, GROUP_W), F32)
        lse_map = jnp.zeros((DIL_BLOCK, GROUP_W), F32)
        for h in range(N_HEADS):
            o_acc = jnp.where(in_head[h], o_all[h], o_acc)
            lse_map = jnp.where(in_head[h], lse[h], lse_map)
        o_ref[0, seq, rows, :] = o_acc.astype(BF16)
        lse_ref[0, seq, rows, :] = lse_map


def _dilated_pattern(p, q, k, v, dil_tb):
    B, dil, L, _ = q.shape
    assert dil == DIL_PATTERNS[p][1] and L % DIL_BLOCK == 0
    nb = L // DIL_BLOCK
    qb = math.gcd(nb, DIL_BLOCKS_PER_STEP)
    seqs = math.gcd(dil, DIL_BLOCKS_PER_STEP // qb)
    cur = pl.BlockSpec((1, seqs, qb * DIL_BLOCK, GROUP_W), lambda b, r, n: (b, r, n, 0))
    prev = pl.BlockSpec((1, seqs, DIL_BLOCK, GROUP_W), lambda b, r, n: (b, r, jnp.maximum(n * qb - 1, 0), 0))
    return pl.pallas_call(
        _dil_kernel,
        out_shape=(jax.ShapeDtypeStruct((B, dil, L, GROUP_W), BF16),
                   jax.ShapeDtypeStruct((B, dil, L, GROUP_W), F32)),
        grid=(B, dil // seqs, nb // qb),
        in_specs=[cur, prev, cur, prev, cur,
                  pl.BlockSpec((1, N_HEADS, DIL_BLOCK, 2 * DIL_BLOCK), lambda b, r, n, p=p: (p, 0, 0, 0))],
        out_specs=(cur, cur),
        compiler_params=_compiler_params(("arbitrary", "arbitrary", "arbitrary")),
        name=f"dilated_{dil}",
    )(q, k, k, v, v, dil_tb)


def _mix_dilations(o_refs, lse_refs, scratch):
    halves = GROUP_W // LANES
    rows = o_refs[0].shape[1] * o_refs[0].shape[2]
    nat = []
    for ref, dst in zip(list(o_refs) + list(lse_refs), scratch):
        dil = ref.shape[1]
        if dil == 1:
            nat.append(ref[0, 0].astype(F32))
            continue
        for r in range(dil):
            for half in range(halves):
                dst[half, pl.ds(r, rows // dil, stride=dil), :] = (
                    ref[0, r, :, half * LANES:(half + 1) * LANES].astype(F32))
        nat.append(jnp.concatenate([dst[half] for half in range(halves)], axis=1))
    os, lses = nat[:len(o_refs)], nat[len(o_refs):]
    m = functools.reduce(jnp.maximum, lses)
    ws = [jnp.exp(l - m) for l in lses]
    tot = functools.reduce(jnp.add, ws)
    return functools.reduce(jnp.add, [(w / tot) * o for w, o in zip(ws, os)])


def _sigmoid(x):
    return 1.0 / (1.0 + jnp.exp(-x))


def _merge_kernel(x_ref, fo_ref, mo_ref, lo_ref, o1_ref, l1_ref, o4_ref, l4_ref, o16_ref, l16_ref, gate_ref, p_ref,
                  wout_ref, png_ref, wpg_ref, wpp_ref, o_ref, *dil_scratch):
    o_dil = _mix_dilations((o1_ref, o4_ref, o16_ref), (l1_ref, l4_ref, l16_ref), dil_scratch)
    mix = jnp.concatenate([fo_ref[...].astype(F32), mo_ref[...].astype(F32), o_dil, lo_ref[...].astype(F32)],
                          axis=1)
    g = gate_ref[...].astype(F32)
    y = jnp.dot((mix * (g * _sigmoid(g))).astype(BF16), wout_ref[...], preferred_element_type=F32)
    x1 = x_ref[...] + y
    hn = _row_norm(x1, png_ref[...]).astype(BF16)
    pg = _sigmoid(jnp.dot(hn, wpg_ref[...], preferred_element_type=F32))
    pp = jnp.dot(p_ref[0].astype(BF16), wpp_ref[...], preferred_element_type=F32)
    o_ref[...] = x1 + pg * pp


def _merge(layer, x2, fo, mo, lo, dil_outs, proj, p3, lp):
    rows = x2.shape[0]
    TM = TM_MERGE
    tiles_per_seq = dil_outs[0].shape[1] * dil_outs[0].shape[2] // TM

    def rspec(w, c=0):
        return pl.BlockSpec((TM, w), lambda i, c=c: (i, c))

    def cspec(shape):
        return pl.BlockSpec(shape, lambda i: (0, 0))

    def dspec(a):
        dil = a.shape[1]
        return pl.BlockSpec((1, dil, TM // dil, GROUP_W),
                            lambda i: (i // tiles_per_seq, 0, lax.rem(i, tiles_per_seq), 0))

    return pl.pallas_call(
        _merge_kernel,
        out_shape=jax.ShapeDtypeStruct((rows, D_MODEL), F32),
        grid=(rows // TM,),
        scratch_shapes=[pltpu.VMEM((GROUP_W // LANES, TM, LANES), F32)] * len(dil_outs),
        in_specs=[rspec(D_MODEL)] + [rspec(GROUP_W)] * 3 + [dspec(a) for a in dil_outs] + [
                  rspec(D_MODEL),
                  pl.BlockSpec((1, TM, PLE_DIM), lambda i: (layer, i, 0)),
                  cspec((D_MODEL, D_MODEL)), cspec((1, D_MODEL)), cspec((D_MODEL, D_MODEL)),
                  cspec((PLE_DIM, D_MODEL))],
        out_specs=rspec(D_MODEL),
        compiler_params=_compiler_params(("arbitrary",)),
        name="merge",
    )(x2, fo, mo, lo, *dil_outs, proj, p3, lp["w_out"], lp["ple_norm_g"], lp["w_ple_gate"], lp["w_ple_proj"])


def _layer_params(i, ln_g, b_forget, qk_gain, mla_q_norm, mla_kv_norm, mla_nope_gain,
                  mla_rope_gain, w_uq, w_ukv, w_out, ple_norm_g, w_ple_gate, w_ple_proj):
    def rope_slab(x1_parts, x2_parts, like):
        pad = jnp.zeros(like.shape[:-1] + (LANES // 2 - 2 * MLA_HALF,), like.dtype)
        return jnp.concatenate(list(x1_parts) + [pad] + list(x2_parts) + [pad], axis=-1)

    per_head = MLA_NOPE + MLA_ROPE
    uq = w_uq[i]
    uq_nope = [uq[:, h * per_head:h * per_head + MLA_NOPE] for h in range(N_HEADS)]
    uq_x1 = [uq[:, h * per_head + MLA_NOPE:h * per_head + MLA_NOPE + MLA_HALF] for h in range(N_HEADS)]
    uq_x2 = [uq[:, h * per_head + MLA_NOPE + MLA_HALF:(h + 1) * per_head] for h in range(N_HEADS)]
    hpg = N_HEADS // MLA_GROUPS
    uq_cols = uq_nope + [rope_slab(uq_x1[g * hpg:(g + 1) * hpg], uq_x2[g * hpg:(g + 1) * hpg], uq)
                         for g in range(MLA_GROUPS)]
    ukv = w_ukv[i]
    ukv_cols = ([ukv[:, h * 2 * HEAD_DIM:h * 2 * HEAD_DIM + HEAD_DIM] for h in range(N_HEADS)]
                + [ukv[:, h * 2 * HEAD_DIM + HEAD_DIM:(h + 1) * 2 * HEAD_DIM] for h in range(N_HEADS)])
    rg = mla_rope_gain[i]
    rope_gain = rope_slab([rg[:, :MLA_HALF]] * hpg, [rg[:, MLA_HALF:]] * hpg, rg)
    return {
        "ln_g": ln_g[i][None, :],
        "gains": jnp.tile(qk_gain[i], (1, N_HEADS)),
        "b_forget": jnp.pad(b_forget[i], (0, LANES - N_HEADS))[None, :],
        "q_norm": mla_q_norm[i][None, :],
        "kv_norm": mla_kv_norm[i][None, :],
        "nope_gain": jnp.tile(mla_nope_gain[i], (1, N_HEADS)),
        "rope_gain": rope_gain,
        "w_uq": jnp.concatenate(uq_cols, axis=1).astype(BF16),
        "w_ukv": jnp.concatenate(ukv_cols, axis=1).astype(BF16),
        "w_out": w_out[i].astype(BF16),
        "ple_norm_g": ple_norm_g[i][None, :],
        "w_ple_gate": w_ple_gate[i].astype(BF16),
        "w_ple_proj": w_ple_proj[i].astype(BF16),
    }


def _constants(S):
    inv = 1.0 / (ROPE_THETA ** (jnp.arange(MLA_HALF, dtype=F32) * 2.0 / MLA_ROPE))
    ang = jnp.arange(S).astype(F32)[:, None] * inv[None, :]
    cos = jnp.tile(jnp.cos(ang), (1, 2 * N_HEADS))
    sin = jnp.tile(jnp.sin(ang), (1, 2 * N_HEADS))
    sign = np.concatenate([-np.ones(LANES // 2, np.float32), np.ones(LANES // 2, np.float32)])
    lane = np.arange(GROUP_W)
    g64 = (lane[:, None] // HEAD_DIM == lane[None, :] // HEAD_DIM).astype(np.float32) / HEAD_DIM
    within = np.arange(LANES) % (LANES // 2)
    rope_head = np.where(within < 2 * MLA_HALF, within // MLA_HALF, -1)
    g32 = ((rope_head[:, None] == rope_head[None, :]) & (rope_head[:, None] >= 0)).astype(np.float32) / MLA_ROPE
    tri = np.tril(np.ones((T_PREP, T_PREP), np.float32))
    ekr = np.zeros((LANES, LANES), np.float32)
    kr_lane0 = IN_OFF["kr"] % LANES
    for c in range(MLA_ROPE):
        half, idx = divmod(c, MLA_HALF)
        for copy in range(N_HEADS // MLA_GROUPS):
            ekr[kr_lane0 + c, half * (LANES // 2) + copy * MLA_HALF + idx] = 1.0
    return {"cos": cos, "sin": sin * sign[None, :], "g64": jnp.asarray(g64, BF16),
            "g32": jnp.asarray(g32, BF16), "tri": jnp.asarray(tri, BF16), "ekr": jnp.asarray(ekr, BF16)}


def kernel(x, p, ln_g, w_in, b_forget, qk_gain, mla_q_norm, mla_kv_norm, mla_nope_gain, mla_rope_gain,
           w_uq, w_ukv, w_out, rel_bias, ple_norm_g, w_ple_gate, w_ple_proj):
    B, S, _ = x.shape
    depth = p.shape[0]
    consts = _constants(S)
    moba_tb, dil_tb = _bias_tables(rel_bias)
    x2 = x.reshape(B * S, D_MODEL)
    for i in range(depth):
        lp = _layer_params(i, ln_g, b_forget, qk_gain, mla_q_norm, mla_kv_norm, mla_nope_gain,
                           mla_rope_gain, w_uq, w_ukv, w_out, ple_norm_g, w_ple_gate, w_ple_proj)
        (fqT, fk, fvT, fck, fcrow, mqT, mk, mvT, mkmean, lqT, lk, lvT, *dil_qkv, gate) = _front(
            i, x2.reshape(B, S, D_MODEL), w_in, consts, lp)
        S_ = S
        fo = _dense_mixer(
            _fox_kernel, "fox", fqT, fk, fvT, (fck, fcrow),
            (pl.BlockSpec((1, N_HEADS, S_, LANES), lambda b, i: (b, 0, 0, 0)),
             pl.BlockSpec((1, 8, N_SUB * TQ), lambda b, i: (b, 0, i))))
        mo = _dense_mixer(
            _moba_kernel, "moba", mqT, mk, mvT, (mkmean, moba_tb),
            (pl.BlockSpec((1, S_ // MOBA_BLOCK, GROUP_W), lambda b, i: (b, 0, 0)),
             pl.BlockSpec((MOBA_TB_ENTRIES, N_HEADS, MOBA_BLOCK, MOBA_BLOCK), lambda b, i: (0, 0, 0, 0))),
            (pltpu.VMEM((N_SUB * N_HEADS, S_ // MOBA_BLOCK, TQ), F32),))
        lo = _dense_mixer(_mla_kernel, "mla", lqT, lk, lvT)
        dil_outs = []
        for pat, (_, dil) in enumerate(DIL_PATTERNS):
            q, k, v = (a.reshape(B, dil, S // dil, GROUP_W) for a in dil_qkv[3 * pat:3 * pat + 3])
            dil_outs += list(_dilated_pattern(pat, q, k, v, dil_tb))
        rs = lambda a: a.reshape(B * S, GROUP_W)
        x2 = _merge(i, x2, rs(fo), rs(mo), rs(lo), dil_outs, gate.reshape(B * S, D_MODEL),
                    p.reshape(depth, B * S, PLE_DIM), lp)
    return x2.reshape(B, S, D_MODEL)
```

```python
import functools
import math

import numpy as np
import jax
import jax.numpy as jnp
from jax import lax
from jax.experimental import pallas as pl
from jax.experimental.pallas import tpu as pltpu

F32 = jnp.float32
BF16 = jnp.bfloat16

D_MODEL = 1024
N_HEADS = 4
HEAD_DIM = 64
GROUP_W = N_HEADS * HEAD_DIM
MLA_NOPE = 64
MLA_ROPE = 32
MLA_HALF = MLA_ROPE // 2
MLA_QK_W = N_HEADS * (MLA_NOPE + MLA_ROPE)
MLA_Q_LORA = 256
MLA_KV_LORA = 128
PLE_DIM = 256
MOBA_BLOCK = 256
MOBA_TOPK = 3
DIL_PATTERNS = ((128, 1), (512, 4), (2048, 16))
DIL_BLOCK = 128
DIL_BLOCKS_PER_STEP = 16
N_BUCKETS = 32
MAX_DISTANCE = 2048
ROPE_THETA = 10000.0
EPS = 1e-6
NEG = -1e30
LOG2E = math.log2(math.e)

LANES = 128
VMEM_LIMIT_BYTES = 56 * 1024 * 1024

COL_FOX, COL_MOBA, COL_DIL = 0, 768, 1536
COL_CQ, COL_CKV, COL_KR, COL_FF, COL_GATE = 2304, 2560, 2688, 2816, 3072
PROJ_W = 4096
PROJ_TN = 512
STAGE_ROWS = 128

IN_OFF = {}
IN_W = 0
for _name, _size in (("fq", GROUP_W), ("fk", GROUP_W), ("fv", GROUP_W), ("ff", N_HEADS),
                     ("mq", GROUP_W), ("mk", GROUP_W), ("mv", GROUP_W),
                     ("dq", GROUP_W), ("dk", GROUP_W), ("dv", GROUP_W),
                     ("cq", MLA_Q_LORA), ("ckv", MLA_KV_LORA), ("kr", MLA_ROPE), ("gate", D_MODEL)):
    IN_OFF[_name] = IN_W
    IN_W += _size

TM_PROJ = 512
T_PREP = 256
FRONT_CHAINS = 2
TQ = 256
TK = 256
N_SUB = 2
COMMON_BLOCKS = 4
MOBA_TB_ENTRIES = 8
TM_MERGE = 1024
V_ROWS = HEAD_DIM + 16
MLA_GROUPS = 2


def _bucket_np(d):
    d = np.maximum(np.asarray(d, np.int64), 0)
    max_exact = N_BUCKETS // 2
    d_f = np.maximum(d, 1).astype(np.float64)
    val = np.log(d_f / max_exact) / math.log(MAX_DISTANCE / max_exact) * (N_BUCKETS - max_exact)
    frac = np.abs(val - np.round(val))
    on_edge = (frac < 1e-5) & (d > max_exact) & (val < N_BUCKETS - max_exact - 0.5)
    assert not on_edge.any(), "distance too close to a bucket edge for a static table"
    large = np.minimum(max_exact + np.floor(val + 1e-9).astype(np.int64), N_BUCKETS - 1)
    return np.where(d < max_exact, d, large)


def _bucket_steps(lo, hi):
    ds = np.arange(lo, hi + 1)
    bs = _bucket_np(ds)
    steps = [(int(ds[i]), int(bs[i])) for i in range(1, len(ds)) if bs[i] != bs[i - 1]]
    return int(bs[0]), steps


def _compiler_params(sem):
    return pltpu.CompilerParams(dimension_semantics=sem, vmem_limit_bytes=VMEM_LIMIT_BYTES)


def _bias_from_steps(d, rb_ref, col, lo, hi):
    b0, steps = _bucket_steps(lo, hi)
    val = jnp.full(d.shape, rb_ref[b0, col], F32)
    for t, b in steps:
        val = jnp.where(d >= t, rb_ref[b, col], val)
    return val


def _bias_tables_kernel(rb_ref, moba_ref, dil_ref):
    kl = lax.broadcasted_iota(jnp.int32, (MOBA_BLOCK, MOBA_BLOCK), 0)
    ql = lax.broadcasted_iota(jnp.int32, (MOBA_BLOCK, MOBA_BLOCK), 1)
    for e in range(MOBA_TB_ENTRIES - 1):
        d = jnp.maximum(e * MOBA_BLOCK + ql - kl, 0)
        lo = max(e * MOBA_BLOCK - (MOBA_BLOCK - 1), 0)
        hi = e * MOBA_BLOCK + (MOBA_BLOCK - 1)
        for h in range(N_HEADS):
            moba_ref[e, h] = _bias_from_steps(d, rb_ref, h, lo, hi) * LOG2E
    for h in range(N_HEADS):
        moba_ref[MOBA_TB_ENTRIES - 1, h] = jnp.full((MOBA_BLOCK, MOBA_BLOCK), rb_ref[N_BUCKETS - 1, h], F32) * LOG2E
    qi = lax.broadcasted_iota(jnp.int32, (DIL_BLOCK, 2 * DIL_BLOCK), 0)
    kj = lax.broadcasted_iota(jnp.int32, (DIL_BLOCK, 2 * DIL_BLOCK), 1)
    rel = qi + DIL_BLOCK - kj
    for p, (_, dil) in enumerate(DIL_PATTERNS):
        d = jnp.maximum(rel * dil, 0)
        for h in range(N_HEADS):
            bias = _bias_from_steps(d, rb_ref, N_HEADS + h, 0, (2 * DIL_BLOCK - 1) * dil)
            dil_ref[p, h] = jnp.where((rel >= 0) & (rel <= DIL_BLOCK), bias, NEG)


def _bias_tables(rel_bias):
    assert _bucket_np((MOBA_TB_ENTRIES - 1) * MOBA_BLOCK - (MOBA_BLOCK - 1)) == N_BUCKETS - 1
    return pl.pallas_call(
        _bias_tables_kernel,
        out_shape=(jax.ShapeDtypeStruct((MOBA_TB_ENTRIES, N_HEADS, MOBA_BLOCK, MOBA_BLOCK), F32),
                   jax.ShapeDtypeStruct((len(DIL_PATTERNS), N_HEADS, DIL_BLOCK, 2 * DIL_BLOCK), F32)),
        in_specs=[pl.BlockSpec(memory_space=pltpu.SMEM)],
        out_specs=(pl.BlockSpec(memory_space=pltpu.VMEM), pl.BlockSpec(memory_space=pltpu.VMEM)),
        compiler_params=pltpu.CompilerParams(vmem_limit_bytes=VMEM_LIMIT_BYTES),
        name="bias_tables",
    )(rel_bias)


def _stage_in_weights(w_ref, ekr_ref, ws_ref):
    W = GROUP_W
    moves = ((COL_FOX, IN_OFF["fq"], 3 * W), (COL_MOBA, IN_OFF["mq"], 3 * W), (COL_DIL, IN_OFF["dq"], 3 * W),
             (COL_CQ, IN_OFF["cq"], MLA_Q_LORA), (COL_CKV, IN_OFF["ckv"], MLA_KV_LORA),
             (COL_GATE, IN_OFF["gate"], D_MODEL))
    kr_base = IN_OFF["kr"] // LANES * LANES
    assert IN_OFF["ff"] % LANES == 0 and IN_OFF["kr"] + MLA_ROPE <= kr_base + LANES
    lane = lax.broadcasted_iota(jnp.int32, (STAGE_ROWS, LANES), 1)
    for r0 in range(0, D_MODEL, STAGE_ROWS):
        rows = slice(r0, r0 + STAGE_ROWS)
        for dst, src, width in moves:
            ws_ref[rows, dst:dst + width] = w_ref[0, rows, src:src + width].astype(BF16)
        kr_win = w_ref[0, rows, kr_base:kr_base + LANES].astype(BF16)
        ws_ref[rows, COL_KR:COL_KR + LANES] = jnp.dot(kr_win, ekr_ref[...], preferred_element_type=F32).astype(BF16)
        ff_win = w_ref[0, rows, IN_OFF["ff"]:IN_OFF["ff"] + LANES]
        ws_ref[rows, COL_FF:COL_FF + LANES] = jnp.where(lane < N_HEADS, ff_win, 0.0).astype(BF16)
        ws_ref[rows, COL_FF + LANES:COL_GATE] = jnp.zeros((STAGE_ROWS, COL_GATE - COL_FF - LANES), BF16)


def _group_mean_sq(x, g_mat):
    return jnp.dot((x * x).astype(BF16), g_mat, preferred_element_type=F32)


def _group_norm(x, gain, g_mat):
    return x * lax.rsqrt(_group_mean_sq(x, g_mat) + EPS) * gain


def _row_norm(x, gain):
    return x * lax.rsqrt(jnp.mean(x * x, axis=-1, keepdims=True) + EPS) * gain


def _log_sigmoid(x):
    return -(jnp.maximum(-x, 0.0) + jnp.log1p(jnp.exp(-jnp.abs(x))))


def _front_kernel(x_ref, lng_ref, w_ref, ekr_ref,
                  gains_ref, bfor_ref, qng_ref, kvng_ref, nopeg_ref, ropeg_ref,
                  wuq_ref, wukv_ref, cos_ref, sin_ref, g64_ref, g32_ref, tri_ref,
                  fqT_ref, fk_ref, fvT_ref, fck_ref, fcrow_ref,
                  mqT_ref, mk_ref, mvT_ref, mkmean_ref,
                  lqT_ref, lk_ref, lvT_ref,
                  dq1_ref, dk1_ref, dv1_ref, dq4_ref, dk4_ref, dv4_ref, dq16_ref, dk16_ref, dv16_ref,
                  gate_ref,
                  ws_ref, carry_ref, sq_ref, sk_ref, sv_ref):
    t = pl.program_id(1)

    @pl.when((pl.program_id(0) == 0) & (t == 0))
    def _():
        _stage_in_weights(w_ref, ekr_ref, ws_ref)

    g64 = g64_ref[...]
    scale = HEAD_DIM ** -0.5
    qscale = (MLA_NOPE + MLA_ROPE) ** -0.5
    W = GROUP_W
    T = T_PREP
    ones_rows = jnp.ones((V_ROWS - HEAD_DIM, T), BF16)

    @pl.when(t == 0)
    def _():
        carry_ref[...] = jnp.zeros_like(carry_ref)

    for ch in range(FRONT_CHAINS):
        rows = slice(ch * T, (ch + 1) * T)

        def store_vT(dst, v):
            vT = v.T.astype(BF16)
            for h in range(N_HEADS):
                dst[0, ch, h] = jnp.concatenate([vT[HEAD_DIM * h:HEAD_DIM * (h + 1)], ones_rows], axis=0)

        x = x_ref[0, rows, :]
        hn = (x * lax.rsqrt(jnp.mean(x * x, axis=-1, keepdims=True) + EPS) * lng_ref[...]).astype(BF16)
        project = lambda lo, width: jnp.dot(hn, ws_ref[:, lo:lo + width], preferred_element_type=F32)
        pf = project(COL_FOX, 3 * W)
        p_ff = project(COL_FF, LANES)
        pm = project(COL_MOBA, 3 * W)

        fq = _group_norm(pf[:, :W], gains_ref[0:1, :], g64) * (scale * LOG2E)
        fk = _group_norm(pf[:, W:2 * W], gains_ref[1:2, :], g64)
        fqT_ref[0, 0, :, rows] = fq.T.astype(BF16)
        fk_ref[0, 0, rows, :] = fk.astype(BF16)
        store_vT(fvT_ref, pf[:, 2 * W:])
        log_f = _log_sigmoid(p_ff + bfor_ref[...])
        tri = tri_ref[...]
        c = carry_ref[...]
        rest = log_f
        for _ in range(3):
            part = rest.astype(BF16)
            c = c + jnp.dot(tri, part, preferred_element_type=F32)
            rest = rest - part.astype(F32)
        carry_ref[...] = c[T - 1:T, :]
        c2 = c * LOG2E
        for h in range(N_HEADS):
            fck_ref[0, h, rows, :] = jnp.broadcast_to(c2[:, h:h + 1], (T, LANES))
        fcrow_ref[0, :, rows] = c2.T[0:8, :]

        pd = project(COL_DIL, 3 * W)
        mq = _group_norm(pm[:, :W], gains_ref[2:3, :], g64) * (scale * LOG2E)
        mk = _group_norm(pm[:, W:2 * W], gains_ref[3:4, :], g64)
        mqT_ref[0, 0, :, rows] = mq.T.astype(BF16)
        mk_ref[0, 0, rows, :] = mk.astype(BF16)
        store_vT(mvT_ref, pm[:, 2 * W:])
        mkmean_ref[0, pl.ds(FRONT_CHAINS * t + ch, 1), :] = jnp.mean(mk, axis=0, keepdims=True)

        p_mla = project(COL_CQ, MLA_Q_LORA + MLA_KV_LORA + LANES)
        dq = _group_norm(pd[:, :W], gains_ref[4:5, :], g64) * scale
        dk = _group_norm(pd[:, W:2 * W], gains_ref[5:6, :], g64)
        dv = pd[:, 2 * W:]
        dq1_ref[0, rows, :] = dq.astype(BF16)
        dk1_ref[0, rows, :] = dk.astype(BF16)
        dv1_ref[0, rows, :] = dv.astype(BF16)
        for val, dst in ((dq, sq_ref), (dk, sk_ref), (dv, sv_ref)):
            for half in range(GROUP_W // LANES):
                dst[ch, half] = val[:, half * LANES:(half + 1) * LANES]
        for dil, outs in ((4, (dq4_ref, dk4_ref, dv4_ref)), (16, (dq16_ref, dk16_ref, dv16_ref))):
            n = T // dil
            for r in range(dil):
                for src, dst in zip((sq_ref, sk_ref, sv_ref), outs):
                    dst[0, r, ch * n:(ch + 1) * n, :] = jnp.concatenate(
                        [src[ch, half, pl.ds(r, n, stride=dil), :] for half in range(GROUP_W // LANES)],
                        axis=1).astype(BF16)

        half_gate = D_MODEL // 2
        gate_ref[0, rows, :half_gate] = project(COL_GATE, half_gate).astype(BF16)
        cos = cos_ref[rows, :]
        sin = sin_ref[rows, :]
        cqn = _row_norm(p_mla[:, :MLA_Q_LORA], qng_ref[...]).astype(BF16)
        rotate = lambda v, cos=cos, sin=sin: v * cos + pltpu.roll(v, LANES // 2, 1) * sin
        qf = jnp.dot(cqn, wuq_ref[...], preferred_element_type=F32)
        q_nope = _group_norm(qf[:, :W], nopeg_ref[0:1, :], g64) * (qscale * LOG2E)
        ckvn = _row_norm(p_mla[:, MLA_Q_LORA:MLA_Q_LORA + MLA_KV_LORA], kvng_ref[...]).astype(BF16)
        kvf = jnp.dot(ckvn, wukv_ref[...], preferred_element_type=F32)
        k_nope = _group_norm(kvf[:, :W], nopeg_ref[1:2, :], g64)
        krs = p_mla[:, MLA_Q_LORA + MLA_KV_LORA:]
        kr_ms = jnp.sum(krs * krs, axis=-1, keepdims=True) * (1.0 / (2 * MLA_ROPE))
        kr = rotate(krs * lax.rsqrt(kr_ms + EPS) * ropeg_ref[1:2, :])
        for g in range(MLA_GROUPS):
            slab = qf[:, W + g * LANES:W + (g + 1) * LANES]
            qr = rotate(_group_norm(slab, ropeg_ref[0:1, :], g32_ref[...])) * (qscale * LOG2E)
            lq = jnp.concatenate([q_nope[:, g * LANES:(g + 1) * LANES], qr], axis=1)
            lqT_ref[0, g, :, rows] = lq.T.astype(BF16)
            lk_ref[0, g, rows, :] = jnp.concatenate([k_nope[:, g * LANES:(g + 1) * LANES], kr],
                                                    axis=1).astype(BF16)
        store_vT(lvT_ref, kvf[:, W:])
        gate_ref[0, rows, half_gate:] = project(COL_GATE + half_gate, half_gate).astype(BF16)


def _front(layer, x3, w_in, consts, lp):
    B, S, _ = x3.shape
    T = T_PREP
    TS = FRONT_CHAINS * T
    NT = S // T
    assert S % TS == 0 and S // MOBA_BLOCK == NT

    def cspec(shape):
        return pl.BlockSpec(shape, lambda b, t: (0,) * len(shape))

    in_specs = [pl.BlockSpec((1, TS, D_MODEL), lambda b, t: (b, t, 0)), cspec((1, D_MODEL)),
                pl.BlockSpec((1, D_MODEL, IN_W), lambda b, t: (layer, 0, 0), pipeline_mode=pl.Buffered(1)),
                cspec((LANES, LANES)),
                cspec((6, GROUP_W)), cspec((1, LANES)), cspec((1, MLA_Q_LORA)), cspec((1, MLA_KV_LORA)),
                cspec((2, GROUP_W)), cspec((2, LANES)),
                cspec((MLA_Q_LORA, MLA_GROUPS * GROUP_W)), cspec((MLA_KV_LORA, 2 * GROUP_W)),
                pl.BlockSpec((TS, LANES), lambda b, t: (t, 0)), pl.BlockSpec((TS, LANES), lambda b, t: (t, 0)),
                cspec((GROUP_W, GROUP_W)), cspec((LANES, LANES)), cspec((T, T))]

    def qT(groups):
        return (jax.ShapeDtypeStruct((B, groups, GROUP_W, S), BF16),
                pl.BlockSpec((1, groups, GROUP_W, TS), lambda b, t: (b, 0, 0, t)))

    def keys(groups):
        return (jax.ShapeDtypeStruct((B, groups, S, GROUP_W), BF16),
                pl.BlockSpec((1, groups, TS, GROUP_W), lambda b, t: (b, 0, t, 0)))

    def rows(w):
        return jax.ShapeDtypeStruct((B, S, w), BF16), pl.BlockSpec((1, TS, w), lambda b, t: (b, t, 0))

    def vT():
        return (jax.ShapeDtypeStruct((B, NT, N_HEADS, V_ROWS, T), BF16),
                pl.BlockSpec((1, FRONT_CHAINS, N_HEADS, V_ROWS, T), lambda b, t: (b, t, 0, 0, 0)))

    def resid(dil):
        return (jax.ShapeDtypeStruct((B, dil, S // dil, GROUP_W), BF16),
                pl.BlockSpec((1, dil, TS // dil, GROUP_W), lambda b, t: (b, 0, t, 0)))

    outs = [qT(1), keys(1), vT(),
            (jax.ShapeDtypeStruct((B, N_HEADS, S, LANES), F32),
             pl.BlockSpec((1, N_HEADS, TS, LANES), lambda b, t: (b, 0, t, 0))),
            (jax.ShapeDtypeStruct((B, 8, S), F32), pl.BlockSpec((1, 8, TS), lambda b, t: (b, 0, t))),
            qT(1), keys(1), vT(),
            (jax.ShapeDtypeStruct((B, NT, GROUP_W), F32), pl.BlockSpec((1, NT, GROUP_W), lambda b, t: (b, 0, 0))),
            qT(MLA_GROUPS), keys(MLA_GROUPS), vT(),
            rows(GROUP_W), rows(GROUP_W), rows(GROUP_W),
            resid(4), resid(4), resid(4), resid(16), resid(16), resid(16),
            rows(D_MODEL)]
    return pl.pallas_call(
        _front_kernel,
        out_shape=tuple(o[0] for o in outs),
        grid=(B, NT // FRONT_CHAINS),
        in_specs=in_specs,
        out_specs=tuple(o[1] for o in outs),
        scratch_shapes=[pltpu.VMEM((D_MODEL, PROJ_W), BF16), pltpu.VMEM((1, LANES), F32)]
        + [pltpu.VMEM((FRONT_CHAINS, GROUP_W // LANES, T, LANES), F32)] * 3,
        compiler_params=_compiler_params(("arbitrary", "arbitrary")),
        name="front",
    )(x3, lp["ln_g"], w_in, consts["ekr"],
      lp["gains"], lp["b_forget"], lp["q_norm"], lp["kv_norm"], lp["nope_gain"], lp["rope_gain"],
      lp["w_uq"], lp["w_ukv"], consts["cos"], consts["sin"], consts["g64"], consts["g32"], consts["tri"])


def _stage_head_queries(qT_ref, qm_ref, groups):
    r = lax.broadcasted_iota(jnp.int32, (GROUP_W, TQ), 0)
    for h in range(N_HEADS):
        if groups == 1:
            g, mask = 0, (r >= HEAD_DIM * h) & (r < HEAD_DIM * (h + 1))
        else:
            g, hl = divmod(h, N_HEADS // groups)
            mask = (r >= HEAD_DIM * hl) & (r < HEAD_DIM * (hl + 1))
            for base in (2 * HEAD_DIM, 2 * HEAD_DIM + LANES // 2):
                lo = base + MLA_HALF * hl
                mask = mask | ((r >= lo) & (r < lo + MLA_HALF))
        for sub in range(N_SUB):
            q = qT_ref[0, g, :, sub * TQ:(sub + 1) * TQ]
            qm_ref[sub * N_HEADS + h] = jnp.where(mask, q, jnp.zeros_like(q))


def _flash_sweep(pair_idx, groups, qm_ref, k_ref, vT_ref, m_ref, acc_ref, u_ref, o_ref, score_fn):
    kpos = lax.broadcasted_iota(jnp.int32, (TK, TQ), 0)
    qpos = lax.broadcasted_iota(jnp.int32, (TK, TQ), 1)
    causal = kpos <= qpos
    m_ref[...] = jnp.full(m_ref.shape, NEG, F32)
    acc_ref[...] = jnp.zeros(acc_ref.shape, F32)
    both = tuple(range(N_SUB))

    def scores(j0, nblk, diag_last, subs):
        row0 = pl.multiple_of(j0 * TK, TK)
        kts = [k_ref[0, g, pl.ds(row0, nblk * TK), :] for g in range(groups)]
        pending = []
        for sub in subs:
            for h in range(N_HEADS):
                c = sub * N_HEADS + h
                s = jnp.dot(kts[h // (N_HEADS // groups)], qm_ref[c], preferred_element_type=F32)
                tops, lives, row = [], [], None
                for blk in range(nblk):
                    rows = slice(blk * TK, (blk + 1) * TK)
                    diagonal = diag_last and blk == nblk - 1
                    u, row, live = score_fn(sub, h, s[rows], j0 + blk, diagonal)
                    if diagonal:
                        u = jnp.where(causal, u, NEG)
                    u_ref[c, rows, :] = u
                    top = jnp.max(u, axis=0, keepdims=True)
                    if row is not None:
                        top = top + row
                    if live is not None:
                        top = jnp.where(live, top, NEG)
                    tops.append(top)
                    lives.append(live)
                m_old = m_ref[c]
                m_new = functools.reduce(jnp.maximum, tops, m_old)
                shift = m_new if row is None else m_new - row
                shifts = [shift if live is None else jnp.where(live, shift, -NEG) for live in lives]
                m_ref[c] = m_new
                pending.append((c, h, shifts, jnp.exp2(m_old - m_new)))
        return pending

    def values(j0, pending):
        for c, h, shifts, alpha in pending:
            pv = None
            for blk, shift in enumerate(shifts):
                p = jnp.exp2((u_ref[c, blk * TK:(blk + 1) * TK, :] - shift).astype(BF16))
                term = jnp.dot(vT_ref[0, j0 + blk, h], p, preferred_element_type=F32)
                pv = term if pv is None else pv + term
            acc_ref[c] = alpha * acc_ref[c] + pv

    def common(j0, nblk):
        values(j0, scores(j0, nblk, False, both))

    def wide_step(t, carry):
        common(COMMON_BLOCKS * t, COMMON_BLOCKS)
        return carry

    n_wide = (N_SUB * pair_idx) // COMMON_BLOCKS
    lax.fori_loop(0, n_wide, wide_step, 0)
    for k in range(COMMON_BLOCKS // N_SUB - 1):
        def leftover(_, carry, k=k):
            common(COMMON_BLOCKS * n_wide + N_SUB * k, N_SUB)
            return carry

        lax.fori_loop(0, (pair_idx - (COMMON_BLOCKS // N_SUB) * n_wide > k).astype(jnp.int32), leftover, 0)
    n_common = N_SUB * pair_idx
    pend_a = scores(n_common, 1, True, (0,))
    pend_b = scores(n_common, 2, True, (1,))
    values(n_common, pend_a)
    values(n_common, pend_b)
    for sub in both:
        outs = []
        for h in range(N_HEADS):
            acc = acc_ref[sub * N_HEADS + h]
            outs.append(acc[:HEAD_DIM] / acc[HEAD_DIM:HEAD_DIM + 1])
        o_ref[0, sub * TQ:(sub + 1) * TQ, :] = jnp.concatenate(outs, axis=0).T.astype(BF16)


def _fox_kernel(qT_ref, k_ref, vT_ref, ck_ref, crow_ref, o_ref, qm_ref, m_ref, acc_ref, u_ref):
    pair_idx = pl.program_id(1)
    _stage_head_queries(qT_ref, qm_ref, 1)

    def score_fn(sub, h, s, j, diagonal):
        ck = ck_ref[0, h, pl.ds(pl.multiple_of(j * TK, TK), TK), :]
        cq = crow_ref[0, h:h + 1, sub * TQ:(sub + 1) * TQ]
        return s - jnp.concatenate([ck] * (TQ // LANES), axis=1), cq, None

    _flash_sweep(pair_idx, 1, qm_ref, k_ref, vT_ref, m_ref, acc_ref, u_ref, o_ref, score_fn)


def _mla_kernel(qT_ref, k_ref, vT_ref, o_ref, qm_ref, m_ref, acc_ref, u_ref):
    pair_idx = pl.program_id(1)
    _stage_head_queries(qT_ref, qm_ref, MLA_GROUPS)
    _flash_sweep(pair_idx, MLA_GROUPS, qm_ref, k_ref, vT_ref, m_ref, acc_ref, u_ref, o_ref,
                 lambda sub, h, s, j, diagonal: (s, None, None))


def _moba_kernel(qT_ref, k_ref, vT_ref, kmean_ref, tb_ref, o_ref, qm_ref, m_ref, acc_ref, u_ref, sel_ref):
    pair_idx = pl.program_id(1)
    _stage_head_queries(qT_ref, qm_ref, 1)
    kmean = kmean_ref[0].astype(BF16)
    nb = kmean.shape[0]
    blk = lax.broadcasted_iota(jnp.int32, (nb, TQ), 0)
    own = [N_SUB * pair_idx + sub for sub in range(N_SUB)]
    for c in range(N_SUB * N_HEADS):
        i = own[c // N_HEADS]
        gate = jnp.dot(kmean, qm_ref[c], preferred_element_type=F32)
        gate = jnp.where(blk < i, gate, NEG)
        rank = jnp.zeros((nb, TQ), jnp.int32)
        for jp in range(nb):
            row = gate[jp:jp + 1, :]
            ahead = (row > gate) | ((row == gate) & (jp < blk))
            rank = rank + ahead.astype(jnp.int32)
        sel_ref[c] = ((rank < MOBA_TOPK) & (blk < i)).astype(F32)

    def score_fn(sub, h, s, j, diagonal):
        u = s + tb_ref[jnp.minimum(own[sub] - j, MOBA_TB_ENTRIES - 1), h]
        live = None if diagonal else sel_ref[sub * N_HEADS + h, pl.ds(j, 1), :] > 0.5
        return u, None, live

    _flash_sweep(pair_idx, 1, qm_ref, k_ref, vT_ref, m_ref, acc_ref, u_ref, o_ref, score_fn)


def _dense_mixer(kernel_fn, name, qT, k, vT, extra_args=(), extra_specs=(), extra_scratch=()):
    B, groups, _, S = qT.shape
    NT = S // TK
    assert TQ == TK and N_SUB == 2 and S % (N_SUB * TQ) == 0
    n_chain = N_SUB * N_HEADS
    in_specs = [pl.BlockSpec((1, groups, GROUP_W, N_SUB * TQ), lambda b, i: (b, 0, 0, i)),
                pl.BlockSpec((1, groups, S, GROUP_W), lambda b, i: (b, 0, 0, 0)),
                pl.BlockSpec((1, NT, N_HEADS, V_ROWS, TK), lambda b, i: (b, 0, 0, 0, 0))] + list(extra_specs)
    scratch = [pltpu.VMEM((n_chain, GROUP_W, TQ), BF16),
               pltpu.VMEM((n_chain, 1, TQ), F32),
               pltpu.VMEM((n_chain, V_ROWS, TQ), F32),
               pltpu.VMEM((n_chain, COMMON_BLOCKS * TK, TQ), F32)] + list(extra_scratch)
    return pl.pallas_call(
        kernel_fn,
        out_shape=jax.ShapeDtypeStruct((B, S, GROUP_W), BF16),
        grid=(B, S // (N_SUB * TQ)),
        in_specs=in_specs,
        out_specs=pl.BlockSpec((1, N_SUB * TQ, GROUP_W), lambda b, i: (b, i, 0)),
        scratch_shapes=scratch,
        compiler_params=_compiler_params(("arbitrary", "arbitrary")),
        name=name,
    )(qT, k, vT, *extra_args)


def _dil_kernel(q_ref, kp_ref, kc_ref, vp_ref, vc_ref, tb_ref, o_ref, lse_ref):
    n = pl.program_id(2)
    qb = q_ref.shape[2] // DIL_BLOCK
    kj = lax.broadcasted_iota(jnp.int32, (DIL_BLOCK, 2 * DIL_BLOCK), 1)
    qlane = lax.broadcasted_iota(jnp.int32, (DIL_BLOCK, GROUP_W), 1)
    in_head = [(qlane >= HEAD_DIM * h) & (qlane < HEAD_DIM * (h + 1)) for h in range(N_HEADS)]
    for seq, c in [(seq, c) for seq in range(q_ref.shape[1]) for c in range(qb)]:
        rows = slice(c * DIL_BLOCK, (c + 1) * DIL_BLOCK)
        q = q_ref[0, seq, rows, :]
        if c == 0:
            kb = jnp.concatenate([kp_ref[0, seq], kc_ref[0, seq, :DIL_BLOCK, :]], axis=0)
            vb = jnp.concatenate([vp_ref[0, seq], vc_ref[0, seq, :DIL_BLOCK, :]], axis=0)
        else:
            kb = kc_ref[0, seq, (c - 1) * DIL_BLOCK:(c + 1) * DIL_BLOCK, :]
            vb = vc_ref[0, seq, (c - 1) * DIL_BLOCK:(c + 1) * DIL_BLOCK, :]
        q_heads = jnp.concatenate([jnp.where(msk, q, jnp.zeros_like(q)) for msk in in_head], axis=0)
        s = lax.dot_general(q_heads, kb, (((1,), (1,)), ((), ())), preferred_element_type=F32)
        s = s.reshape(N_HEADS, DIL_BLOCK, 2 * DIL_BLOCK)
        s = s + tb_ref[0]
        if c == 0:
            s = jnp.where(((kj >= DIL_BLOCK) | (n > 0))[None], s, NEG)
        m = jnp.max(s, axis=-1, keepdims=True)
        e = jnp.exp(s - m)
        l = jnp.sum(e, axis=-1, keepdims=True)
        lse = m + jnp.log(l)
        pr = (e * (1.0 / l)).astype(BF16)
        o_all = jnp.dot(pr.reshape(N_HEADS * DIL_BLOCK, 2 * DIL_BLOCK), vb, preferred_element_type=F32)
        o_all = o_all.reshape(N_HEADS, DIL_BLOCK, GROUP_W)
        o_acc = jnp.zeros((DIL_BLOCK, GROUP_W), F32)
        lse_map = jnp.zeros((DIL_BLOCK, GROUP_W), F32)
        for h in range(N_HEADS):
            o_acc = jnp.where(in_head[h], o_all[h], o_acc)
            lse_map = jnp.where(in_head[h], lse[h], lse_map)
        o_ref[0, seq, rows, :] = o_acc.astype(BF16)
        lse_ref[0, seq, rows, :] = lse_map


def _dilated_pattern(p, q, k, v, dil_tb):
    B, dil, L, _ = q.shape
    assert dil == DIL_PATTERNS[p][1] and L % DIL_BLOCK == 0
    nb = L // DIL_BLOCK
    qb = math.gcd(nb, DIL_BLOCKS_PER_STEP)
    seqs = math.gcd(dil, DIL_BLOCKS_PER_STEP // qb)
    cur = pl.BlockSpec((1, seqs, qb * DIL_BLOCK, GROUP_W), lambda b, r, n: (b, r, n, 0))
    prev = pl.BlockSpec((1, seqs, DIL_BLOCK, GROUP_W), lambda b, r, n: (b, r, jnp.maximum(n * qb - 1, 0), 0))
    return pl.pallas_call(
        _dil_kernel,
        out_shape=(jax.ShapeDtypeStruct((B, dil, L, GROUP_W), BF16),
                   jax.ShapeDtypeStruct((B, dil, L, GROUP_W), F32)),
        grid=(B, dil // seqs, nb // qb),
        in_specs=[cur, prev, cur, prev, cur,
                  pl.BlockSpec((1, N_HEADS, DIL_BLOCK, 2 * DIL_BLOCK), lambda b, r, n, p=p: (p, 0, 0, 0))],
        out_specs=(cur, cur),
        compiler_params=_compiler_params(("arbitrary", "arbitrary", "arbitrary")),
        name=f"dilated_{dil}",
    )(q, k, k, v, v, dil_tb)


def _mix_dilations(o_refs, lse_refs, scratch):
    halves = GROUP_W // LANES
    rows = o_refs[0].shape[1] * o_refs[0].shape[2]
    nat = []
    for ref, dst in zip(list(o_refs) + list(lse_refs), scratch):
        dil = ref.shape[1]
        if dil == 1:
            nat.append(ref[0, 0].astype(F32))
            continue
        for r in range(dil):
            for half in range(halves):
                dst[half, pl.ds(r, rows // dil, stride=dil), :] = (
                    ref[0, r, :, half * LANES:(half + 1) * LANES].astype(F32))
        nat.append(jnp.concatenate([dst[half] for half in range(halves)], axis=1))
    os, lses = nat[:len(o_refs)], nat[len(o_refs):]
    m = functools.reduce(jnp.maximum, lses)
    ws = [jnp.exp(l - m) for l in lses]
    tot = functools.reduce(jnp.add, ws)
    return functools.reduce(jnp.add, [(w / tot) * o for w, o in zip(ws, os)])


def _sigmoid(x):
    return 1.0 / (1.0 + jnp.exp(-x))


def _merge_kernel(x_ref, fo_ref, mo_ref, lo_ref, o1_ref, l1_ref, o4_ref, l4_ref, o16_ref, l16_ref, gate_ref, p_ref,
                  wout_ref, png_ref, wpg_ref, wpp_ref, o_ref, *dil_scratch):
    o_dil = _mix_dilations((o1_ref, o4_ref, o16_ref), (l1_ref, l4_ref, l16_ref), dil_scratch)
    mix = jnp.concatenate([fo_ref[...].astype(F32), mo_ref[...].astype(F32), o_dil, lo_ref[...].astype(F32)],
                          axis=1)
    g = gate_ref[...].astype(F32)
    y = jnp.dot((mix * (g * _sigmoid(g))).astype(BF16), wout_ref[...], preferred_element_type=F32)
    x1 = x_ref[...] + y
    hn = _row_norm(x1, png_ref[...]).astype(BF16)
    pg = _sigmoid(jnp.dot(hn, wpg_ref[...], preferred_element_type=F32))
    pp = jnp.dot(p_ref[0].astype(BF16), wpp_ref[...], preferred_element_type=F32)
    o_ref[...] = x1 + pg * pp


def _merge(layer, x2, fo, mo, lo, dil_outs, proj, p3, lp):
    rows = x2.shape[0]
    TM = TM_MERGE
    tiles_per_seq = dil_outs[0].shape[1] * dil_outs[0].shape[2] // TM

    def rspec(w, c=0):
        return pl.BlockSpec((TM, w), lambda i, c=c: (i, c))

    def cspec(shape):
        return pl.BlockSpec(shape, lambda i: (0, 0))

    def dspec(a):
        dil = a.shape[1]
        return pl.BlockSpec((1, dil, TM // dil, GROUP_W),
                            lambda i: (i // tiles_per_seq, 0, lax.rem(i, tiles_per_seq), 0))

    return pl.pallas_call(
        _merge_kernel,
        out_shape=jax.ShapeDtypeStruct((rows, D_MODEL), F32),
        grid=(rows // TM,),
        scratch_shapes=[pltpu.VMEM((GROUP_W // LANES, TM, LANES), F32)] * len(dil_outs),
        in_specs=[rspec(D_MODEL)] + [rspec(GROUP_W)] * 3 + [dspec(a) for a in dil_outs] + [
                  rspec(D_MODEL),
                  pl.BlockSpec((1, TM, PLE_DIM), lambda i: (layer, i, 0)),
                  cspec((D_MODEL, D_MODEL)), cspec((1, D_MODEL)), cspec((D_MODEL, D_MODEL)),
                  cspec((PLE_DIM, D_MODEL))],
        out_specs=rspec(D_MODEL),
        compiler_params=_compiler_params(("arbitrary",)),
        name="merge",
    )(x2, fo, mo, lo, *dil_outs, proj, p3, lp["w_out"], lp["ple_norm_g"], lp["w_ple_gate"], lp["w_ple_proj"])


def _layer_params(i, ln_g, b_forget, qk_gain, mla_q_norm, mla_kv_norm, mla_nope_gain,
                  mla_rope_gain, w_uq, w_ukv, w_out, ple_norm_g, w_ple_gate, w_ple_proj):
    def rope_slab(x1_parts, x2_parts, like):
        pad = jnp.zeros(like.shape[:-1] + (LANES // 2 - 2 * MLA_HALF,), like.dtype)
        return jnp.concatenate(list(x1_parts) + [pad] + list(x2_parts) + [pad], axis=-1)

    per_head = MLA_NOPE + MLA_ROPE
    uq = w_uq[i]
    uq_nope = [uq[:, h * per_head:h * per_head + MLA_NOPE] for h in range(N_HEADS)]
    uq_x1 = [uq[:, h * per_head + MLA_NOPE:h * per_head + MLA_NOPE + MLA_HALF] for h in range(N_HEADS)]
    uq_x2 = [uq[:, h * per_head + MLA_NOPE + MLA_HALF:(h + 1) * per_head] for h in range(N_HEADS)]
    hpg = N_HEADS // MLA_GROUPS
    uq_cols = uq_nope + [rope_slab(uq_x1[g * hpg:(g + 1) * hpg], uq_x2[g * hpg:(g + 1) * hpg], uq)
                         for g in range(MLA_GROUPS)]
    ukv = w_ukv[i]
    ukv_cols = ([ukv[:, h * 2 * HEAD_DIM:h * 2 * HEAD_DIM + HEAD_DIM] for h in range(N_HEADS)]
                + [ukv[:, h * 2 * HEAD_DIM + HEAD_DIM:(h + 1) * 2 * HEAD_DIM] for h in range(N_HEADS)])
    rg = mla_rope_gain[i]
    rope_gain = rope_slab([rg[:, :MLA_HALF]] * hpg, [rg[:, MLA_HALF:]] * hpg, rg)
    return {
        "ln_g": ln_g[i][None, :],
        "gains": jnp.tile(qk_gain[i], (1, N_HEADS)),
        "b_forget": jnp.pad(b_forget[i], (0, LANES - N_HEADS))[None, :],
        "q_norm": mla_q_norm[i][None, :],
        "kv_norm": mla_kv_norm[i][None, :],
        "nope_gain": jnp.tile(mla_nope_gain[i], (1, N_HEADS)),
        "rope_gain": rope_gain,
        "w_uq": jnp.concatenate(uq_cols, axis=1).astype(BF16),
        "w_ukv": jnp.concatenate(ukv_cols, axis=1).astype(BF16),
        "w_out": w_out[i].astype(BF16),
        "ple_norm_g": ple_norm_g[i][None, :],
        "w_ple_gate": w_ple_gate[i].astype(BF16),
        "w_ple_proj": w_ple_proj[i].astype(BF16),
    }


def _constants(S):
    inv = 1.0 / (ROPE_THETA ** (jnp.arange(MLA_HALF, dtype=F32) * 2.0 / MLA_ROPE))
    ang = jnp.arange(S).astype(F32)[:, None] * inv[None, :]
    cos = jnp.tile(jnp.cos(ang), (1, 2 * N_HEADS))
    sin = jnp.tile(jnp.sin(ang), (1, 2 * N_HEADS))
    sign = np.concatenate([-np.ones(LANES // 2, np.float32), np.ones(LANES // 2, np.float32)])
    lane = np.arange(GROUP_W)
    g64 = (lane[:, None] // HEAD_DIM == lane[None, :] // HEAD_DIM).astype(np.float32) / HEAD_DIM
    within = np.arange(LANES) % (LANES // 2)
    rope_head = np.where(within < 2 * MLA_HALF, within // MLA_HALF, -1)
    g32 = ((rope_head[:, None] == rope_head[None, :]) & (rope_head[:, None] >= 0)).astype(np.float32) / MLA_ROPE
    tri = np.tril(np.ones((T_PREP, T_PREP), np.float32))
    ekr = np.zeros((LANES, LANES), np.float32)
    kr_lane0 = IN_OFF["kr"] % LANES
    for c in range(MLA_ROPE):
        half, idx = divmod(c, MLA_HALF)
        for copy in range(N_HEADS // MLA_GROUPS):
            ekr[kr_lane0 + c, half * (LANES // 2) + copy * MLA_HALF + idx] = 1.0
    return {"cos": cos, "sin": sin * sign[None, :], "g64": jnp.asarray(g64, BF16),
            "g32": jnp.asarray(g32, BF16), "tri": jnp.asarray(tri, BF16), "ekr": jnp.asarray(ekr, BF16)}


def kernel(x, p, ln_g, w_in, b_forget, qk_gain, mla_q_norm, mla_kv_norm, mla_nope_gain, mla_rope_gain,
           w_uq, w_ukv, w_out, rel_bias, ple_norm_g, w_ple_gate, w_ple_proj):
    B, S, _ = x.shape
    depth = p.shape[0]
    consts = _constants(S)
    moba_tb, dil_tb = _bias_tables(rel_bias)
    x2 = x.reshape(B * S, D_MODEL)
    for i in range(depth):
        lp = _layer_params(i, ln_g, b_forget, qk_gain, mla_q_norm, mla_kv_norm, mla_nope_gain,
                           mla_rope_gain, w_uq, w_ukv, w_out, ple_norm_g, w_ple_gate, w_ple_proj)
        (fqT, fk, fvT, fck, fcrow, mqT, mk, mvT, mkmean, lqT, lk, lvT, *dil_qkv, gate) = _front(
            i, x2.reshape(B, S, D_MODEL), w_in, consts, lp)
        S_ = S
        fo = _dense_mixer(
            _fox_kernel, "fox", fqT, fk, fvT, (fck, fcrow),
            (pl.BlockSpec((1, N_HEADS, S_, LANES), lambda b, i: (b, 0, 0, 0)),
             pl.BlockSpec((1, 8, N_SUB * TQ), lambda b, i: (b, 0, i))))
        mo = _dense_mixer(
            _moba_kernel, "moba", mqT, mk, mvT, (mkmean, moba_tb),
            (pl.BlockSpec((1, S_ // MOBA_BLOCK, GROUP_W), lambda b, i: (b, 0, 0)),
             pl.BlockSpec((MOBA_TB_ENTRIES, N_HEADS, MOBA_BLOCK, MOBA_BLOCK), lambda b, i: (0, 0, 0, 0))),
            (pltpu.VMEM((N_SUB * N_HEADS, S_ // MOBA_BLOCK, TQ), F32),))
        lo = _dense_mixer(_mla_kernel, "mla", lqT, lk, lvT)
        dil_outs = []
        for pat, (_, dil) in enumerate(DIL_PATTERNS):
            q, k, v = (a.reshape(B, dil, S // dil, GROUP_W) for a in dil_qkv[3 * pat:3 * pat + 3])
            dil_outs += list(_dilated_pattern(pat, q, k, v, dil_tb))
        rs = lambda a: a.reshape(B * S, GROUP_W)
        x2 = _merge(i, x2, rs(fo), rs(mo), rs(lo), dil_outs, gate.reshape(B * S, D_MODEL),
                    p.reshape(depth, B * S, PLE_DIM), lp)
    return x2.reshape(B, S, D_MODEL)
```

```python
import functools
import math

import numpy as np
import jax
import jax.numpy as jnp
from jax import lax
from jax.experimental import pallas as pl
from jax.experimental.pallas import tpu as pltpu

F32 = jnp.float32
BF16 = jnp.bfloat16

D_MODEL = 1024
N_HEADS = 4
HEAD_DIM = 64
GROUP_W = N_HEADS * HEAD_DIM
MLA_NOPE = 64
MLA_ROPE = 32
MLA_HALF = MLA_ROPE // 2
MLA_Q_LORA = 256
MLA_KV_LORA = 128
PLE_DIM = 256
MOBA_BLOCK = 256
MOBA_TOPK = 3
DIL_PATTERNS = ((128, 1), (512, 4), (2048, 16))
DIL_BLOCK = 128
DIL_BLOCKS_PER_STEP = 16
N_BUCKETS = 32
MAX_DISTANCE = 2048
ROPE_THETA = 10000.0
EPS = 1e-6
NEG = -1e30
LOG2E = math.log2(math.e)

LANES = 128
VMEM_LIMIT_BYTES = 56 * 1024 * 1024

COL_FOX, COL_MOBA, COL_DIL = 0, 768, 1536
COL_CQ, COL_CKV, COL_KR, COL_FF, COL_GATE = 2304, 2560, 2688, 2816, 3072
PROJ_W = 4096
STAGE_ROWS = 128

IN_OFF = {}
IN_W = 0
for _name, _size in (("fq", GROUP_W), ("fk", GROUP_W), ("fv", GROUP_W), ("ff", N_HEADS),
                     ("mq", GROUP_W), ("mk", GROUP_W), ("mv", GROUP_W),
                     ("dq", GROUP_W), ("dk", GROUP_W), ("dv", GROUP_W),
                     ("cq", MLA_Q_LORA), ("ckv", MLA_KV_LORA), ("kr", MLA_ROPE), ("gate", D_MODEL)):
    IN_OFF[_name] = IN_W
    IN_W += _size

T_PREP = 256
FRONT_CHAINS = 2
TQ = 256
TK = 256
N_SUB = 2
COMMON_BLOCKS = 4
MOBA_TB_ENTRIES = 8
TM_MERGE = 1024
V_ROWS = HEAD_DIM + 16
MLA_GROUPS = 2


def _bucket_np(d):
    d = np.maximum(np.asarray(d, np.int64), 0)
    max_exact = N_BUCKETS // 2
    d_f = np.maximum(d, 1).astype(np.float64)
    val = np.log(d_f / max_exact) / math.log(MAX_DISTANCE / max_exact) * (N_BUCKETS - max_exact)
    frac = np.abs(val - np.round(val))
    on_edge = (frac < 1e-5) & (d > max_exact) & (val < N_BUCKETS - max_exact - 0.5)
    assert not on_edge.any(), "distance too close to a bucket edge for a static table"
    large = np.minimum(max_exact + np.floor(val + 1e-9).astype(np.int64), N_BUCKETS - 1)
    return np.where(d < max_exact, d, large)


def _bucket_steps(lo, hi):
    ds = np.arange(lo, hi + 1)
    bs = _bucket_np(ds)
    steps = [(int(ds[i]), int(bs[i])) for i in range(1, len(ds)) if bs[i] != bs[i - 1]]
    return int(bs[0]), steps


def _compiler_params(sem):
    return pltpu.CompilerParams(dimension_semantics=sem, vmem_limit_bytes=VMEM_LIMIT_BYTES)


def _bias_from_steps(d, rb_ref, col, lo, hi):
    b0, steps = _bucket_steps(lo, hi)
    val = jnp.full(d.shape, rb_ref[b0, col], F32)
    for t, b in steps:
        val = jnp.where(d >= t, rb_ref[b, col], val)
    return val


def _bias_tables_kernel(rb_ref, moba_ref, dil_ref):
    kl = lax.broadcasted_iota(jnp.int32, (MOBA_BLOCK, MOBA_BLOCK), 0)
    ql = lax.broadcasted_iota(jnp.int32, (MOBA_BLOCK, MOBA_BLOCK), 1)
    for e in range(MOBA_TB_ENTRIES - 1):
        d = jnp.maximum(e * MOBA_BLOCK + ql - kl, 0)
        lo = max(e * MOBA_BLOCK - (MOBA_BLOCK - 1), 0)
        hi = e * MOBA_BLOCK + (MOBA_BLOCK - 1)
        for h in range(N_HEADS):
            moba_ref[e, h] = _bias_from_steps(d, rb_ref, h, lo, hi) * LOG2E
    for h in range(N_HEADS):
        moba_ref[MOBA_TB_ENTRIES - 1, h] = jnp.full((MOBA_BLOCK, MOBA_BLOCK), rb_ref[N_BUCKETS - 1, h], F32) * LOG2E
    qi = lax.broadcasted_iota(jnp.int32, (DIL_BLOCK, 2 * DIL_BLOCK), 0)
    kj = lax.broadcasted_iota(jnp.int32, (DIL_BLOCK, 2 * DIL_BLOCK), 1)
    rel = qi + DIL_BLOCK - kj
    for p, (_, dil) in enumerate(DIL_PATTERNS):
        d = jnp.maximum(rel * dil, 0)
        for h in range(N_HEADS):
            bias = _bias_from_steps(d, rb_ref, N_HEADS + h, 0, (2 * DIL_BLOCK - 1) * dil)
            dil_ref[p, h] = jnp.where((rel >= 0) & (rel <= DIL_BLOCK), bias, NEG)


def _bias_tables(rel_bias):
    assert _bucket_np((MOBA_TB_ENTRIES - 1) * MOBA_BLOCK - (MOBA_BLOCK - 1)) == N_BUCKETS - 1
    return pl.pallas_call(
        _bias_tables_kernel,
        out_shape=(jax.ShapeDtypeStruct((MOBA_TB_ENTRIES, N_HEADS, MOBA_BLOCK, MOBA_BLOCK), F32),
                   jax.ShapeDtypeStruct((len(DIL_PATTERNS), N_HEADS, DIL_BLOCK, 2 * DIL_BLOCK), F32)),
        in_specs=[pl.BlockSpec(memory_space=pltpu.SMEM)],
        out_specs=(pl.BlockSpec(memory_space=pltpu.VMEM), pl.BlockSpec(memory_space=pltpu.VMEM)),
        compiler_params=pltpu.CompilerParams(vmem_limit_bytes=VMEM_LIMIT_BYTES),
        name="bias_tables",
    )(rel_bias)


def _stage_in_weights(w_ref, ekr_ref, ws_ref):
    W = GROUP_W
    moves = ((COL_FOX, IN_OFF["fq"], 3 * W), (COL_MOBA, IN_OFF["mq"], 3 * W), (COL_DIL, IN_OFF["dq"], 3 * W),
             (COL_CQ, IN_OFF["cq"], MLA_Q_LORA), (COL_CKV, IN_OFF["ckv"], MLA_KV_LORA),
             (COL_GATE, IN_OFF["gate"], D_MODEL))
    kr_base = IN_OFF["kr"] // LANES * LANES
    assert IN_OFF["ff"] % LANES == 0 and IN_OFF["kr"] + MLA_ROPE <= kr_base + LANES
    lane = lax.broadcasted_iota(jnp.int32, (STAGE_ROWS, LANES), 1)
    for r0 in range(0, D_MODEL, STAGE_ROWS):
        rows = slice(r0, r0 + STAGE_ROWS)
        for dst, src, width in moves:
            ws_ref[rows, dst:dst + width] = w_ref[0, rows, src:src + width].astype(BF16)
        kr_win = w_ref[0, rows, kr_base:kr_base + LANES].astype(BF16)
        ws_ref[rows, COL_KR:COL_KR + LANES] = jnp.dot(kr_win, ekr_ref[...], preferred_element_type=F32).astype(BF16)
        ff_win = w_ref[0, rows, IN_OFF["ff"]:IN_OFF["ff"] + LANES]
        ws_ref[rows, COL_FF:COL_FF + LANES] = jnp.where(lane < N_HEADS, ff_win, 0.0).astype(BF16)
        ws_ref[rows, COL_FF + LANES:COL_GATE] = jnp.zeros((STAGE_ROWS, COL_GATE - COL_FF - LANES), BF16)


def _group_mean_sq(x, g_mat):
    return jnp.dot((x * x).astype(BF16), g_mat, preferred_element_type=F32)


def _group_norm(x, gain, g_mat):
    return x * lax.rsqrt(_group_mean_sq(x, g_mat) + EPS) * gain


def _row_norm(x, gain):
    return x * lax.rsqrt(jnp.mean(x * x, axis=-1, keepdims=True) + EPS) * gain


def _log_sigmoid(x):
    return -(jnp.maximum(-x, 0.0) + jnp.log1p(jnp.exp(-jnp.abs(x))))


def _front_kernel(x_ref, lng_ref, w_ref, ekr_ref,
                  gains_ref, bfor_ref, qng_ref, kvng_ref, nopeg_ref, ropeg_ref,
                  wuq_ref, wukv_ref, cos_ref, sin_ref, g64_ref, g32_ref, tri_ref,
                  fqT_ref, fk_ref, fvT_ref, fck_ref, fcrow_ref,
                  mqT_ref, mk_ref, mvT_ref, mkmean_ref,
                  lqT_ref, lk_ref, lvT_ref,
                  dq1_ref, dk1_ref, dv1_ref, dq4_ref, dk4_ref, dv4_ref, dq16_ref, dk16_ref, dv16_ref,
                  gate_ref,
                  ws_ref, carry_ref, sq_ref, sk_ref, sv_ref):
    t = pl.program_id(1)

    @pl.when((pl.program_id(0) == 0) & (t == 0))
    def _():
        _stage_in_weights(w_ref, ekr_ref, ws_ref)

    g64 = g64_ref[...]
    scale = HEAD_DIM ** -0.5
    qscale = (MLA_NOPE + MLA_ROPE) ** -0.5
    W = GROUP_W
    T = T_PREP
    ones_rows = jnp.ones((V_ROWS - HEAD_DIM, T), BF16)

    @pl.when(t == 0)
    def _():
        carry_ref[...] = jnp.zeros_like(carry_ref)

    for ch in range(FRONT_CHAINS):
        rows = slice(ch * T, (ch + 1) * T)

        def store_vT(dst, v):
            vT = v.T.astype(BF16)
            for h in range(N_HEADS):
                dst[0, ch, h] = jnp.concatenate([vT[HEAD_DIM * h:HEAD_DIM * (h + 1)], ones_rows], axis=0)

        x = x_ref[0, rows, :]
        hn = (x * lax.rsqrt(jnp.mean(x * x, axis=-1, keepdims=True) + EPS) * lng_ref[...]).astype(BF16)
        project = lambda lo, width: jnp.dot(hn, ws_ref[:, lo:lo + width], preferred_element_type=F32)
        pf = project(COL_FOX, 3 * W)
        p_ff = project(COL_FF, LANES)
        pm = project(COL_MOBA, 3 * W)

        fq = _group_norm(pf[:, :W], gains_ref[0:1, :], g64) * (scale * LOG2E)
        fk = _group_norm(pf[:, W:2 * W], gains_ref[1:2, :], g64)
        fqT_ref[0, 0, :, rows] = fq.T.astype(BF16)
        fk_ref[0, 0, rows, :] = fk.astype(BF16)
        store_vT(fvT_ref, pf[:, 2 * W:])
        log_f = _log_sigmoid(p_ff + bfor_ref[...])
        tri = tri_ref[...]
        c = carry_ref[...]
        rest = log_f
        for _ in range(3):
            part = rest.astype(BF16)
            c = c + jnp.dot(tri, part, preferred_element_type=F32)
            rest = rest - part.astype(F32)
        carry_ref[...] = c[T - 1:T, :]
        c2 = c * LOG2E
        for h in range(N_HEADS):
            fck_ref[0, h, rows, :] = jnp.broadcast_to(c2[:, h:h + 1], (T, LANES))
        fcrow_ref[0, :, rows] = c2.T[0:8, :]

        pd = project(COL_DIL, 3 * W)
        mq = _group_norm(pm[:, :W], gains_ref[2:3, :], g64) * (scale * LOG2E)
        mk = _group_norm(pm[:, W:2 * W], gains_ref[3:4, :], g64)
        mqT_ref[0, 0, :, rows] = mq.T.astype(BF16)
        mk_ref[0, 0, rows, :] = mk.astype(BF16)
        store_vT(mvT_ref, pm[:, 2 * W:])
        mkmean_ref[0, pl.ds(FRONT_CHAINS * t + ch, 1), :] = jnp.mean(mk, axis=0, keepdims=True)

        p_mla = project(COL_CQ, MLA_Q_LORA + MLA_KV_LORA + LANES)
        dq = _group_norm(pd[:, :W], gains_ref[4:5, :], g64) * scale
        dk = _group_norm(pd[:, W:2 * W], gains_ref[5:6, :], g64)
        dv = pd[:, 2 * W:]
        dq1_ref[0, rows, :] = dq.astype(BF16)
        dk1_ref[0, rows, :] = dk.astype(BF16)
        dv1_ref[0, rows, :] = dv.astype(BF16)
        for val, dst in ((dq, sq_ref), (dk, sk_ref), (dv, sv_ref)):
            for half in range(GROUP_W // LANES):
                dst[ch, half] = val[:, half * LANES:(half + 1) * LANES]
        for dil, outs in ((4, (dq4_ref, dk4_ref, dv4_ref)), (16, (dq16_ref, dk16_ref, dv16_ref))):
            n = T // dil
            for r in range(dil):
                for src, dst in zip((sq_ref, sk_ref, sv_ref), outs):
                    dst[0, r, ch * n:(ch + 1) * n, :] = jnp.concatenate(
                        [src[ch, half, pl.ds(r, n, stride=dil), :] for half in range(GROUP_W // LANES)],
                        axis=1).astype(BF16)

        half_gate = D_MODEL // 2
        gate_ref[0, rows, :half_gate] = project(COL_GATE, half_gate).astype(BF16)
        cos = cos_ref[rows, :]
        sin = sin_ref[rows, :]
        cqn = _row_norm(p_mla[:, :MLA_Q_LORA], qng_ref[...]).astype(BF16)
        rotate = lambda v, cos=cos, sin=sin: v * cos + pltpu.roll(v, LANES // 2, 1) * sin
        qf = jnp.dot(cqn, wuq_ref[...], preferred_element_type=F32)
        q_nope = _group_norm(qf[:, :W], nopeg_ref[0:1, :], g64) * (qscale * LOG2E)
        ckvn = _row_norm(p_mla[:, MLA_Q_LORA:MLA_Q_LORA + MLA_KV_LORA], kvng_ref[...]).astype(BF16)
        kvf = jnp.dot(ckvn, wukv_ref[...], preferred_element_type=F32)
        k_nope = _group_norm(kvf[:, :W], nopeg_ref[1:2, :], g64)
        krs = p_mla[:, MLA_Q_LORA + MLA_KV_LORA:]
        kr_ms = jnp.sum(krs * krs, axis=-1, keepdims=True) * (1.0 / (2 * MLA_ROPE))
        kr = rotate(krs * lax.rsqrt(kr_ms + EPS) * ropeg_ref[1:2, :])
        for g in range(MLA_GROUPS):
            slab = qf[:, W + g * LANES:W + (g + 1) * LANES]
            qr = rotate(_group_norm(slab, ropeg_ref[0:1, :], g32_ref[...])) * (qscale * LOG2E)
            lq = jnp.concatenate([q_nope[:, g * LANES:(g + 1) * LANES], qr], axis=1)
            lqT_ref[0, g, :, rows] = lq.T.astype(BF16)
            lk_ref[0, g, rows, :] = jnp.concatenate([k_nope[:, g * LANES:(g + 1) * LANES], kr],
                                                    axis=1).astype(BF16)
        store_vT(lvT_ref, kvf[:, W:])
        gate_ref[0, rows, half_gate:] = project(COL_GATE + half_gate, half_gate).astype(BF16)


def _front(layer, x3, w_in, consts, lp):
    B, S, _ = x3.shape
    T = T_PREP
    TS = FRONT_CHAINS * T
    NT = S // T
    assert S % TS == 0 and S // MOBA_BLOCK == NT

    def cspec(shape):
        return pl.BlockSpec(shape, lambda b, t: (0,) * len(shape))

    in_specs = [pl.BlockSpec((1, TS, D_MODEL), lambda b, t: (b, t, 0)), cspec((1, D_MODEL)),
                pl.BlockSpec((1, D_MODEL, IN_W), lambda b, t: (layer, 0, 0), pipeline_mode=pl.Buffered(1)),
                cspec((LANES, LANES)),
                cspec((6, GROUP_W)), cspec((1, LANES)), cspec((1, MLA_Q_LORA)), cspec((1, MLA_KV_LORA)),
                cspec((2, GROUP_W)), cspec((2, LANES)),
                cspec((MLA_Q_LORA, MLA_GROUPS * GROUP_W)), cspec((MLA_KV_LORA, 2 * GROUP_W)),
                pl.BlockSpec((TS, LANES), lambda b, t: (t, 0)), pl.BlockSpec((TS, LANES), lambda b, t: (t, 0)),
                cspec((GROUP_W, GROUP_W)), cspec((LANES, LANES)), cspec((T, T))]

    def qT(groups):
        return (jax.ShapeDtypeStruct((B, groups, GROUP_W, S), BF16),
                pl.BlockSpec((1, groups, GROUP_W, TS), lambda b, t: (b, 0, 0, t)))

    def keys(groups):
        return (jax.ShapeDtypeStruct((B, groups, S, GROUP_W), BF16),
                pl.BlockSpec((1, groups, TS, GROUP_W), lambda b, t: (b, 0, t, 0)))

    def rows(w):
        return jax.ShapeDtypeStruct((B, S, w), BF16), pl.BlockSpec((1, TS, w), lambda b, t: (b, t, 0))

    def vT():
        return (jax.ShapeDtypeStruct((B, NT, N_HEADS, V_ROWS, T), BF16),
                pl.BlockSpec((1, FRONT_CHAINS, N_HEADS, V_ROWS, T), lambda b, t: (b, t, 0, 0, 0)))

    def resid(dil):
        return (jax.ShapeDtypeStruct((B, dil, S // dil, GROUP_W), BF16),
                pl.BlockSpec((1, dil, TS // dil, GROUP_W), lambda b, t: (b, 0, t, 0)))

    outs = [qT(1), keys(1), vT(),
            (jax.ShapeDtypeStruct((B, N_HEADS, S, LANES), F32),
             pl.BlockSpec((1, N_HEADS, TS, LANES), lambda b, t: (b, 0, t, 0))),
            (jax.ShapeDtypeStruct((B, 8, S), F32), pl.BlockSpec((1, 8, TS), lambda b, t: (b, 0, t))),
            qT(1), keys(1), vT(),
            (jax.ShapeDtypeStruct((B, NT, GROUP_W), F32), pl.BlockSpec((1, NT, GROUP_W), lambda b, t: (b, 0, 0))),
            qT(MLA_GROUPS), keys(MLA_GROUPS), vT(),
            rows(GROUP_W), rows(GROUP_W), rows(GROUP_W),
            resid(4), resid(4), resid(4), resid(16), resid(16), resid(16),
            rows(D_MODEL)]
    return pl.pallas_call(
        _front_kernel,
        out_shape=tuple(o[0] for o in outs),
        grid=(B, NT // FRONT_CHAINS),
        in_specs=in_specs,
        out_specs=tuple(o[1] for o in outs),
        scratch_shapes=[pltpu.VMEM((D_MODEL, PROJ_W), BF16), pltpu.VMEM((1, LANES), F32)]
        + [pltpu.VMEM((FRONT_CHAINS, GROUP_W // LANES, T, LANES), F32)] * 3,
        compiler_params=_compiler_params(("arbitrary", "arbitrary")),
        name="front",
    )(x3, lp["ln_g"], w_in, consts["ekr"],
      lp["gains"], lp["b_forget"], lp["q_norm"], lp["kv_norm"], lp["nope_gain"], lp["rope_gain"],
      lp["w_uq"], lp["w_ukv"], consts["cos"], consts["sin"], consts["g64"], consts["g32"], consts["tri"])


def _stage_head_queries(qT_ref, qm_ref, groups):
    r = lax.broadcasted_iota(jnp.int32, (GROUP_W, TQ), 0)
    for h in range(N_HEADS):
        if groups == 1:
            g, mask = 0, (r >= HEAD_DIM * h) & (r < HEAD_DIM * (h + 1))
        else:
            g, hl = divmod(h, N_HEADS // groups)
            mask = (r >= HEAD_DIM * hl) & (r < HEAD_DIM * (hl + 1))
            for base in (2 * HEAD_DIM, 2 * HEAD_DIM + LANES // 2):
                lo = base + MLA_HALF * hl
                mask = mask | ((r >= lo) & (r < lo + MLA_HALF))
        for sub in range(N_SUB):
            q = qT_ref[0, g, :, sub * TQ:(sub + 1) * TQ]
            qm_ref[sub * N_HEADS + h] = jnp.where(mask, q, jnp.zeros_like(q))


def _flash_sweep(pair_idx, groups, qm_ref, k_ref, vT_ref, m_ref, acc_ref, u_ref, o_ref, score_fn):
    kpos = lax.broadcasted_iota(jnp.int32, (TK, TQ), 0)
    qpos = lax.broadcasted_iota(jnp.int32, (TK, TQ), 1)
    causal = kpos <= qpos
    m_ref[...] = jnp.full(m_ref.shape, NEG, F32)
    acc_ref[...] = jnp.zeros(acc_ref.shape, F32)
    both = tuple(range(N_SUB))

    def scores(j0, nblk, diag_last, subs):
        row0 = pl.multiple_of(j0 * TK, TK)
        kts = [k_ref[0, g, pl.ds(row0, nblk * TK), :] for g in range(groups)]
        pending = []
        for sub in subs:
            for h in range(N_HEADS):
                c = sub * N_HEADS + h
                s = jnp.dot(kts[h // (N_HEADS // groups)], qm_ref[c], preferred_element_type=F32)
                tops, lives, row = [], [], None
                for blk in range(nblk):
                    rows = slice(blk * TK, (blk + 1) * TK)
                    diagonal = diag_last and blk == nblk - 1
                    u, row, live = score_fn(sub, h, s[rows], j0 + blk, diagonal)
                    if diagonal:
                        u = jnp.where(causal, u, NEG)
                    u_ref[c, rows, :] = u
                    top = jnp.max(u, axis=0, keepdims=True)
                    if row is not None:
                        top = top + row
                    if live is not None:
                        top = jnp.where(live, top, NEG)
                    tops.append(top)
                    lives.append(live)
                m_old = m_ref[c]
                m_new = functools.reduce(jnp.maximum, tops, m_old)
                shift = m_new if row is None else m_new - row
                shifts = [shift if live is None else jnp.where(live, shift, -NEG) for live in lives]
                m_ref[c] = m_new
                pending.append((c, h, shifts, jnp.exp2(m_old - m_new)))
        return pending

    def values(j0, pending):
        for c, h, shifts, alpha in pending:
            pv = None
            for blk, shift in enumerate(shifts):
                p = jnp.exp2((u_ref[c, blk * TK:(blk + 1) * TK, :] - shift).astype(BF16))
                term = jnp.dot(vT_ref[0, j0 + blk, h], p, preferred_element_type=F32)
                pv = term if pv is None else pv + term
            acc_ref[c] = alpha * acc_ref[c] + pv

    def common(j0, nblk):
        values(j0, scores(j0, nblk, False, both))

    def wide_step(t, carry):
        common(COMMON_BLOCKS * t, COMMON_BLOCKS)
        return carry

    n_wide = (N_SUB * pair_idx) // COMMON_BLOCKS
    lax.fori_loop(0, n_wide, wide_step, 0)
    for k in range(COMMON_BLOCKS // N_SUB - 1):
        def leftover(_, carry, k=k):
            common(COMMON_BLOCKS * n_wide + N_SUB * k, N_SUB)
            return carry

        lax.fori_loop(0, (pair_idx - (COMMON_BLOCKS // N_SUB) * n_wide > k).astype(jnp.int32), leftover, 0)
    n_common = N_SUB * pair_idx
    pend_a = scores(n_common, 1, True, (0,))
    pend_b = scores(n_common, 2, True, (1,))
    values(n_common, pend_a)
    values(n_common, pend_b)
    for sub in both:
        outs = []
        for h in range(N_HEADS):
            acc = acc_ref[sub * N_HEADS + h]
            outs.append(acc[:HEAD_DIM] / acc[HEAD_DIM:HEAD_DIM + 1])
        o_ref[0, sub * TQ:(sub + 1) * TQ, :] = jnp.concatenate(outs, axis=0).T.astype(BF16)


def _fox_kernel(qT_ref, k_ref, vT_ref, ck_ref, crow_ref, o_ref, qm_ref, m_ref, acc_ref, u_ref):
    pair_idx = pl.program_id(1)
    _stage_head_queries(qT_ref, qm_ref, 1)

    def score_fn(sub, h, s, j, diagonal):
        ck = ck_ref[0, h, pl.ds(pl.multiple_of(j * TK, TK), TK), :]
        cq = crow_ref[0, h:h + 1, sub * TQ:(sub + 1) * TQ]
        return s - jnp.concatenate([ck] * (TQ // LANES), axis=1), cq, None

    _flash_sweep(pair_idx, 1, qm_ref, k_ref, vT_ref, m_ref, acc_ref, u_ref, o_ref, score_fn)


def _mla_kernel(qT_ref, k_ref, vT_ref, o_ref, qm_ref, m_ref, acc_ref, u_ref):
    pair_idx = pl.program_id(1)
    _stage_head_queries(qT_ref, qm_ref, MLA_GROUPS)
    _flash_sweep(pair_idx, MLA_GROUPS, qm_ref, k_ref, vT_ref, m_ref, acc_ref, u_ref, o_ref,
                 lambda sub, h, s, j, diagonal: (s, None, None))


def _moba_kernel(qT_ref, k_ref, vT_ref, kmean_ref, tb_ref, o_ref, qm_ref, m_ref, acc_ref, u_ref, sel_ref):
    pair_idx = pl.program_id(1)
    _stage_head_queries(qT_ref, qm_ref, 1)
    kmean = kmean_ref[0].astype(BF16)
    nb = kmean.shape[0]
    blk = lax.broadcasted_iota(jnp.int32, (nb, TQ), 0)
    own = [N_SUB * pair_idx + sub for sub in range(N_SUB)]
    for c in range(N_SUB * N_HEADS):
        i = own[c // N_HEADS]
        gate = jnp.dot(kmean, qm_ref[c], preferred_element_type=F32)
        gate = jnp.where(blk < i, gate, NEG)
        rank = jnp.zeros((nb, TQ), jnp.int32)
        for jp in range(nb):
            row = gate[jp:jp + 1, :]
            ahead = (row > gate) | ((row == gate) & (jp < blk))
            rank = rank + ahead.astype(jnp.int32)
        sel_ref[c] = ((rank < MOBA_TOPK) & (blk < i)).astype(F32)

    def score_fn(sub, h, s, j, diagonal):
        u = s + tb_ref[jnp.minimum(own[sub] - j, MOBA_TB_ENTRIES - 1), h]
        live = None if diagonal else sel_ref[sub * N_HEADS + h, pl.ds(j, 1), :] > 0.5
        return u, None, live

    _flash_sweep(pair_idx, 1, qm_ref, k_ref, vT_ref, m_ref, acc_ref, u_ref, o_ref, score_fn)


def _dense_mixer(kernel_fn, name, qT, k, vT, extra_args=(), extra_specs=(), extra_scratch=()):
    B, groups, _, S = qT.shape
    NT = S // TK
    assert TQ == TK and N_SUB == 2 and S % (N_SUB * TQ) == 0
    n_chain = N_SUB * N_HEADS
    in_specs = [pl.BlockSpec((1, groups, GROUP_W, N_SUB * TQ), lambda b, i: (b, 0, 0, i)),
                pl.BlockSpec((1, groups, S, GROUP_W), lambda b, i: (b, 0, 0, 0)),
                pl.BlockSpec((1, NT, N_HEADS, V_ROWS, TK), lambda b, i: (b, 0, 0, 0, 0))] + list(extra_specs)
    scratch = [pltpu.VMEM((n_chain, GROUP_W, TQ), BF16),
               pltpu.VMEM((n_chain, 1, TQ), F32),
               pltpu.VMEM((n_chain, V_ROWS, TQ), F32),
               pltpu.VMEM((n_chain, COMMON_BLOCKS * TK, TQ), F32)] + list(extra_scratch)
    return pl.pallas_call(
        kernel_fn,
        out_shape=jax.ShapeDtypeStruct((B, S, GROUP_W), BF16),
        grid=(B, S // (N_SUB * TQ)),
        in_specs=in_specs,
        out_specs=pl.BlockSpec((1, N_SUB * TQ, GROUP_W), lambda b, i: (b, i, 0)),
        scratch_shapes=scratch,
        compiler_params=_compiler_params(("arbitrary", "arbitrary")),
        name=name,
    )(qT, k, vT, *extra_args)


def _dil_kernel(q_ref, kp_ref, kc_ref, vp_ref, vc_ref, tb_ref, o_ref, lse_ref):
    n = pl.program_id(2)
    qb = q_ref.shape[2] // DIL_BLOCK
    kj = lax.broadcasted_iota(jnp.int32, (DIL_BLOCK, 2 * DIL_BLOCK), 1)
    qlane = lax.broadcasted_iota(jnp.int32, (DIL_BLOCK, GROUP_W), 1)
    in_head = [(qlane >= HEAD_DIM * h) & (qlane < HEAD_DIM * (h + 1)) for h in range(N_HEADS)]
    for seq, c in [(seq, c) for seq in range(q_ref.shape[1]) for c in range(qb)]:
        rows = slice(c * DIL_BLOCK, (c + 1) * DIL_BLOCK)
        q = q_ref[0, seq, rows, :]
        if c == 0:
            kb = jnp.concatenate([kp_ref[0, seq], kc_ref[0, seq, :DIL_BLOCK, :]], axis=0)
            vb = jnp.concatenate([vp_ref[0, seq], vc_ref[0, seq, :DIL_BLOCK, :]], axis=0)
        else:
            kb = kc_ref[0, seq, (c - 1) * DIL_BLOCK:(c + 1) * DIL_BLOCK, :]
            vb = vc_ref[0, seq, (c - 1) * DIL_BLOCK:(c + 1) * DIL_BLOCK, :]
        q_heads = jnp.concatenate([jnp.where(msk, q, jnp.zeros_like(q)) for msk in in_head], axis=0)
        s = lax.dot_general(q_heads, kb, (((1,), (1,)), ((), ())), preferred_element_type=F32)
        s = s.reshape(N_HEADS, DIL_BLOCK, 2 * DIL_BLOCK)
        s = s + tb_ref[0]
        if c == 0:
            s = jnp.where(((kj >= DIL_BLOCK) | (n > 0))[None], s, NEG)
        m = jnp.max(s, axis=-1, keepdims=True)
        e = jnp.exp(s - m)
        l = jnp.sum(e, axis=-1, keepdims=True)
        lse = m + jnp.log(l)
        pr = (e * (1.0 / l)).astype(BF16)
        o_all = jnp.dot(pr.reshape(N_HEADS * DIL_BLOCK, 2 * DIL_BLOCK), vb, preferred_element_type=F32)
        o_all = o_all.reshape(N_HEADS, DIL_BLOCK, GROUP_W)
        o_acc = jnp.zeros((DIL_BLOCK, GROUP_W), F32)
        lse_map = jnp.zeros((DIL_BLOCK, GROUP_W), F32)
        for h in range(N_HEADS):
            o_acc = jnp.where(in_head[h], o_all[h], o_acc)
            lse_map = jnp.where(in_head[h], lse[h], lse_map)
        o_ref[0, seq, rows, :] = o_acc.astype(BF16)
        lse_ref[0, seq, rows, :] = lse_map


def _dilated_pattern(p, q, k, v, dil_tb):
    B, dil, L, _ = q.shape
    assert dil == DIL_PATTERNS[p][1] and L % DIL_BLOCK == 0
    nb = L // DIL_BLOCK
    qb = math.gcd(nb, DIL_BLOCKS_PER_STEP)
    seqs = math.gcd(dil, DIL_BLOCKS_PER_STEP // qb)
    cur = pl.BlockSpec((1, seqs, qb * DIL_BLOCK, GROUP_W), lambda b, r, n: (b, r, n, 0))
    prev = pl.BlockSpec((1, seqs, DIL_BLOCK, GROUP_W), lambda b, r, n: (b, r, jnp.maximum(n * qb - 1, 0), 0))
    return pl.pallas_call(
        _dil_kernel,
        out_shape=(jax.ShapeDtypeStruct((B, dil, L, GROUP_W), BF16),
                   jax.ShapeDtypeStruct((B, dil, L, GROUP_W), F32)),
        grid=(B, dil // seqs, nb // qb),
        in_specs=[cur, prev, cur, prev, cur,
                  pl.BlockSpec((1, N_HEADS, DIL_BLOCK, 2 * DIL_BLOCK), lambda b, r, n, p=p: (p, 0, 0, 0))],
        out_specs=(cur, cur),
        compiler_params=_compiler_params(("arbitrary", "arbitrary", "arbitrary")),
        name=f"dilated_{dil}",
    )(q, k, k, v, v, dil_tb)


def _mix_dilations(o_refs, lse_refs, scratch):
    halves = GROUP_W // LANES
    rows = o_refs[0].shape[1] * o_refs[0].shape[2]
    nat = []
    for ref, dst in zip(list(o_refs) + list(lse_refs), scratch):
        dil = ref.shape[1]
        if dil == 1:
            nat.append(ref[0, 0].astype(F32))
            continue
        for r in range(dil):
            for half in range(halves):
                dst[half, pl.ds(r, rows // dil, stride=dil), :] = (
                    ref[0, r, :, half * LANES:(half + 1) * LANES].astype(F32))
        nat.append(jnp.concatenate([dst[half] for half in range(halves)], axis=1))
    os, lses = nat[:len(o_refs)], nat[len(o_refs):]
    m = functools.reduce(jnp.maximum, lses)
    ws = [jnp.exp(l - m) for l in lses]
    tot = functools.reduce(jnp.add, ws)
    return functools.reduce(jnp.add, [(w / tot) * o for w, o in zip(ws, os)])


def _sigmoid(x):
    return 1.0 / (1.0 + jnp.exp(-x))


def _merge_kernel(x_ref, fo_ref, mo_ref, lo_ref, o1_ref, l1_ref, o4_ref, l4_ref, o16_ref, l16_ref, gate_ref, p_ref,
                  wout_ref, png_ref, wpg_ref, wpp_ref, o_ref, *dil_scratch):
    o_dil = _mix_dilations((o1_ref, o4_ref, o16_ref), (l1_ref, l4_ref, l16_ref), dil_scratch)
    mix = jnp.concatenate([fo_ref[...].astype(F32), mo_ref[...].astype(F32), o_dil, lo_ref[...].astype(F32)],
                          axis=1)
    g = gate_ref[...].astype(F32)
    y = jnp.dot((mix * (g * _sigmoid(g))).astype(BF16), wout_ref[...], preferred_element_type=F32)
    x1 = x_ref[...] + y
    hn = _row_norm(x1, png_ref[...]).astype(BF16)
    pg = _sigmoid(jnp.dot(hn, wpg_ref[...], preferred_element_type=F32))
    pp = jnp.dot(p_ref[0].astype(BF16), wpp_ref[...], preferred_element_type=F32)
    o_ref[...] = x1 + pg * pp


def _merge(layer, x2, fo, mo, lo, dil_outs, proj, p3, lp):
    rows = x2.shape[0]
    TM = TM_MERGE
    tiles_per_seq = dil_outs[0].shape[1] * dil_outs[0].shape[2] // TM

    def rspec(w, c=0):
        return pl.BlockSpec((TM, w), lambda i, c=c: (i, c))

    def cspec(shape):
        return pl.BlockSpec(shape, lambda i: (0, 0))

    def dspec(a):
        dil = a.shape[1]
        return pl.BlockSpec((1, dil, TM // dil, GROUP_W),
                            lambda i: (i // tiles_per_seq, 0, lax.rem(i, tiles_per_seq), 0))

    return pl.pallas_call(
        _merge_kernel,
        out_shape=jax.ShapeDtypeStruct((rows, D_MODEL), F32),
        grid=(rows // TM,),
        scratch_shapes=[pltpu.VMEM((GROUP_W // LANES, TM, LANES), F32)] * len(dil_outs),
        in_specs=[rspec(D_MODEL)] + [rspec(GROUP_W)] * 3 + [dspec(a) for a in dil_outs] + [
                  rspec(D_MODEL),
                  pl.BlockSpec((1, TM, PLE_DIM), lambda i: (layer, i, 0)),
                  cspec((D_MODEL, D_MODEL)), cspec((1, D_MODEL)), cspec((D_MODEL, D_MODEL)),
                  cspec((PLE_DIM, D_MODEL))],
        out_specs=rspec(D_MODEL),
        compiler_params=_compiler_params(("arbitrary",)),
        name="merge",
    )(x2, fo, mo, lo, *dil_outs, proj, p3, lp["w_out"], lp["ple_norm_g"], lp["w_ple_gate"], lp["w_ple_proj"])


def _layer_params(i, ln_g, b_forget, qk_gain, mla_q_norm, mla_kv_norm, mla_nope_gain,
                  mla_rope_gain, w_uq, w_ukv, w_out, ple_norm_g, w_ple_gate, w_ple_proj):
    def rope_slab(x1_parts, x2_parts, like):
        pad = jnp.zeros(like.shape[:-1] + (LANES // 2 - 2 * MLA_HALF,), like.dtype)
        return jnp.concatenate(list(x1_parts) + [pad] + list(x2_parts) + [pad], axis=-1)

    per_head = MLA_NOPE + MLA_ROPE
    uq = w_uq[i]
    uq_nope = [uq[:, h * per_head:h * per_head + MLA_NOPE] for h in range(N_HEADS)]
    uq_x1 = [uq[:, h * per_head + MLA_NOPE:h * per_head + MLA_NOPE + MLA_HALF] for h in range(N_HEADS)]
    uq_x2 = [uq[:, h * per_head + MLA_NOPE + MLA_HALF:(h + 1) * per_head] for h in range(N_HEADS)]
    hpg = N_HEADS // MLA_GROUPS
    uq_cols = uq_nope + [rope_slab(uq_x1[g * hpg:(g + 1) * hpg], uq_x2[g * hpg:(g + 1) * hpg], uq)
                         for g in range(MLA_GROUPS)]
    ukv = w_ukv[i]
    ukv_cols = ([ukv[:, h * 2 * HEAD_DIM:h * 2 * HEAD_DIM + HEAD_DIM] for h in range(N_HEADS)]
                + [ukv[:, h * 2 * HEAD_DIM + HEAD_DIM:(h + 1) * 2 * HEAD_DIM] for h in range(N_HEADS)])
    rg = mla_rope_gain[i]
    rope_gain = rope_slab([rg[:, :MLA_HALF]] * hpg, [rg[:, MLA_HALF:]] * hpg, rg)
    return {
        "ln_g": ln_g[i][None, :],
        "gains": jnp.tile(qk_gain[i], (1, N_HEADS)),
        "b_forget": jnp.pad(b_forget[i], (0, LANES - N_HEADS))[None, :],
        "q_norm": mla_q_norm[i][None, :],
        "kv_norm": mla_kv_norm[i][None, :],
        "nope_gain": jnp.tile(mla_nope_gain[i], (1, N_HEADS)),
        "rope_gain": rope_gain,
        "w_uq": jnp.concatenate(uq_cols, axis=1).astype(BF16),
        "w_ukv": jnp.concatenate(ukv_cols, axis=1).astype(BF16),
        "w_out": w_out[i].astype(BF16),
        "ple_norm_g": ple_norm_g[i][None, :],
        "w_ple_gate": w_ple_gate[i].astype(BF16),
        "w_ple_proj": w_ple_proj[i].astype(BF16),
    }


def _constants(S):
    inv = 1.0 / (ROPE_THETA ** (jnp.arange(MLA_HALF, dtype=F32) * 2.0 / MLA_ROPE))
    ang = jnp.arange(S).astype(F32)[:, None] * inv[None, :]
    cos = jnp.tile(jnp.cos(ang), (1, 2 * N_HEADS))
    sin = jnp.tile(jnp.sin(ang), (1, 2 * N_HEADS))
    sign = np.concatenate([-np.ones(LANES // 2, np.float32), np.ones(LANES // 2, np.float32)])
    lane = np.arange(GROUP_W)
    g64 = (lane[:, None] // HEAD_DIM == lane[None, :] // HEAD_DIM).astype(np.float32) / HEAD_DIM
    within = np.arange(LANES) % (LANES // 2)
    rope_head = np.where(within < 2 * MLA_HALF, within // MLA_HALF, -1)
    g32 = ((rope_head[:, None] == rope_head[None, :]) & (rope_head[:, None] >= 0)).astype(np.float32) / MLA_ROPE
    tri = np.tril(np.ones((T_PREP, T_PREP), np.float32))
    ekr = np.zeros((LANES, LANES), np.float32)
    kr_lane0 = IN_OFF["kr"] % LANES
    for c in range(MLA_ROPE):
        half, idx = divmod(c, MLA_HALF)
        for copy in range(N_HEADS // MLA_GROUPS):
            ekr[kr_lane0 + c, half * (LANES // 2) + copy * MLA_HALF + idx] = 1.0
    return {"cos": cos, "sin": sin * sign[None, :], "g64": jnp.asarray(g64, BF16),
            "g32": jnp.asarray(g32, BF16), "tri": jnp.asarray(tri, BF16), "ekr": jnp.asarray(ekr, BF16)}


def kernel(x, p, ln_g, w_in, b_forget, qk_gain, mla_q_norm, mla_kv_norm, mla_nope_gain, mla_rope_gain,
           w_uq, w_ukv, w_out, rel_bias, ple_norm_g, w_ple_gate, w_ple_proj):
    B, S, _ = x.shape
    depth = p.shape[0]
    consts = _constants(S)
    moba_tb, dil_tb = _bias_tables(rel_bias)
    x2 = x.reshape(B * S, D_MODEL)
    for i in range(depth):
        lp = _layer_params(i, ln_g, b_forget, qk_gain, mla_q_norm, mla_kv_norm, mla_nope_gain,
                           mla_rope_gain, w_uq, w_ukv, w_out, ple_norm_g, w_ple_gate, w_ple_proj)
        (fqT, fk, fvT, fck, fcrow, mqT, mk, mvT, mkmean, lqT, lk, lvT, *dil_qkv, gate) = _front(
            i, x2.reshape(B, S, D_MODEL), w_in, consts, lp)
        fo = _dense_mixer(
            _fox_kernel, "fox", fqT, fk, fvT, (fck, fcrow),
            (pl.BlockSpec((1, N_HEADS, S, LANES), lambda b, t: (b, 0, 0, 0)),
             pl.BlockSpec((1, 8, N_SUB * TQ), lambda b, t: (b, 0, t))))
        mo = _dense_mixer(
            _moba_kernel, "moba", mqT, mk, mvT, (mkmean, moba_tb),
            (pl.BlockSpec((1, S // MOBA_BLOCK, GROUP_W), lambda b, t: (b, 0, 0)),
             pl.BlockSpec((MOBA_TB_ENTRIES, N_HEADS, MOBA_BLOCK, MOBA_BLOCK), lambda b, t: (0, 0, 0, 0))),
            (pltpu.VMEM((N_SUB * N_HEADS, S // MOBA_BLOCK, TQ), F32),))
        lo = _dense_mixer(_mla_kernel, "mla", lqT, lk, lvT)
        dil_outs = []
        for pat, (_, dil) in enumerate(DIL_PATTERNS):
            q, k, v = (a.reshape(B, dil, S // dil, GROUP_W) for a in dil_qkv[3 * pat:3 * pat + 3])
            dil_outs += list(_dilated_pattern(pat, q, k, v, dil_tb))
        rs = lambda a: a.reshape(B * S, GROUP_W)
        x2 = _merge(i, x2, rs(fo), rs(mo), rs(lo), dil_outs, gate.reshape(B * S, D_MODEL),
                    p.reshape(depth, B * S, PLE_DIM), lp)
    return x2.reshape(B, S, D_MODEL)
```

```python
import functools
import math

import numpy as np
import jax
import jax.numpy as jnp
from jax import lax
from jax.experimental import pallas as pl
from jax.experimental.pallas import tpu as pltpu

F32 = jnp.float32
BF16 = jnp.bfloat16

D_MODEL = 1024
N_HEADS = 4
HEAD_DIM = 64
GROUP_W = N_HEADS * HEAD_DIM
MLA_NOPE = 64
MLA_ROPE = 32
MLA_HALF = MLA_ROPE // 2
MLA_Q_LORA = 256
MLA_KV_LORA = 128
PLE_DIM = 256
MOBA_BLOCK = 256
MOBA_TOPK = 3
DIL_PATTERNS = ((128, 1), (512, 4), (2048, 16))
DIL_BLOCK = 128
DIL_BLOCKS_PER_STEP = 16
N_BUCKETS = 32
MAX_DISTANCE = 2048
ROPE_THETA = 10000.0
EPS = 1e-6
NEG = -1e30
LOG2E = math.log2(math.e)

LANES = 128
VMEM_LIMIT_BYTES = 56 * 1024 * 1024

COL_FOX, COL_MOBA, COL_DIL = 0, 768, 1536
COL_CQ, COL_CKV, COL_KR, COL_FF, COL_GATE = 2304, 2560, 2688, 2816, 3072
PROJ_W = 4096
STAGE_ROWS = 128

IN_OFF = {}
IN_W = 0
for _name, _size in (("fq", GROUP_W), ("fk", GROUP_W), ("fv", GROUP_W), ("ff", N_HEADS),
                     ("mq", GROUP_W), ("mk", GROUP_W), ("mv", GROUP_W),
                     ("dq", GROUP_W), ("dk", GROUP_W), ("dv", GROUP_W),
                     ("cq", MLA_Q_LORA), ("ckv", MLA_KV_LORA), ("kr", MLA_ROPE), ("gate", D_MODEL)):
    IN_OFF[_name] = IN_W
    IN_W += _size

T_PREP = 256
FRONT_CHAINS = 2
TQ = 256
TK = 256
N_SUB = 2
COMMON_BLOCKS = 4
MOBA_TB_ENTRIES = 8
TM_MERGE = 1024
V_ROWS = HEAD_DIM + 16
MLA_GROUPS = 2


def _bucket_np(d):
    d = np.maximum(np.asarray(d, np.int64), 0)
    max_exact = N_BUCKETS // 2
    d_f = np.maximum(d, 1).astype(np.float64)
    val = np.log(d_f / max_exact) / math.log(MAX_DISTANCE / max_exact) * (N_BUCKETS - max_exact)
    frac = np.abs(val - np.round(val))
    on_edge = (frac < 1e-5) & (d > max_exact) & (val < N_BUCKETS - max_exact - 0.5)
    assert not on_edge.any(), "distance too close to a bucket edge for a static table"
    large = np.minimum(max_exact + np.floor(val + 1e-9).astype(np.int64), N_BUCKETS - 1)
    return np.where(d < max_exact, d, large)


def _bucket_steps(lo, hi):
    ds = np.arange(lo, hi + 1)
    bs = _bucket_np(ds)
    steps = [(int(ds[i]), int(bs[i])) for i in range(1, len(ds)) if bs[i] != bs[i - 1]]
    return int(bs[0]), steps


def _compiler_params(sem):
    return pltpu.CompilerParams(dimension_semantics=sem, vmem_limit_bytes=VMEM_LIMIT_BYTES)


def _bias_from_steps(d, rb_ref, col, lo, hi):
    b0, steps = _bucket_steps(lo, hi)
    val = jnp.full(d.shape, rb_ref[b0, col], F32)
    for t, b in steps:
        val = jnp.where(d >= t, rb_ref[b, col], val)
    return val


def _bias_tables_kernel(rb_ref, moba_ref, dil_ref):
    kl = lax.broadcasted_iota(jnp.int32, (MOBA_BLOCK, MOBA_BLOCK), 0)
    ql = lax.broadcasted_iota(jnp.int32, (MOBA_BLOCK, MOBA_BLOCK), 1)
    for e in range(MOBA_TB_ENTRIES - 1):
        d = jnp.maximum(e * MOBA_BLOCK + ql - kl, 0)
        lo = max(e * MOBA_BLOCK - (MOBA_BLOCK - 1), 0)
        hi = e * MOBA_BLOCK + (MOBA_BLOCK - 1)
        for h in range(N_HEADS):
            moba_ref[e, h] = _bias_from_steps(d, rb_ref, h, lo, hi) * LOG2E
    for h in range(N_HEADS):
        moba_ref[MOBA_TB_ENTRIES - 1, h] = jnp.full((MOBA_BLOCK, MOBA_BLOCK), rb_ref[N_BUCKETS - 1, h], F32) * LOG2E
    qi = lax.broadcasted_iota(jnp.int32, (DIL_BLOCK, 2 * DIL_BLOCK), 0)
    kj = lax.broadcasted_iota(jnp.int32, (DIL_BLOCK, 2 * DIL_BLOCK), 1)
    rel = qi + DIL_BLOCK - kj
    for p, (_, dil) in enumerate(DIL_PATTERNS):
        d = jnp.maximum(rel * dil, 0)
        for h in range(N_HEADS):
            bias = _bias_from_steps(d, rb_ref, N_HEADS + h, 0, (2 * DIL_BLOCK - 1) * dil)
            dil_ref[p, h] = jnp.where((rel >= 0) & (rel <= DIL_BLOCK), bias, NEG)


def _bias_tables(rel_bias):
    assert _bucket_np((MOBA_TB_ENTRIES - 1) * MOBA_BLOCK - (MOBA_BLOCK - 1)) == N_BUCKETS - 1
    return pl.pallas_call(
        _bias_tables_kernel,
        out_shape=(jax.ShapeDtypeStruct((MOBA_TB_ENTRIES, N_HEADS, MOBA_BLOCK, MOBA_BLOCK), F32),
                   jax.ShapeDtypeStruct((len(DIL_PATTERNS), N_HEADS, DIL_BLOCK, 2 * DIL_BLOCK), F32)),
        in_specs=[pl.BlockSpec(memory_space=pltpu.SMEM)],
        out_specs=(pl.BlockSpec(memory_space=pltpu.VMEM), pl.BlockSpec(memory_space=pltpu.VMEM)),
        compiler_params=pltpu.CompilerParams(vmem_limit_bytes=VMEM_LIMIT_BYTES),
        name="bias_tables",
    )(rel_bias)


def _stage_in_weights(w_ref, ekr_ref, ws_ref):
    W = GROUP_W
    moves = ((COL_FOX, IN_OFF["fq"], 3 * W), (COL_MOBA, IN_OFF["mq"], 3 * W), (COL_DIL, IN_OFF["dq"], 3 * W),
             (COL_CQ, IN_OFF["cq"], MLA_Q_LORA), (COL_CKV, IN_OFF["ckv"], MLA_KV_LORA),
             (COL_GATE, IN_OFF["gate"], D_MODEL))
    kr_base = IN_OFF["kr"] // LANES * LANES
    assert IN_OFF["ff"] % LANES == 0 and IN_OFF["kr"] + MLA_ROPE <= kr_base + LANES
    lane = lax.broadcasted_iota(jnp.int32, (STAGE_ROWS, LANES), 1)
    for r0 in range(0, D_MODEL, STAGE_ROWS):
        rows = slice(r0, r0 + STAGE_ROWS)
        for dst, src, width in moves:
            ws_ref[rows, dst:dst + width] = w_ref[0, rows, src:src + width].astype(BF16)
        kr_win = w_ref[0, rows, kr_base:kr_base + LANES].astype(BF16)
        ws_ref[rows, COL_KR:COL_KR + LANES] = jnp.dot(kr_win, ekr_ref[...], preferred_element_type=F32).astype(BF16)
        ff_win = w_ref[0, rows, IN_OFF["ff"]:IN_OFF["ff"] + LANES]
        ws_ref[rows, COL_FF:COL_FF + LANES] = jnp.where(lane < N_HEADS, ff_win, 0.0).astype(BF16)
        ws_ref[rows, COL_FF + LANES:COL_GATE] = jnp.zeros((STAGE_ROWS, COL_GATE - COL_FF - LANES), BF16)


def _group_mean_sq(x, g_mat):
    return jnp.dot((x * x).astype(BF16), g_mat, preferred_element_type=F32)


def _group_norm(x, gain, g_mat):
    return x * lax.rsqrt(_group_mean_sq(x, g_mat) + EPS) * gain


def _row_norm(x, gain):
    return x * lax.rsqrt(jnp.mean(x * x, axis=-1, keepdims=True) + EPS) * gain


def _log_sigmoid(x):
    return -(jnp.maximum(-x, 0.0) + jnp.log1p(jnp.exp(-jnp.abs(x))))


def _front_kernel(x_ref, lng_ref, w_ref, ekr_ref,
                  gains_ref, bfor_ref, qng_ref, kvng_ref, nopeg_ref, ropeg_ref,
                  wuq_ref, wukv_ref, cos_ref, sin_ref, g64_ref, g32_ref, tri_ref,
                  fqT_ref, fk_ref, fvT_ref, fck_ref, fcrow_ref,
                  mqT_ref, mk_ref, mvT_ref, mkmean_ref,
                  lqT_ref, lk_ref, lvT_ref,
                  dq1_ref, dk1_ref, dv1_ref, dq4_ref, dk4_ref, dv4_ref, dq16_ref, dk16_ref, dv16_ref,
                  gate_ref,
                  ws_ref, carry_ref, sq_ref, sk_ref, sv_ref):
    t = pl.program_id(1)

    @pl.when((pl.program_id(0) == 0) & (t == 0))
    def _():
        _stage_in_weights(w_ref, ekr_ref, ws_ref)

    g64 = g64_ref[...]
    scale = HEAD_DIM ** -0.5
    qscale = (MLA_NOPE + MLA_ROPE) ** -0.5
    W = GROUP_W
    T = T_PREP
    ones_rows = jnp.ones((V_ROWS - HEAD_DIM, T), BF16)

    @pl.when(t == 0)
    def _():
        carry_ref[...] = jnp.zeros_like(carry_ref)

    for ch in range(FRONT_CHAINS):
        rows = slice(ch * T, (ch + 1) * T)

        def store_vT(dst, v):
            vT = v.T.astype(BF16)
            for h in range(N_HEADS):
                dst[0, ch, h] = jnp.concatenate([vT[HEAD_DIM * h:HEAD_DIM * (h + 1)], ones_rows], axis=0)

        x = x_ref[0, rows, :]
        hn = (x * lax.rsqrt(jnp.mean(x * x, axis=-1, keepdims=True) + EPS) * lng_ref[...]).astype(BF16)
        project = lambda lo, width: jnp.dot(hn, ws_ref[:, lo:lo + width], preferred_element_type=F32)
        pf = project(COL_FOX, 3 * W)
        p_ff = project(COL_FF, LANES)
        pm = project(COL_MOBA, 3 * W)

        fq = _group_norm(pf[:, :W], gains_ref[0:1, :], g64) * (scale * LOG2E)
        fk = _group_norm(pf[:, W:2 * W], gains_ref[1:2, :], g64)
        fqT_ref[0, 0, :, rows] = fq.T.astype(BF16)
        fk_ref[0, 0, rows, :] = fk.astype(BF16)
        store_vT(fvT_ref, pf[:, 2 * W:])
        log_f = _log_sigmoid(p_ff + bfor_ref[...])
        tri = tri_ref[...]
        c = carry_ref[...]
        rest = log_f
        for _ in range(3):
            part = rest.astype(BF16)
            c = c + jnp.dot(tri, part, preferred_element_type=F32)
            rest = rest - part.astype(F32)
        carry_ref[...] = c[T - 1:T, :]
        c2 = c * LOG2E
        for h in range(N_HEADS):
            fck_ref[0, h, rows, :] = jnp.broadcast_to(c2[:, h:h + 1], (T, LANES))
        fcrow_ref[0, :, rows] = c2.T[0:8, :]

        pd = project(COL_DIL, 3 * W)
        mq = _group_norm(pm[:, :W], gains_ref[2:3, :], g64) * (scale * LOG2E)
        mk = _group_norm(pm[:, W:2 * W], gains_ref[3:4, :], g64)
        mqT_ref[0, 0, :, rows] = mq.T.astype(BF16)
        mk_ref[0, 0, rows, :] = mk.astype(BF16)
        store_vT(mvT_ref, pm[:, 2 * W:])
        mkmean_ref[0, pl.ds(FRONT_CHAINS * t + ch, 1), :] = jnp.mean(mk, axis=0, keepdims=True)

        p_mla = project(COL_CQ, MLA_Q_LORA + MLA_KV_LORA + LANES)
        dq = _group_norm(pd[:, :W], gains_ref[4:5, :], g64) * scale
        dk = _group_norm(pd[:, W:2 * W], gains_ref[5:6, :], g64)
        dv = pd[:, 2 * W:]
        dq1_ref[0, rows, :] = dq.astype(BF16)
        dk1_ref[0, rows, :] = dk.astype(BF16)
        dv1_ref[0, rows, :] = dv.astype(BF16)
        for val, dst in ((dq, sq_ref), (dk, sk_ref), (dv, sv_ref)):
            for half in range(GROUP_W // LANES):
                dst[ch, half] = val[:, half * LANES:(half + 1) * LANES]
        for dil, outs in ((4, (dq4_ref, dk4_ref, dv4_ref)), (16, (dq16_ref, dk16_ref, dv16_ref))):
            n = T // dil
            for r in range(dil):
                for src, dst in zip((sq_ref, sk_ref, sv_ref), outs):
                    dst[0, r, ch * n:(ch + 1) * n, :] = jnp.concatenate(
                        [src[ch, half, pl.ds(r, n, stride=dil), :] for half in range(GROUP_W // LANES)],
                        axis=1).astype(BF16)

        half_gate = D_MODEL // 2
        gate_ref[0, rows, :half_gate] = project(COL_GATE, half_gate).astype(BF16)
        cos = cos_ref[rows, :]
        sin = sin_ref[rows, :]
        cqn = _row_norm(p_mla[:, :MLA_Q_LORA], qng_ref[...]).astype(BF16)
        rotate = lambda v, cos=cos, sin=sin: v * cos + pltpu.roll(v, LANES // 2, 1) * sin
        qf = jnp.dot(cqn, wuq_ref[...], preferred_element_type=F32)
        q_nope = _group_norm(qf[:, :W], nopeg_ref[0:1, :], g64) * (qscale * LOG2E)
        ckvn = _row_norm(p_mla[:, MLA_Q_LORA:MLA_Q_LORA + MLA_KV_LORA], kvng_ref[...]).astype(BF16)
        kvf = jnp.dot(ckvn, wukv_ref[...], preferred_element_type=F32)
        k_nope = _group_norm(kvf[:, :W], nopeg_ref[1:2, :], g64)
        krs = p_mla[:, MLA_Q_LORA + MLA_KV_LORA:]
        kr_ms = jnp.sum(krs * krs, axis=-1, keepdims=True) * (1.0 / (2 * MLA_ROPE))
        kr = rotate(krs * lax.rsqrt(kr_ms + EPS) * ropeg_ref[1:2, :])
        for g in range(MLA_GROUPS):
            slab = qf[:, W + g * LANES:W + (g + 1) * LANES]
            qr = rotate(_group_norm(slab, ropeg_ref[0:1, :], g32_ref[...])) * (qscale * LOG2E)
            lq = jnp.concatenate([q_nope[:, g * LANES:(g + 1) * LANES], qr], axis=1)
            lqT_ref[0, g, :, rows] = lq.T.astype(BF16)
            lk_ref[0, g, rows, :] = jnp.concatenate([k_nope[:, g * LANES:(g + 1) * LANES], kr],
                                                    axis=1).astype(BF16)
        store_vT(lvT_ref, kvf[:, W:])
        gate_ref[0, rows, half_gate:] = project(COL_GATE + half_gate, half_gate).astype(BF16)


def _front(layer, x3, w_in, consts, lp):
    B, S, _ = x3.shape
    T = T_PREP
    TS = FRONT_CHAINS * T
    NT = S // T
    assert S % TS == 0 and S // MOBA_BLOCK == NT

    def cspec(shape):
        return pl.BlockSpec(shape, lambda b, t: (0,) * len(shape))

    in_specs = [pl.BlockSpec((1, TS, D_MODEL), lambda b, t: (b, t, 0)), cspec((1, D_MODEL)),
                pl.BlockSpec((1, D_MODEL, IN_W), lambda b, t: (layer, 0, 0), pipeline_mode=pl.Buffered(1)),
                cspec((LANES, LANES)),
                cspec((6, GROUP_W)), cspec((1, LANES)), cspec((1, MLA_Q_LORA)), cspec((1, MLA_KV_LORA)),
                cspec((2, GROUP_W)), cspec((2, LANES)),
                cspec((MLA_Q_LORA, MLA_GROUPS * GROUP_W)), cspec((MLA_KV_LORA, 2 * GROUP_W)),
                pl.BlockSpec((TS, LANES), lambda b, t: (t, 0)), pl.BlockSpec((TS, LANES), lambda b, t: (t, 0)),
                cspec((GROUP_W, GROUP_W)), cspec((LANES, LANES)), cspec((T, T))]

    def qT(groups):
        return (jax.ShapeDtypeStruct((B, groups, GROUP_W, S), BF16),
                pl.BlockSpec((1, groups, GROUP_W, TS), lambda b, t: (b, 0, 0, t)))

    def keys(groups):
        return (jax.ShapeDtypeStruct((B, groups, S, GROUP_W), BF16),
                pl.BlockSpec((1, groups, TS, GROUP_W), lambda b, t: (b, 0, t, 0)))

    def rows(w):
        return jax.ShapeDtypeStruct((B, S, w), BF16), pl.BlockSpec((1, TS, w), lambda b, t: (b, t, 0))

    def vT():
        return (jax.ShapeDtypeStruct((B, NT, N_HEADS, V_ROWS, T), BF16),
                pl.BlockSpec((1, FRONT_CHAINS, N_HEADS, V_ROWS, T), lambda b, t: (b, t, 0, 0, 0)))

    def resid(dil):
        return (jax.ShapeDtypeStruct((B, dil, S // dil, GROUP_W), BF16),
                pl.BlockSpec((1, dil, TS // dil, GROUP_W), lambda b, t: (b, 0, t, 0)))

    outs = [qT(1), keys(1), vT(),
            (jax.ShapeDtypeStruct((B, N_HEADS, S, LANES), F32),
             pl.BlockSpec((1, N_HEADS, TS, LANES), lambda b, t: (b, 0, t, 0))),
            (jax.ShapeDtypeStruct((B, 8, S), F32), pl.BlockSpec((1, 8, TS), lambda b, t: (b, 0, t))),
            qT(1), keys(1), vT(),
            (jax.ShapeDtypeStruct((B, NT, GROUP_W), F32), pl.BlockSpec((1, NT, GROUP_W), lambda b, t: (b, 0, 0))),
            qT(MLA_GROUPS), keys(MLA_GROUPS), vT(),
            rows(GROUP_W), rows(GROUP_W), rows(GROUP_W),
            resid(4), resid(4), resid(4), resid(16), resid(16), resid(16),
            rows(D_MODEL)]
    return pl.pallas_call(
        _front_kernel,
        out_shape=tuple(o[0] for o in outs),
        grid=(B, NT // FRONT_CHAINS),
        in_specs=in_specs,
        out_specs=tuple(o[1] for o in outs),
        scratch_shapes=[pltpu.VMEM((D_MODEL, PROJ_W), BF16), pltpu.VMEM((1, LANES), F32)]
        + [pltpu.VMEM((FRONT_CHAINS, GROUP_W // LANES, T, LANES), F32)] * 3,
        compiler_params=_compiler_params(("arbitrary", "arbitrary")),
        name="front",
    )(x3, lp["ln_g"], w_in, consts["ekr"],
      lp["gains"], lp["b_forget"], lp["q_norm"], lp["kv_norm"], lp["nope_gain"], lp["rope_gain"],
      lp["w_uq"], lp["w_ukv"], consts["cos"], consts["sin"], consts["g64"], consts["g32"], consts["tri"])


def _stage_head_queries(qT_ref, qm_ref, groups):
    r = lax.broadcasted_iota(jnp.int32, (GROUP_W, TQ), 0)
    for h in range(N_HEADS):
        if groups == 1:
            g, mask = 0, (r >= HEAD_DIM * h) & (r < HEAD_DIM * (h + 1))
        else:
            g, hl = divmod(h, N_HEADS // groups)
            mask = (r >= HEAD_DIM * hl) & (r < HEAD_DIM * (hl + 1))
            for base in (2 * HEAD_DIM, 2 * HEAD_DIM + LANES // 2):
                lo = base + MLA_HALF * hl
                mask = mask | ((r >= lo) & (r < lo + MLA_HALF))
        for sub in range(N_SUB):
            q = qT_ref[0, g, :, sub * TQ:(sub + 1) * TQ]
            qm_ref[sub * N_HEADS + h] = jnp.where(mask, q, jnp.zeros_like(q))


def _flash_sweep(pair_idx, groups, qm_ref, k_ref, vT_ref, m_ref, acc_ref, u_ref, o_ref, score_fn):
    kpos = lax.broadcasted_iota(jnp.int32, (TK, TQ), 0)
    qpos = lax.broadcasted_iota(jnp.int32, (TK, TQ), 1)
    causal = kpos <= qpos
    m_ref[...] = jnp.full(m_ref.shape, NEG, F32)
    acc_ref[...] = jnp.zeros(acc_ref.shape, F32)
    both = tuple(range(N_SUB))

    def scores(j0, nblk, diag_last, subs):
        row0 = pl.multiple_of(j0 * TK, TK)
        kts = [k_ref[0, g, pl.ds(row0, nblk * TK), :] for g in range(groups)]
        pending = []
        for sub in subs:
            for h in range(N_HEADS):
                c = sub * N_HEADS + h
                s = jnp.dot(kts[h // (N_HEADS // groups)], qm_ref[c], preferred_element_type=F32)
                tops, lives, row = [], [], None
                for blk in range(nblk):
                    rows = slice(blk * TK, (blk + 1) * TK)
                    diagonal = diag_last and blk == nblk - 1
                    u, row, live = score_fn(sub, h, s[rows], j0 + blk, diagonal)
                    if diagonal:
                        u = jnp.where(causal, u, NEG)
                    u_ref[c, rows, :] = u
                    top = jnp.max(u, axis=0, keepdims=True)
                    if row is not None:
                        top = top + row
                    if live is not None:
                        top = jnp.where(live, top, NEG)
                    tops.append(top)
                    lives.append(live)
                m_old = m_ref[c]
                m_new = functools.reduce(jnp.maximum, tops, m_old)
                shift = m_new if row is None else m_new - row
                shifts = [shift if live is None else jnp.where(live, shift, -NEG) for live in lives]
                m_ref[c] = m_new
                pending.append((c, h, shifts, jnp.exp2(m_old - m_new)))
        return pending

    def values(j0, pending):
        for c, h, shifts, alpha in pending:
            pv = None
            for blk, shift in enumerate(shifts):
                p = jnp.exp2((u_ref[c, blk * TK:(blk + 1) * TK, :] - shift).astype(BF16))
                term = jnp.dot(vT_ref[0, j0 + blk, h], p, preferred_element_type=F32)
                pv = term if pv is None else pv + term
            acc_ref[c] = alpha * acc_ref[c] + pv

    def common(j0, nblk):
        values(j0, scores(j0, nblk, False, both))

    def wide_step(t, carry):
        common(COMMON_BLOCKS * t, COMMON_BLOCKS)
        return carry

    n_wide = (N_SUB * pair_idx) // COMMON_BLOCKS
    lax.fori_loop(0, n_wide, wide_step, 0)
    for k in range(COMMON_BLOCKS // N_SUB - 1):
        def leftover(_, carry, k=k):
            common(COMMON_BLOCKS * n_wide + N_SUB * k, N_SUB)
            return carry

        lax.fori_loop(0, (pair_idx - (COMMON_BLOCKS // N_SUB) * n_wide > k).astype(jnp.int32), leftover, 0)
    n_common = N_SUB * pair_idx
    pend_a = scores(n_common, 1, True, (0,))
    pend_b = scores(n_common, 2, True, (1,))
    values(n_common, pend_a)
    values(n_common, pend_b)
    for sub in both:
        outs = []
        for h in range(N_HEADS):
            acc = acc_ref[sub * N_HEADS + h]
            outs.append(acc[:HEAD_DIM] / acc[HEAD_DIM:HEAD_DIM + 1])
        o_ref[0, sub * TQ:(sub + 1) * TQ, :] = jnp.concatenate(outs, axis=0).T.astype(BF16)


def _fox_kernel(qT_ref, k_ref, vT_ref, ck_ref, crow_ref, o_ref, qm_ref, m_ref, acc_ref, u_ref):
    pair_idx = pl.program_id(1)
    _stage_head_queries(qT_ref, qm_ref, 1)

    def score_fn(sub, h, s, j, diagonal):
        ck = ck_ref[0, h, pl.ds(pl.multiple_of(j * TK, TK), TK), :]
        cq = crow_ref[0, h:h + 1, sub * TQ:(sub + 1) * TQ]
        return s - jnp.concatenate([ck] * (TQ // LANES), axis=1), cq, None

    _flash_sweep(pair_idx, 1, qm_ref, k_ref, vT_ref, m_ref, acc_ref, u_ref, o_ref, score_fn)


def _mla_kernel(qT_ref, k_ref, vT_ref, o_ref, qm_ref, m_ref, acc_ref, u_ref):
    pair_idx = pl.program_id(1)
    _stage_head_queries(qT_ref, qm_ref, MLA_GROUPS)
    _flash_sweep(pair_idx, MLA_GROUPS, qm_ref, k_ref, vT_ref, m_ref, acc_ref, u_ref, o_ref,
                 lambda sub, h, s, j, diagonal: (s, None, None))


def _moba_kernel(qT_ref, k_ref, vT_ref, kmean_ref, tb_ref, o_ref, qm_ref, m_ref, acc_ref, u_ref, sel_ref):
    pair_idx = pl.program_id(1)
    _stage_head_queries(qT_ref, qm_ref, 1)
    kmean = kmean_ref[0].astype(BF16)
    nb = kmean.shape[0]
    blk = lax.broadcasted_iota(jnp.int32, (nb, TQ), 0)
    own = [N_SUB * pair_idx + sub for sub in range(N_SUB)]
    for c in range(N_SUB * N_HEADS):
        i = own[c // N_HEADS]
        gate = jnp.dot(kmean, qm_ref[c], preferred_element_type=F32)
        gate = jnp.where(blk < i, gate, NEG)
        rank = jnp.zeros((nb, TQ), jnp.int32)
        for jp in range(nb):
            row = gate[jp:jp + 1, :]
            ahead = (row > gate) | ((row == gate) & (jp < blk))
            rank = rank + ahead.astype(jnp.int32)
        sel_ref[c] = ((rank < MOBA_TOPK) & (blk < i)).astype(F32)

    def score_fn(sub, h, s, j, diagonal):
        u = s + tb_ref[jnp.minimum(own[sub] - j, MOBA_TB_ENTRIES - 1), h]
        live = None if diagonal else sel_ref[sub * N_HEADS + h, pl.ds(j, 1), :] > 0.5
        return u, None, live

    _flash_sweep(pair_idx, 1, qm_ref, k_ref, vT_ref, m_ref, acc_ref, u_ref, o_ref, score_fn)


def _dense_mixer(kernel_fn, name, qT, k, vT, extra_args=(), extra_specs=(), extra_scratch=()):
    B, groups, _, S = qT.shape
    NT = S // TK
    assert TQ == TK and N_SUB == 2 and S % (N_SUB * TQ) == 0
    n_chain = N_SUB * N_HEADS
    in_specs = [pl.BlockSpec((1, groups, GROUP_W, N_SUB * TQ), lambda b, i: (b, 0, 0, i)),
                pl.BlockSpec((1, groups, S, GROUP_W), lambda b, i: (b, 0, 0, 0)),
                pl.BlockSpec((1, NT, N_HEADS, V_ROWS, TK), lambda b, i: (b, 0, 0, 0, 0))] + list(extra_specs)
    scratch = [pltpu.VMEM((n_chain, GROUP_W, TQ), BF16),
               pltpu.VMEM((n_chain, 1, TQ), F32),
               pltpu.VMEM((n_chain, V_ROWS, TQ), F32),
               pltpu.VMEM((n_chain, COMMON_BLOCKS * TK, TQ), F32)] + list(extra_scratch)
    return pl.pallas_call(
        kernel_fn,
        out_shape=jax.ShapeDtypeStruct((B, S, GROUP_W), BF16),
        grid=(B, S // (N_SUB * TQ)),
        in_specs=in_specs,
        out_specs=pl.BlockSpec((1, N_SUB * TQ, GROUP_W), lambda b, i: (b, i, 0)),
        scratch_shapes=scratch,
        compiler_params=_compiler_params(("arbitrary", "arbitrary")),
        name=name,
    )(qT, k, vT, *extra_args)


def _dil_kernel(q_ref, kp_ref, kc_ref, vp_ref, vc_ref, tb_ref, o_ref, lse_ref):
    n = pl.program_id(2)
    qb = q_ref.shape[2] // DIL_BLOCK
    kj = lax.broadcasted_iota(jnp.int32, (DIL_BLOCK, 2 * DIL_BLOCK), 1)
    qlane = lax.broadcasted_iota(jnp.int32, (DIL_BLOCK, GROUP_W), 1)
    in_head = [(qlane >= HEAD_DIM * h) & (qlane < HEAD_DIM * (h + 1)) for h in range(N_HEADS)]
    for seq, c in [(seq, c) for seq in range(q_ref.shape[1]) for c in range(qb)]:
        rows = slice(c * DIL_BLOCK, (c + 1) * DIL_BLOCK)
        q = q_ref[0, seq, rows, :]
        if c == 0:
            kb = jnp.concatenate([kp_ref[0, seq], kc_ref[0, seq, :DIL_BLOCK, :]], axis=0)
            vb = jnp.concatenate([vp_ref[0, seq], vc_ref[0, seq, :DIL_BLOCK, :]], axis=0)
        else:
            kb = kc_ref[0, seq, (c - 1) * DIL_BLOCK:(c + 1) * DIL_BLOCK, :]
            vb = vc_ref[0, seq, (c - 1) * DIL_BLOCK:(c + 1) * DIL_BLOCK, :]
        q_heads = jnp.concatenate([jnp.where(msk, q, jnp.zeros_like(q)) for msk in in_head], axis=0)
        s = lax.dot_general(q_heads, kb, (((1,), (1,)), ((), ())), preferred_element_type=F32)
        s = s.reshape(N_HEADS, DIL_BLOCK, 2 * DIL_BLOCK)
        s = s + tb_ref[0]
        if c == 0:
            s = jnp.where(((kj >= DIL_BLOCK) | (n > 0))[None], s, NEG)
        m = jnp.max(s, axis=-1, keepdims=True)
        e = jnp.exp(s - m)
        l = jnp.sum(e, axis=-1, keepdims=True)
        lse = m + jnp.log(l)
        pr = (e * (1.0 / l)).astype(BF16)
        o_all = jnp.dot(pr.reshape(N_HEADS * DIL_BLOCK, 2 * DIL_BLOCK), vb, preferred_element_type=F32)
        o_all = o_all.reshape(N_HEADS, DIL_BLOCK, GROUP_W)
        o_acc = jnp.zeros((DIL_BLOCK, GROUP_W), F32)
        lse_map = jnp.zeros((DIL_BLOCK, GROUP_W), F32)
        for h in range(N_HEADS):
            o_acc = jnp.where(in_head[h], o_all[h], o_acc)
            lse_map = jnp.where(in_head[h], lse[h], lse_map)
        o_ref[0, seq, rows, :] = o_acc.astype(BF16)
        lse_ref[0, seq, rows, :] = lse_map


def _dilated_pattern(p, q, k, v, dil_tb):
    B, dil, L, _ = q.shape
    assert dil == DIL_PATTERNS[p][1] and L % DIL_BLOCK == 0
    nb = L // DIL_BLOCK
    qb = math.gcd(nb, DIL_BLOCKS_PER_STEP)
    seqs = math.gcd(dil, DIL_BLOCKS_PER_STEP // qb)
    cur = pl.BlockSpec((1, seqs, qb * DIL_BLOCK, GROUP_W), lambda b, r, n: (b, r, n, 0))
    prev = pl.BlockSpec((1, seqs, DIL_BLOCK, GROUP_W), lambda b, r, n: (b, r, jnp.maximum(n * qb - 1, 0), 0))
    return pl.pallas_call(
        _dil_kernel,
        out_shape=(jax.ShapeDtypeStruct((B, dil, L, GROUP_W), BF16),
                   jax.ShapeDtypeStruct((B, dil, L, GROUP_W), F32)),
        grid=(B, dil // seqs, nb // qb),
        in_specs=[cur, prev, cur, prev, cur,
                  pl.BlockSpec((1, N_HEADS, DIL_BLOCK, 2 * DIL_BLOCK), lambda b, r, n, p=p: (p, 0, 0, 0))],
        out_specs=(cur, cur),
        compiler_params=_compiler_params(("arbitrary", "arbitrary", "arbitrary")),
        name=f"dilated_{dil}",
    )(q, k, k, v, v, dil_tb)


def _mix_dilations(o_refs, lse_refs, scratch):
    halves = GROUP_W // LANES
    rows = o_refs[0].shape[1] * o_refs[0].shape[2]
    nat = []
    for ref, dst in zip(list(o_refs) + list(lse_refs), scratch):
        dil = ref.shape[1]
        if dil == 1:
            nat.append(ref[0, 0].astype(F32))
            continue
        for r in range(dil):
            for half in range(halves):
                dst[half, pl.ds(r, rows // dil, stride=dil), :] = (
                    ref[0, r, :, half * LANES:(half + 1) * LANES].astype(F32))
        nat.append(jnp.concatenate([dst[half] for half in range(halves)], axis=1))
    os, lses = nat[:len(o_refs)], nat[len(o_refs):]
    m = functools.reduce(jnp.maximum, lses)
    ws = [jnp.exp(l - m) for l in lses]
    tot = functools.reduce(jnp.add, ws)
    return functools.reduce(jnp.add, [(w / tot) * o for w, o in zip(ws, os)])


def _sigmoid(x):
    return 1.0 / (1.0 + jnp.exp(-x))


def _merge_kernel(x_ref, fo_ref, mo_ref, lo_ref, o1_ref, l1_ref, o4_ref, l4_ref, o16_ref, l16_ref, gate_ref, p_ref,
                  wout_ref, png_ref, wpg_ref, wpp_ref, o_ref, *dil_scratch):
    o_dil = _mix_dilations((o1_ref, o4_ref, o16_ref), (l1_ref, l4_ref, l16_ref), dil_scratch)
    mix = jnp.concatenate([fo_ref[...].astype(F32), mo_ref[...].astype(F32), o_dil, lo_ref[...].astype(F32)],
                          axis=1)
    g = gate_ref[...].astype(F32)
    y = jnp.dot((mix * (g * _sigmoid(g))).astype(BF16), wout_ref[...], preferred_element_type=F32)
    x1 = x_ref[...] + y
    hn = _row_norm(x1, png_ref[...]).astype(BF16)
    pg = _sigmoid(jnp.dot(hn, wpg_ref[...], preferred_element_type=F32))
    pp = jnp.dot(p_ref[0].astype(BF16), wpp_ref[...], preferred_element_type=F32)
    o_ref[...] = x1 + pg * pp


def _merge(layer, x2, fo, mo, lo, dil_outs, proj, p3, lp):
    rows = x2.shape[0]
    TM = TM_MERGE
    tiles_per_seq = dil_outs[0].shape[1] * dil_outs[0].shape[2] // TM

    def rspec(w, c=0):
        return pl.BlockSpec((TM, w), lambda i, c=c: (i, c))

    def cspec(shape):
        return pl.BlockSpec(shape, lambda i: (0, 0))

    def dspec(a):
        dil = a.shape[1]
        return pl.BlockSpec((1, dil, TM // dil, GROUP_W),
                            lambda i: (i // tiles_per_seq, 0, lax.rem(i, tiles_per_seq), 0))

    return pl.pallas_call(
        _merge_kernel,
        out_shape=jax.ShapeDtypeStruct((rows, D_MODEL), F32),
        grid=(rows // TM,),
        scratch_shapes=[pltpu.VMEM((GROUP_W // LANES, TM, LANES), F32)] * len(dil_outs),
        in_specs=[rspec(D_MODEL)] + [rspec(GROUP_W)] * 3 + [dspec(a) for a in dil_outs] + [
                  rspec(D_MODEL),
                  pl.BlockSpec((1, TM, PLE_DIM), lambda i: (layer, i, 0)),
                  cspec((D_MODEL, D_MODEL)), cspec((1, D_MODEL)), cspec((D_MODEL, D_MODEL)),
                  cspec((PLE_DIM, D_MODEL))],
        out_specs=rspec(D_MODEL),
        compiler_params=_compiler_params(("arbitrary",)),
        name="merge",
    )(x2, fo, mo, lo, *dil_outs, proj, p3, lp["w_out"], lp["ple_norm_g"], lp["w_ple_gate"], lp["w_ple_proj"])


def _layer_params(i, ln_g, b_forget, qk_gain, mla_q_norm, mla_kv_norm, mla_nope_gain,
                  mla_rope_gain, w_uq, w_ukv, w_out, ple_norm_g, w_ple_gate, w_ple_proj):
    def rope_slab(x1_parts, x2_parts, like):
        pad = jnp.zeros(like.shape[:-1] + (LANES // 2 - 2 * MLA_HALF,), like.dtype)
        return jnp.concatenate(list(x1_parts) + [pad] + list(x2_parts) + [pad], axis=-1)

    per_head = MLA_NOPE + MLA_ROPE
    uq = w_uq[i]
    uq_nope = [uq[:, h * per_head:h * per_head + MLA_NOPE] for h in range(N_HEADS)]
    uq_x1 = [uq[:, h * per_head + MLA_NOPE:h * per_head + MLA_NOPE + MLA_HALF] for h in range(N_HEADS)]
    uq_x2 = [uq[:, h * per_head + MLA_NOPE + MLA_HALF:(h + 1) * per_head] for h in range(N_HEADS)]
    hpg = N_HEADS // MLA_GROUPS
    uq_cols = uq_nope + [rope_slab(uq_x1[g * hpg:(g + 1) * hpg], uq_x2[g * hpg:(g + 1) * hpg], uq)
                         for g in range(MLA_GROUPS)]
    ukv = w_ukv[i]
    ukv_cols = ([ukv[:, h * 2 * HEAD_DIM:h * 2 * HEAD_DIM + HEAD_DIM] for h in range(N_HEADS)]
                + [ukv[:, h * 2 * HEAD_DIM + HEAD_DIM:(h + 1) * 2 * HEAD_DIM] for h in range(N_HEADS)])
    rg = mla_rope_gain[i]
    rope_gain = rope_slab([rg[:, :MLA_HALF]] * hpg, [rg[:, MLA_HALF:]] * hpg, rg)
    return {
        "ln_g": ln_g[i][None, :],
        "gains": jnp.tile(qk_gain[i], (1, N_HEADS)),
        "b_forget": jnp.pad(b_forget[i], (0, LANES - N_HEADS))[None, :],
        "q_norm": mla_q_norm[i][None, :],
        "kv_norm": mla_kv_norm[i][None, :],
        "nope_gain": jnp.tile(mla_nope_gain[i], (1, N_HEADS)),
        "rope_gain": rope_gain,
        "w_uq": jnp.concatenate(uq_cols, axis=1).astype(BF16),
        "w_ukv": jnp.concatenate(ukv_cols, axis=1).astype(BF16),
        "w_out": w_out[i].astype(BF16),
        "ple_norm_g": ple_norm_g[i][None, :],
        "w_ple_gate": w_ple_gate[i].astype(BF16),
        "w_ple_proj": w_ple_proj[i].astype(BF16),
    }


def _constants(S):
    inv = 1.0 / (ROPE_THETA ** (jnp.arange(MLA_HALF, dtype=F32) * 2.0 / MLA_ROPE))
    ang = jnp.arange(S).astype(F32)[:, None] * inv[None, :]
    cos = jnp.tile(jnp.cos(ang), (1, 2 * N_HEADS))
    sin = jnp.tile(jnp.sin(ang), (1, 2 * N_HEADS))
    sign = np.concatenate([-np.ones(LANES // 2, np.float32), np.ones(LANES // 2, np.float32)])
    lane = np.arange(GROUP_W)
    g64 = (lane[:, None] // HEAD_DIM == lane[None, :] // HEAD_DIM).astype(np.float32) / HEAD_DIM
    within = np.arange(LANES) % (LANES // 2)
    rope_head = np.where(within < 2 * MLA_HALF, within // MLA_HALF, -1)
    g32 = ((rope_head[:, None] == rope_head[None, :]) & (rope_head[:, None] >= 0)).astype(np.float32) / MLA_ROPE
    tri = np.tril(np.ones((T_PREP, T_PREP), np.float32))
    ekr = np.zeros((LANES, LANES), np.float32)
    kr_lane0 = IN_OFF["kr"] % LANES
    for c in range(MLA_ROPE):
        half, idx = divmod(c, MLA_HALF)
        for copy in range(N_HEADS // MLA_GROUPS):
            ekr[kr_lane0 + c, half * (LANES // 2) + copy * MLA_HALF + idx] = 1.0
    return {"cos": cos, "sin": sin * sign[None, :], "g64": jnp.asarray(g64, BF16),
            "g32": jnp.asarray(g32, BF16), "tri": jnp.asarray(tri, BF16), "ekr": jnp.asarray(ekr, BF16)}


def kernel(x, p, ln_g, w_in, b_forget, qk_gain, mla_q_norm, mla_kv_norm, mla_nope_gain, mla_rope_gain,
           w_uq, w_ukv, w_out, rel_bias, ple_norm_g, w_ple_gate, w_ple_proj):
    B, S, _ = x.shape
    depth = p.shape[0]
    consts = _constants(S)
    moba_tb, dil_tb = _bias_tables(rel_bias)
    x2 = x.reshape(B * S, D_MODEL)
    w_in_bf16 = w_in.astype(BF16)
    for i in range(depth):
        lp = _layer_params(i, ln_g, b_forget, qk_gain, mla_q_norm, mla_kv_norm, mla_nope_gain,
                           mla_rope_gain, w_uq, w_ukv, w_out, ple_norm_g, w_ple_gate, w_ple_proj)
        (fqT, fk, fvT, fck, fcrow, mqT, mk, mvT, mkmean, lqT, lk, lvT, *dil_qkv, gate) = _front(
            i, x2.reshape(B, S, D_MODEL), w_in_bf16, consts, lp)
        fo = _dense_mixer(
            _fox_kernel, "fox", fqT, fk, fvT, (fck, fcrow),
            (pl.BlockSpec((1, N_HEADS, S, LANES), lambda b, t: (b, 0, 0, 0)),
             pl.BlockSpec((1, 8, N_SUB * TQ), lambda b, t: (b, 0, t))))
        mo = _dense_mixer(
            _moba_kernel, "moba", mqT, mk, mvT, (mkmean, moba_tb),
            (pl.BlockSpec((1, S // MOBA_BLOCK, GROUP_W), lambda b, t: (b, 0, 0)),
             pl.BlockSpec((MOBA_TB_ENTRIES, N_HEADS, MOBA_BLOCK, MOBA_BLOCK), lambda b, t: (0, 0, 0, 0))),
            (pltpu.VMEM((N_SUB * N_HEADS, S // MOBA_BLOCK, TQ), F32),))
        lo = _dense_mixer(_mla_kernel, "mla", lqT, lk, lvT)
        dil_outs = []
        for pat, (_, dil) in enumerate(DIL_PATTERNS):
            q, k, v = (a.reshape(B, dil, S // dil, GROUP_W) for a in dil_qkv[3 * pat:3 * pat + 3])
            dil_outs += list(_dilated_pattern(pat, q, k, v, dil_tb))
        rs = lambda a: a.reshape(B * S, GROUP_W)
        x2 = _merge(i, x2, rs(fo), rs(mo), rs(lo), dil_outs, gate.reshape(B * S, D_MODEL),
                    p.reshape(depth, B * S, PLE_DIM), lp)
    return x2.reshape(B, S, D_MODEL)
```

```python
import functools
import math

import numpy as np
import jax
import jax.numpy as jnp
from jax import lax
from jax.experimental import pallas as pl
from jax.experimental.pallas import tpu as pltpu

F32 = jnp.float32
BF16 = jnp.bfloat16

D_MODEL = 1024
N_HEADS = 4
HEAD_DIM = 64
GROUP_W = N_HEADS * HEAD_DIM
MLA_NOPE = 64
MLA_ROPE = 32
MLA_HALF = MLA_ROPE // 2
MLA_Q_LORA = 256
MLA_KV_LORA = 128
PLE_DIM = 256
MOBA_BLOCK = 256
MOBA_TOPK = 3
DIL_PATTERNS = ((128, 1), (512, 4), (2048, 16))
DIL_BLOCK = 128
DIL_BLOCKS_PER_STEP = 16
N_BUCKETS = 32
MAX_DISTANCE = 2048
ROPE_THETA = 10000.0
EPS = 1e-6
NEG = -1e30
LOG2E = math.log2(math.e)

LANES = 128
VMEM_LIMIT_BYTES = 56 * 1024 * 1024

COL_FOX, COL_MOBA, COL_DIL = 0, 768, 1536
COL_CQ, COL_CKV, COL_KR, COL_FF, COL_GATE = 2304, 2560, 2688, 2816, 3072
PROJ_W = 4096
STAGE_ROWS = 128

IN_OFF = {}
IN_W = 0
for _name, _size in (("fq", GROUP_W), ("fk", GROUP_W), ("fv", GROUP_W), ("ff", N_HEADS),
                     ("mq", GROUP_W), ("mk", GROUP_W), ("mv", GROUP_W),
                     ("dq", GROUP_W), ("dk", GROUP_W), ("dv", GROUP_W),
                     ("cq", MLA_Q_LORA), ("ckv", MLA_KV_LORA), ("kr", MLA_ROPE), ("gate", D_MODEL)):
    IN_OFF[_name] = IN_W
    IN_W += _size

T_PREP = 256
FRONT_CHAINS = 2
TQ = 256
TK = 256
N_SUB = 2
COMMON_BLOCKS = 4
MOBA_TB_ENTRIES = 8
TM_MERGE = 1024
V_ROWS = HEAD_DIM + 16
MLA_GROUPS = 2


def _bucket_np(d):
    d = np.maximum(np.asarray(d, np.int64), 0)
    max_exact = N_BUCKETS // 2
    d_f = np.maximum(d, 1).astype(np.float64)
    val = np.log(d_f / max_exact) / math.log(MAX_DISTANCE / max_exact) * (N_BUCKETS - max_exact)
    frac = np.abs(val - np.round(val))
    on_edge = (frac < 1e-5) & (d > max_exact) & (val < N_BUCKETS - max_exact - 0.5)
    assert not on_edge.any(), "distance too close to a bucket edge for a static table"
    large = np.minimum(max_exact + np.floor(val + 1e-9).astype(np.int64), N_BUCKETS - 1)
    return np.where(d < max_exact, d, large)


def _bucket_steps(lo, hi):
    ds = np.arange(lo, hi + 1)
    bs = _bucket_np(ds)
    steps = [(int(ds[i]), int(bs[i])) for i in range(1, len(ds)) if bs[i] != bs[i - 1]]
    return int(bs[0]), steps


def _compiler_params(sem):
    return pltpu.CompilerParams(dimension_semantics=sem, vmem_limit_bytes=VMEM_LIMIT_BYTES)


def _bias_from_steps(d, rb_ref, col, lo, hi):
    b0, steps = _bucket_steps(lo, hi)
    val = jnp.full(d.shape, rb_ref[b0, col], F32)
    for t, b in steps:
        val = jnp.where(d >= t, rb_ref[b, col], val)
    return val


def _bias_tables_kernel(rb_ref, moba_ref, dil_ref):
    kl = lax.broadcasted_iota(jnp.int32, (MOBA_BLOCK, MOBA_BLOCK), 0)
    ql = lax.broadcasted_iota(jnp.int32, (MOBA_BLOCK, MOBA_BLOCK), 1)
    for e in range(MOBA_TB_ENTRIES - 1):
        d = jnp.maximum(e * MOBA_BLOCK + ql - kl, 0)
        lo = max(e * MOBA_BLOCK - (MOBA_BLOCK - 1), 0)
        hi = e * MOBA_BLOCK + (MOBA_BLOCK - 1)
        for h in range(N_HEADS):
            moba_ref[e, h] = _bias_from_steps(d, rb_ref, h, lo, hi) * LOG2E
    for h in range(N_HEADS):
        moba_ref[MOBA_TB_ENTRIES - 1, h] = jnp.full((MOBA_BLOCK, MOBA_BLOCK), rb_ref[N_BUCKETS - 1, h], F32) * LOG2E
    qi = lax.broadcasted_iota(jnp.int32, (DIL_BLOCK, 2 * DIL_BLOCK), 0)
    kj = lax.broadcasted_iota(jnp.int32, (DIL_BLOCK, 2 * DIL_BLOCK), 1)
    rel = qi + DIL_BLOCK - kj
    for p, (_, dil) in enumerate(DIL_PATTERNS):
        d = jnp.maximum(rel * dil, 0)
        for h in range(N_HEADS):
            bias = _bias_from_steps(d, rb_ref, N_HEADS + h, 0, (2 * DIL_BLOCK - 1) * dil)
            dil_ref[p, h] = jnp.where((rel >= 0) & (rel <= DIL_BLOCK), bias, NEG)


def _bias_tables(rel_bias):
    assert _bucket_np((MOBA_TB_ENTRIES - 1) * MOBA_BLOCK - (MOBA_BLOCK - 1)) == N_BUCKETS - 1
    return pl.pallas_call(
        _bias_tables_kernel,
        out_shape=(jax.ShapeDtypeStruct((MOBA_TB_ENTRIES, N_HEADS, MOBA_BLOCK, MOBA_BLOCK), F32),
                   jax.ShapeDtypeStruct((len(DIL_PATTERNS), N_HEADS, DIL_BLOCK, 2 * DIL_BLOCK), F32)),
        in_specs=[pl.BlockSpec(memory_space=pltpu.SMEM)],
        out_specs=(pl.BlockSpec(memory_space=pltpu.VMEM), pl.BlockSpec(memory_space=pltpu.VMEM)),
        compiler_params=pltpu.CompilerParams(vmem_limit_bytes=VMEM_LIMIT_BYTES),
        name="bias_tables",
    )(rel_bias)


def _stage_in_weights(w_ref, ekr_ref, ws_ref):
    W = GROUP_W
    moves = ((COL_FOX, IN_OFF["fq"], 3 * W), (COL_MOBA, IN_OFF["mq"], 3 * W), (COL_DIL, IN_OFF["dq"], 3 * W),
             (COL_CQ, IN_OFF["cq"], MLA_Q_LORA), (COL_CKV, IN_OFF["ckv"], MLA_KV_LORA),
             (COL_GATE, IN_OFF["gate"], D_MODEL))
    kr_base = IN_OFF["kr"] // LANES * LANES
    assert IN_OFF["ff"] % LANES == 0 and IN_OFF["kr"] + MLA_ROPE <= kr_base + LANES
    lane = lax.broadcasted_iota(jnp.int32, (STAGE_ROWS, LANES), 1)
    for r0 in range(0, D_MODEL, STAGE_ROWS):
        rows = slice(r0, r0 + STAGE_ROWS)
        for dst, src, width in moves:
            ws_ref[rows, dst:dst + width] = w_ref[0, rows, src:src + width].astype(BF16)
        kr_win = w_ref[0, rows, kr_base:kr_base + LANES].astype(BF16)
        ws_ref[rows, COL_KR:COL_KR + LANES] = jnp.dot(kr_win, ekr_ref[...], preferred_element_type=F32).astype(BF16)
        ff_win = w_ref[0, rows, IN_OFF["ff"]:IN_OFF["ff"] + LANES]
        ws_ref[rows, COL_FF:COL_FF + LANES] = jnp.where(lane < N_HEADS, ff_win, 0.0).astype(BF16)
        ws_ref[rows, COL_FF + LANES:COL_GATE] = jnp.zeros((STAGE_ROWS, COL_GATE - COL_FF - LANES), BF16)


def _group_mean_sq(x, g_mat):
    return jnp.dot((x * x).astype(BF16), g_mat, preferred_element_type=F32)


def _group_norm(x, gain, g_mat):
    return x * lax.rsqrt(_group_mean_sq(x, g_mat) + EPS) * gain


def _row_norm(x, gain):
    return x * lax.rsqrt(jnp.mean(x * x, axis=-1, keepdims=True) + EPS) * gain


def _log_sigmoid(x):
    return -(jnp.maximum(-x, 0.0) + jnp.log1p(jnp.exp(-jnp.abs(x))))


def _front_kernel(x_ref, lng_ref, w_ref, ekr_ref,
                  gains_ref, bfor_ref, qng_ref, kvng_ref, nopeg_ref, ropeg_ref,
                  wuq_ref, wukv_ref, cos_ref, sin_ref, g64_ref, g32_ref, tri_ref,
                  fqT_ref, fk_ref, fvT_ref, fck_ref, fcrow_ref,
                  mqT_ref, mk_ref, mvT_ref, mkmean_ref,
                  lqT_ref, lk_ref, lvT_ref,
                  dq1_ref, dk1_ref, dv1_ref, dq4_ref, dk4_ref, dv4_ref, dq16_ref, dk16_ref, dv16_ref,
                  gate_ref,
                  ws_ref, carry_ref, sq_ref, sk_ref, sv_ref):
    t = pl.program_id(1)

    @pl.when((pl.program_id(0) == 0) & (t == 0))
    def _():
        _stage_in_weights(w_ref, ekr_ref, ws_ref)

    g64 = g64_ref[...]
    scale = HEAD_DIM ** -0.5
    qscale = (MLA_NOPE + MLA_ROPE) ** -0.5
    W = GROUP_W
    T = T_PREP
    ones_rows = jnp.ones((V_ROWS - HEAD_DIM, T), BF16)

    @pl.when(t == 0)
    def _():
        carry_ref[...] = jnp.zeros_like(carry_ref)

    for ch in range(FRONT_CHAINS):
        rows = slice(ch * T, (ch + 1) * T)

        def store_vT(dst, v):
            vT = v.T.astype(BF16)
            for h in range(N_HEADS):
                dst[0, ch, h] = jnp.concatenate([vT[HEAD_DIM * h:HEAD_DIM * (h + 1)], ones_rows], axis=0)

        x = x_ref[0, rows, :]
        hn = (x * lax.rsqrt(jnp.mean(x * x, axis=-1, keepdims=True) + EPS) * lng_ref[...]).astype(BF16)
        project = lambda lo, width: jnp.dot(hn, ws_ref[:, lo:lo + width], preferred_element_type=F32)
        pf = project(COL_FOX, 3 * W)
        p_ff = project(COL_FF, LANES)
        pm = project(COL_MOBA, 3 * W)

        fq = _group_norm(pf[:, :W], gains_ref[0:1, :], g64) * (scale * LOG2E)
        fk = _group_norm(pf[:, W:2 * W], gains_ref[1:2, :], g64)
        fqT_ref[0, 0, :, rows] = fq.T.astype(BF16)
        fk_ref[0, 0, rows, :] = fk.astype(BF16)
        store_vT(fvT_ref, pf[:, 2 * W:])
        log_f = _log_sigmoid(p_ff + bfor_ref[...])
        tri = tri_ref[...]
        c = carry_ref[...]
        rest = log_f
        for _ in range(3):
            part = rest.astype(BF16)
            c = c + jnp.dot(tri, part, preferred_element_type=F32)
            rest = rest - part.astype(F32)
        carry_ref[...] = c[T - 1:T, :]
        c2 = c * LOG2E
        for h in range(N_HEADS):
            fck_ref[0, h, rows, :] = jnp.broadcast_to(c2[:, h:h + 1], (T, LANES))
        fcrow_ref[0, :, rows] = c2.T[0:8, :]

        pd = project(COL_DIL, 3 * W)
        mq = _group_norm(pm[:, :W], gains_ref[2:3, :], g64) * (scale * LOG2E)
        mk = _group_norm(pm[:, W:2 * W], gains_ref[3:4, :], g64)
        mqT_ref[0, 0, :, rows] = mq.T.astype(BF16)
        mk_ref[0, 0, rows, :] = mk.astype(BF16)
        store_vT(mvT_ref, pm[:, 2 * W:])
        mkmean_ref[0, pl.ds(FRONT_CHAINS * t + ch, 1), :] = jnp.mean(mk, axis=0, keepdims=True)

        p_mla = project(COL_CQ, MLA_Q_LORA + MLA_KV_LORA + LANES)
        dq = _group_norm(pd[:, :W], gains_ref[4:5, :], g64) * scale
        dk = _group_norm(pd[:, W:2 * W], gains_ref[5:6, :], g64)
        dv = pd[:, 2 * W:]
        dq1_ref[0, rows, :] = dq.astype(BF16)
        dk1_ref[0, rows, :] = dk.astype(BF16)
        dv1_ref[0, rows, :] = dv.astype(BF16)
        for val, dst in ((dq, sq_ref), (dk, sk_ref), (dv, sv_ref)):
            for half in range(GROUP_W // LANES):
                dst[ch, half] = val[:, half * LANES:(half + 1) * LANES]
        for dil, outs in ((4, (dq4_ref, dk4_ref, dv4_ref)), (16, (dq16_ref, dk16_ref, dv16_ref))):
            n = T // dil
            for r in range(dil):
                for src, dst in zip((sq_ref, sk_ref, sv_ref), outs):
                    dst[0, r, ch * n:(ch + 1) * n, :] = jnp.concatenate(
                        [src[ch, half, pl.ds(r, n, stride=dil), :] for half in range(GROUP_W // LANES)],
                        axis=1).astype(BF16)

        half_gate = D_MODEL // 2
        gate_ref[0, rows, :half_gate] = project(COL_GATE, half_gate).astype(BF16)
        cos = cos_ref[rows, :]
        sin = sin_ref[rows, :]
        cqn = _row_norm(p_mla[:, :MLA_Q_LORA], qng_ref[...]).astype(BF16)
        rotate = lambda v, cos=cos, sin=sin: v * cos + pltpu.roll(v, LANES // 2, 1) * sin
        qf = jnp.dot(cqn, wuq_ref[...], preferred_element_type=F32)
        q_nope = _group_norm(qf[:, :W], nopeg_ref[0:1, :], g64) * (qscale * LOG2E)
        ckvn = _row_norm(p_mla[:, MLA_Q_LORA:MLA_Q_LORA + MLA_KV_LORA], kvng_ref[...]).astype(BF16)
        kvf = jnp.dot(ckvn, wukv_ref[...], preferred_element_type=F32)
        k_nope = _group_norm(kvf[:, :W], nopeg_ref[1:2, :], g64)
        krs = p_mla[:, MLA_Q_LORA + MLA_KV_LORA:]
        kr_ms = jnp.sum(krs * krs, axis=-1, keepdims=True) * (1.0 / (2 * MLA_ROPE))
        kr = rotate(krs * lax.rsqrt(kr_ms + EPS) * ropeg_ref[1:2, :])
        for g in range(MLA_GROUPS):
            slab = qf[:, W + g * LANES:W + (g + 1) * LANES]
            qr = rotate(_group_norm(slab, ropeg_ref[0:1, :], g32_ref[...])) * (qscale * LOG2E)
            lq = jnp.concatenate([q_nope[:, g * LANES:(g + 1) * LANES], qr], axis=1)
            lqT_ref[0, g, :, rows] = lq.T.astype(BF16)
            lk_ref[0, g, rows, :] = jnp.concatenate([k_nope[:, g * LANES:(g + 1) * LANES], kr],
                                                    axis=1).astype(BF16)
        store_vT(lvT_ref, kvf[:, W:])
        gate_ref[0, rows, half_gate:] = project(COL_GATE + half_gate, half_gate).astype(BF16)


def _front(layer, x3, w_in, consts, lp):
    B, S, _ = x3.shape
    T = T_PREP
    TS = FRONT_CHAINS * T
    NT = S // T
    assert S % TS == 0 and S // MOBA_BLOCK == NT

    def cspec(shape):
        return pl.BlockSpec(shape, lambda b, t: (0,) * len(shape))

    in_specs = [pl.BlockSpec((1, TS, D_MODEL), lambda b, t: (b, t, 0)), cspec((1, D_MODEL)),
                pl.BlockSpec((1, D_MODEL, IN_W), lambda b, t: (layer, 0, 0), pipeline_mode=pl.Buffered(1)),
                cspec((LANES, LANES)),
                cspec((6, GROUP_W)), cspec((1, LANES)), cspec((1, MLA_Q_LORA)), cspec((1, MLA_KV_LORA)),
                cspec((2, GROUP_W)), cspec((2, LANES)),
                cspec((MLA_Q_LORA, MLA_GROUPS * GROUP_W)), cspec((MLA_KV_LORA, 2 * GROUP_W)),
                pl.BlockSpec((TS, LANES), lambda b, t: (t, 0)), pl.BlockSpec((TS, LANES), lambda b, t: (t, 0)),
                cspec((GROUP_W, GROUP_W)), cspec((LANES, LANES)), cspec((T, T))]

    def qT(groups):
        return (jax.ShapeDtypeStruct((B, groups, GROUP_W, S), BF16),
                pl.BlockSpec((1, groups, GROUP_W, TS), lambda b, t: (b, 0, 0, t)))

    def keys(groups):
        return (jax.ShapeDtypeStruct((B, groups, S, GROUP_W), BF16),
                pl.BlockSpec((1, groups, TS, GROUP_W), lambda b, t: (b, 0, t, 0)))

    def rows(w):
        return jax.ShapeDtypeStruct((B, S, w), BF16), pl.BlockSpec((1, TS, w), lambda b, t: (b, t, 0))

    def vT():
        return (jax.ShapeDtypeStruct((B, NT, N_HEADS, V_ROWS, T), BF16),
                pl.BlockSpec((1, FRONT_CHAINS, N_HEADS, V_ROWS, T), lambda b, t: (b, t, 0, 0, 0)))

    def resid(dil):
        return (jax.ShapeDtypeStruct((B, dil, S // dil, GROUP_W), BF16),
                pl.BlockSpec((1, dil, TS // dil, GROUP_W), lambda b, t: (b, 0, t, 0)))

    outs = [qT(1), keys(1), vT(),
            (jax.ShapeDtypeStruct((B, N_HEADS, S, LANES), F32),
             pl.BlockSpec((1, N_HEADS, TS, LANES), lambda b, t: (b, 0, t, 0))),
            (jax.ShapeDtypeStruct((B, 8, S), F32), pl.BlockSpec((1, 8, TS), lambda b, t: (b, 0, t))),
            qT(1), keys(1), vT(),
            (jax.ShapeDtypeStruct((B, NT, GROUP_W), F32), pl.BlockSpec((1, NT, GROUP_W), lambda b, t: (b, 0, 0))),
            qT(MLA_GROUPS), keys(MLA_GROUPS), vT(),
            rows(GROUP_W), rows(GROUP_W), rows(GROUP_W),
            resid(4), resid(4), resid(4), resid(16), resid(16), resid(16),
            rows(D_MODEL)]
    return pl.pallas_call(
        _front_kernel,
        out_shape=tuple(o[0] for o in outs),
        grid=(B, NT // FRONT_CHAINS),
        in_specs=in_specs,
        out_specs=tuple(o[1] for o in outs),
        scratch_shapes=[pltpu.VMEM((D_MODEL, PROJ_W), BF16), pltpu.VMEM((1, LANES), F32)]
        + [pltpu.VMEM((FRONT_CHAINS, GROUP_W // LANES, T, LANES), F32)] * 3,
        compiler_params=_compiler_params(("arbitrary", "arbitrary")),
        name="front",
    )(x3, lp["ln_g"], w_in, consts["ekr"],
      lp["gains"], lp["b_forget"], lp["q_norm"], lp["kv_norm"], lp["nope_gain"], lp["rope_gain"],
      lp["w_uq"], lp["w_ukv"], consts["cos"], consts["sin"], consts["g64"], consts["g32"], consts["tri"])


def _stage_head_queries(qT_ref, qm_ref, groups):
    r = lax.broadcasted_iota(jnp.int32, (GROUP_W, TQ), 0)
    for h in range(N_HEADS):
        if groups == 1:
            g, mask = 0, (r >= HEAD_DIM * h) & (r < HEAD_DIM * (h + 1))
        else:
            g, hl = divmod(h, N_HEADS // groups)
            mask = (r >= HEAD_DIM * hl) & (r < HEAD_DIM * (hl + 1))
            for base in (2 * HEAD_DIM, 2 * HEAD_DIM + LANES // 2):
                lo = base + MLA_HALF * hl
                mask = mask | ((r >= lo) & (r < lo + MLA_HALF))
        for sub in range(N_SUB):
            q = qT_ref[0, g, :, sub * TQ:(sub + 1) * TQ]
            qm_ref[sub * N_HEADS + h] = jnp.where(mask, q, jnp.zeros_like(q))


def _flash_sweep(pair_idx, groups, qm_ref, k_ref, vT_ref, m_ref, acc_ref, u_ref, o_ref, score_fn):
    kpos = lax.broadcasted_iota(jnp.int32, (TK, TQ), 0)
    qpos = lax.broadcasted_iota(jnp.int32, (TK, TQ), 1)
    causal = kpos <= qpos
    m_ref[...] = jnp.full(m_ref.shape, NEG, F32)
    acc_ref[...] = jnp.zeros(acc_ref.shape, F32)
    both = tuple(range(N_SUB))

    def scores(j0, nblk, diag_last, subs):
        row0 = pl.multiple_of(j0 * TK, TK)
        kts = [k_ref[0, g, pl.ds(row0, nblk * TK), :] for g in range(groups)]
        pending = []
        for sub in subs:
            for h in range(N_HEADS):
                c = sub * N_HEADS + h
                s = jnp.dot(kts[h // (N_HEADS // groups)], qm_ref[c], preferred_element_type=F32)
                tops, lives, row = [], [], None
                for blk in range(nblk):
                    rows = slice(blk * TK, (blk + 1) * TK)
                    diagonal = diag_last and blk == nblk - 1
                    u, row, live = score_fn(sub, h, s[rows], j0 + blk, diagonal)
                    if diagonal:
                        u = jnp.where(causal, u, NEG)
                    u_ref[c, rows, :] = u
                    top = jnp.max(u, axis=0, keepdims=True)
                    if row is not None:
                        top = top + row
                    if live is not None:
                        top = jnp.where(live, top, NEG)
                    tops.append(top)
                    lives.append(live)
                m_old = m_ref[c]
                m_new = functools.reduce(jnp.maximum, tops, m_old)
                shift = m_new if row is None else m_new - row
                shifts = [shift if live is None else jnp.where(live, shift, -NEG) for live in lives]
                m_ref[c] = m_new
                pending.append((c, h, shifts, jnp.exp2(m_old - m_new)))
        return pending

    def values(j0, pending):
        for c, h, shifts, alpha in pending:
            pv = None
            for blk, shift in enumerate(shifts):
                p = jnp.exp2(u_ref[c, blk * TK:(blk + 1) * TK, :] - shift).astype(BF16)
                term = jnp.dot(vT_ref[0, j0 + blk, h], p, preferred_element_type=F32)
                pv = term if pv is None else pv + term
            acc_ref[c] = alpha * acc_ref[c] + pv

    def common(j0, nblk):
        values(j0, scores(j0, nblk, False, both))

    def wide_step(t, carry):
        common(COMMON_BLOCKS * t, COMMON_BLOCKS)
        return carry

    n_wide = (N_SUB * pair_idx) // COMMON_BLOCKS
    lax.fori_loop(0, n_wide, wide_step, 0)
    for k in range(COMMON_BLOCKS // N_SUB - 1):
        def leftover(_, carry, k=k):
            common(COMMON_BLOCKS * n_wide + N_SUB * k, N_SUB)
            return carry

        lax.fori_loop(0, (pair_idx - (COMMON_BLOCKS // N_SUB) * n_wide > k).astype(jnp.int32), leftover, 0)
    n_common = N_SUB * pair_idx
    pend_a = scores(n_common, 1, True, (0,))
    pend_b = scores(n_common, 2, True, (1,))
    values(n_common, pend_a)
    values(n_common, pend_b)
    for sub in both:
        outs = []
        for h in range(N_HEADS):
            acc = acc_ref[sub * N_HEADS + h]
            outs.append(acc[:HEAD_DIM] / acc[HEAD_DIM:HEAD_DIM + 1])
        o_ref[0, sub * TQ:(sub + 1) * TQ, :] = jnp.concatenate(outs, axis=0).T.astype(BF16)


def _fox_kernel(qT_ref, k_ref, vT_ref, ck_ref, crow_ref, o_ref, qm_ref, m_ref, acc_ref, u_ref):
    pair_idx = pl.program_id(1)
    _stage_head_queries(qT_ref, qm_ref, 1)

    def score_fn(sub, h, s, j, diagonal):
        ck = ck_ref[0, h, pl.ds(pl.multiple_of(j * TK, TK), TK), :]
        cq = crow_ref[0, h:h + 1, sub * TQ:(sub + 1) * TQ]
        return s - jnp.concatenate([ck] * (TQ // LANES), axis=1), cq, None

    _flash_sweep(pair_idx, 1, qm_ref, k_ref, vT_ref, m_ref, acc_ref, u_ref, o_ref, score_fn)


def _mla_kernel(qT_ref, k_ref, vT_ref, o_ref, qm_ref, m_ref, acc_ref, u_ref):
    pair_idx = pl.program_id(1)
    _stage_head_queries(qT_ref, qm_ref, MLA_GROUPS)
    _flash_sweep(pair_idx, MLA_GROUPS, qm_ref, k_ref, vT_ref, m_ref, acc_ref, u_ref, o_ref,
                 lambda sub, h, s, j, diagonal: (s, None, None))


def _moba_kernel(qT_ref, k_ref, vT_ref, kmean_ref, tb_ref, o_ref, qm_ref, m_ref, acc_ref, u_ref, sel_ref):
    pair_idx = pl.program_id(1)
    _stage_head_queries(qT_ref, qm_ref, 1)
    kmean = kmean_ref[0].astype(BF16)
    nb = kmean.shape[0]
    blk = lax.broadcasted_iota(jnp.int32, (nb, TQ), 0)
    own = [N_SUB * pair_idx + sub for sub in range(N_SUB)]
    for c in range(N_SUB * N_HEADS):
        i = own[c // N_HEADS]
        gate = jnp.dot(kmean, qm_ref[c], preferred_element_type=F32)
        gate = jnp.where(blk < i, gate, NEG)
        rank = jnp.zeros((nb, TQ), jnp.int32)
        for jp in range(nb):
            row = gate[jp:jp + 1, :]
            ahead = (row > gate) | ((row == gate) & (jp < blk))
            rank = rank + ahead.astype(jnp.int32)
        sel_ref[c] = ((rank < MOBA_TOPK) & (blk < i)).astype(F32)

    def score_fn(sub, h, s, j, diagonal):
        u = s + tb_ref[jnp.minimum(own[sub] - j, MOBA_TB_ENTRIES - 1), h]
        live = None if diagonal else sel_ref[sub * N_HEADS + h, pl.ds(j, 1), :] > 0.5
        return u, None, live

    _flash_sweep(pair_idx, 1, qm_ref, k_ref, vT_ref, m_ref, acc_ref, u_ref, o_ref, score_fn)


def _dense_mixer(kernel_fn, name, qT, k, vT, extra_args=(), extra_specs=(), extra_scratch=()):
    B, groups, _, S = qT.shape
    NT = S // TK
    assert TQ == TK and N_SUB == 2 and S % (N_SUB * TQ) == 0
    n_chain = N_SUB * N_HEADS
    in_specs = [pl.BlockSpec((1, groups, GROUP_W, N_SUB * TQ), lambda b, i: (b, 0, 0, i)),
                pl.BlockSpec((1, groups, S, GROUP_W), lambda b, i: (b, 0, 0, 0)),
                pl.BlockSpec((1, NT, N_HEADS, V_ROWS, TK), lambda b, i: (b, 0, 0, 0, 0))] + list(extra_specs)
    scratch = [pltpu.VMEM((n_chain, GROUP_W, TQ), BF16),
               pltpu.VMEM((n_chain, 1, TQ), F32),
               pltpu.VMEM((n_chain, V_ROWS, TQ), F32),
               pltpu.VMEM((n_chain, COMMON_BLOCKS * TK, TQ), F32)] + list(extra_scratch)
    return pl.pallas_call(
        kernel_fn,
        out_shape=jax.ShapeDtypeStruct((B, S, GROUP_W), BF16),
        grid=(B, S // (N_SUB * TQ)),
        in_specs=in_specs,
        out_specs=pl.BlockSpec((1, N_SUB * TQ, GROUP_W), lambda b, i: (b, i, 0)),
        scratch_shapes=scratch,
        compiler_params=_compiler_params(("arbitrary", "arbitrary")),
        name=name,
    )(qT, k, vT, *extra_args)


def _dil_kernel(q_ref, kp_ref, kc_ref, vp_ref, vc_ref, tb_ref, o_ref, lse_ref):
    n = pl.program_id(2)
    qb = q_ref.shape[2] // DIL_BLOCK
    kj = lax.broadcasted_iota(jnp.int32, (DIL_BLOCK, 2 * DIL_BLOCK), 1)
    qlane = lax.broadcasted_iota(jnp.int32, (DIL_BLOCK, GROUP_W), 1)
    in_head = [(qlane >= HEAD_DIM * h) & (qlane < HEAD_DIM * (h + 1)) for h in range(N_HEADS)]
    for seq, c in [(seq, c) for seq in range(q_ref.shape[1]) for c in range(qb)]:
        rows = slice(c * DIL_BLOCK, (c + 1) * DIL_BLOCK)
        q = q_ref[0, seq, rows, :]
        if c == 0:
            kb = jnp.concatenate([kp_ref[0, seq], kc_ref[0, seq, :DIL_BLOCK, :]], axis=0)
            vb = jnp.concatenate([vp_ref[0, seq], vc_ref[0, seq, :DIL_BLOCK, :]], axis=0)
        else:
            kb = kc_ref[0, seq, (c - 1) * DIL_BLOCK:(c + 1) * DIL_BLOCK, :]
            vb = vc_ref[0, seq, (c - 1) * DIL_BLOCK:(c + 1) * DIL_BLOCK, :]
        q_heads = jnp.concatenate([jnp.where(msk, q, jnp.zeros_like(q)) for msk in in_head], axis=0)
        s = lax.dot_general(q_heads, kb, (((1,), (1,)), ((), ())), preferred_element_type=F32)
        s = s.reshape(N_HEADS, DIL_BLOCK, 2 * DIL_BLOCK)
        s = s + tb_ref[0]
        if c == 0:
            s = jnp.where(((kj >= DIL_BLOCK) | (n > 0))[None], s, NEG)
        m = jnp.max(s, axis=-1, keepdims=True)
        e = jnp.exp(s - m)
        l = jnp.sum(e, axis=-1, keepdims=True)
        lse = m + jnp.log(l)
        pr = (e * (1.0 / l)).astype(BF16)
        o_all = jnp.dot(pr.reshape(N_HEADS * DIL_BLOCK, 2 * DIL_BLOCK), vb, preferred_element_type=F32)
        o_all = o_all.reshape(N_HEADS, DIL_BLOCK, GROUP_W)
        o_acc = jnp.zeros((DIL_BLOCK, GROUP_W), F32)
        lse_map = jnp.zeros((DIL_BLOCK, GROUP_W), F32)
        for h in range(N_HEADS):
            o_acc = jnp.where(in_head[h], o_all[h], o_acc)
            lse_map = jnp.where(in_head[h], lse[h], lse_map)
        o_ref[0, seq, rows, :] = o_acc.astype(BF16)
        lse_ref[0, seq, rows, :] = lse_map


def _dilated_pattern(p, q, k, v, dil_tb):
    B, dil, L, _ = q.shape
    assert dil == DIL_PATTERNS[p][1] and L % DIL_BLOCK == 0
    nb = L // DIL_BLOCK
    qb = math.gcd(nb, DIL_BLOCKS_PER_STEP)
    seqs = math.gcd(dil, DIL_BLOCKS_PER_STEP // qb)
    cur = pl.BlockSpec((1, seqs, qb * DIL_BLOCK, GROUP_W), lambda b, r, n: (b, r, n, 0))
    prev = pl.BlockSpec((1, seqs, DIL_BLOCK, GROUP_W), lambda b, r, n: (b, r, jnp.maximum(n * qb - 1, 0), 0))
    return pl.pallas_call(
        _dil_kernel,
        out_shape=(jax.ShapeDtypeStruct((B, dil, L, GROUP_W), BF16),
                   jax.ShapeDtypeStruct((B, dil, L, GROUP_W), F32)),
        grid=(B, dil // seqs, nb // qb),
        in_specs=[cur, prev, cur, prev, cur,
                  pl.BlockSpec((1, N_HEADS, DIL_BLOCK, 2 * DIL_BLOCK), lambda b, r, n, p=p: (p, 0, 0, 0))],
        out_specs=(cur, cur),
        compiler_params=_compiler_params(("arbitrary", "arbitrary", "arbitrary")),
        name=f"dilated_{dil}",
    )(q, k, k, v, v, dil_tb)


def _mix_dilations(o_refs, lse_refs, scratch):
    halves = GROUP_W // LANES
    rows = o_refs[0].shape[1] * o_refs[0].shape[2]
    nat = []
    for ref, dst in zip(list(o_refs) + list(lse_refs), scratch):
        dil = ref.shape[1]
        if dil == 1:
            nat.append(ref[0, 0].astype(F32))
            continue
        for r in range(dil):
            for half in range(halves):
                dst[half, pl.ds(r, rows // dil, stride=dil), :] = (
                    ref[0, r, :, half * LANES:(half + 1) * LANES].astype(F32))
        nat.append(jnp.concatenate([dst[half] for half in range(halves)], axis=1))
    os, lses = nat[:len(o_refs)], nat[len(o_refs):]
    m = functools.reduce(jnp.maximum, lses)
    ws = [jnp.exp(l - m) for l in lses]
    tot = functools.reduce(jnp.add, ws)
    return functools.reduce(jnp.add, [(w / tot) * o for w, o in zip(ws, os)])


def _sigmoid(x):
    return 1.0 / (1.0 + jnp.exp(-x))


def _merge_kernel(x_ref, fo_ref, mo_ref, lo_ref, o1_ref, l1_ref, o4_ref, l4_ref, o16_ref, l16_ref, gate_ref, p_ref,
                  wout_ref, png_ref, wpg_ref, wpp_ref, o_ref, *dil_scratch):
    o_dil = _mix_dilations((o1_ref, o4_ref, o16_ref), (l1_ref, l4_ref, l16_ref), dil_scratch)
    mix = jnp.concatenate([fo_ref[...].astype(F32), mo_ref[...].astype(F32), o_dil, lo_ref[...].astype(F32)],
                          axis=1)
    g = gate_ref[...].astype(F32)
    y = jnp.dot((mix * (g * _sigmoid(g))).astype(BF16), wout_ref[...], preferred_element_type=F32)
    x1 = x_ref[...] + y
    hn = _row_norm(x1, png_ref[...]).astype(BF16)
    pg = _sigmoid(jnp.dot(hn, wpg_ref[...], preferred_element_type=F32))
    pp = jnp.dot(p_ref[0].astype(BF16), wpp_ref[...], preferred_element_type=F32)
    o_ref[...] = x1 + pg * pp


def _merge(layer, x2, fo, mo, lo, dil_outs, proj, p3, lp):
    rows = x2.shape[0]
    TM = TM_MERGE
    tiles_per_seq = dil_outs[0].shape[1] * dil_outs[0].shape[2] // TM

    def rspec(w, c=0):
        return pl.BlockSpec((TM, w), lambda i, c=c: (i, c))

    def cspec(shape):
        return pl.BlockSpec(shape, lambda i: (0, 0))

    def dspec(a):
        dil = a.shape[1]
        return pl.BlockSpec((1, dil, TM // dil, GROUP_W),
                            lambda i: (i // tiles_per_seq, 0, lax.rem(i, tiles_per_seq), 0))

    return pl.pallas_call(
        _merge_kernel,
        out_shape=jax.ShapeDtypeStruct((rows, D_MODEL), F32),
        grid=(rows // TM,),
        scratch_shapes=[pltpu.VMEM((GROUP_W // LANES, TM, LANES), F32)] * len(dil_outs),
        in_specs=[rspec(D_MODEL)] + [rspec(GROUP_W)] * 3 + [dspec(a) for a in dil_outs] + [
                  rspec(D_MODEL),
                  pl.BlockSpec((1, TM, PLE_DIM), lambda i: (layer, i, 0)),
                  cspec((D_MODEL, D_MODEL)), cspec((1, D_MODEL)), cspec((D_MODEL, D_MODEL)),
                  cspec((PLE_DIM, D_MODEL))],
        out_specs=rspec(D_MODEL),
        compiler_params=_compiler_params(("arbitrary",)),
        name="merge",
    )(x2, fo, mo, lo, *dil_outs, proj, p3, lp["w_out"], lp["ple_norm_g"], lp["w_ple_gate"], lp["w_ple_proj"])


def _layer_params(i, ln_g, b_forget, qk_gain, mla_q_norm, mla_kv_norm, mla_nope_gain,
                  mla_rope_gain, w_uq, w_ukv, w_out, ple_norm_g, w_ple_gate, w_ple_proj):
    def rope_slab(x1_parts, x2_parts, like):
        pad = jnp.zeros(like.shape[:-1] + (LANES // 2 - 2 * MLA_HALF,), like.dtype)
        return jnp.concatenate(list(x1_parts) + [pad] + list(x2_parts) + [pad], axis=-1)

    per_head = MLA_NOPE + MLA_ROPE
    uq = w_uq[i]
    uq_nope = [uq[:, h * per_head:h * per_head + MLA_NOPE] for h in range(N_HEADS)]
    uq_x1 = [uq[:, h * per_head + MLA_NOPE:h * per_head + MLA_NOPE + MLA_HALF] for h in range(N_HEADS)]
    uq_x2 = [uq[:, h * per_head + MLA_NOPE + MLA_HALF:(h + 1) * per_head] for h in range(N_HEADS)]
    hpg = N_HEADS // MLA_GROUPS
    uq_cols = uq_nope + [rope_slab(uq_x1[g * hpg:(g + 1) * hpg], uq_x2[g * hpg:(g + 1) * hpg], uq)
                         for g in range(MLA_GROUPS)]
    ukv = w_ukv[i]
    ukv_cols = ([ukv[:, h * 2 * HEAD_DIM:h * 2 * HEAD_DIM + HEAD_DIM] for h in range(N_HEADS)]
                + [ukv[:, h * 2 * HEAD_DIM + HEAD_DIM:(h + 1) * 2 * HEAD_DIM] for h in range(N_HEADS)])
    rg = mla_rope_gain[i]
    rope_gain = rope_slab([rg[:, :MLA_HALF]] * hpg, [rg[:, MLA_HALF:]] * hpg, rg)
    return {
        "ln_g": ln_g[i][None, :],
        "gains": jnp.tile(qk_gain[i], (1, N_HEADS)),
        "b_forget": jnp.pad(b_forget[i], (0, LANES - N_HEADS))[None, :],
        "q_norm": mla_q_norm[i][None, :],
        "kv_norm": mla_kv_norm[i][None, :],
        "nope_gain": jnp.tile(mla_nope_gain[i], (1, N_HEADS)),
        "rope_gain": rope_gain,
        "w_uq": jnp.concatenate(uq_cols, axis=1).astype(BF16),
        "w_ukv": jnp.concatenate(ukv_cols, axis=1).astype(BF16),
        "w_out": w_out[i].astype(BF16),
        "ple_norm_g": ple_norm_g[i][None, :],
        "w_ple_gate": w_ple_gate[i].astype(BF16),
        "w_ple_proj": w_ple_proj[i].astype(BF16),
    }


def _constants(S):
    inv = 1.0 / (ROPE_THETA ** (jnp.arange(MLA_HALF, dtype=F32) * 2.0 / MLA_ROPE))
    ang = jnp.arange(S).astype(F32)[:, None] * inv[None, :]
    cos = jnp.tile(jnp.cos(ang), (1, 2 * N_HEADS))
    sin = jnp.tile(jnp.sin(ang), (1, 2 * N_HEADS))
    sign = np.concatenate([-np.ones(LANES // 2, np.float32), np.ones(LANES // 2, np.float32)])
    lane = np.arange(GROUP_W)
    g64 = (lane[:, None] // HEAD_DIM == lane[None, :] // HEAD_DIM).astype(np.float32) / HEAD_DIM
    within = np.arange(LANES) % (LANES // 2)
    rope_head = np.where(within < 2 * MLA_HALF, within // MLA_HALF, -1)
    g32 = ((rope_head[:, None] == rope_head[None, :]) & (rope_head[:, None] >= 0)).astype(np.float32) / MLA_ROPE
    tri = np.tril(np.ones((T_PREP, T_PREP), np.float32))
    ekr = np.zeros((LANES, LANES), np.float32)
    kr_lane0 = IN_OFF["kr"] % LANES
    for c in range(MLA_ROPE):
        half, idx = divmod(c, MLA_HALF)
        for copy in range(N_HEADS // MLA_GROUPS):
            ekr[kr_lane0 + c, half * (LANES // 2) + copy * MLA_HALF + idx] = 1.0
    return {"cos": cos, "sin": sin * sign[None, :], "g64": jnp.asarray(g64, BF16),
            "g32": jnp.asarray(g32, BF16), "tri": jnp.asarray(tri, BF16), "ekr": jnp.asarray(ekr, BF16)}


def kernel(x, p, ln_g, w_in, b_forget, qk_gain, mla_q_norm, mla_kv_norm, mla_nope_gain, mla_rope_gain,
           w_uq, w_ukv, w_out, rel_bias, ple_norm_g, w_ple_gate, w_ple_proj):
    B, S, _ = x.shape
    depth = p.shape[0]
    consts = _constants(S)
    moba_tb, dil_tb = _bias_tables(rel_bias)
    x2 = x.reshape(B * S, D_MODEL)
    w_in_bf16 = w_in.astype(BF16)
    for i in range(depth):
        lp = _layer_params(i, ln_g, b_forget, qk_gain, mla_q_norm, mla_kv_norm, mla_nope_gain,
                           mla_rope_gain, w_uq, w_ukv, w_out, ple_norm_g, w_ple_gate, w_ple_proj)
        (fqT, fk, fvT, fck, fcrow, mqT, mk, mvT, mkmean, lqT, lk, lvT, *dil_qkv, gate) = _front(
            i, x2.reshape(B, S, D_MODEL), w_in_bf16, consts, lp)
        fo = _dense_mixer(
            _fox_kernel, "fox", fqT, fk, fvT, (fck, fcrow),
            (pl.BlockSpec((1, N_HEADS, S, LANES), lambda b, t: (b, 0, 0, 0)),
             pl.BlockSpec((1, 8, N_SUB * TQ), lambda b, t: (b, 0, t))))
        mo = _dense_mixer(
            _moba_kernel, "moba", mqT, mk, mvT, (mkmean, moba_tb),
            (pl.BlockSpec((1, S // MOBA_BLOCK, GROUP_W), lambda b, t: (b, 0, 0)),
             pl.BlockSpec((MOBA_TB_ENTRIES, N_HEADS, MOBA_BLOCK, MOBA_BLOCK), lambda b, t: (0, 0, 0, 0))),
            (pltpu.VMEM((N_SUB * N_HEADS, S // MOBA_BLOCK, TQ), F32),))
        lo = _dense_mixer(_mla_kernel, "mla", lqT, lk, lvT)
        dil_outs = []
        for pat, (_, dil) in enumerate(DIL_PATTERNS):
            q, k, v = (a.reshape(B, dil, S // dil, GROUP_W) for a in dil_qkv[3 * pat:3 * pat + 3])
            dil_outs += list(_dilated_pattern(pat, q, k, v, dil_tb))
        rs = lambda a: a.reshape(B * S, GROUP_W)
        x2 = _merge(i, x2, rs(fo), rs(mo), rs(lo), dil_outs, gate.reshape(B * S, D_MODEL),
                    p.reshape(depth, B * S, PLE_DIM), lp)
    return x2.reshape(B, S, D_MODEL)
```
